```python
import jax, jax.numpy as jnp
from jax import lax
import numpy as np

D_MODEL = 2048
BATCH = 8
SEQ = 2048
DEPTH = 1

CHUNK = 64
NORM_EPS = 1e-6
RWKV_HEADS = 16
RWKV_HEAD_DIM = 64
RWKV_DIM = RWKV_HEADS * RWKV_HEAD_DIM
DECAY_LORA = 96
AAA_LORA = 96
GATE_LORA = 256
GN_EPS = 64e-5
RWKV_SIZES = (RWKV_DIM, RWKV_DIM, RWKV_DIM, DECAY_LORA, AAA_LORA, GATE_LORA)
RWKV_COLS = 3 * RWKV_DIM + DECAY_LORA + AAA_LORA + GATE_LORA
MLSTM_HEADS = 4
MLSTM_QK_DIM = 128
MLSTM_V_DIM = 256
MLSTM_QK = MLSTM_HEADS * MLSTM_QK_DIM
MLSTM_DIM = MLSTM_HEADS * MLSTM_V_DIM
CONV_WIDTH = 4
GATE_SOFTCAP = 15.0
MLSTM_SIZES = (MLSTM_QK, MLSTM_QK, MLSTM_DIM, MLSTM_HEADS, MLSTM_HEADS, MLSTM_DIM)
MLSTM_COLS = 2 * MLSTM_QK + 2 * MLSTM_DIM + 2 * MLSTM_HEADS
N_BRANCHES = 2
GATE_COLS = N_BRANCHES * D_MODEL
IN_COLS = RWKV_COLS + MLSTM_COLS + GATE_COLS
N_EXPERTS = 32
TOP_K = 4
EXPERT_FF = 2048
SWIGLU_LIMIT = 7.0
SWIGLU_ALPHA = 1.702
EXPERT_BLOCK = 256

kernel_name = "hybrid_rwkv7_mlstm_moe_block"


def rmsnorm(x, w):
    xf = x.astype(jnp.float32)
    y = xf * lax.rsqrt(jnp.mean(xf * xf, axis=-1, keepdims=True) + NORM_EPS)
    return (y * w.astype(jnp.float32)).astype(x.dtype)


def split_cols(t, sizes):
    out, off = [], 0
    for s in sizes:
        out.append(t[..., off:off + s])
        off += s
    return out


def token_shift(p):
    return jnp.pad(p, ((0, 0), (1, 0), (0, 0)))[:, :-1]


def causal_depthwise_conv(u, w):
    return lax.conv_general_dilated(
        u, w[:, None, :].astype(u.dtype), window_strides=(1,),
        padding=[(CONV_WIDTH - 1, 0)], dimension_numbers=("NWC", "WIO", "NWC"),
        feature_group_count=u.shape[-1])


def rwkv7_time_mix(p, mu, w0, w_up, a0, a_up, g_up, k_k, k_a, r_k, ln_w, ln_b):
    Bsz, S, _ = p.shape
    H, N = RWKV_HEADS, RWKV_HEAD_DIM
    p = p.astype(jnp.float32)
    p = p + (token_shift(p) - p) * mu
    r, k, v, wd, ad, gd = split_cols(p, RWKV_SIZES)
    w_log = -jax.nn.softplus(-(w0 + jnp.tanh(wd) @ w_up)) - 0.5
    decay = jnp.exp(-jnp.exp(w_log))
    a = jax.nn.sigmoid(a0 + ad @ a_up)
    g = jax.nn.sigmoid(gd) @ g_up
    heads = lambda t: t.reshape(Bsz, S, H, N)
    kk = heads(k * k_k)
    kk = kk / jnp.maximum(jnp.sqrt(jnp.sum(kk * kk, axis=-1, keepdims=True)), 1e-12)
    k = k * (1.0 + (a - 1.0) * k_a)
    r_h, k_h, v_h, w_h, a_h = heads(r), heads(k), heads(v), heads(decay), heads(a)

    def step(state, xs):
        r_t, w_t, k_t, v_t, kk_t, a_t = xs
        sa = jnp.einsum("bhvk,bhk->bhv", state, -kk_t)
        state = (state * w_t[:, :, None, :]
                 + sa[..., None] * (kk_t * a_t)[:, :, None, :]
                 + v_t[..., None] * k_t[:, :, None, :])
        return state, jnp.einsum("bhvk,bhk->bhv", state, r_t)

    tm = lambda t: jnp.moveaxis(t, 1, 0)
    s0 = jnp.zeros((Bsz, H, N, N), jnp.float32)
    _, y = lax.scan(step, s0, (tm(r_h), tm(w_h), tm(k_h), tm(v_h), tm(kk), tm(a_h)))
    y = jnp.moveaxis(y, 0, 1)
    mean = jnp.mean(y, axis=-1, keepdims=True)
    var = jnp.mean(jnp.square(y - mean), axis=-1, keepdims=True)
    y = ((y - mean) * lax.rsqrt(var + GN_EPS)).reshape(Bsz, S, RWKV_DIM) * ln_w + ln_b
    bonus = jnp.sum(r_h * k_h * r_k, axis=-1, keepdims=True) * v_h
    return (y + bonus.reshape(Bsz, S, RWKV_DIM)) * g


def mlstm_chunkwise(q, k, v, log_i, log_f):
    Bsz, S, H, DK = q.shape
    DV = v.shape[-1]
    NC = S // CHUNK

    def to_chunks(t):
        t = t.reshape(Bsz, NC, CHUNK, H, *t.shape[3:])
        return jnp.moveaxis(t, (1, 3), (0, 2))

    causal = jnp.tril(jnp.ones((CHUNK, CHUNK), bool))

    def step(carry, xs):
        C, n, m = carry
        qc, kc, vc, li, lf = xs
        b = jnp.cumsum(lf, axis=-1)
        Dm = jnp.where(causal, b[..., :, None] - b[..., None, :] + li[..., None, :], -jnp.inf)
        inter = b + m[..., None]
        m_t = jnp.maximum(inter, jnp.max(Dm, axis=-1))
        s = jnp.einsum("bhtd,bhjd->bhtj", qc, kc) * jnp.exp(Dm - m_t[..., None])
        w_inter = jnp.exp(inter - m_t)
        num = (jnp.einsum("bhtj,bhjv->bhtv", s, vc)
               + w_inter[..., None] * jnp.einsum("bhtd,bhdv->bhtv", qc, C))
        den = jnp.sum(s, axis=-1) + w_inter * jnp.einsum("bhtd,bhd->bht", qc, n)
        h = num / jnp.maximum(jnp.abs(den), jnp.exp(-m_t))[..., None]
        g_tot = b[..., -1]
        a = li + g_tot[..., None] - b
        m_new = jnp.maximum(g_tot + m, jnp.max(a, axis=-1))
        dec = jnp.exp(g_tot + m - m_new)
        wk = jnp.exp(a - m_new[..., None])
        C_new = dec[..., None, None] * C + jnp.einsum("bhl,bhld,bhlv->bhdv", wk, kc, vc)
        n_new = dec[..., None] * n + jnp.einsum("bhl,bhld->bhd", wk, kc)
        return (C_new, n_new, m_new), h

    init = (jnp.zeros((Bsz, H, DK, DV), jnp.float32), jnp.zeros((Bsz, H, DK), jnp.float32),
            jnp.zeros((Bsz, H), jnp.float32))
    _, h = lax.scan(step, init, (to_chunks(q), to_chunks(k), to_chunks(v),
                                 to_chunks(log_i), to_chunks(log_f)))
    return jnp.moveaxis(h, (0, 2), (1, 3)).reshape(Bsz, S, H, DV)


def mlstm_branch(p, conv_w, i_b, f_b, norm_w):
    Bsz, S, _ = p.shape
    H = MLSTM_HEADS
    p = p.astype(jnp.float32)
    q, k, v, i_raw, f_raw, o_raw = split_cols(p, MLSTM_SIZES)
    qk = jax.nn.silu(causal_depthwise_conv(jnp.concatenate([q, k], axis=-1), conv_w))
    q, k = qk[..., :MLSTM_QK], qk[..., MLSTM_QK:]
    i_pre = GATE_SOFTCAP * jnp.tanh((i_raw + i_b) / GATE_SOFTCAP)
    f_pre = GATE_SOFTCAP * jnp.tanh((f_raw + f_b) / GATE_SOFTCAP)
    h = mlstm_chunkwise(q.reshape(Bsz, S, H, MLSTM_QK_DIM) * (MLSTM_QK_DIM ** -0.5),
                        k.reshape(Bsz, S, H, MLSTM_QK_DIM),
                        v.reshape(Bsz, S, H, MLSTM_V_DIM),
                        i_pre, jax.nn.log_sigmoid(f_pre))
    h = h * lax.rsqrt(jnp.mean(h * h, axis=-1, keepdims=True) + NORM_EPS)
    return h.reshape(Bsz, S, MLSTM_DIM) * norm_w * jax.nn.sigmoid(o_raw)


def moe_ffn(h, router_w, router_b, w_gu, b_gu, w_down, b_down):
    Bsz, S, D = h.shape
    T = Bsz * S
    TK = T * TOP_K
    NB = -(-TK // EXPERT_BLOCK) + N_EXPERTS
    P = NB * EXPERT_BLOCK
    xt = h.reshape(T, D)
    logits = xt.astype(jnp.float32) @ router_w.astype(jnp.float32) + router_b.astype(jnp.float32)
    top_logits, top_idx = lax.top_k(logits, TOP_K)
    top_w = jax.nn.softmax(top_logits, axis=-1)
    flat_e = top_idx.reshape(TK).astype(jnp.int32)
    flat_tok = jnp.repeat(jnp.arange(T, dtype=jnp.int32), TOP_K)
    flat_w = top_w.reshape(TK)
    se, stok, sw = lax.sort((flat_e, flat_tok, flat_w), num_keys=1, is_stable=True)
    counts = jnp.zeros(N_EXPERTS, jnp.int32).at[flat_e].add(1)
    start = jnp.cumsum(counts) - counts
    padded = (counts + EXPERT_BLOCK - 1) // EXPERT_BLOCK * EXPERT_BLOCK
    pad_end = jnp.cumsum(padded)
    dest = (pad_end - padded)[se] + jnp.arange(TK, dtype=jnp.int32) - start[se]
    row_tok = jnp.zeros(P, jnp.int32).at[dest].set(stok)
    row_w = jnp.zeros(P, jnp.float32).at[dest].set(sw)
    block_e = jnp.minimum(
        jnp.searchsorted(pad_end, jnp.arange(NB, dtype=jnp.int32) * EXPERT_BLOCK, side="right"),
        N_EXPERTS - 1)

    def expert_block(args):
        tok, wt, e = args
        xb = xt[tok].astype(jnp.float32)
        gu = xb @ w_gu[e] + b_gu[e]
        gate = jnp.minimum(gu[:, :EXPERT_FF], SWIGLU_LIMIT)
        up = jnp.clip(gu[:, EXPERT_FF:], -SWIGLU_LIMIT, SWIGLU_LIMIT)
        act = (up + 1.0) * gate * jax.nn.sigmoid(SWIGLU_ALPHA * gate)
        return ((act @ w_down[e] + b_down[e]) * wt[:, None]).astype(jnp.float32)

    ys = lax.map(expert_block, (row_tok.reshape(NB, EXPERT_BLOCK),
                                row_w.reshape(NB, EXPERT_BLOCK), block_e))
    out = jnp.zeros((T, D), jnp.float32).at[row_tok].add(ys.reshape(P, D))
    return out.reshape(Bsz, S, D)


def setup_inputs(seed: int = 0) -> dict:
    key = jax.random.key(seed)
    ks = iter(jax.random.split(key, 40))
    nrm = lambda shape, scale: jax.random.normal(next(ks), shape, jnp.float32) * scale
    uni = lambda shape, lo, hi: jax.random.uniform(next(ks), shape, jnp.float32, lo, hi)
    L, D = DEPTH, D_MODEL
    return {
        "x": nrm((BATCH, SEQ, D), 1.0),
        "norm1_w": 1.0 + nrm((L, D), 0.02),
        "w_in": nrm((L, D, IN_COLS), D ** -0.5),
        "b_gate": nrm((L, GATE_COLS), 0.1),
        "rwkv_mu": uni((L, RWKV_COLS), 0.0, 1.0),
        "rwkv_w0": uni((L, RWKV_DIM), -6.0, -1.0),
        "rwkv_w_up": nrm((L, DECAY_LORA, RWKV_DIM), 0.3 * DECAY_LORA ** -0.5),
        "rwkv_a0": nrm((L, RWKV_DIM), 0.1),
        "rwkv_a_up": nrm((L, AAA_LORA, RWKV_DIM), 0.3 * AAA_LORA ** -0.5),
        "rwkv_g_up": nrm((L, GATE_LORA, RWKV_DIM), GATE_LORA ** -0.5),
        "rwkv_k_k": 0.85 + nrm((L, RWKV_DIM), 0.05),
        "rwkv_k_a": 1.0 + nrm((L, RWKV_DIM), 0.05),
        "rwkv_r_k": nrm((L, RWKV_HEADS, RWKV_HEAD_DIM), 0.1),
        "rwkv_ln_w": 1.0 + nrm((L, RWKV_DIM), 0.02),
        "rwkv_ln_b": nrm((L, RWKV_DIM), 0.02),
        "mlstm_conv_w": nrm((L, CONV_WIDTH, 2 * MLSTM_QK), CONV_WIDTH ** -0.5),
        "mlstm_i_b": nrm((L, MLSTM_HEADS), 0.1),
        "mlstm_f_b": jnp.linspace(3.0, 6.0, MLSTM_HEADS, dtype=jnp.float32)[None] + nrm((L, MLSTM_HEADS), 0.1),
        "mlstm_norm_w": 1.0 + nrm((L, MLSTM_DIM), 0.02),
        "w_branch_a": nrm((L, RWKV_DIM, D), RWKV_DIM ** -0.5),
        "w_branch_b": nrm((L, MLSTM_DIM, D), MLSTM_DIM ** -0.5),
        "w_out": nrm((L, D, D), D ** -0.5),
        "norm2_w": 1.0 + nrm((L, D), 0.02),
        "router_w": nrm((L, D, N_EXPERTS), D ** -0.5),
        "router_b": nrm((L, N_EXPERTS), 0.01),
        "w_gu": nrm((L, N_EXPERTS, D, 2 * EXPERT_FF), D ** -0.5),
        "b_gu": nrm((L, N_EXPERTS, 2 * EXPERT_FF), 0.01),
        "w_down": nrm((L, N_EXPERTS, EXPERT_FF, D), EXPERT_FF ** -0.5),
        "b_down": nrm((L, N_EXPERTS, D), 0.01),
        "final_norm_w": 1.0 + nrm((D,), 0.02),
    }


def reference(x, norm1_w, w_in, b_gate, rwkv_mu, rwkv_w0, rwkv_w_up, rwkv_a0, rwkv_a_up,
              rwkv_g_up, rwkv_k_k, rwkv_k_a, rwkv_r_k, rwkv_ln_w, rwkv_ln_b, mlstm_conv_w,
              mlstm_i_b, mlstm_f_b, mlstm_norm_w, w_branch_a, w_branch_b, w_out, norm2_w,
              router_w, router_b, w_gu, b_gu, w_down, b_down, final_norm_w):
    for l in range(DEPTH):
        hn = rmsnorm(x, norm1_w[l])
        proj = hn @ w_in[l]
        p_rwkv = proj[..., :RWKV_COLS]
        p_mlstm = proj[..., RWKV_COLS:RWKV_COLS + MLSTM_COLS]
        p_gate = proj[..., RWKV_COLS + MLSTM_COLS:]
        o_a = rwkv7_time_mix(p_rwkv, rwkv_mu[l], rwkv_w0[l], rwkv_w_up[l], rwkv_a0[l],
                             rwkv_a_up[l], rwkv_g_up[l], rwkv_k_k[l], rwkv_k_a[l], rwkv_r_k[l],
                             rwkv_ln_w[l], rwkv_ln_b[l])
        o_b = mlstm_branch(p_mlstm, mlstm_conv_w[l], mlstm_i_b[l], mlstm_f_b[l], mlstm_norm_w[l])
        y_a = o_a @ w_branch_a[l]
        y_b = o_b @ w_branch_b[l]
        gates = jax.nn.sigmoid(p_gate.astype(jnp.float32) + b_gate[l])
        merged = gates[..., :D_MODEL] * y_a + gates[..., D_MODEL:] * y_b
        x = x + (merged @ w_out[l]).astype(x.dtype)
        x = x + moe_ffn(rmsnorm(x, norm2_w[l]), router_w[l], router_b[l], w_gu[l], b_gu[l],
                        w_down[l], b_down[l]).astype(x.dtype)
    return rmsnorm(x, final_norm_w)
```

```python
import functools

import jax
import jax.numpy as jnp
import numpy as np
from jax import lax
from jax.experimental import pallas as pl
from jax.experimental.pallas import tpu as pltpu

F32 = jnp.float32
BF16 = jnp.bfloat16

D_MODEL = 2048
CHUNK = 64
NORM_EPS = 1e-6
RWKV_HEADS = 16
RWKV_HEAD_DIM = 64
RWKV_DIM = 1024
DECAY_LORA = 96
AAA_LORA = 96
GATE_LORA = 256
GN_EPS = 64e-5
RWKV_COLS = 3 * RWKV_DIM + DECAY_LORA + AAA_LORA + GATE_LORA
MLSTM_HEADS = 4
MLSTM_QK_DIM = 128
MLSTM_V_DIM = 256
MLSTM_QK = 512
MLSTM_DIM = 1024
CONV_WIDTH = 4
GATE_SOFTCAP = 15.0
MLSTM_COLS = 2 * MLSTM_QK + 2 * MLSTM_DIM + 2 * MLSTM_HEADS
N_EXPERTS = 32
TOP_K = 4
EXPERT_FF = 2048
SWIGLU_LIMIT = 7.0
SWIGLU_ALPHA = 1.702

LANES = 128
SUBLANES = 8
VMEM_LIMIT = 56 * 1024 * 1024

LORA_PAD = 128
RW_WD = 3 * RWKV_DIM
RW_AD = RW_WD + LORA_PAD
RW_GD = RW_AD + LORA_PAD
RW_COLS_P = RW_GD + GATE_LORA
ML_V = 2 * MLSTM_QK
ML_IF = ML_V + MLSTM_DIM
ML_O = ML_IF + LANES
ML_COLS_P = ML_O + MLSTM_DIM
ROUTER_PAD = 128

ROW_BLOCK = 256


def _cparams(sem):
    return pltpu.CompilerParams(dimension_semantics=sem, vmem_limit_bytes=VMEM_LIMIT)


def _bdot(a, b):
    return jnp.dot(a.astype(BF16), b.astype(BF16), preferred_element_type=F32)


def _split3(x):
    hi = x.astype(BF16)
    r1 = x - hi.astype(F32)
    mid = r1.astype(BF16)
    lo = (r1 - mid.astype(F32)).astype(BF16)
    return hi, mid, lo


def _dot_exact_lhs(mat_bf16, x):
    hi, mid, lo = _split3(x)
    return (jnp.dot(mat_bf16, hi, preferred_element_type=F32)
            + jnp.dot(mat_bf16, mid, preferred_element_type=F32)
            + jnp.dot(mat_bf16, lo, preferred_element_type=F32))


def _dot_exact_rhs(x, mat_bf16):
    hi, mid, lo = _split3(x)
    return (jnp.dot(hi, mat_bf16, preferred_element_type=F32)
            + jnp.dot(mid, mat_bf16, preferred_element_type=F32)
            + jnp.dot(lo, mat_bf16, preferred_element_type=F32))


def _sigmoid(x):
    return 1.0 / (1.0 + jnp.exp(-x))


def _softplus(x):
    return jnp.maximum(x, 0.0) + jnp.log(1.0 + jnp.exp(-jnp.abs(x)))


def _rmsnorm_kernel(x_ref, w_ref, o_ref):
    x = x_ref[...]
    y = x * lax.rsqrt(jnp.mean(x * x, axis=-1, keepdims=True) + NORM_EPS)
    o_ref[...] = (y * w_ref[...]).astype(o_ref.dtype)


def rmsnorm_rows(x, w, tm=512):
    T, D = x.shape
    return pl.pallas_call(
        _rmsnorm_kernel,
        grid=(T // tm,),
        in_specs=[pl.BlockSpec((tm, D), lambda i: (i, 0)),
                  pl.BlockSpec((1, D), lambda i: (0, 0))],
        out_specs=pl.BlockSpec((tm, D), lambda i: (i, 0)),
        out_shape=jax.ShapeDtypeStruct((T, D), BF16),
        compiler_params=_cparams(("parallel",)),
        name="rmsnorm1",
    )(x, w.reshape(1, D))


def _mm_kernel(a_ref, b_ref, o_ref):
    o_ref[...] = jnp.dot(a_ref[...], b_ref[...], preferred_element_type=F32).astype(o_ref.dtype)


def matmul(a, b, tm, tn, out_dtype, name):
    M, K = a.shape
    N = b.shape[1]
    return pl.pallas_call(
        _mm_kernel,
        grid=(N // tn, M // tm),
        in_specs=[pl.BlockSpec((tm, K), lambda j, i: (i, 0)),
                  pl.BlockSpec((K, tn), lambda j, i: (0, j))],
        out_specs=pl.BlockSpec((tm, tn), lambda j, i: (i, j)),
        out_shape=jax.ShapeDtypeStruct((M, N), out_dtype),
        compiler_params=_cparams(("parallel", "parallel")),
        name=name,
    )(a, b)


def _head_sum_mat():
    r = lax.broadcasted_iota(jnp.int32, (LANES, LANES), 0) // RWKV_HEAD_DIM
    c = lax.broadcasted_iota(jnp.int32, (LANES, LANES), 1) // RWKV_HEAD_DIM
    return jnp.where(r == c, 1.0, 0.0).astype(BF16)


def _rwkv_prep_kernel(p_ref, mu_ref, w0_ref, wup_ref, a0_ref, aup_ref, gup_ref, kk_ref, ka_ref,
                      r_out, k_out, v_out, kk_out, b_out, lw_out, g_out, carry_ref):
    i = pl.program_id(1)

    @pl.when(i == 0)
    def _():
        carry_ref[...] = jnp.zeros_like(carry_ref)

    p = p_ref[...]
    tq = p.shape[0]
    row = lax.broadcasted_iota(jnp.int32, (tq, 1), 0)
    prev = jnp.where(row == 0, carry_ref[0:1, :], pltpu.roll(p, 1, 0))
    carry_ref[0:1, :] = p[tq - 1:tq, :]
    p = p + (prev - p) * mu_ref[...]

    r = p[:, 0:RWKV_DIM]
    k = p[:, RWKV_DIM:2 * RWKV_DIM]
    v = p[:, 2 * RWKV_DIM:3 * RWKV_DIM]
    wd = p[:, RW_WD:RW_AD]
    ad = p[:, RW_AD:RW_GD]
    gd = p[:, RW_GD:RW_COLS_P]

    w_log = -_softplus(-(w0_ref[...] + _bdot(jnp.tanh(wd), wup_ref[...]))) - 0.5
    lw = -jnp.exp(w_log)
    a = _sigmoid(a0_ref[...] + _bdot(ad, aup_ref[...]))
    g = _bdot(_sigmoid(gd), gup_ref[...])

    kk = k * kk_ref[...]
    hs = _head_sum_mat()
    nrm2 = jnp.concatenate(
        [_dot_exact_rhs(kk[:, c:c + LANES] * kk[:, c:c + LANES], hs) for c in range(0, RWKV_DIM, LANES)],
        axis=1)
    kk = kk / jnp.maximum(jnp.sqrt(nrm2), 1e-12)
    k = k * (1.0 + (a - 1.0) * ka_ref[...])

    r_out[...] = r
    k_out[...] = k
    v_out[...] = v
    kk_out[...] = kk
    b_out[...] = kk * a
    lw_out[...] = lw
    g_out[...] = g


def rwkv_prep(p_rwkv, mu, w0, w_up, a0, a_up, g_up, k_k, k_a, tq=256):
    B, S, C = p_rwkv.shape
    blk = lambda n: pl.BlockSpec((None, tq, n), lambda b, i: (b, i, 0))
    full = lambda a: pl.BlockSpec(a.shape, lambda b, i: (0,) * a.ndim)
    params = (mu, w0, w_up, a0, a_up, g_up, k_k, k_a)
    out = jax.ShapeDtypeStruct((B, S, RWKV_DIM), F32)
    return pl.pallas_call(
        _rwkv_prep_kernel,
        grid=(B, S // tq),
        in_specs=[blk(C)] + [full(a) for a in params],
        out_specs=[blk(RWKV_DIM)] * 7,
        out_shape=[out] * 7,
        scratch_shapes=[pltpu.VMEM((SUBLANES, C), F32)],
        compiler_params=_cparams(("parallel", "arbitrary")),
        name="rwkv_prep",
    )(p_rwkv, *params)


def _rwkv_scan_kernel(r_ref, k_ref, v_ref, kk_ref, b_ref, lw_ref, g_ref, rk_ref, lnw_ref, lnb_ref,
                      o_ref, h_ref):
    c = pl.program_id(1)

    @pl.when(c == 0)
    def _():
        h_ref[...] = jnp.zeros_like(h_ref)

    L = CHUNK
    L2 = 2 * L
    ri = lax.broadcasted_iota(jnp.int32, (L, L), 0)
    ci = lax.broadcasted_iota(jnp.int32, (L, L), 1)
    tril = jnp.where(ri >= ci, 1.0, 0.0).astype(BF16)

    lw = lw_ref[...]
    cum = _dot_exact_lhs(tril, lw)
    cum_end = cum[L - 1:L, :]
    w_in = jnp.exp(cum)
    w_prev = jnp.exp(cum - lw)
    w_inv = jnp.exp(-cum)
    w_tail = jnp.exp(cum_end - cum)
    w_end = jnp.exp(cum_end)

    kk = kk_ref[...]
    bb = b_ref[...]
    kx = k_ref[...]
    rx = r_ref[...]
    vx = v_ref[...]
    a_hat = -kk * w_prev
    r_hat = rx * w_in
    b_hat = bb * w_inv
    k_hat = kx * w_inv
    b_til = bb * w_tail
    k_til = kx * w_tail
    rkk = rx * kx * rk_ref[...]

    lane = lax.broadcasted_iota(jnp.int32, (1, LANES), 1)
    m_lo = jnp.where(lane < RWKV_HEAD_DIM, 1.0, 0.0)
    m_hi = 1.0 - m_lo

    def stack(x):
        return jnp.concatenate([x * m_lo, x * m_hi], axis=0)

    r2 = lax.broadcasted_iota(jnp.int32, (L2, L2), 0)
    c2 = lax.broadcasted_iota(jnp.int32, (L2, L2), 1)
    same_head = (r2 // L) == (c2 // L)
    strict = same_head & (r2 > c2)
    incl = same_head & (r2 >= c2)
    diag16 = (r2 // 16) == (c2 // 16)
    eye = jnp.where(r2 == c2, 1.0, 0.0)
    hs = _head_sum_mat()

    for p in range(RWKV_DIM // LANES):
        sl = slice(p * LANES, (p + 1) * LANES)
        a_st = stack(a_hat[:, sl])
        r_st = stack(r_hat[:, sl])
        v_st = stack(vx[:, sl])
        lhs = jnp.concatenate([a_st, r_st], axis=0).astype(BF16)
        rhs = jnp.concatenate([stack(b_hat[:, sl]), stack(k_hat[:, sl])], axis=0).astype(BF16)
        sc = lax.dot_general(lhs, rhs, (((1,), (1,)), ((), ())), preferred_element_type=F32)
        n_ab = jnp.where(strict, sc[:L2, :L2], 0.0)
        a_ak = jnp.where(strict, sc[:L2, L2:], 0.0)
        a_rb = jnp.where(incl, sc[L2:, :L2], 0.0)
        a_rk = jnp.where(incl, sc[L2:, L2:], 0.0)

        nd = jnp.where(diag16, n_ab, 0.0)
        noff = n_ab - nd
        x1 = eye + nd
        s2 = _bdot(nd, nd)
        x2 = x1 + _bdot(x1, s2)
        s4 = _bdot(s2, s2)
        x3 = x2 + _bdot(x2, s4)
        s8 = _bdot(s4, s4)
        t_d = x3 + _bdot(x3, s8)
        m1 = _bdot(t_d, noff)
        m2 = _bdot(m1, m1)
        m3 = _bdot(m1, m2)
        t_inv = _bdot(eye + m1 + m2 + m3, t_d)

        h0 = h_ref[p]
        ah = jnp.dot(lhs, h0.astype(BF16), preferred_element_type=F32)
        x = ah[:L2] + _bdot(a_ak, v_st)
        u = _bdot(t_inv, x)
        uv = jnp.concatenate([u, v_st], axis=0).astype(BF16)
        y_st = ah[L2:] + jnp.dot(jnp.concatenate([a_rb, a_rk], axis=1).astype(BF16), uv,
                                 preferred_element_type=F32)
        y = y_st[:L] + y_st[L:]

        upd_l = jnp.concatenate([stack(b_til[:, sl]), stack(k_til[:, sl])], axis=0).astype(BF16)
        upd = lax.dot_general(upd_l, uv, (((0,), (0,)), ((), ())), preferred_element_type=F32)
        w_col = jnp.sum(eye * w_end[:, sl], axis=1, keepdims=True)
        h_ref[p] = w_col * h0 + upd

        mean = _dot_exact_rhs(y, hs) * (1.0 / RWKV_HEAD_DIM)
        d = y - mean
        var = _dot_exact_rhs(d * d, hs) * (1.0 / RWKV_HEAD_DIM)
        yn = d * lax.rsqrt(var + GN_EPS) * lnw_ref[:, sl] + lnb_ref[:, sl]
        bonus = _dot_exact_rhs(rkk[:, sl], hs) * vx[:, sl]
        o_ref[:, sl] = ((yn + bonus) * g_ref[:, sl]).astype(o_ref.dtype)


def rwkv_scan(r, k, v, kk, b, lw, g, r_k, ln_w, ln_b):
    B, S, C = r.shape
    blk = pl.BlockSpec((None, CHUNK, C), lambda bb, c: (bb, c, 0))
    full = pl.BlockSpec((1, C), lambda bb, c: (0, 0))
    return pl.pallas_call(
        _rwkv_scan_kernel,
        grid=(B, S // CHUNK),
        in_specs=[blk] * 7 + [full] * 3,
        out_specs=blk,
        out_shape=jax.ShapeDtypeStruct((B, S, C), BF16),
        scratch_shapes=[pltpu.VMEM((C // LANES, LANES, LANES), F32)],
        compiler_params=_cparams(("parallel", "arbitrary")),
        name="rwkv_scan",
    )(r, k, v, kk, b, lw, g, r_k, ln_w, ln_b)


def _pad_rows(w, n):
    return jnp.pad(w, ((0, n - w.shape[0]), (0, 0)))


def rwkv_branch(p_pad, mu, w0, w_up, a0, a_up, g_up, k_k, k_a, r_k, ln_w, ln_b):
    row = lambda t: t.reshape(1, -1)
    mu_p = jnp.concatenate([mu[:RW_WD], jnp.pad(mu[RW_WD:RW_WD + DECAY_LORA], (0, LORA_PAD - DECAY_LORA)),
                            jnp.pad(mu[RW_WD + DECAY_LORA:RW_WD + DECAY_LORA + AAA_LORA], (0, LORA_PAD - AAA_LORA)),
                            mu[RW_WD + DECAY_LORA + AAA_LORA:]])
    outs = rwkv_prep(p_pad, row(mu_p), row(w0), _pad_rows(w_up, LORA_PAD).astype(BF16), row(a0),
                     _pad_rows(a_up, LORA_PAD).astype(BF16), g_up.astype(BF16), row(k_k), row(k_a))
    return rwkv_scan(*outs, row(r_k), row(ln_w), row(ln_b))


def _mlstm_kernel(p_ref, convw_ref, ifb_ref, nw_ref, o_ref, carry_ref, c_ref, n_ref, m_ref):
    ci = pl.program_id(1)

    @pl.when(ci == 0)
    def _():
        carry_ref[...] = jnp.zeros_like(carry_ref)
        c_ref[...] = jnp.zeros_like(c_ref)
        n_ref[...] = jnp.zeros_like(n_ref)
        m_ref[...] = jnp.zeros_like(m_ref)

    L = CHUNK
    u = p_ref[:, 0:ML_V]
    ext = jnp.concatenate([carry_ref[...], u], axis=0)
    carry_ref[...] = u[L - SUBLANES:L, :]
    conv = convw_ref[CONV_WIDTH - 1:CONV_WIDTH, :] * u
    for j in range(1, CONV_WIDTH):
        conv = conv + convw_ref[CONV_WIDTH - 1 - j:CONV_WIDTH - j, :] * pltpu.roll(ext, j, 0)[SUBLANES:, :]
    qk = conv * _sigmoid(conv)

    pre = GATE_SOFTCAP * jnp.tanh((p_ref[:, ML_IF:ML_O] + ifb_ref[...]) * (1.0 / GATE_SOFTCAP))
    lane = lax.broadcasted_iota(jnp.int32, (1, LANES), 1)
    comb = jnp.where(lane < MLSTM_HEADS, pre, -_softplus(-pre))
    comb_t = comb.T
    ri = lax.broadcasted_iota(jnp.int32, (L, L), 0)
    cj = lax.broadcasted_iota(jnp.int32, (L, L), 1)
    causal = ri >= cj
    tril = jnp.where(causal, 1.0, 0.0).astype(BF16)
    triu = jnp.where(ri <= cj, 1.0, 0.0).astype(BF16)
    b_col = _dot_exact_lhs(tril, comb)
    b_row = _dot_exact_rhs(comb_t, triu)

    for h in range(MLSTM_HEADS):
        qh = qk[:, h * MLSTM_QK_DIM:(h + 1) * MLSTM_QK_DIM] * (MLSTM_QK_DIM ** -0.5)
        kh = qk[:, MLSTM_QK + h * MLSTM_QK_DIM:MLSTM_QK + (h + 1) * MLSTM_QK_DIM]
        vh = p_ref[:, ML_V + h * MLSTM_V_DIM:ML_V + (h + 1) * MLSTM_V_DIM]
        fcol = MLSTM_HEADS + h
        bcol = b_col[:, fcol:fcol + 1]
        brow = b_row[fcol:fcol + 1, :]
        lirow = comb_t[h:h + 1, :]
        licol = comb[:, h:h + 1]
        m_prev = m_ref[h][0:1, 0:1]
        n_prev = n_ref[h][0:1, :]
        c_prev = c_ref[h]

        dm = jnp.where(causal, bcol - brow + lirow, -jnp.inf)
        inter = bcol + m_prev
        m_t = jnp.maximum(inter, jnp.max(dm, axis=-1, keepdims=True))
        s = lax.dot_general(qh.astype(BF16), kh.astype(BF16), (((1,), (1,)), ((), ())),
                            preferred_element_type=F32) * jnp.exp(dm - m_t)
        w_inter = jnp.exp(inter - m_t)
        num = _bdot(s, vh) + w_inter * _bdot(qh, c_prev)
        den = jnp.sum(s, axis=-1, keepdims=True) + w_inter * jnp.sum(qh * n_prev, axis=-1, keepdims=True)
        hh = num / jnp.maximum(jnp.abs(den), jnp.exp(-m_t))

        g_tot = bcol[L - 1:L, :]
        a = licol + g_tot - bcol
        m_new = jnp.maximum(g_tot + m_prev, jnp.max(a, axis=0, keepdims=True))
        dec = jnp.exp(g_tot + m_prev - m_new)
        wkk = jnp.exp(a - m_new) * kh
        c_ref[h] = dec * c_prev + lax.dot_general(wkk.astype(BF16), vh.astype(BF16), (((0,), (0,)), ((), ())),
                                                  preferred_element_type=F32)
        n_ref[h] = jnp.broadcast_to(dec * n_prev + jnp.sum(wkk, axis=0, keepdims=True), (SUBLANES, LANES))
        m_ref[h] = jnp.broadcast_to(m_new, (SUBLANES, LANES))

        vs = slice(h * MLSTM_V_DIM, (h + 1) * MLSTM_V_DIM)
        hn = hh * lax.rsqrt(jnp.mean(hh * hh, axis=-1, keepdims=True) + NORM_EPS)
        o_raw = p_ref[:, ML_O + h * MLSTM_V_DIM:ML_O + (h + 1) * MLSTM_V_DIM]
        o_ref[:, vs] = (hn * nw_ref[:, vs] * _sigmoid(o_raw)).astype(o_ref.dtype)


def mlstm_branch(p_pad, conv_w, i_b, f_b, norm_w):
    B, S, C = p_pad.shape
    ifb = jnp.pad(jnp.concatenate([i_b, f_b]), (0, LANES - 2 * MLSTM_HEADS)).reshape(1, LANES)
    full = lambda a: pl.BlockSpec(a.shape, lambda b, c: (0,) * a.ndim)
    nw = norm_w.reshape(1, MLSTM_DIM)
    return pl.pallas_call(
        _mlstm_kernel,
        grid=(B, S // CHUNK),
        in_specs=[pl.BlockSpec((None, CHUNK, C), lambda b, c: (b, c, 0)), full(conv_w), full(ifb), full(nw)],
        out_specs=pl.BlockSpec((None, CHUNK, MLSTM_DIM), lambda b, c: (b, c, 0)),
        out_shape=jax.ShapeDtypeStruct((B, S, MLSTM_DIM), BF16),
        scratch_shapes=[pltpu.VMEM((SUBLANES, ML_V), F32),
                        pltpu.VMEM((MLSTM_HEADS, MLSTM_QK_DIM, MLSTM_V_DIM), F32),
                        pltpu.VMEM((MLSTM_HEADS, SUBLANES, LANES), F32),
                        pltpu.VMEM((MLSTM_HEADS, SUBLANES, LANES), F32)],
        compiler_params=_cparams(("parallel", "arbitrary")),
        name="mlstm_scan",
    )(p_pad, conv_w, ifb, nw)


HI_MASK = 0xFFFF0000


def _pack_bf16_pairs(hb_f32):
    c = hb_f32.shape[1] // 2
    u = pltpu.bitcast(hb_f32, jnp.uint32)
    return u[:, c:] | (u[:, :c] >> 16)


def _unpack_bf16_pairs(xu):
    lo = pltpu.bitcast(xu << 16, F32).astype(BF16)
    hi = pltpu.bitcast(xu & jnp.uint32(HI_MASK), F32).astype(BF16)
    return lo, hi


def _merge_kernel(oa_ref, ob_ref, pg_ref, bg_ref, x_ref, wa_ref, wb_ref, wo_ref, n2_ref, rwh_ref, rwl_ref,
                  rb_ref, x1_ref, xp_ref, lg_ref):
    ya = jnp.dot(oa_ref[...], wa_ref[...], preferred_element_type=F32)
    yb = jnp.dot(ob_ref[...], wb_ref[...], preferred_element_type=F32)
    gates = _sigmoid(pg_ref[...] + bg_ref[...])
    merged = gates[:, :D_MODEL] * ya + gates[:, D_MODEL:] * yb
    x1 = x_ref[...] + jnp.dot(merged.astype(BF16), wo_ref[...], preferred_element_type=F32)
    x1_ref[...] = x1
    hn = x1 * lax.rsqrt(jnp.mean(x1 * x1, axis=-1, keepdims=True) + NORM_EPS) * n2_ref[...]
    hi = hn.astype(BF16)
    xp_ref[...] = _pack_bf16_pairs(hi.astype(F32))
    lo = (hn - hi.astype(F32)).astype(BF16)
    lg_ref[...] = (jnp.dot(hi, rwh_ref[...], preferred_element_type=F32)
                   + jnp.dot(hi, rwl_ref[...], preferred_element_type=F32)
                   + jnp.dot(lo, rwh_ref[...], preferred_element_type=F32) + rb_ref[...])


def merge_project(o_a, o_b, p_gate, b_gate, x, w_a, w_b, w_out, norm2_w, router_w, router_b, tm=256):
    T, D = x.shape
    rw = jnp.pad(router_w, ((0, 0), (0, ROUTER_PAD - N_EXPERTS)))
    rw_hi = rw.astype(BF16)
    rw_lo = (rw - rw_hi.astype(F32)).astype(BF16)
    rb = jnp.pad(router_b, (0, ROUTER_PAD - N_EXPERTS)).reshape(1, ROUTER_PAD)
    rows = lambda n: pl.BlockSpec((tm, n), lambda i: (i, 0))
    full = lambda a: pl.BlockSpec(a.shape, lambda i: (0, 0))
    params = (w_a.astype(BF16), w_b.astype(BF16), w_out.astype(BF16), norm2_w.reshape(1, D), rw_hi, rw_lo, rb)
    bg = b_gate.reshape(1, 2 * D)
    return pl.pallas_call(
        _merge_kernel,
        grid=(T // tm,),
        in_specs=[rows(o_a.shape[1]), rows(o_b.shape[1]), rows(2 * D), full(bg), rows(D)] + [full(a) for a in params],
        out_specs=[rows(D), rows(D // 2), rows(ROUTER_PAD)],
        out_shape=[jax.ShapeDtypeStruct((T, D), F32), jax.ShapeDtypeStruct((T, D // 2), jnp.uint32),
                   jax.ShapeDtypeStruct((T, ROUTER_PAD), F32)],
        compiler_params=_cparams(("parallel",)),
        name="merge_project",
    )(o_a, o_b, p_gate, bg, x, *params)


def _row_copy(src_hbm, dst, src_row, dst_row, sem):
    return pltpu.make_async_copy(src_hbm.at[pl.ds(src_row, 1), :], dst.at[pl.ds(dst_row, 1), :], sem)


def _gather_rows(idx_ref, src_hbm, dst, sem):
    n = dst.shape[0]

    def body(i, carry):
        _row_copy(src_hbm, dst, idx_ref[0, 0, i], i, sem).start()
        return carry

    lax.fori_loop(0, n, body, 0)
    pltpu.make_async_copy(src_hbm.at[pl.ds(0, n), :], dst, sem).wait()


def _gather_kernel(tok_ref, xp_hbm, o_ref, sem):
    _gather_rows(tok_ref, xp_hbm, o_ref, sem)


def gather_rows(xp, row_tok):
    nb, _, r = row_tok.shape
    c = xp.shape[1]
    return pl.pallas_call(
        _gather_kernel,
        grid=(nb,),
        in_specs=[pl.BlockSpec((1, 1, r), lambda i: (i, 0, 0), memory_space=pltpu.SMEM),
                  pl.BlockSpec(memory_space=pl.ANY)],
        out_specs=pl.BlockSpec((r, c), lambda i: (i, 0)),
        out_shape=jax.ShapeDtypeStruct((nb * r, c), xp.dtype),
        scratch_shapes=[pltpu.SemaphoreType.DMA(())],
        compiler_params=_cparams(("arbitrary",)),
        name="moe_gather",
    )(row_tok, xp)


def _moe_up_kernel(se_ref, sf_ref, sb_ref, xs_ref, wg_ref, wu_ref, bg_ref, bu_ref, h_ref):
    lo, hi = _unpack_bf16_pairs(xs_ref[...])
    half = lo.shape[1]

    def proj(w_ref, b_ref):
        return (jnp.dot(lo, w_ref[:half, :].astype(BF16), preferred_element_type=F32)
                + jnp.dot(hi, w_ref[half:, :].astype(BF16), preferred_element_type=F32) + b_ref[...])

    gate = jnp.minimum(proj(wg_ref, bg_ref), SWIGLU_LIMIT)
    up = jnp.clip(proj(wu_ref, bu_ref), -SWIGLU_LIMIT, SWIGLU_LIMIT)
    h_ref[...] = ((up + 1.0) * gate * _sigmoid(SWIGLU_ALPHA * gate)).astype(h_ref.dtype)


def moe_up(xs, w_gu, b_gu, step_e, step_f, step_b, tf=512):
    P = xs.shape[0]
    E, D, F2 = w_gu.shape
    F = F2 // 2
    nf = F // tf
    b3 = b_gu.reshape(E, 1, F2)
    return pl.pallas_call(
        _moe_up_kernel,
        grid_spec=pltpu.PrefetchScalarGridSpec(
            num_scalar_prefetch=3,
            grid=(step_e.shape[0],),
            in_specs=[pl.BlockSpec((ROW_BLOCK, D // 2), lambda s, se, sf, sb: (sb[s], 0)),
                      pl.BlockSpec((None, D, tf), lambda s, se, sf, sb: (se[s], 0, sf[s])),
                      pl.BlockSpec((None, D, tf), lambda s, se, sf, sb: (se[s], 0, nf + sf[s])),
                      pl.BlockSpec((None, 1, tf), lambda s, se, sf, sb: (se[s], 0, sf[s])),
                      pl.BlockSpec((None, 1, tf), lambda s, se, sf, sb: (se[s], 0, nf + sf[s]))],
            out_specs=pl.BlockSpec((ROW_BLOCK, tf), lambda s, se, sf, sb: (sb[s], sf[s]))),
        out_shape=jax.ShapeDtypeStruct((P, F), BF16),
        compiler_params=_cparams(("arbitrary",)),
        name="moe_up",
    )(step_e, step_f, step_b, xs, w_gu, w_gu, b3, b3)


def _moe_down_kernel(se_ref, sn_ref, sb_ref, h_ref, wd_ref, bd_ref, rw_ref, y_ref):
    y = jnp.dot(h_ref[...], wd_ref[...].astype(BF16), preferred_element_type=F32) + bd_ref[...]
    y_ref[...] = y * rw_ref[...]


def moe_down(h, w_down, b_down, row_w, step_e, step_n, step_b, tn=512):
    P, F = h.shape
    E, _, D = w_down.shape
    b3 = b_down.reshape(E, 1, D)
    return pl.pallas_call(
        _moe_down_kernel,
        grid_spec=pltpu.PrefetchScalarGridSpec(
            num_scalar_prefetch=3,
            grid=(step_e.shape[0],),
            in_specs=[pl.BlockSpec((ROW_BLOCK, F), lambda s, se, sn, sb: (sb[s], 0)),
                      pl.BlockSpec((None, F, tn), lambda s, se, sn, sb: (se[s], 0, sn[s])),
                      pl.BlockSpec((None, 1, tn), lambda s, se, sn, sb: (se[s], 0, sn[s])),
                      pl.BlockSpec((ROW_BLOCK, 1), lambda s, se, sn, sb: (sb[s], 0))],
            out_specs=pl.BlockSpec((ROW_BLOCK, tn), lambda s, se, sn, sb: (sb[s], sn[s]))),
        out_shape=jax.ShapeDtypeStruct((P, D), F32),
        compiler_params=_cparams(("arbitrary",)),
        name="moe_down",
    )(step_e, step_n, step_b, h, w_down, b3, row_w)


def _combine_kernel(dest_ref, x1_ref, fw_ref, ys_hbm, o_ref, buf, sem):
    _gather_rows(dest_ref, ys_hbm, buf, sem)
    tq = x1_ref.shape[0]
    acc = x1_ref[...]
    for j in range(TOP_K):
        acc = acc + buf[j * tq:(j + 1) * tq, :]
    o_ref[...] = acc * lax.rsqrt(jnp.mean(acc * acc, axis=-1, keepdims=True) + NORM_EPS) * fw_ref[...]


def combine(x1, ys, dest, final_w, tq=128):
    T, D = x1.shape
    dest_blk = dest.reshape(T // tq, tq, TOP_K).transpose(0, 2, 1).reshape(T // tq, 1, TOP_K * tq)
    return pl.pallas_call(
        _combine_kernel,
        grid=(T // tq,),
        in_specs=[pl.BlockSpec((1, 1, TOP_K * tq), lambda i: (i, 0, 0), memory_space=pltpu.SMEM),
                  pl.BlockSpec((tq, D), lambda i: (i, 0)),
                  pl.BlockSpec((1, D), lambda i: (0, 0)),
                  pl.BlockSpec(memory_space=pl.ANY)],
        out_specs=pl.BlockSpec((tq, D), lambda i: (i, 0)),
        out_shape=jax.ShapeDtypeStruct((T, D), F32),
        scratch_shapes=[pltpu.VMEM((TOP_K * tq, D), F32), pltpu.SemaphoreType.DMA(())],
        compiler_params=_cparams(("arbitrary",)),
        name="moe_combine",
    )(dest_blk, x1, final_w.reshape(1, D), ys)


def _routing(logits):
    T = logits.shape[0]
    TK = T * TOP_K
    NB = TK // ROW_BLOCK + N_EXPERTS
    top_logits, top_idx = lax.top_k(logits[:, :N_EXPERTS], TOP_K)
    top_w = jax.nn.softmax(top_logits, axis=-1)
    flat_e = top_idx.reshape(TK).astype(jnp.int32)
    onehot = (flat_e[:, None] == jnp.arange(N_EXPERTS, dtype=jnp.int32)[None, :]).astype(jnp.int32)
    csum = jnp.cumsum(onehot, axis=0)
    rank = jnp.take_along_axis(csum, flat_e[:, None], axis=1)[:, 0] - 1
    counts = csum[-1]
    padded = (counts + ROW_BLOCK - 1) // ROW_BLOCK * ROW_BLOCK
    pad_end = jnp.cumsum(padded)
    dest = (pad_end - padded)[flat_e] + rank
    flat_tok = jnp.repeat(jnp.arange(T, dtype=jnp.int32), TOP_K)
    row_tok = jnp.zeros(NB * ROW_BLOCK, jnp.int32).at[dest].set(flat_tok)
    row_w = jnp.zeros(NB * ROW_BLOCK, F32).at[dest].set(top_w.reshape(TK))
    block_e = jnp.minimum(
        jnp.searchsorted(pad_end, jnp.arange(NB, dtype=jnp.int32) * ROW_BLOCK, side="right"),
        N_EXPERTS - 1).astype(jnp.int32)
    return dest.reshape(T, TOP_K), row_tok.reshape(NB, 1, ROW_BLOCK), row_w.reshape(NB * ROW_BLOCK, 1), block_e


def _steps(block_e, n_tiles):
    nb = block_e.shape[0]
    b = jnp.tile(jnp.arange(nb, dtype=jnp.int32), n_tiles)
    f = jnp.repeat(jnp.arange(n_tiles, dtype=jnp.int32), nb)
    order = jnp.argsort((block_e[b] * n_tiles + f) * nb + b)
    return block_e[b][order], f[order], b[order]


def _repack_w_in(w_in):
    z = lambda n: jnp.zeros((w_in.shape[0], n), w_in.dtype)
    o = RWKV_COLS
    w_r = jnp.concatenate([w_in[:, :RW_WD], w_in[:, RW_WD:RW_WD + DECAY_LORA], z(LORA_PAD - DECAY_LORA),
                           w_in[:, RW_WD + DECAY_LORA:RW_WD + DECAY_LORA + AAA_LORA], z(LORA_PAD - AAA_LORA),
                           w_in[:, RW_WD + DECAY_LORA + AAA_LORA:o]], axis=1)
    w_m = jnp.concatenate([w_in[:, o:o + ML_IF], w_in[:, o + ML_IF:o + ML_IF + 2 * MLSTM_HEADS],
                           z(LANES - 2 * MLSTM_HEADS), w_in[:, o + ML_IF + 2 * MLSTM_HEADS:o + MLSTM_COLS]], axis=1)
    w_g = w_in[:, o + MLSTM_COLS:]
    return w_r.astype(BF16), w_m.astype(BF16), w_g.astype(BF16)


def kernel(x, norm1_w, w_in, b_gate, rwkv_mu, rwkv_w0, rwkv_w_up, rwkv_a0, rwkv_a_up, rwkv_g_up, rwkv_k_k,
           rwkv_k_a, rwkv_r_k, rwkv_ln_w, rwkv_ln_b, mlstm_conv_w, mlstm_i_b, mlstm_f_b, mlstm_norm_w,
           w_branch_a, w_branch_b, w_out, norm2_w, router_w, router_b, w_gu, b_gu, w_down, b_down,
           final_norm_w):
    B, S, D = x.shape
    T = B * S
    xt = x.reshape(T, D)
    assert norm1_w.shape[0] == 1, "single-layer block: the final rmsnorm is fused into the MoE combine"
    for l in range(1):
        hn = rmsnorm_rows(xt, norm1_w[l])
        w_r, w_m, w_g = _repack_w_in(w_in[l])
        p_r = matmul(hn, w_r, 512, RW_COLS_P // 2, F32, "proj_rwkv").reshape(B, S, RW_COLS_P)
        p_m = matmul(hn, w_m, 256, ML_COLS_P, F32, "proj_mlstm").reshape(B, S, ML_COLS_P)
        p_g = matmul(hn, w_g, 512, D, F32, "proj_gate")
        o_a = rwkv_branch(p_r, rwkv_mu[l], rwkv_w0[l], rwkv_w_up[l], rwkv_a0[l], rwkv_a_up[l], rwkv_g_up[l],
                          rwkv_k_k[l], rwkv_k_a[l], rwkv_r_k[l].reshape(-1), rwkv_ln_w[l], rwkv_ln_b[l])
        o_b = mlstm_branch(p_m, mlstm_conv_w[l], mlstm_i_b[l], mlstm_f_b[l], mlstm_norm_w[l])
        x1, xp, logits = merge_project(o_a.reshape(T, RWKV_DIM), o_b.reshape(T, MLSTM_DIM), p_g, b_gate[l], xt,
                                       w_branch_a[l], w_branch_b[l], w_out[l], norm2_w[l], router_w[l],
                                       router_b[l])
        dest, row_tok, row_w, block_e = _routing(logits)
        xs = gather_rows(xp, row_tok)
        h = moe_up(xs, w_gu[l], b_gu[l], *_steps(block_e, EXPERT_FF // 512))
        ys = moe_down(h, w_down[l], b_down[l], row_w, *_steps(block_e, D // 512))
        xt = combine(x1, ys, dest, final_norm_w)
    return xt.reshape(B, S, D)
```

```python
import functools

import jax
import jax.numpy as jnp
import numpy as np
from jax import lax
from jax.experimental import pallas as pl
from jax.experimental.pallas import tpu as pltpu

F32 = jnp.float32
BF16 = jnp.bfloat16

D_MODEL = 2048
CHUNK = 64
NORM_EPS = 1e-6
RWKV_HEADS = 16
RWKV_HEAD_DIM = 64
RWKV_DIM = 1024
DECAY_LORA = 96
AAA_LORA = 96
GATE_LORA = 256
GN_EPS = 64e-5
RWKV_COLS = 3 * RWKV_DIM + DECAY_LORA + AAA_LORA + GATE_LORA
MLSTM_HEADS = 4
MLSTM_QK_DIM = 128
MLSTM_V_DIM = 256
MLSTM_QK = 512
MLSTM_DIM = 1024
CONV_WIDTH = 4
GATE_SOFTCAP = 15.0
MLSTM_COLS = 2 * MLSTM_QK + 2 * MLSTM_DIM + 2 * MLSTM_HEADS
N_EXPERTS = 32
TOP_K = 4
EXPERT_FF = 2048
SWIGLU_LIMIT = 7.0
SWIGLU_ALPHA = 1.702

LANES = 128
SUBLANES = 8
VMEM_LIMIT = 56 * 1024 * 1024

LORA_PAD = 128
RW_WD = 3 * RWKV_DIM
RW_AD = RW_WD + LORA_PAD
RW_GD = RW_AD + LORA_PAD
RW_COLS_P = RW_GD + GATE_LORA
ML_V = 2 * MLSTM_QK
ML_IF = ML_V + MLSTM_DIM
ML_O = ML_IF + LANES
ML_COLS_P = ML_O + MLSTM_DIM
ROUTER_PAD = 128

ROW_BLOCK = 256


def _cparams(sem):
    return pltpu.CompilerParams(dimension_semantics=sem, vmem_limit_bytes=VMEM_LIMIT)


def _bdot(a, b):
    return jnp.dot(a.astype(BF16), b.astype(BF16), preferred_element_type=F32)


def _split3(x):
    hi = x.astype(BF16)
    r1 = x - hi.astype(F32)
    mid = r1.astype(BF16)
    lo = (r1 - mid.astype(F32)).astype(BF16)
    return hi, mid, lo


def _dot_exact_lhs(mat_bf16, x):
    hi, mid, lo = _split3(x)
    return (jnp.dot(mat_bf16, hi, preferred_element_type=F32)
            + jnp.dot(mat_bf16, mid, preferred_element_type=F32)
            + jnp.dot(mat_bf16, lo, preferred_element_type=F32))


def _dot_exact_rhs(x, mat_bf16):
    hi, mid, lo = _split3(x)
    return (jnp.dot(hi, mat_bf16, preferred_element_type=F32)
            + jnp.dot(mid, mat_bf16, preferred_element_type=F32)
            + jnp.dot(lo, mat_bf16, preferred_element_type=F32))


def _sigmoid(x):
    return 1.0 / (1.0 + jnp.exp(-x))


def _softplus(x):
    return jnp.maximum(x, 0.0) + jnp.log(1.0 + jnp.exp(-jnp.abs(x)))


def _rmsnorm_kernel(x_ref, w_ref, o_ref):
    x = x_ref[...]
    y = x * lax.rsqrt(jnp.mean(x * x, axis=-1, keepdims=True) + NORM_EPS)
    o_ref[...] = (y * w_ref[...]).astype(o_ref.dtype)


def rmsnorm_rows(x, w, tm=512):
    T, D = x.shape
    return pl.pallas_call(
        _rmsnorm_kernel,
        grid=(T // tm,),
        in_specs=[pl.BlockSpec((tm, D), lambda i: (i, 0)),
                  pl.BlockSpec((1, D), lambda i: (0, 0))],
        out_specs=pl.BlockSpec((tm, D), lambda i: (i, 0)),
        out_shape=jax.ShapeDtypeStruct((T, D), BF16),
        compiler_params=_cparams(("parallel",)),
        name="rmsnorm1",
    )(x, w.reshape(1, D))


def _mm_kernel(a_ref, b_ref, o_ref):
    o_ref[...] = jnp.dot(a_ref[...], b_ref[...], preferred_element_type=F32).astype(o_ref.dtype)


def matmul(a, b, tm, tn, out_dtype, name):
    M, K = a.shape
    N = b.shape[1]
    return pl.pallas_call(
        _mm_kernel,
        grid=(N // tn, M // tm),
        in_specs=[pl.BlockSpec((tm, K), lambda j, i: (i, 0)),
                  pl.BlockSpec((K, tn), lambda j, i: (0, j))],
        out_specs=pl.BlockSpec((tm, tn), lambda j, i: (i, j)),
        out_shape=jax.ShapeDtypeStruct((M, N), out_dtype),
        compiler_params=_cparams(("parallel", "parallel")),
        name=name,
    )(a, b)


def _head_sum_mat():
    r = lax.broadcasted_iota(jnp.int32, (LANES, LANES), 0) // RWKV_HEAD_DIM
    c = lax.broadcasted_iota(jnp.int32, (LANES, LANES), 1) // RWKV_HEAD_DIM
    return jnp.where(r == c, 1.0, 0.0).astype(BF16)


def _rwkv_prep_kernel(p_ref, mu_ref, w0_ref, wup_ref, a0_ref, aup_ref, gup_ref, kk_ref, ka_ref,
                      r_out, k_out, v_out, kk_out, b_out, lw_out, g_out, carry_ref):
    i = pl.program_id(1)

    @pl.when(i == 0)
    def _():
        carry_ref[...] = jnp.zeros_like(carry_ref)

    p = p_ref[...]
    tq = p.shape[0]
    row = lax.broadcasted_iota(jnp.int32, (tq, 1), 0)
    prev = jnp.where(row == 0, carry_ref[0:1, :], pltpu.roll(p, 1, 0))
    carry_ref[0:1, :] = p[tq - 1:tq, :]
    p = p + (prev - p) * mu_ref[...]

    r = p[:, 0:RWKV_DIM]
    k = p[:, RWKV_DIM:2 * RWKV_DIM]
    v = p[:, 2 * RWKV_DIM:3 * RWKV_DIM]
    wd = p[:, RW_WD:RW_AD]
    ad = p[:, RW_AD:RW_GD]
    gd = p[:, RW_GD:RW_COLS_P]

    w_log = -_softplus(-(w0_ref[...] + _bdot(jnp.tanh(wd), wup_ref[...]))) - 0.5
    lw = -jnp.exp(w_log)
    a = _sigmoid(a0_ref[...] + _bdot(ad, aup_ref[...]))
    g = _bdot(_sigmoid(gd), gup_ref[...])

    kk = k * kk_ref[...]
    hs = _head_sum_mat()
    nrm2 = jnp.concatenate(
        [_dot_exact_rhs(kk[:, c:c + LANES] * kk[:, c:c + LANES], hs) for c in range(0, RWKV_DIM, LANES)],
        axis=1)
    kk = kk / jnp.maximum(jnp.sqrt(nrm2), 1e-12)
    k = k * (1.0 + (a - 1.0) * ka_ref[...])

    r_out[...] = r
    k_out[...] = k
    v_out[...] = v
    kk_out[...] = kk
    b_out[...] = kk * a
    lw_out[...] = lw
    g_out[...] = g


def rwkv_prep(p_rwkv, mu, w0, w_up, a0, a_up, g_up, k_k, k_a, tq=256):
    B, S, C = p_rwkv.shape
    blk = lambda n: pl.BlockSpec((None, tq, n), lambda b, i: (b, i, 0))
    full = lambda a: pl.BlockSpec(a.shape, lambda b, i: (0,) * a.ndim)
    params = (mu, w0, w_up, a0, a_up, g_up, k_k, k_a)
    out = jax.ShapeDtypeStruct((B, S, RWKV_DIM), F32)
    return pl.pallas_call(
        _rwkv_prep_kernel,
        grid=(B, S // tq),
        in_specs=[blk(C)] + [full(a) for a in params],
        out_specs=[blk(RWKV_DIM)] * 7,
        out_shape=[out] * 7,
        scratch_shapes=[pltpu.VMEM((SUBLANES, C), F32)],
        compiler_params=_cparams(("parallel", "arbitrary")),
        name="rwkv_prep",
    )(p_rwkv, *params)


def _rwkv_scan_kernel(r_ref, k_ref, v_ref, kk_ref, b_ref, lw_ref, g_ref, rk_ref, lnw_ref, lnb_ref,
                      o_ref, h_ref):
    c = pl.program_id(1)

    @pl.when(c == 0)
    def _():
        h_ref[...] = jnp.zeros_like(h_ref)

    L = CHUNK
    L2 = 2 * L
    ri = lax.broadcasted_iota(jnp.int32, (L, L), 0)
    ci = lax.broadcasted_iota(jnp.int32, (L, L), 1)
    tril = jnp.where(ri >= ci, 1.0, 0.0).astype(BF16)

    lw = lw_ref[...]
    cum = _dot_exact_lhs(tril, lw)
    cum_end = cum[L - 1:L, :]
    w_in = jnp.exp(cum)
    w_prev = jnp.exp(cum - lw)
    w_inv = jnp.exp(-cum)
    w_tail = jnp.exp(cum_end - cum)
    w_end = jnp.exp(cum_end)

    kk = kk_ref[...]
    bb = b_ref[...]
    kx = k_ref[...]
    rx = r_ref[...]
    vx = v_ref[...]
    a_hat = -kk * w_prev
    r_hat = rx * w_in
    b_hat = bb * w_inv
    k_hat = kx * w_inv
    b_til = bb * w_tail
    k_til = kx * w_tail
    rkk = rx * kx * rk_ref[...]

    lane = lax.broadcasted_iota(jnp.int32, (1, LANES), 1)
    m_lo = jnp.where(lane < RWKV_HEAD_DIM, 1.0, 0.0)
    m_hi = 1.0 - m_lo

    def stack(x):
        return jnp.concatenate([x * m_lo, x * m_hi], axis=0)

    r2 = lax.broadcasted_iota(jnp.int32, (L2, L2), 0)
    c2 = lax.broadcasted_iota(jnp.int32, (L2, L2), 1)
    same_head = (r2 // L) == (c2 // L)
    strict = same_head & (r2 > c2)
    incl = same_head & (r2 >= c2)
    diag16 = (r2 // 16) == (c2 // 16)
    eye = jnp.where(r2 == c2, 1.0, 0.0)
    hs = _head_sum_mat()

    pairs = range(RWKV_DIM // LANES)
    sls = [slice(p * LANES, (p + 1) * LANES) for p in pairs]
    v_st = [stack(vx[:, sl]) for sl in sls]
    lhs = [jnp.concatenate([stack(a_hat[:, sl]), stack(r_hat[:, sl])], axis=0).astype(BF16) for sl in sls]
    rhs = [jnp.concatenate([stack(b_hat[:, sl]), stack(k_hat[:, sl])], axis=0).astype(BF16) for sl in sls]
    sc = [lax.dot_general(lhs[p], rhs[p], (((1,), (1,)), ((), ())), preferred_element_type=F32) for p in pairs]
    n_ab = [jnp.where(strict, sc[p][:L2, :L2], 0.0) for p in pairs]
    a_ak = [jnp.where(strict, sc[p][:L2, L2:], 0.0).astype(BF16) for p in pairs]
    a_r = [jnp.concatenate([jnp.where(incl, sc[p][L2:, :L2], 0.0), jnp.where(incl, sc[p][L2:, L2:], 0.0)],
                           axis=1).astype(BF16) for p in pairs]

    nd = [jnp.where(diag16, n_ab[p], 0.0) for p in pairs]
    noff = [(n_ab[p] - nd[p]).astype(BF16) for p in pairs]
    ndb = [nd[p].astype(BF16) for p in pairs]
    s2 = [jnp.dot(ndb[p], ndb[p], preferred_element_type=F32).astype(BF16) for p in pairs]
    s4 = [jnp.dot(s2[p], s2[p], preferred_element_type=F32).astype(BF16) for p in pairs]
    s8 = [jnp.dot(s4[p], s4[p], preferred_element_type=F32).astype(BF16) for p in pairs]
    x1 = [eye + nd[p] for p in pairs]
    x2 = [x1[p] + _bdot(x1[p], s2[p]) for p in pairs]
    x3 = [x2[p] + _bdot(x2[p], s4[p]) for p in pairs]
    t_d = [(x3[p] + _bdot(x3[p], s8[p])).astype(BF16) for p in pairs]
    m1 = [jnp.dot(t_d[p], noff[p], preferred_element_type=F32) for p in pairs]
    m1b = [m1[p].astype(BF16) for p in pairs]
    m2 = [jnp.dot(m1b[p], m1b[p], preferred_element_type=F32) for p in pairs]
    m3 = [jnp.dot(m1b[p], m2[p].astype(BF16), preferred_element_type=F32) for p in pairs]
    t_inv = [jnp.dot((eye + m1[p] + m2[p] + m3[p]).astype(BF16), t_d[p], preferred_element_type=F32).astype(BF16)
             for p in pairs]

    h0 = [h_ref[p] for p in pairs]
    ah = [jnp.dot(lhs[p], h0[p].astype(BF16), preferred_element_type=F32) for p in pairs]
    x = [ah[p][:L2] + jnp.dot(a_ak[p], v_st[p].astype(BF16), preferred_element_type=F32) for p in pairs]
    u = [jnp.dot(t_inv[p], x[p].astype(BF16), preferred_element_type=F32) for p in pairs]
    uv = [jnp.concatenate([u[p], v_st[p]], axis=0).astype(BF16) for p in pairs]
    y_st = [ah[p][L2:] + jnp.dot(a_r[p], uv[p], preferred_element_type=F32) for p in pairs]
    y = [y_st[p][:L] + y_st[p][L:] for p in pairs]

    for p in pairs:
        sl = sls[p]
        upd_l = jnp.concatenate([stack(b_til[:, sl]), stack(k_til[:, sl])], axis=0).astype(BF16)
        upd = lax.dot_general(upd_l, uv[p], (((0,), (0,)), ((), ())), preferred_element_type=F32)
        w_col = jnp.sum(eye * w_end[:, sl], axis=1, keepdims=True)
        h_ref[p] = w_col * h0[p] + upd

    def head_sums(vals):
        parts = []
        for t in vals:
            hi = t.astype(BF16)
            parts += [hi, (t - hi.astype(F32)).astype(BF16)]
        res = jnp.dot(jnp.concatenate(parts, axis=0), hs, preferred_element_type=F32)
        return [res[2 * i * L:(2 * i + 1) * L] + res[(2 * i + 1) * L:(2 * i + 2) * L] for i in range(len(vals))]

    sums1 = [head_sums([y[p], rkk[:, sls[p]]]) for p in pairs]
    d = [y[p] - sums1[p][0] * (1.0 / RWKV_HEAD_DIM) for p in pairs]
    var = [head_sums([d[p] * d[p]])[0] * (1.0 / RWKV_HEAD_DIM) for p in pairs]
    for p in pairs:
        sl = sls[p]
        yn = d[p] * lax.rsqrt(var[p] + GN_EPS) * lnw_ref[:, sl] + lnb_ref[:, sl]
        o_ref[:, sl] = ((yn + sums1[p][1] * vx[:, sl]) * g_ref[:, sl]).astype(o_ref.dtype)


def rwkv_scan(r, k, v, kk, b, lw, g, r_k, ln_w, ln_b):
    B, S, C = r.shape
    blk = pl.BlockSpec((None, CHUNK, C), lambda bb, c: (bb, c, 0))
    full = pl.BlockSpec((1, C), lambda bb, c: (0, 0))
    return pl.pallas_call(
        _rwkv_scan_kernel,
        grid=(B, S // CHUNK),
        in_specs=[blk] * 7 + [full] * 3,
        out_specs=blk,
        out_shape=jax.ShapeDtypeStruct((B, S, C), BF16),
        scratch_shapes=[pltpu.VMEM((C // LANES, LANES, LANES), F32)],
        compiler_params=_cparams(("parallel", "arbitrary")),
        name="rwkv_scan",
    )(r, k, v, kk, b, lw, g, r_k, ln_w, ln_b)


def _pad_rows(w, n):
    return jnp.pad(w, ((0, n - w.shape[0]), (0, 0)))


def rwkv_branch(p_pad, mu, w0, w_up, a0, a_up, g_up, k_k, k_a, r_k, ln_w, ln_b):
    row = lambda t: t.reshape(1, -1)
    mu_p = jnp.concatenate([mu[:RW_WD], jnp.pad(mu[RW_WD:RW_WD + DECAY_LORA], (0, LORA_PAD - DECAY_LORA)),
                            jnp.pad(mu[RW_WD + DECAY_LORA:RW_WD + DECAY_LORA + AAA_LORA], (0, LORA_PAD - AAA_LORA)),
                            mu[RW_WD + DECAY_LORA + AAA_LORA:]])
    outs = rwkv_prep(p_pad, row(mu_p), row(w0), _pad_rows(w_up, LORA_PAD).astype(BF16), row(a0),
                     _pad_rows(a_up, LORA_PAD).astype(BF16), g_up.astype(BF16), row(k_k), row(k_a))
    return rwkv_scan(*outs, row(r_k), row(ln_w), row(ln_b))


def _mlstm_kernel(p_ref, convw_ref, ifb_ref, nw_ref, o_ref, carry_ref, c_ref, n_ref, m_ref):
    ci = pl.program_id(1)

    @pl.when(ci == 0)
    def _():
        carry_ref[...] = jnp.zeros_like(carry_ref)
        c_ref[...] = jnp.zeros_like(c_ref)
        n_ref[...] = jnp.zeros_like(n_ref)
        m_ref[...] = jnp.zeros_like(m_ref)

    L = CHUNK
    u = p_ref[:, 0:ML_V]
    ext = jnp.concatenate([carry_ref[...], u], axis=0)
    carry_ref[...] = u[L - SUBLANES:L, :]
    conv = convw_ref[CONV_WIDTH - 1:CONV_WIDTH, :] * u
    for j in range(1, CONV_WIDTH):
        conv = conv + convw_ref[CONV_WIDTH - 1 - j:CONV_WIDTH - j, :] * pltpu.roll(ext, j, 0)[SUBLANES:, :]
    qk = conv * _sigmoid(conv)

    pre = GATE_SOFTCAP * jnp.tanh((p_ref[:, ML_IF:ML_O] + ifb_ref[...]) * (1.0 / GATE_SOFTCAP))
    lane = lax.broadcasted_iota(jnp.int32, (1, LANES), 1)
    comb = jnp.where(lane < MLSTM_HEADS, pre, -_softplus(-pre))
    comb_t = comb.T
    ri = lax.broadcasted_iota(jnp.int32, (L, L), 0)
    cj = lax.broadcasted_iota(jnp.int32, (L, L), 1)
    causal = ri >= cj
    tril = jnp.where(causal, 1.0, 0.0).astype(BF16)
    triu = jnp.where(ri <= cj, 1.0, 0.0).astype(BF16)
    b_col = _dot_exact_lhs(tril, comb)
    b_row = _dot_exact_rhs(comb_t, triu)

    for h in range(MLSTM_HEADS):
        qh = qk[:, h * MLSTM_QK_DIM:(h + 1) * MLSTM_QK_DIM] * (MLSTM_QK_DIM ** -0.5)
        kh = qk[:, MLSTM_QK + h * MLSTM_QK_DIM:MLSTM_QK + (h + 1) * MLSTM_QK_DIM]
        vh = p_ref[:, ML_V + h * MLSTM_V_DIM:ML_V + (h + 1) * MLSTM_V_DIM]
        fcol = MLSTM_HEADS + h
        bcol = b_col[:, fcol:fcol + 1]
        brow = b_row[fcol:fcol + 1, :]
        lirow = comb_t[h:h + 1, :]
        licol = comb[:, h:h + 1]
        m_prev = m_ref[h][0:1, 0:1]
        n_prev = n_ref[h][0:1, :]
        c_prev = c_ref[h]

        dm = jnp.where(causal, bcol - brow + lirow, -jnp.inf)
        inter = bcol + m_prev
        m_t = jnp.maximum(inter, jnp.max(dm, axis=-1, keepdims=True))
        s = lax.dot_general(qh.astype(BF16), kh.astype(BF16), (((1,), (1,)), ((), ())),
                            preferred_element_type=F32) * jnp.exp(dm - m_t)
        w_inter = jnp.exp(inter - m_t)
        num = _bdot(s, vh) + w_inter * _bdot(qh, c_prev)
        den = jnp.sum(s, axis=-1, keepdims=True) + w_inter * jnp.sum(qh * n_prev, axis=-1, keepdims=True)
        hh = num / jnp.maximum(jnp.abs(den), jnp.exp(-m_t))

        g_tot = bcol[L - 1:L, :]
        a = licol + g_tot - bcol
        m_new = jnp.maximum(g_tot + m_prev, jnp.max(a, axis=0, keepdims=True))
        dec = jnp.exp(g_tot + m_prev - m_new)
        wkk = jnp.exp(a - m_new) * kh
        c_ref[h] = dec * c_prev + lax.dot_general(wkk.astype(BF16), vh.astype(BF16), (((0,), (0,)), ((), ())),
                                                  preferred_element_type=F32)
        n_ref[h] = jnp.broadcast_to(dec * n_prev + jnp.sum(wkk, axis=0, keepdims=True), (SUBLANES, LANES))
        m_ref[h] = jnp.broadcast_to(m_new, (SUBLANES, LANES))

        vs = slice(h * MLSTM_V_DIM, (h + 1) * MLSTM_V_DIM)
        hn = hh * lax.rsqrt(jnp.mean(hh * hh, axis=-1, keepdims=True) + NORM_EPS)
        o_raw = p_ref[:, ML_O + h * MLSTM_V_DIM:ML_O + (h + 1) * MLSTM_V_DIM]
        o_ref[:, vs] = (hn * nw_ref[:, vs] * _sigmoid(o_raw)).astype(o_ref.dtype)


def mlstm_branch(p_pad, conv_w, i_b, f_b, norm_w):
    B, S, C = p_pad.shape
    ifb = jnp.pad(jnp.concatenate([i_b, f_b]), (0, LANES - 2 * MLSTM_HEADS)).reshape(1, LANES)
    full = lambda a: pl.BlockSpec(a.shape, lambda b, c: (0,) * a.ndim)
    nw = norm_w.reshape(1, MLSTM_DIM)
    return pl.pallas_call(
        _mlstm_kernel,
        grid=(B, S // CHUNK),
        in_specs=[pl.BlockSpec((None, CHUNK, C), lambda b, c: (b, c, 0)), full(conv_w), full(ifb), full(nw)],
        out_specs=pl.BlockSpec((None, CHUNK, MLSTM_DIM), lambda b, c: (b, c, 0)),
        out_shape=jax.ShapeDtypeStruct((B, S, MLSTM_DIM), BF16),
        scratch_shapes=[pltpu.VMEM((SUBLANES, ML_V), F32),
                        pltpu.VMEM((MLSTM_HEADS, MLSTM_QK_DIM, MLSTM_V_DIM), F32),
                        pltpu.VMEM((MLSTM_HEADS, SUBLANES, LANES), F32),
                        pltpu.VMEM((MLSTM_HEADS, SUBLANES, LANES), F32)],
        compiler_params=_cparams(("parallel", "arbitrary")),
        name="mlstm_scan",
    )(p_pad, conv_w, ifb, nw)


HI_MASK = 0xFFFF0000


def _pack_bf16_pairs(hb_f32):
    c = hb_f32.shape[1] // 2
    u = pltpu.bitcast(hb_f32, jnp.uint32)
    return u[:, c:] | (u[:, :c] >> 16)


def _unpack_bf16_pairs(xu):
    lo = pltpu.bitcast(xu << 16, F32).astype(BF16)
    hi = pltpu.bitcast(xu & jnp.uint32(HI_MASK), F32).astype(BF16)
    return lo, hi


def _merge_kernel(oa_ref, ob_ref, pg_ref, bg_ref, x_ref, wa_ref, wb_ref, wo_ref, n2_ref, rwh_ref, rwl_ref,
                  rb_ref, x1_ref, xp_ref, lg_ref):
    ya = jnp.dot(oa_ref[...], wa_ref[...], preferred_element_type=F32)
    yb = jnp.dot(ob_ref[...], wb_ref[...], preferred_element_type=F32)
    gates = _sigmoid(pg_ref[...] + bg_ref[...])
    merged = gates[:, :D_MODEL] * ya + gates[:, D_MODEL:] * yb
    x1 = x_ref[...] + jnp.dot(merged.astype(BF16), wo_ref[...], preferred_element_type=F32)
    x1_ref[...] = x1
    hn = x1 * lax.rsqrt(jnp.mean(x1 * x1, axis=-1, keepdims=True) + NORM_EPS) * n2_ref[...]
    hi = hn.astype(BF16)
    xp_ref[...] = _pack_bf16_pairs(hi.astype(F32))
    lo = (hn - hi.astype(F32)).astype(BF16)
    lg_ref[...] = (jnp.dot(hi, rwh_ref[...], preferred_element_type=F32)
                   + jnp.dot(hi, rwl_ref[...], preferred_element_type=F32)
                   + jnp.dot(lo, rwh_ref[...], preferred_element_type=F32) + rb_ref[...])


def merge_project(o_a, o_b, p_gate, b_gate, x, w_a, w_b, w_out, norm2_w, router_w, router_b, tm=256):
    T, D = x.shape
    rw = jnp.pad(router_w, ((0, 0), (0, ROUTER_PAD - N_EXPERTS)))
    rw_hi = rw.astype(BF16)
    rw_lo = (rw - rw_hi.astype(F32)).astype(BF16)
    rb = jnp.pad(router_b, (0, ROUTER_PAD - N_EXPERTS)).reshape(1, ROUTER_PAD)
    rows = lambda n: pl.BlockSpec((tm, n), lambda i: (i, 0))
    full = lambda a: pl.BlockSpec(a.shape, lambda i: (0, 0))
    params = (w_a.astype(BF16), w_b.astype(BF16), w_out.astype(BF16), norm2_w.reshape(1, D), rw_hi, rw_lo, rb)
    bg = b_gate.reshape(1, 2 * D)
    return pl.pallas_call(
        _merge_kernel,
        grid=(T // tm,),
        in_specs=[rows(o_a.shape[1]), rows(o_b.shape[1]), rows(2 * D), full(bg), rows(D)] + [full(a) for a in params],
        out_specs=[rows(D), rows(D // 2), rows(ROUTER_PAD)],
        out_shape=[jax.ShapeDtypeStruct((T, D), F32), jax.ShapeDtypeStruct((T, D // 2), jnp.uint32),
                   jax.ShapeDtypeStruct((T, ROUTER_PAD), F32)],
        compiler_params=_cparams(("parallel",)),
        name="merge_project",
    )(o_a, o_b, p_gate, bg, x, *params)


UP_TILE = 1024
DISPATCH_TOKENS = 256
COMBINE_TOKENS = 128
DMA_UNROLL = 8


def _row_copy(src, dst, src_row, dst_row, sem):
    return pltpu.make_async_copy(src.at[pl.ds(src_row, 1), :], dst.at[pl.ds(dst_row, 1), :], sem)


def _dispatch_kernel(zrow_ref, dest_ref, xp_hbm, xs_hbm, zbuf, zsem, sem):
    i = pl.program_id(0)
    tq = dest_ref.shape[2] // TOP_K

    def zero_copy(k):
        row = pl.multiple_of(zrow_ref[k], ROW_BLOCK)
        return pltpu.make_async_copy(zbuf, xs_hbm.at[pl.ds(row, ROW_BLOCK), :], zsem)

    @pl.when(i == 0)
    def _():
        zbuf[...] = jnp.zeros_like(zbuf)
        for k in range(zrow_ref.shape[0]):
            pl.when(zrow_ref[k] >= 0)(lambda k=k: zero_copy(k).start())
        for k in range(zrow_ref.shape[0]):
            pl.when(zrow_ref[k] >= 0)(lambda k=k: zero_copy(k).wait())

    def body(t, carry):
        for j in range(TOP_K):
            _row_copy(xp_hbm, xs_hbm, i * tq + t, dest_ref[0, 0, TOP_K * t + j], sem).start()
        return carry

    lax.fori_loop(0, tq, body, 0, unroll=DMA_UNROLL // TOP_K)
    n = TOP_K * tq
    pltpu.make_async_copy(xp_hbm.at[pl.ds(0, n), :], xs_hbm.at[pl.ds(0, n), :], sem).wait()


def moe_dispatch(xp, dest, zrow, n_rows):
    T, C = xp.shape
    tq = DISPATCH_TOKENS
    dest_blk = dest.reshape(T // tq, 1, TOP_K * tq)
    return pl.pallas_call(
        _dispatch_kernel,
        grid_spec=pltpu.PrefetchScalarGridSpec(
            num_scalar_prefetch=1,
            grid=(T // tq,),
            in_specs=[pl.BlockSpec((1, 1, TOP_K * tq), lambda i, zr: (i, 0, 0), memory_space=pltpu.SMEM),
                      pl.BlockSpec(memory_space=pl.ANY)],
            out_specs=pl.BlockSpec(memory_space=pl.ANY),
            scratch_shapes=[pltpu.VMEM((ROW_BLOCK, C), xp.dtype), pltpu.SemaphoreType.DMA(()),
                            pltpu.SemaphoreType.DMA(())]),
        out_shape=jax.ShapeDtypeStruct((n_rows, C), xp.dtype),
        compiler_params=_cparams(("arbitrary",)),
        name="moe_dispatch",
    )(zrow, dest_blk, xp)


def _first_of_group(s, *key_refs):
    prev = jnp.maximum(s - 1, 0)
    first = s == 0
    for ref in key_refs:
        first = first | (ref[s] != ref[prev])
    return first


def _moe_up_kernel(nv_ref, se_ref, sf_ref, sb_ref, so_ref, xs_ref, wg_ref, wu_ref, bg_ref, bu_ref, h_ref,
                   wgb, wub):
    s = pl.program_id(0)

    @pl.when(s >= nv_ref[0])
    def _():
        h_ref[...] = jnp.zeros_like(h_ref)

    @pl.when(s < nv_ref[0])
    def _():
        @pl.when(_first_of_group(s, se_ref, sf_ref))
        def _():
            wgb[...] = wg_ref[...].astype(BF16)
            wub[...] = wu_ref[...].astype(BF16)

        lo, hi = _unpack_bf16_pairs(xs_ref[...])
        half = lo.shape[1]

        def proj(wb, b_ref):
            return (jnp.dot(lo, wb[:half, :], preferred_element_type=F32)
                    + jnp.dot(hi, wb[half:, :], preferred_element_type=F32) + b_ref[...])

        gate = jnp.minimum(proj(wgb, bg_ref), SWIGLU_LIMIT)
        up = jnp.clip(proj(wub, bu_ref), -SWIGLU_LIMIT, SWIGLU_LIMIT)
        h_ref[...] = ((up + 1.0) * gate * _sigmoid(SWIGLU_ALPHA * gate)).astype(h_ref.dtype)


def moe_up(xs, w_gu, b_gu, n_valid, step_e, step_f, step_b, step_o):
    P = xs.shape[0]
    E, D, F2 = w_gu.shape
    F = F2 // 2
    tf = UP_TILE
    nf = F // tf
    b3 = b_gu.reshape(E, 1, F2)
    return pl.pallas_call(
        _moe_up_kernel,
        grid_spec=pltpu.PrefetchScalarGridSpec(
            num_scalar_prefetch=5,
            grid=(step_e.shape[0],),
            in_specs=[pl.BlockSpec((ROW_BLOCK, D // 2), lambda s, nv, se, sf, sb, so: (sb[s], 0)),
                      pl.BlockSpec((None, D, tf), lambda s, nv, se, sf, sb, so: (se[s], 0, sf[s])),
                      pl.BlockSpec((None, D, tf), lambda s, nv, se, sf, sb, so: (se[s], 0, nf + sf[s])),
                      pl.BlockSpec((None, 1, tf), lambda s, nv, se, sf, sb, so: (se[s], 0, sf[s])),
                      pl.BlockSpec((None, 1, tf), lambda s, nv, se, sf, sb, so: (se[s], 0, nf + sf[s]))],
            out_specs=pl.BlockSpec((ROW_BLOCK, tf), lambda s, nv, se, sf, sb, so: (sb[s], so[s])),
            scratch_shapes=[pltpu.VMEM((D, tf), BF16), pltpu.VMEM((D, tf), BF16)]),
        out_shape=jax.ShapeDtypeStruct((P, F), BF16),
        compiler_params=_cparams(("arbitrary",)),
        name="moe_up",
    )(n_valid, step_e, step_f, step_b, step_o, xs, w_gu, w_gu, b3, b3)


def _moe_down_kernel(nv_ref, se_ref, sb_ref, h_ref, wd_ref, bd_ref, y_ref, wdb):
    s = pl.program_id(0)

    @pl.when(s >= nv_ref[0])
    def _():
        y_ref[...] = jnp.zeros_like(y_ref)

    @pl.when(s < nv_ref[0])
    def _():
        @pl.when(_first_of_group(s, se_ref))
        def _():
            wdb[...] = wd_ref[...].astype(BF16)

        y_ref[...] = jnp.dot(h_ref[...], wdb[...], preferred_element_type=F32) + bd_ref[...]


def moe_down(h, w_down, b_down, n_valid, step_e, step_b):
    P, F = h.shape
    E, _, D = w_down.shape
    b3 = b_down.reshape(E, 1, D)
    return pl.pallas_call(
        _moe_down_kernel,
        grid_spec=pltpu.PrefetchScalarGridSpec(
            num_scalar_prefetch=3,
            grid=(step_e.shape[0],),
            in_specs=[pl.BlockSpec((ROW_BLOCK, F), lambda s, nv, se, sb: (sb[s], 0)),
                      pl.BlockSpec((None, F, D), lambda s, nv, se, sb: (se[s], 0, 0)),
                      pl.BlockSpec((None, 1, D), lambda s, nv, se, sb: (se[s], 0, 0))],
            out_specs=pl.BlockSpec((ROW_BLOCK, D), lambda s, nv, se, sb: (sb[s], 0)),
            scratch_shapes=[pltpu.VMEM((F, D), BF16)]),
        out_shape=jax.ShapeDtypeStruct((P, D), F32),
        compiler_params=_cparams(("arbitrary",)),
        name="moe_down",
    )(n_valid, step_e, step_b, h, w_down, b3)


def _combine_kernel(dcur_ref, dnxt_ref, x1_ref, w_ref, fw_ref, ys_hbm, o_ref, buf, sems):
    i = pl.program_id(0)
    n_steps = pl.num_programs(0)
    tq = x1_ref.shape[0]
    n = TOP_K * tq
    slot = lax.rem(i, 2)

    def issue(idx_ref, sl):
        def body(t, carry):
            _row_copy(ys_hbm, buf.at[sl], idx_ref[0, 0, t], t, sems.at[sl]).start()
            return carry

        lax.fori_loop(0, n, body, 0, unroll=DMA_UNROLL)

    pl.when(i == 0)(lambda: issue(dcur_ref, 0))
    pl.when(i + 1 < n_steps)(lambda: issue(dnxt_ref, 1 - slot))
    pltpu.make_async_copy(ys_hbm.at[pl.ds(0, n), :], buf.at[slot], sems.at[slot]).wait()

    acc = x1_ref[...]
    for j in range(TOP_K):
        acc = acc + w_ref[:, j:j + 1] * buf[slot, j * tq:(j + 1) * tq, :]
    o_ref[...] = acc * lax.rsqrt(jnp.mean(acc * acc, axis=-1, keepdims=True) + NORM_EPS) * fw_ref[...]


def combine(x1, ys, dest, top_w, final_w):
    T, D = x1.shape
    tq = COMBINE_TOKENS
    nblk = T // tq
    dest_blk = dest.reshape(nblk, tq, TOP_K).transpose(0, 2, 1).reshape(nblk, 1, TOP_K * tq)
    idx_spec = lambda f: pl.BlockSpec((1, 1, TOP_K * tq), f, memory_space=pltpu.SMEM)
    return pl.pallas_call(
        _combine_kernel,
        grid=(nblk,),
        in_specs=[idx_spec(lambda i: (i, 0, 0)),
                  idx_spec(lambda i: (jnp.minimum(i + 1, nblk - 1), 0, 0)),
                  pl.BlockSpec((tq, D), lambda i: (i, 0)),
                  pl.BlockSpec((tq, TOP_K), lambda i: (i, 0)),
                  pl.BlockSpec((1, D), lambda i: (0, 0)),
                  pl.BlockSpec(memory_space=pl.ANY)],
        out_specs=pl.BlockSpec((tq, D), lambda i: (i, 0)),
        out_shape=jax.ShapeDtypeStruct((T, D), F32),
        scratch_shapes=[pltpu.VMEM((2, TOP_K * tq, D), F32), pltpu.SemaphoreType.DMA((2,))],
        compiler_params=_cparams(("arbitrary",)),
        name="moe_combine",
    )(dest_blk, dest_blk, x1, top_w, final_w.reshape(1, D), ys)


def _routing(logits):
    T = logits.shape[0]
    TK = T * TOP_K
    NB = TK // ROW_BLOCK + N_EXPERTS
    top_logits, top_idx = lax.top_k(logits[:, :N_EXPERTS], TOP_K)
    top_w = jax.nn.softmax(top_logits, axis=-1)
    flat_e = top_idx.reshape(TK).astype(jnp.int32)
    onehot = (flat_e[:, None] == jnp.arange(N_EXPERTS, dtype=jnp.int32)[None, :]).astype(jnp.int32)
    csum = jnp.cumsum(onehot, axis=0)
    rank = jnp.take_along_axis(csum, flat_e[:, None], axis=1)[:, 0] - 1
    counts = csum[-1]
    padded = (counts + ROW_BLOCK - 1) // ROW_BLOCK * ROW_BLOCK
    pad_end = jnp.cumsum(padded).astype(jnp.int32)
    dest = ((pad_end - padded)[flat_e] + rank).astype(jnp.int32)
    block_e = jnp.minimum(
        jnp.searchsorted(pad_end, jnp.arange(NB, dtype=jnp.int32) * ROW_BLOCK, side="right"),
        N_EXPERTS - 1).astype(jnp.int32)
    nb_used = pad_end[-1] // ROW_BLOCK
    tail = nb_used + jnp.arange(N_EXPERTS, dtype=jnp.int32)
    zrow = jnp.concatenate([jnp.where(padded > 0, pad_end - ROW_BLOCK, -1),
                            jnp.where(tail < NB, tail * ROW_BLOCK, -1)]).astype(jnp.int32)
    return dest.reshape(T, TOP_K), top_w, block_e, nb_used, zrow


def _steps(block_e, nb_used, n_tiles):
    nb = block_e.shape[0]
    b = jnp.tile(jnp.arange(nb, dtype=jnp.int32), n_tiles)
    f = jnp.repeat(jnp.arange(n_tiles, dtype=jnp.int32), nb)
    key = jnp.where(b < nb_used, (block_e[b] * n_tiles + f) * nb + b, (N_EXPERTS * n_tiles + f) * nb + b)
    n_valid = (nb_used * n_tiles).astype(jnp.int32)
    order = jnp.argsort(key)
    w_order = order[jnp.minimum(jnp.arange(nb * n_tiles), n_valid - 1)]
    return n_valid.reshape(1), block_e[b][w_order], f[w_order], b[order], f[order]


def _repack_w_in(w_in):
    z = lambda n: jnp.zeros((w_in.shape[0], n), w_in.dtype)
    o = RWKV_COLS
    w_r = jnp.concatenate([w_in[:, :RW_WD], w_in[:, RW_WD:RW_WD + DECAY_LORA], z(LORA_PAD - DECAY_LORA),
                           w_in[:, RW_WD + DECAY_LORA:RW_WD + DECAY_LORA + AAA_LORA], z(LORA_PAD - AAA_LORA),
                           w_in[:, RW_WD + DECAY_LORA + AAA_LORA:o]], axis=1)
    w_m = jnp.concatenate([w_in[:, o:o + ML_IF], w_in[:, o + ML_IF:o + ML_IF + 2 * MLSTM_HEADS],
                           z(LANES - 2 * MLSTM_HEADS), w_in[:, o + ML_IF + 2 * MLSTM_HEADS:o + MLSTM_COLS]], axis=1)
    w_g = w_in[:, o + MLSTM_COLS:]
    return w_r.astype(BF16), w_m.astype(BF16), w_g.astype(BF16)


def kernel(x, norm1_w, w_in, b_gate, rwkv_mu, rwkv_w0, rwkv_w_up, rwkv_a0, rwkv_a_up, rwkv_g_up, rwkv_k_k,
           rwkv_k_a, rwkv_r_k, rwkv_ln_w, rwkv_ln_b, mlstm_conv_w, mlstm_i_b, mlstm_f_b, mlstm_norm_w,
           w_branch_a, w_branch_b, w_out, norm2_w, router_w, router_b, w_gu, b_gu, w_down, b_down,
           final_norm_w):
    B, S, D = x.shape
    T = B * S
    xt = x.reshape(T, D)
    assert norm1_w.shape[0] == 1, "single-layer block: the final rmsnorm is fused into the MoE combine"
    for l in range(1):
        hn = rmsnorm_rows(xt, norm1_w[l])
        w_r, w_m, w_g = _repack_w_in(w_in[l])
        p_r = matmul(hn, w_r, 512, RW_COLS_P // 2, F32, "proj_rwkv").reshape(B, S, RW_COLS_P)
        p_m = matmul(hn, w_m, 256, ML_COLS_P, F32, "proj_mlstm").reshape(B, S, ML_COLS_P)
        p_g = matmul(hn, w_g, 512, D, F32, "proj_gate")
        o_a = rwkv_branch(p_r, rwkv_mu[l], rwkv_w0[l], rwkv_w_up[l], rwkv_a0[l], rwkv_a_up[l], rwkv_g_up[l],
                          rwkv_k_k[l], rwkv_k_a[l], rwkv_r_k[l].reshape(-1), rwkv_ln_w[l], rwkv_ln_b[l])
        o_b = mlstm_branch(p_m, mlstm_conv_w[l], mlstm_i_b[l], mlstm_f_b[l], mlstm_norm_w[l])
        x1, xp, logits = merge_project(o_a.reshape(T, RWKV_DIM), o_b.reshape(T, MLSTM_DIM), p_g, b_gate[l], xt,
                                       w_branch_a[l], w_branch_b[l], w_out[l], norm2_w[l], router_w[l],
                                       router_b[l])
        dest, top_w, block_e, nb_used, zrow = _routing(logits)
        xs = moe_dispatch(xp, dest, zrow, block_e.shape[0] * ROW_BLOCK)
        h = moe_up(xs, w_gu[l], b_gu[l], *_steps(block_e, nb_used, EXPERT_FF // UP_TILE))
        nv, se, _, sb, _ = _steps(block_e, nb_used, 1)
        ys = moe_down(h, w_down[l], b_down[l], nv, se, sb)
        xt = combine(x1, ys, dest, top_w, final_norm_w)
    return xt.reshape(B, S, D)
```

```python
import functools

import jax
import jax.numpy as jnp
import numpy as np
from jax import lax
from jax.experimental import pallas as pl
from jax.experimental.pallas import tpu as pltpu

F32 = jnp.float32
BF16 = jnp.bfloat16

D_MODEL = 2048
CHUNK = 64
NORM_EPS = 1e-6
RWKV_HEADS = 16
RWKV_HEAD_DIM = 64
RWKV_DIM = 1024
DECAY_LORA = 96
AAA_LORA = 96
GATE_LORA = 256
GN_EPS = 64e-5
RWKV_COLS = 3 * RWKV_DIM + DECAY_LORA + AAA_LORA + GATE_LORA
MLSTM_HEADS = 4
MLSTM_QK_DIM = 128
MLSTM_V_DIM = 256
MLSTM_QK = 512
MLSTM_DIM = 1024
CONV_WIDTH = 4
GATE_SOFTCAP = 15.0
MLSTM_COLS = 2 * MLSTM_QK + 2 * MLSTM_DIM + 2 * MLSTM_HEADS
N_EXPERTS = 32
TOP_K = 4
EXPERT_FF = 2048
SWIGLU_LIMIT = 7.0
SWIGLU_ALPHA = 1.702

LANES = 128
SUBLANES = 8
VMEM_LIMIT = 56 * 1024 * 1024

LORA_PAD = 128
RW_WD = 3 * RWKV_DIM
RW_AD = RW_WD + LORA_PAD
RW_GD = RW_AD + LORA_PAD
RW_COLS_P = RW_GD + GATE_LORA
ML_V = 2 * MLSTM_QK
ML_IF = ML_V + MLSTM_DIM
ML_O = ML_IF + LANES
ML_COLS_P = ML_O + MLSTM_DIM
ROUTER_PAD = 128

ROW_BLOCK = 512
ROW_HALF = ROW_BLOCK // 2


def _cparams(sem):
    return pltpu.CompilerParams(dimension_semantics=sem, vmem_limit_bytes=VMEM_LIMIT)


def _bdot(a, b):
    return jnp.dot(a.astype(BF16), b.astype(BF16), preferred_element_type=F32)


def _split3(x):
    hi = x.astype(BF16)
    r1 = x - hi.astype(F32)
    mid = r1.astype(BF16)
    lo = (r1 - mid.astype(F32)).astype(BF16)
    return hi, mid, lo


def _dot_exact_lhs(mat_bf16, x):
    hi, mid, lo = _split3(x)
    return (jnp.dot(mat_bf16, hi, preferred_element_type=F32)
            + jnp.dot(mat_bf16, mid, preferred_element_type=F32)
            + jnp.dot(mat_bf16, lo, preferred_element_type=F32))


def _dot_exact_rhs(x, mat_bf16):
    hi, mid, lo = _split3(x)
    return (jnp.dot(hi, mat_bf16, preferred_element_type=F32)
            + jnp.dot(mid, mat_bf16, preferred_element_type=F32)
            + jnp.dot(lo, mat_bf16, preferred_element_type=F32))


def _sigmoid(x):
    return 1.0 / (1.0 + jnp.exp(-x))


def _softplus(x):
    return jnp.maximum(x, 0.0) + jnp.log(1.0 + jnp.exp(-jnp.abs(x)))


def _rmsnorm_kernel(x_ref, w_ref, o_ref):
    x = x_ref[...]
    y = x * lax.rsqrt(jnp.mean(x * x, axis=-1, keepdims=True) + NORM_EPS)
    o_ref[...] = (y * w_ref[...]).astype(o_ref.dtype)


def rmsnorm_rows(x, w, tm=512):
    T, D = x.shape
    return pl.pallas_call(
        _rmsnorm_kernel,
        grid=(T // tm,),
        in_specs=[pl.BlockSpec((tm, D), lambda i: (i, 0)),
                  pl.BlockSpec((1, D), lambda i: (0, 0))],
        out_specs=pl.BlockSpec((tm, D), lambda i: (i, 0)),
        out_shape=jax.ShapeDtypeStruct((T, D), BF16),
        compiler_params=_cparams(("parallel",)),
        name="rmsnorm1",
    )(x, w.reshape(1, D))


def _mm_kernel(a_ref, b_ref, o_ref):
    o_ref[...] = jnp.dot(a_ref[...], b_ref[...], preferred_element_type=F32).astype(o_ref.dtype)


def matmul(a, b, tm, tn, out_dtype, name):
    M, K = a.shape
    N = b.shape[1]
    return pl.pallas_call(
        _mm_kernel,
        grid=(N // tn, M // tm),
        in_specs=[pl.BlockSpec((tm, K), lambda j, i: (i, 0)),
                  pl.BlockSpec((K, tn), lambda j, i: (0, j))],
        out_specs=pl.BlockSpec((tm, tn), lambda j, i: (i, j)),
        out_shape=jax.ShapeDtypeStruct((M, N), out_dtype),
        compiler_params=_cparams(("parallel", "parallel")),
        name=name,
    )(a, b)


def _head_sum_mat():
    r = lax.broadcasted_iota(jnp.int32, (LANES, LANES), 0) // RWKV_HEAD_DIM
    c = lax.broadcasted_iota(jnp.int32, (LANES, LANES), 1) // RWKV_HEAD_DIM
    return jnp.where(r == c, 1.0, 0.0).astype(BF16)


def _rwkv_prep_kernel(p_ref, mu_ref, w0_ref, wup_ref, a0_ref, aup_ref, gup_ref, kk_ref, ka_ref,
                      r_out, k_out, v_out, kk_out, b_out, lw_out, g_out, carry_ref):
    i = pl.program_id(1)

    @pl.when(i == 0)
    def _():
        carry_ref[...] = jnp.zeros_like(carry_ref)

    p = p_ref[...]
    tq = p.shape[0]
    row = lax.broadcasted_iota(jnp.int32, (tq, 1), 0)
    prev = jnp.where(row == 0, carry_ref[0:1, :], pltpu.roll(p, 1, 0))
    carry_ref[0:1, :] = p[tq - 1:tq, :]
    p = p + (prev - p) * mu_ref[...]

    r = p[:, 0:RWKV_DIM]
    k = p[:, RWKV_DIM:2 * RWKV_DIM]
    v = p[:, 2 * RWKV_DIM:3 * RWKV_DIM]
    wd = p[:, RW_WD:RW_AD]
    ad = p[:, RW_AD:RW_GD]
    gd = p[:, RW_GD:RW_COLS_P]

    w_log = -_softplus(-(w0_ref[...] + _bdot(jnp.tanh(wd), wup_ref[...]))) - 0.5
    lw = -jnp.exp(w_log)
    a = _sigmoid(a0_ref[...] + _bdot(ad, aup_ref[...]))
    g = _bdot(_sigmoid(gd), gup_ref[...])

    kk = k * kk_ref[...]
    hs = _head_sum_mat()
    nrm2 = jnp.concatenate(
        [_dot_exact_rhs(kk[:, c:c + LANES] * kk[:, c:c + LANES], hs) for c in range(0, RWKV_DIM, LANES)],
        axis=1)
    kk = kk / jnp.maximum(jnp.sqrt(nrm2), 1e-12)
    k = k * (1.0 + (a - 1.0) * ka_ref[...])

    r_out[...] = r
    k_out[...] = k
    v_out[...] = v
    kk_out[...] = kk
    b_out[...] = kk * a
    lw_out[...] = lw
    g_out[...] = g


def rwkv_prep(p_rwkv, mu, w0, w_up, a0, a_up, g_up, k_k, k_a, tq=256):
    B, S, C = p_rwkv.shape
    blk = lambda n: pl.BlockSpec((None, tq, n), lambda b, i: (b, i, 0))
    full = lambda a: pl.BlockSpec(a.shape, lambda b, i: (0,) * a.ndim)
    params = (mu, w0, w_up, a0, a_up, g_up, k_k, k_a)
    out = jax.ShapeDtypeStruct((B, S, RWKV_DIM), F32)
    return pl.pallas_call(
        _rwkv_prep_kernel,
        grid=(B, S // tq),
        in_specs=[blk(C)] + [full(a) for a in params],
        out_specs=[blk(RWKV_DIM)] * 7,
        out_shape=[out] * 7,
        scratch_shapes=[pltpu.VMEM((SUBLANES, C), F32)],
        compiler_params=_cparams(("parallel", "arbitrary")),
        name="rwkv_prep",
    )(p_rwkv, *params)


def _rwkv_scan_kernel(r_ref, k_ref, v_ref, kk_ref, b_ref, lw_ref, g_ref, rk_ref, lnw_ref, lnb_ref,
                      o_ref, h_ref):
    c = pl.program_id(1)

    @pl.when(c == 0)
    def _():
        h_ref[...] = jnp.zeros_like(h_ref)

    L = CHUNK
    L2 = 2 * L
    ri = lax.broadcasted_iota(jnp.int32, (L, L), 0)
    ci = lax.broadcasted_iota(jnp.int32, (L, L), 1)
    tril = jnp.where(ri >= ci, 1.0, 0.0).astype(BF16)

    lw = lw_ref[...]
    cum = _dot_exact_lhs(tril, lw)
    cum_end = cum[L - 1:L, :]
    w_in = jnp.exp(cum)
    w_prev = jnp.exp(cum - lw)
    w_inv = jnp.exp(-cum)
    w_tail = jnp.exp(cum_end - cum)
    w_end = jnp.exp(cum_end)

    kk = kk_ref[...]
    bb = b_ref[...]
    kx = k_ref[...]
    rx = r_ref[...]
    vx = v_ref[...]
    a_hat = -kk * w_prev
    r_hat = rx * w_in
    b_hat = bb * w_inv
    k_hat = kx * w_inv
    b_til = bb * w_tail
    k_til = kx * w_tail
    rkk = rx * kx * rk_ref[...]

    lane = lax.broadcasted_iota(jnp.int32, (1, LANES), 1)
    m_lo = jnp.where(lane < RWKV_HEAD_DIM, 1.0, 0.0)
    m_hi = 1.0 - m_lo

    def stack(x):
        return jnp.concatenate([x * m_lo, x * m_hi], axis=0)

    r2 = lax.broadcasted_iota(jnp.int32, (L2, L2), 0)
    c2 = lax.broadcasted_iota(jnp.int32, (L2, L2), 1)
    same_head = (r2 // L) == (c2 // L)
    strict = same_head & (r2 > c2)
    incl = same_head & (r2 >= c2)
    diag16 = (r2 // 16) == (c2 // 16)
    eye = jnp.where(r2 == c2, 1.0, 0.0)
    hs = _head_sum_mat()

    pairs = range(RWKV_DIM // LANES)
    sls = [slice(p * LANES, (p + 1) * LANES) for p in pairs]
    v_st = [stack(vx[:, sl]) for sl in sls]
    lhs = [jnp.concatenate([stack(a_hat[:, sl]), stack(r_hat[:, sl])], axis=0).astype(BF16) for sl in sls]
    rhs = [jnp.concatenate([stack(b_hat[:, sl]), stack(k_hat[:, sl])], axis=0).astype(BF16) for sl in sls]
    sc = [lax.dot_general(lhs[p], rhs[p], (((1,), (1,)), ((), ())), preferred_element_type=F32) for p in pairs]
    n_ab = [jnp.where(strict, sc[p][:L2, :L2], 0.0) for p in pairs]
    a_ak = [jnp.where(strict, sc[p][:L2, L2:], 0.0).astype(BF16) for p in pairs]
    a_r = [jnp.concatenate([jnp.where(incl, sc[p][L2:, :L2], 0.0), jnp.where(incl, sc[p][L2:, L2:], 0.0)],
                           axis=1).astype(BF16) for p in pairs]

    nd = [jnp.where(diag16, n_ab[p], 0.0) for p in pairs]
    noff = [(n_ab[p] - nd[p]).astype(BF16) for p in pairs]
    ndb = [nd[p].astype(BF16) for p in pairs]
    s2 = [jnp.dot(ndb[p], ndb[p], preferred_element_type=F32).astype(BF16) for p in pairs]
    s4 = [jnp.dot(s2[p], s2[p], preferred_element_type=F32).astype(BF16) for p in pairs]
    s8 = [jnp.dot(s4[p], s4[p], preferred_element_type=F32).astype(BF16) for p in pairs]
    x1 = [eye + nd[p] for p in pairs]
    x2 = [x1[p] + _bdot(x1[p], s2[p]) for p in pairs]
    x3 = [x2[p] + _bdot(x2[p], s4[p]) for p in pairs]
    t_d = [(x3[p] + _bdot(x3[p], s8[p])).astype(BF16) for p in pairs]
    m1 = [jnp.dot(t_d[p], noff[p], preferred_element_type=F32) for p in pairs]
    m1b = [m1[p].astype(BF16) for p in pairs]
    m2 = [jnp.dot(m1b[p], m1b[p], preferred_element_type=F32) for p in pairs]
    m3 = [jnp.dot(m1b[p], m2[p].astype(BF16), preferred_element_type=F32) for p in pairs]
    t_inv = [jnp.dot((eye + m1[p] + m2[p] + m3[p]).astype(BF16), t_d[p], preferred_element_type=F32).astype(BF16)
             for p in pairs]

    h0 = [h_ref[p] for p in pairs]
    ah = [jnp.dot(lhs[p], h0[p].astype(BF16), preferred_element_type=F32) for p in pairs]
    x = [ah[p][:L2] + jnp.dot(a_ak[p], v_st[p].astype(BF16), preferred_element_type=F32) for p in pairs]
    u = [jnp.dot(t_inv[p], x[p].astype(BF16), preferred_element_type=F32) for p in pairs]
    uv = [jnp.concatenate([u[p], v_st[p]], axis=0).astype(BF16) for p in pairs]
    y_st = [ah[p][L2:] + jnp.dot(a_r[p], uv[p], preferred_element_type=F32) for p in pairs]
    y = [y_st[p][:L] + y_st[p][L:] for p in pairs]

    for p in pairs:
        sl = sls[p]
        upd_l = jnp.concatenate([stack(b_til[:, sl]), stack(k_til[:, sl])], axis=0).astype(BF16)
        upd = lax.dot_general(upd_l, uv[p], (((0,), (0,)), ((), ())), preferred_element_type=F32)
        w_col = jnp.sum(eye * w_end[:, sl], axis=1, keepdims=True)
        h_ref[p] = w_col * h0[p] + upd

    def head_sums(vals):
        parts = []
        for t in vals:
            hi = t.astype(BF16)
            parts += [hi, (t - hi.astype(F32)).astype(BF16)]
        res = jnp.dot(jnp.concatenate(parts, axis=0), hs, preferred_element_type=F32)
        return [res[2 * i * L:(2 * i + 1) * L] + res[(2 * i + 1) * L:(2 * i + 2) * L] for i in range(len(vals))]

    sums1 = [head_sums([y[p], rkk[:, sls[p]]]) for p in pairs]
    d = [y[p] - sums1[p][0] * (1.0 / RWKV_HEAD_DIM) for p in pairs]
    var = [head_sums([d[p] * d[p]])[0] * (1.0 / RWKV_HEAD_DIM) for p in pairs]
    for p in pairs:
        sl = sls[p]
        yn = d[p] * lax.rsqrt(var[p] + GN_EPS) * lnw_ref[:, sl] + lnb_ref[:, sl]
        o_ref[:, sl] = ((yn + sums1[p][1] * vx[:, sl]) * g_ref[:, sl]).astype(o_ref.dtype)


def rwkv_scan(r, k, v, kk, b, lw, g, r_k, ln_w, ln_b):
    B, S, C = r.shape
    blk = pl.BlockSpec((None, CHUNK, C), lambda bb, c: (bb, c, 0))
    full = pl.BlockSpec((1, C), lambda bb, c: (0, 0))
    return pl.pallas_call(
        _rwkv_scan_kernel,
        grid=(B, S // CHUNK),
        in_specs=[blk] * 7 + [full] * 3,
        out_specs=blk,
        out_shape=jax.ShapeDtypeStruct((B, S, C), BF16),
        scratch_shapes=[pltpu.VMEM((C // LANES, LANES, LANES), F32)],
        compiler_params=_cparams(("parallel", "arbitrary")),
        name="rwkv_scan",
    )(r, k, v, kk, b, lw, g, r_k, ln_w, ln_b)


def _pad_rows(w, n):
    return jnp.pad(w, ((0, n - w.shape[0]), (0, 0)))


def rwkv_branch(p_pad, mu, w0, w_up, a0, a_up, g_up, k_k, k_a, r_k, ln_w, ln_b):
    row = lambda t: t.reshape(1, -1)
    mu_p = jnp.concatenate([mu[:RW_WD], jnp.pad(mu[RW_WD:RW_WD + DECAY_LORA], (0, LORA_PAD - DECAY_LORA)),
                            jnp.pad(mu[RW_WD + DECAY_LORA:RW_WD + DECAY_LORA + AAA_LORA], (0, LORA_PAD - AAA_LORA)),
                            mu[RW_WD + DECAY_LORA + AAA_LORA:]])
    outs = rwkv_prep(p_pad, row(mu_p), row(w0), _pad_rows(w_up, LORA_PAD).astype(BF16), row(a0),
                     _pad_rows(a_up, LORA_PAD).astype(BF16), g_up.astype(BF16), row(k_k), row(k_a))
    return rwkv_scan(*outs, row(r_k), row(ln_w), row(ln_b))


def _mlstm_kernel(p_ref, convw_ref, ifb_ref, nw_ref, o_ref, carry_ref, c_ref, n_ref, m_ref):
    ci = pl.program_id(1)

    @pl.when(ci == 0)
    def _():
        carry_ref[...] = jnp.zeros_like(carry_ref)
        c_ref[...] = jnp.zeros_like(c_ref)
        n_ref[...] = jnp.zeros_like(n_ref)
        m_ref[...] = jnp.zeros_like(m_ref)

    L = CHUNK
    u = p_ref[:, 0:ML_V]
    ext = jnp.concatenate([carry_ref[...], u], axis=0)
    carry_ref[...] = u[L - SUBLANES:L, :]
    conv = convw_ref[CONV_WIDTH - 1:CONV_WIDTH, :] * u
    for j in range(1, CONV_WIDTH):
        conv = conv + convw_ref[CONV_WIDTH - 1 - j:CONV_WIDTH - j, :] * pltpu.roll(ext, j, 0)[SUBLANES:, :]
    qk = conv * _sigmoid(conv)

    pre = GATE_SOFTCAP * jnp.tanh((p_ref[:, ML_IF:ML_O] + ifb_ref[...]) * (1.0 / GATE_SOFTCAP))
    lane = lax.broadcasted_iota(jnp.int32, (1, LANES), 1)
    comb = jnp.where(lane < MLSTM_HEADS, pre, -_softplus(-pre))
    comb_t = comb.T
    ri = lax.broadcasted_iota(jnp.int32, (L, L), 0)
    cj = lax.broadcasted_iota(jnp.int32, (L, L), 1)
    causal = ri >= cj
    tril = jnp.where(causal, 1.0, 0.0).astype(BF16)
    triu = jnp.where(ri <= cj, 1.0, 0.0).astype(BF16)
    b_col = _dot_exact_lhs(tril, comb)
    b_row = _dot_exact_rhs(comb_t, triu)

    H = range(MLSTM_HEADS)
    dk, dv = MLSTM_QK_DIM, MLSTM_V_DIM
    qh = [qk[:, h * dk:(h + 1) * dk] * (dk ** -0.5) for h in H]
    kh = [qk[:, MLSTM_QK + h * dk:MLSTM_QK + (h + 1) * dk] for h in H]
    vh = [p_ref[:, ML_V + h * dv:ML_V + (h + 1) * dv].astype(BF16) for h in H]
    qb = [qh[h].astype(BF16) for h in H]
    bcol = [b_col[:, MLSTM_HEADS + h:MLSTM_HEADS + h + 1] for h in H]
    brow = [b_row[MLSTM_HEADS + h:MLSTM_HEADS + h + 1, :] for h in H]
    m_prev = [m_ref[h][0:1, 0:1] for h in H]
    n_prev = [n_ref[h][0:1, :] for h in H]
    c_prev = [c_ref[h] for h in H]

    qk_t = [lax.dot_general(qb[h], kh[h].astype(BF16), (((1,), (1,)), ((), ())), preferred_element_type=F32)
            for h in H]
    qc = [jnp.dot(qb[h], c_prev[h].astype(BF16), preferred_element_type=F32) for h in H]
    dm = [jnp.where(causal, bcol[h] - brow[h] + comb_t[h:h + 1, :], -jnp.inf) for h in H]
    inter = [bcol[h] + m_prev[h] for h in H]
    m_t = [jnp.maximum(inter[h], jnp.max(dm[h], axis=-1, keepdims=True)) for h in H]
    s = [qk_t[h] * jnp.exp(dm[h] - m_t[h]) for h in H]
    w_inter = [jnp.exp(inter[h] - m_t[h]) for h in H]
    num = [jnp.dot(s[h].astype(BF16), vh[h], preferred_element_type=F32) + w_inter[h] * qc[h] for h in H]
    den = [jnp.sum(s[h], axis=-1, keepdims=True) + w_inter[h] * jnp.sum(qh[h] * n_prev[h], axis=-1, keepdims=True)
           for h in H]
    hh = [num[h] / jnp.maximum(jnp.abs(den[h]), jnp.exp(-m_t[h])) for h in H]

    g_tot = [bcol[h][L - 1:L, :] for h in H]
    a = [comb[:, h:h + 1] + g_tot[h] - bcol[h] for h in H]
    m_new = [jnp.maximum(g_tot[h] + m_prev[h], jnp.max(a[h], axis=0, keepdims=True)) for h in H]
    dec = [jnp.exp(g_tot[h] + m_prev[h] - m_new[h]) for h in H]
    wkk = [jnp.exp(a[h] - m_new[h]) * kh[h] for h in H]
    for h in H:
        c_ref[h] = dec[h] * c_prev[h] + lax.dot_general(wkk[h].astype(BF16), vh[h], (((0,), (0,)), ((), ())),
                                                        preferred_element_type=F32)
        n_ref[h] = jnp.broadcast_to(dec[h] * n_prev[h] + jnp.sum(wkk[h], axis=0, keepdims=True),
                                    (SUBLANES, LANES))
        m_ref[h] = jnp.broadcast_to(m_new[h], (SUBLANES, LANES))
    for h in H:
        vs = slice(h * dv, (h + 1) * dv)
        hn = hh[h] * lax.rsqrt(jnp.mean(hh[h] * hh[h], axis=-1, keepdims=True) + NORM_EPS)
        o_raw = p_ref[:, ML_O + h * dv:ML_O + (h + 1) * dv]
        o_ref[:, vs] = (hn * nw_ref[:, vs] * _sigmoid(o_raw)).astype(o_ref.dtype)


def mlstm_branch(p_pad, conv_w, i_b, f_b, norm_w):
    B, S, C = p_pad.shape
    ifb = jnp.pad(jnp.concatenate([i_b, f_b]), (0, LANES - 2 * MLSTM_HEADS)).reshape(1, LANES)
    full = lambda a: pl.BlockSpec(a.shape, lambda b, c: (0,) * a.ndim)
    nw = norm_w.reshape(1, MLSTM_DIM)
    return pl.pallas_call(
        _mlstm_kernel,
        grid=(B, S // CHUNK),
        in_specs=[pl.BlockSpec((None, CHUNK, C), lambda b, c: (b, c, 0)), full(conv_w), full(ifb), full(nw)],
        out_specs=pl.BlockSpec((None, CHUNK, MLSTM_DIM), lambda b, c: (b, c, 0)),
        out_shape=jax.ShapeDtypeStruct((B, S, MLSTM_DIM), BF16),
        scratch_shapes=[pltpu.VMEM((SUBLANES, ML_V), F32),
                        pltpu.VMEM((MLSTM_HEADS, MLSTM_QK_DIM, MLSTM_V_DIM), F32),
                        pltpu.VMEM((MLSTM_HEADS, SUBLANES, LANES), F32),
                        pltpu.VMEM((MLSTM_HEADS, SUBLANES, LANES), F32)],
        compiler_params=_cparams(("parallel", "arbitrary")),
        name="mlstm_scan",
    )(p_pad, conv_w, ifb, nw)


HI_MASK = 0xFFFF0000


def _pack_bf16_pairs(hb_f32):
    c = hb_f32.shape[1] // 2
    u = pltpu.bitcast(hb_f32, jnp.uint32)
    return u[:, c:] | (u[:, :c] >> 16)


def _unpack_bf16_pairs(xu):
    lo = pltpu.bitcast(xu << 16, F32).astype(BF16)
    hi = pltpu.bitcast(xu & jnp.uint32(HI_MASK), F32).astype(BF16)
    return lo, hi


def _merge_kernel(oa_ref, ob_ref, pg_ref, bg_ref, x_ref, wa_ref, wb_ref, wo_ref, n2_ref, rwh_ref, rwl_ref,
                  rb_ref, x1_ref, xp_ref, lg_ref):
    ya = jnp.dot(oa_ref[...], wa_ref[...], preferred_element_type=F32)
    yb = jnp.dot(ob_ref[...], wb_ref[...], preferred_element_type=F32)
    gates = _sigmoid(pg_ref[...] + bg_ref[...])
    merged = gates[:, :D_MODEL] * ya + gates[:, D_MODEL:] * yb
    x1 = x_ref[...] + jnp.dot(merged.astype(BF16), wo_ref[...], preferred_element_type=F32)
    x1_ref[...] = x1
    hn = x1 * lax.rsqrt(jnp.mean(x1 * x1, axis=-1, keepdims=True) + NORM_EPS) * n2_ref[...]
    hi = hn.astype(BF16)
    xp_ref[...] = _pack_bf16_pairs(hi.astype(F32))
    lo = (hn - hi.astype(F32)).astype(BF16)
    lg_ref[...] = (jnp.dot(hi, rwh_ref[...], preferred_element_type=F32)
                   + jnp.dot(hi, rwl_ref[...], preferred_element_type=F32)
                   + jnp.dot(lo, rwh_ref[...], preferred_element_type=F32) + rb_ref[...])


def merge_project(o_a, o_b, p_gate, b_gate, x, w_a, w_b, w_out, norm2_w, router_w, router_b, tm=256):
    T, D = x.shape
    rw = jnp.pad(router_w, ((0, 0), (0, ROUTER_PAD - N_EXPERTS)))
    rw_hi = rw.astype(BF16)
    rw_lo = (rw - rw_hi.astype(F32)).astype(BF16)
    rb = jnp.pad(router_b, (0, ROUTER_PAD - N_EXPERTS)).reshape(1, ROUTER_PAD)
    rows = lambda n: pl.BlockSpec((tm, n), lambda i: (i, 0))
    full = lambda a: pl.BlockSpec(a.shape, lambda i: (0, 0))
    params = (w_a.astype(BF16), w_b.astype(BF16), w_out.astype(BF16), norm2_w.reshape(1, D), rw_hi, rw_lo, rb)
    bg = b_gate.reshape(1, 2 * D)
    return pl.pallas_call(
        _merge_kernel,
        grid=(T // tm,),
        in_specs=[rows(o_a.shape[1]), rows(o_b.shape[1]), rows(2 * D), full(bg), rows(D)] + [full(a) for a in params],
        out_specs=[rows(D), rows(D // 2), rows(ROUTER_PAD)],
        out_shape=[jax.ShapeDtypeStruct((T, D), F32), jax.ShapeDtypeStruct((T, D // 2), jnp.uint32),
                   jax.ShapeDtypeStruct((T, ROUTER_PAD), F32)],
        compiler_params=_cparams(("parallel",)),
        name="merge_project",
    )(o_a, o_b, p_gate, bg, x, *params)


UP_TILE = 1024
DOWN_TILE = 1024
RANK_GROUP = 256
DISPATCH_TOKENS = 256
COMBINE_TOKENS = 128
DMA_UNROLL = 8


def _row_copy(src, dst, src_row, dst_row, sem):
    return pltpu.make_async_copy(src.at[pl.ds(src_row, 1), :], dst.at[pl.ds(dst_row, 1), :], sem)


def _dispatch_kernel(zrow_ref, dest_ref, xp_ref, xs_hbm, zbuf, zsem, sem):
    i = pl.program_id(0)
    tq = dest_ref.shape[2] // TOP_K

    def zero_copy(k):
        row = pl.multiple_of(zrow_ref[k], ROW_BLOCK)
        return pltpu.make_async_copy(zbuf, xs_hbm.at[pl.ds(row, ROW_BLOCK), :], zsem)

    @pl.when(i == 0)
    def _():
        zbuf[...] = jnp.zeros_like(zbuf)
        for k in range(zrow_ref.shape[0]):
            pl.when(zrow_ref[k] >= 0)(lambda k=k: zero_copy(k).start())
        for k in range(zrow_ref.shape[0]):
            pl.when(zrow_ref[k] >= 0)(lambda k=k: zero_copy(k).wait())

    def body(t, carry):
        for j in range(TOP_K):
            _row_copy(xp_ref, xs_hbm, t, dest_ref[0, 0, TOP_K * t + j], sem).start()
        return carry

    lax.fori_loop(0, tq, body, 0, unroll=DMA_UNROLL // TOP_K)
    for j in range(TOP_K):
        pltpu.make_async_copy(xp_ref, xs_hbm.at[pl.ds(0, tq), :], sem).wait()


def moe_dispatch(xp, dest, zrow, n_rows):
    T, C = xp.shape
    tq = DISPATCH_TOKENS
    dest_blk = dest.reshape(T // tq, 1, TOP_K * tq)
    return pl.pallas_call(
        _dispatch_kernel,
        grid_spec=pltpu.PrefetchScalarGridSpec(
            num_scalar_prefetch=1,
            grid=(T // tq,),
            in_specs=[pl.BlockSpec((1, 1, TOP_K * tq), lambda i, zr: (i, 0, 0), memory_space=pltpu.SMEM),
                      pl.BlockSpec((tq, C), lambda i, zr: (i, 0))],
            out_specs=pl.BlockSpec(memory_space=pl.ANY),
            scratch_shapes=[pltpu.VMEM((ROW_BLOCK, C), xp.dtype), pltpu.SemaphoreType.DMA(()),
                            pltpu.SemaphoreType.DMA(())]),
        out_shape=jax.ShapeDtypeStruct((n_rows, C), xp.dtype),
        compiler_params=_cparams(("arbitrary",)),
        name="moe_dispatch",
    )(zrow, dest_blk, xp)


def _first_of_group(s, *key_refs):
    prev = jnp.maximum(s - 1, 0)
    first = s == 0
    for ref in key_refs:
        first = first | (ref[s] != ref[prev])
    return first


def _for_used_rows(s, nv_ref, full_ref, out_ref, compute):
    valid = s < nv_ref[0]
    full = full_ref[s] > 0

    @pl.when(jnp.logical_not(valid))
    def _():
        out_ref[...] = jnp.zeros_like(out_ref)

    @pl.when(valid & full)
    def _():
        compute(ROW_BLOCK)

    @pl.when(valid & jnp.logical_not(full))
    def _():
        compute(ROW_HALF)
        out_ref[ROW_HALF:, :] = jnp.zeros((ROW_BLOCK - ROW_HALF, out_ref.shape[1]), out_ref.dtype)


def _moe_up_kernel(nv_ref, se_ref, sf_ref, sb_ref, so_ref, full_ref, xs_ref, wg_ref, wu_ref, bg_ref, bu_ref,
                   h_ref, wgb, wub):
    s = pl.program_id(0)

    @pl.when((s < nv_ref[0]) & _first_of_group(s, se_ref, sf_ref))
    def _():
        wgb[...] = wg_ref[...].astype(BF16)
        wub[...] = wu_ref[...].astype(BF16)

    def compute(rows):
        lo, hi = _unpack_bf16_pairs(xs_ref[:rows, :])
        half = lo.shape[1]

        def proj(wb, b_ref):
            return (jnp.dot(lo, wb[:half, :], preferred_element_type=F32)
                    + jnp.dot(hi, wb[half:, :], preferred_element_type=F32) + b_ref[...])

        gate = jnp.minimum(proj(wgb, bg_ref), SWIGLU_LIMIT)
        up = jnp.clip(proj(wub, bu_ref), -SWIGLU_LIMIT, SWIGLU_LIMIT)
        h_ref[:rows, :] = ((up + 1.0) * gate * _sigmoid(SWIGLU_ALPHA * gate)).astype(h_ref.dtype)

    _for_used_rows(s, nv_ref, full_ref, h_ref, compute)


def moe_up(xs, w_gu, b_gu, n_valid, step_e, step_f, step_b, step_o, step_full):
    P = xs.shape[0]
    E, D, F2 = w_gu.shape
    F = F2 // 2
    tf = UP_TILE
    nf = F // tf
    b3 = b_gu.reshape(E, 1, F2)
    return pl.pallas_call(
        _moe_up_kernel,
        grid_spec=pltpu.PrefetchScalarGridSpec(
            num_scalar_prefetch=6,
            grid=(step_e.shape[0],),
            in_specs=[pl.BlockSpec((ROW_BLOCK, D // 2), lambda s, nv, se, sf, sb, so, fl: (sb[s], 0)),
                      pl.BlockSpec((None, D, tf), lambda s, nv, se, sf, sb, so, fl: (se[s], 0, sf[s])),
                      pl.BlockSpec((None, D, tf), lambda s, nv, se, sf, sb, so, fl: (se[s], 0, nf + sf[s])),
                      pl.BlockSpec((None, 1, tf), lambda s, nv, se, sf, sb, so, fl: (se[s], 0, sf[s])),
                      pl.BlockSpec((None, 1, tf), lambda s, nv, se, sf, sb, so, fl: (se[s], 0, nf + sf[s]))],
            out_specs=pl.BlockSpec((ROW_BLOCK, tf), lambda s, nv, se, sf, sb, so, fl: (sb[s], so[s])),
            scratch_shapes=[pltpu.VMEM((D, tf), BF16), pltpu.VMEM((D, tf), BF16)]),
        out_shape=jax.ShapeDtypeStruct((P, F), BF16),
        compiler_params=_cparams(("arbitrary",)),
        name="moe_up",
    )(n_valid, step_e, step_f, step_b, step_o, step_full, xs, w_gu, w_gu, b3, b3)


def _moe_down_kernel(nv_ref, se_ref, sf_ref, sb_ref, so_ref, full_ref, h_ref, wd_ref, bd_ref, y_ref, wdb):
    s = pl.program_id(0)

    @pl.when((s < nv_ref[0]) & _first_of_group(s, se_ref, sf_ref))
    def _():
        wdb[...] = wd_ref[...].astype(BF16)

    def compute(rows):
        y_ref[:rows, :] = jnp.dot(h_ref[:rows, :], wdb[...], preferred_element_type=F32) + bd_ref[...]

    _for_used_rows(s, nv_ref, full_ref, y_ref, compute)


def moe_down(h, w_down, b_down, n_valid, step_e, step_f, step_b, step_o, step_full):
    P, F = h.shape
    E, _, D = w_down.shape
    tn = DOWN_TILE
    b3 = b_down.reshape(E, 1, D)
    return pl.pallas_call(
        _moe_down_kernel,
        grid_spec=pltpu.PrefetchScalarGridSpec(
            num_scalar_prefetch=6,
            grid=(step_e.shape[0],),
            in_specs=[pl.BlockSpec((ROW_BLOCK, F), lambda s, nv, se, sf, sb, so, fl: (sb[s], 0)),
                      pl.BlockSpec((None, F, tn), lambda s, nv, se, sf, sb, so, fl: (se[s], 0, sf[s])),
                      pl.BlockSpec((None, 1, tn), lambda s, nv, se, sf, sb, so, fl: (se[s], 0, sf[s]))],
            out_specs=pl.BlockSpec((ROW_BLOCK, tn), lambda s, nv, se, sf, sb, so, fl: (sb[s], so[s])),
            scratch_shapes=[pltpu.VMEM((F, tn), BF16)]),
        out_shape=jax.ShapeDtypeStruct((P, D), F32),
        compiler_params=_cparams(("arbitrary",)),
        name="moe_down",
    )(n_valid, step_e, step_f, step_b, step_o, step_full, h, w_down, b3)


def _combine_kernel(dcur_ref, dnxt_ref, x1_ref, w_ref, fw_ref, ys_hbm, o_ref, buf, sems):
    i = pl.program_id(0)
    n_steps = pl.num_programs(0)
    tq = x1_ref.shape[0]
    n = TOP_K * tq
    slot = lax.rem(i, 2)

    def issue(idx_ref, sl):
        def body(t, carry):
            _row_copy(ys_hbm, buf.at[sl], idx_ref[0, 0, t], t, sems.at[sl]).start()
            return carry

        lax.fori_loop(0, n, body, 0, unroll=DMA_UNROLL)

    pl.when(i == 0)(lambda: issue(dcur_ref, 0))
    pl.when(i + 1 < n_steps)(lambda: issue(dnxt_ref, 1 - slot))
    pltpu.make_async_copy(ys_hbm.at[pl.ds(0, n), :], buf.at[slot], sems.at[slot]).wait()

    acc = x1_ref[...]
    for j in range(TOP_K):
        acc = acc + w_ref[:, j:j + 1] * buf[slot, j * tq:(j + 1) * tq, :]
    o_ref[...] = acc * lax.rsqrt(jnp.mean(acc * acc, axis=-1, keepdims=True) + NORM_EPS) * fw_ref[...]


def combine(x1, ys, dest, top_w, final_w):
    T, D = x1.shape
    tq = COMBINE_TOKENS
    nblk = T // tq
    dest_blk = dest.reshape(nblk, tq, TOP_K).transpose(0, 2, 1).reshape(nblk, 1, TOP_K * tq)
    idx_spec = lambda f: pl.BlockSpec((1, 1, TOP_K * tq), f, memory_space=pltpu.SMEM)
    return pl.pallas_call(
        _combine_kernel,
        grid=(nblk,),
        in_specs=[idx_spec(lambda i: (i, 0, 0)),
                  idx_spec(lambda i: (jnp.minimum(i + 1, nblk - 1), 0, 0)),
                  pl.BlockSpec((tq, D), lambda i: (i, 0)),
                  pl.BlockSpec((tq, TOP_K), lambda i: (i, 0)),
                  pl.BlockSpec((1, D), lambda i: (0, 0)),
                  pl.BlockSpec(memory_space=pl.ANY)],
        out_specs=pl.BlockSpec((tq, D), lambda i: (i, 0)),
        out_shape=jax.ShapeDtypeStruct((T, D), F32),
        scratch_shapes=[pltpu.VMEM((2, TOP_K * tq, D), F32), pltpu.SemaphoreType.DMA((2,))],
        compiler_params=_cparams(("arbitrary",)),
        name="moe_combine",
    )(dest_blk, dest_blk, x1, top_w, final_w.reshape(1, D), ys)


def _routing(logits):
    T = logits.shape[0]
    TK = T * TOP_K
    NB = TK // ROW_BLOCK + N_EXPERTS
    top_logits, top_idx = lax.top_k(logits[:, :N_EXPERTS], TOP_K)
    top_w = jax.nn.softmax(top_logits, axis=-1)
    flat_e = top_idx.reshape(TK).astype(jnp.int32)
    onehot = (flat_e[:, None] == jnp.arange(N_EXPERTS, dtype=jnp.int32)[None, :]).astype(F32)
    oh = onehot.reshape(TK // RANK_GROUP, RANK_GROUP, N_EXPERTS)
    local = jnp.einsum("ts,gse->gte", jnp.tril(jnp.ones((RANK_GROUP, RANK_GROUP), F32)), oh)
    tot = local[:, -1, :]
    offs = jnp.cumsum(tot, axis=0) - tot
    rank = jnp.sum(oh * (local + offs[:, None, :]), axis=-1).reshape(TK).astype(jnp.int32) - 1
    counts = (offs[-1] + tot[-1]).astype(jnp.int32)
    padded = (counts + ROW_BLOCK - 1) // ROW_BLOCK * ROW_BLOCK
    pad_end = jnp.cumsum(padded).astype(jnp.int32)
    dest = ((pad_end - padded)[flat_e] + rank).astype(jnp.int32)
    block_e = jnp.minimum(
        jnp.searchsorted(pad_end, jnp.arange(NB, dtype=jnp.int32) * ROW_BLOCK, side="right"),
        N_EXPERTS - 1).astype(jnp.int32)
    nb_used = pad_end[-1] // ROW_BLOCK
    blk = jnp.arange(NB, dtype=jnp.int32)
    rows_left = counts[block_e] - (blk - ((pad_end - padded) // ROW_BLOCK)[block_e]) * ROW_BLOCK
    block_full = ((blk < nb_used) & (rows_left > ROW_HALF)).astype(jnp.int32)
    tail = nb_used + jnp.arange(N_EXPERTS, dtype=jnp.int32)
    zrow = jnp.concatenate([jnp.where(padded > 0, pad_end - ROW_BLOCK, -1),
                            jnp.where(tail < NB, tail * ROW_BLOCK, -1)]).astype(jnp.int32)
    return dest.reshape(T, TOP_K), top_w, block_e, block_full, nb_used, zrow


def _steps(block_e, block_full, nb_used, n_tiles):
    nb = block_e.shape[0]
    b = jnp.tile(jnp.arange(nb, dtype=jnp.int32), n_tiles)
    f = jnp.repeat(jnp.arange(n_tiles, dtype=jnp.int32), nb)
    key = jnp.where(b < nb_used, (block_e[b] * n_tiles + f) * nb + b, (N_EXPERTS * n_tiles + f) * nb + b)
    n_valid = (nb_used * n_tiles).astype(jnp.int32)
    order = jnp.argsort(key)
    w_order = order[jnp.minimum(jnp.arange(nb * n_tiles), n_valid - 1)]
    return n_valid.reshape(1), block_e[b][w_order], f[w_order], b[order], f[order], block_full[b][order]


def _repack_w_in(w_in):
    z = lambda n: jnp.zeros((w_in.shape[0], n), w_in.dtype)
    o = RWKV_COLS
    w_r = jnp.concatenate([w_in[:, :RW_WD], w_in[:, RW_WD:RW_WD + DECAY_LORA], z(LORA_PAD - DECAY_LORA),
                           w_in[:, RW_WD + DECAY_LORA:RW_WD + DECAY_LORA + AAA_LORA], z(LORA_PAD - AAA_LORA),
                           w_in[:, RW_WD + DECAY_LORA + AAA_LORA:o]], axis=1)
    w_m = jnp.concatenate([w_in[:, o:o + ML_IF], w_in[:, o + ML_IF:o + ML_IF + 2 * MLSTM_HEADS],
                           z(LANES - 2 * MLSTM_HEADS), w_in[:, o + ML_IF + 2 * MLSTM_HEADS:o + MLSTM_COLS]], axis=1)
    w_g = w_in[:, o + MLSTM_COLS:]
    return w_r.astype(BF16), w_m.astype(BF16), w_g.astype(BF16)


def kernel(x, norm1_w, w_in, b_gate, rwkv_mu, rwkv_w0, rwkv_w_up, rwkv_a0, rwkv_a_up, rwkv_g_up, rwkv_k_k,
           rwkv_k_a, rwkv_r_k, rwkv_ln_w, rwkv_ln_b, mlstm_conv_w, mlstm_i_b, mlstm_f_b, mlstm_norm_w,
           w_branch_a, w_branch_b, w_out, norm2_w, router_w, router_b, w_gu, b_gu, w_down, b_down,
           final_norm_w):
    B, S, D = x.shape
    T = B * S
    xt = x.reshape(T, D)
    assert norm1_w.shape[0] == 1, "single-layer block: the final rmsnorm is fused into the MoE combine"
    for l in range(1):
        hn = rmsnorm_rows(xt, norm1_w[l])
        w_r, w_m, w_g = _repack_w_in(w_in[l])
        p_r = matmul(hn, w_r, 512, RW_COLS_P // 2, F32, "proj_rwkv").reshape(B, S, RW_COLS_P)
        p_m = matmul(hn, w_m, 256, ML_COLS_P, F32, "proj_mlstm").reshape(B, S, ML_COLS_P)
        p_g = matmul(hn, w_g, 512, D, F32, "proj_gate")
        o_a = rwkv_branch(p_r, rwkv_mu[l], rwkv_w0[l], rwkv_w_up[l], rwkv_a0[l], rwkv_a_up[l], rwkv_g_up[l],
                          rwkv_k_k[l], rwkv_k_a[l], rwkv_r_k[l].reshape(-1), rwkv_ln_w[l], rwkv_ln_b[l])
        o_b = mlstm_branch(p_m, mlstm_conv_w[l], mlstm_i_b[l], mlstm_f_b[l], mlstm_norm_w[l])
        x1, xp, logits = merge_project(o_a.reshape(T, RWKV_DIM), o_b.reshape(T, MLSTM_DIM), p_g, b_gate[l], xt,
                                       w_branch_a[l], w_branch_b[l], w_out[l], norm2_w[l], router_w[l],
                                       router_b[l])
        dest, top_w, block_e, block_full, nb_used, zrow = _routing(logits)
        xs = moe_dispatch(xp, dest, zrow, block_e.shape[0] * ROW_BLOCK)
        assert EXPERT_FF // UP_TILE == D // DOWN_TILE, "up and down share one step order"
        steps = _steps(block_e, block_full, nb_used, EXPERT_FF // UP_TILE)
        h = moe_up(xs, w_gu[l], b_gu[l], *steps)
        ys = moe_down(h, w_down[l], b_down[l], *steps)
        xt = combine(x1, ys, dest, top_w, final_norm_w)
    return xt.reshape(B, S, D)
```

```python
import functools

import jax
import jax.numpy as jnp
import numpy as np
from jax import lax
from jax.experimental import pallas as pl
from jax.experimental.pallas import tpu as pltpu

F32 = jnp.float32
BF16 = jnp.bfloat16

D_MODEL = 2048
CHUNK = 64
NORM_EPS = 1e-6
RWKV_HEADS = 16
RWKV_HEAD_DIM = 64
RWKV_DIM = 1024
DECAY_LORA = 96
AAA_LORA = 96
GATE_LORA = 256
GN_EPS = 64e-5
RWKV_COLS = 3 * RWKV_DIM + DECAY_LORA + AAA_LORA + GATE_LORA
MLSTM_HEADS = 4
MLSTM_QK_DIM = 128
MLSTM_V_DIM = 256
MLSTM_QK = 512
MLSTM_DIM = 1024
CONV_WIDTH = 4
GATE_SOFTCAP = 15.0
MLSTM_COLS = 2 * MLSTM_QK + 2 * MLSTM_DIM + 2 * MLSTM_HEADS
N_EXPERTS = 32
TOP_K = 4
EXPERT_FF = 2048
SWIGLU_LIMIT = 7.0
SWIGLU_ALPHA = 1.702

LANES = 128
SUBLANES = 8
VMEM_LIMIT = 56 * 1024 * 1024

LORA_PAD = 128
RW_WD = 3 * RWKV_DIM
RW_AD = RW_WD + LORA_PAD
RW_GD = RW_AD + LORA_PAD
RW_COLS_P = RW_GD + GATE_LORA
ML_V = 2 * MLSTM_QK
ML_IF = ML_V + MLSTM_DIM
ML_O = ML_IF + LANES
ML_COLS_P = ML_O + MLSTM_DIM
ROUTER_PAD = 128

ROW_BLOCK = 512
ROW_HALF = ROW_BLOCK // 2


def _cparams(sem):
    return pltpu.CompilerParams(dimension_semantics=sem, vmem_limit_bytes=VMEM_LIMIT)


def _bdot(a, b):
    return jnp.dot(a.astype(BF16), b.astype(BF16), preferred_element_type=F32)


def _split3(x):
    hi = x.astype(BF16)
    r1 = x - hi.astype(F32)
    mid = r1.astype(BF16)
    lo = (r1 - mid.astype(F32)).astype(BF16)
    return hi, mid, lo


def _dot_exact_lhs(mat_bf16, x):
    hi, mid, lo = _split3(x)
    return (jnp.dot(mat_bf16, hi, preferred_element_type=F32)
            + jnp.dot(mat_bf16, mid, preferred_element_type=F32)
            + jnp.dot(mat_bf16, lo, preferred_element_type=F32))


def _dot_exact_rhs(x, mat_bf16):
    hi, mid, lo = _split3(x)
    return (jnp.dot(hi, mat_bf16, preferred_element_type=F32)
            + jnp.dot(mid, mat_bf16, preferred_element_type=F32)
            + jnp.dot(lo, mat_bf16, preferred_element_type=F32))


def _sigmoid(x):
    return 1.0 / (1.0 + jnp.exp(-x))


def _softplus(x):
    return jnp.maximum(x, 0.0) + jnp.log(1.0 + jnp.exp(-jnp.abs(x)))


def _rmsnorm_kernel(x_ref, w_ref, o_ref):
    x = x_ref[...]
    y = x * lax.rsqrt(jnp.mean(x * x, axis=-1, keepdims=True) + NORM_EPS)
    o_ref[...] = (y * w_ref[...]).astype(o_ref.dtype)


def rmsnorm_rows(x, w, tm=512):
    T, D = x.shape
    return pl.pallas_call(
        _rmsnorm_kernel,
        grid=(T // tm,),
        in_specs=[pl.BlockSpec((tm, D), lambda i: (i, 0)),
                  pl.BlockSpec((1, D), lambda i: (0, 0))],
        out_specs=pl.BlockSpec((tm, D), lambda i: (i, 0)),
        out_shape=jax.ShapeDtypeStruct((T, D), BF16),
        compiler_params=_cparams(("parallel",)),
        name="rmsnorm1",
    )(x, w.reshape(1, D))


def _mm_kernel(a_ref, b_ref, o_ref):
    o_ref[...] = jnp.dot(a_ref[...], b_ref[...], preferred_element_type=F32).astype(o_ref.dtype)


def matmul(a, b, tm, tn, out_dtype, name):
    M, K = a.shape
    N = b.shape[1]
    return pl.pallas_call(
        _mm_kernel,
        grid=(N // tn, M // tm),
        in_specs=[pl.BlockSpec((tm, K), lambda j, i: (i, 0)),
                  pl.BlockSpec((K, tn), lambda j, i: (0, j))],
        out_specs=pl.BlockSpec((tm, tn), lambda j, i: (i, j)),
        out_shape=jax.ShapeDtypeStruct((M, N), out_dtype),
        compiler_params=_cparams(("parallel", "parallel")),
        name=name,
    )(a, b)


def _head_sum_mat():
    r = lax.broadcasted_iota(jnp.int32, (LANES, LANES), 0) // RWKV_HEAD_DIM
    c = lax.broadcasted_iota(jnp.int32, (LANES, LANES), 1) // RWKV_HEAD_DIM
    return jnp.where(r == c, 1.0, 0.0).astype(BF16)


def _rwkv_prep_kernel(p_ref, mu_ref, w0_ref, wup_ref, a0_ref, aup_ref, gup_ref, kk_ref, ka_ref,
                      r_out, k_out, v_out, kk_out, b_out, lw_out, g_out, carry_ref):
    i = pl.program_id(1)

    @pl.when(i == 0)
    def _():
        carry_ref[...] = jnp.zeros_like(carry_ref)

    p = p_ref[...]
    tq = p.shape[0]
    row = lax.broadcasted_iota(jnp.int32, (tq, 1), 0)
    prev = jnp.where(row == 0, carry_ref[0:1, :], pltpu.roll(p, 1, 0))
    carry_ref[0:1, :] = p[tq - 1:tq, :]
    p = p + (prev - p) * mu_ref[...]

    r = p[:, 0:RWKV_DIM]
    k = p[:, RWKV_DIM:2 * RWKV_DIM]
    v = p[:, 2 * RWKV_DIM:3 * RWKV_DIM]
    wd = p[:, RW_WD:RW_AD]
    ad = p[:, RW_AD:RW_GD]
    gd = p[:, RW_GD:RW_COLS_P]

    w_log = -_softplus(-(w0_ref[...] + _bdot(jnp.tanh(wd), wup_ref[...]))) - 0.5
    lw = -jnp.exp(w_log)
    a = _sigmoid(a0_ref[...] + _bdot(ad, aup_ref[...]))
    g = _bdot(_sigmoid(gd), gup_ref[...])

    kk = k * kk_ref[...]
    hs = _head_sum_mat()
    nrm2 = jnp.concatenate(
        [_dot_exact_rhs(kk[:, c:c + LANES] * kk[:, c:c + LANES], hs) for c in range(0, RWKV_DIM, LANES)],
        axis=1)
    kk = kk / jnp.maximum(jnp.sqrt(nrm2), 1e-12)
    k = k * (1.0 + (a - 1.0) * ka_ref[...])

    r_out[...] = r
    k_out[...] = k
    v_out[...] = v
    kk_out[...] = kk
    b_out[...] = kk * a
    lw_out[...] = lw
    g_out[...] = g


def rwkv_prep(p_rwkv, mu, w0, w_up, a0, a_up, g_up, k_k, k_a, tq=256):
    B, S, C = p_rwkv.shape
    blk = lambda n: pl.BlockSpec((None, tq, n), lambda b, i: (b, i, 0))
    full = lambda a: pl.BlockSpec(a.shape, lambda b, i: (0,) * a.ndim)
    params = (mu, w0, w_up, a0, a_up, g_up, k_k, k_a)
    out = jax.ShapeDtypeStruct((B, S, RWKV_DIM), F32)
    return pl.pallas_call(
        _rwkv_prep_kernel,
        grid=(B, S // tq),
        in_specs=[blk(C)] + [full(a) for a in params],
        out_specs=[blk(RWKV_DIM)] * 7,
        out_shape=[out] * 7,
        scratch_shapes=[pltpu.VMEM((SUBLANES, C), F32)],
        compiler_params=_cparams(("parallel", "arbitrary")),
        name="rwkv_prep",
    )(p_rwkv, *params)


def _rwkv_scan_kernel(r_ref, k_ref, v_ref, kk_ref, b_ref, lw_ref, g_ref, rk_ref, lnw_ref, lnb_ref,
                      o_ref, h_ref):
    c = pl.program_id(1)

    @pl.when(c == 0)
    def _():
        h_ref[...] = jnp.zeros_like(h_ref)

    L = CHUNK
    L2 = 2 * L
    ri = lax.broadcasted_iota(jnp.int32, (L, L), 0)
    ci = lax.broadcasted_iota(jnp.int32, (L, L), 1)
    tril = jnp.where(ri >= ci, 1.0, 0.0).astype(BF16)

    lw = lw_ref[...]
    cum = _dot_exact_lhs(tril, lw)
    cum_end = cum[L - 1:L, :]
    w_in = jnp.exp(cum)
    w_prev = jnp.exp(cum - lw)
    w_inv = jnp.exp(-cum)
    w_tail = jnp.exp(cum_end - cum)
    w_end = jnp.exp(cum_end)

    kk = kk_ref[...]
    bb = b_ref[...]
    kx = k_ref[...]
    rx = r_ref[...]
    vx = v_ref[...]
    a_hat = -kk * w_prev
    r_hat = rx * w_in
    b_hat = bb * w_inv
    k_hat = kx * w_inv
    b_til = bb * w_tail
    k_til = kx * w_tail
    rkk = rx * kx * rk_ref[...]

    lane = lax.broadcasted_iota(jnp.int32, (1, LANES), 1)
    m_lo = jnp.where(lane < RWKV_HEAD_DIM, 1.0, 0.0)
    m_hi = 1.0 - m_lo

    def stack(x):
        return jnp.concatenate([x * m_lo, x * m_hi], axis=0)

    r2 = lax.broadcasted_iota(jnp.int32, (L2, L2), 0)
    c2 = lax.broadcasted_iota(jnp.int32, (L2, L2), 1)
    same_head = (r2 // L) == (c2 // L)
    strict = same_head & (r2 > c2)
    incl = same_head & (r2 >= c2)
    diag16 = (r2 // 16) == (c2 // 16)
    eye = jnp.where(r2 == c2, 1.0, 0.0)
    hs = _head_sum_mat()

    pairs = range(RWKV_DIM // LANES)
    sls = [slice(p * LANES, (p + 1) * LANES) for p in pairs]
    v_st = [stack(vx[:, sl]) for sl in sls]
    lhs = [jnp.concatenate([stack(a_hat[:, sl]), stack(r_hat[:, sl])], axis=0).astype(BF16) for sl in sls]
    rhs = [jnp.concatenate([stack(b_hat[:, sl]), stack(k_hat[:, sl])], axis=0).astype(BF16) for sl in sls]
    sc = [lax.dot_general(lhs[p], rhs[p], (((1,), (1,)), ((), ())), preferred_element_type=F32) for p in pairs]
    n_ab = [jnp.where(strict, sc[p][:L2, :L2], 0.0) for p in pairs]
    a_ak = [jnp.where(strict, sc[p][:L2, L2:], 0.0).astype(BF16) for p in pairs]
    a_r = [jnp.concatenate([jnp.where(incl, sc[p][L2:, :L2], 0.0), jnp.where(incl, sc[p][L2:, L2:], 0.0)],
                           axis=1).astype(BF16) for p in pairs]

    nd = [jnp.where(diag16, n_ab[p], 0.0) for p in pairs]
    noff = [(n_ab[p] - nd[p]).astype(BF16) for p in pairs]
    ndb = [nd[p].astype(BF16) for p in pairs]
    s2 = [jnp.dot(ndb[p], ndb[p], preferred_element_type=F32).astype(BF16) for p in pairs]
    s4 = [jnp.dot(s2[p], s2[p], preferred_element_type=F32).astype(BF16) for p in pairs]
    s8 = [jnp.dot(s4[p], s4[p], preferred_element_type=F32).astype(BF16) for p in pairs]
    x1 = [eye + nd[p] for p in pairs]
    x2 = [x1[p] + _bdot(x1[p], s2[p]) for p in pairs]
    x3 = [x2[p] + _bdot(x2[p], s4[p]) for p in pairs]
    t_d = [(x3[p] + _bdot(x3[p], s8[p])).astype(BF16) for p in pairs]
    m1 = [jnp.dot(t_d[p], noff[p], preferred_element_type=F32) for p in pairs]
    m1b = [m1[p].astype(BF16) for p in pairs]
    m2 = [jnp.dot(m1b[p], m1b[p], preferred_element_type=F32) for p in pairs]
    m3 = [jnp.dot(m1b[p], m2[p].astype(BF16), preferred_element_type=F32) for p in pairs]
    t_inv = [jnp.dot((eye + m1[p] + m2[p] + m3[p]).astype(BF16), t_d[p], preferred_element_type=F32).astype(BF16)
             for p in pairs]

    h0 = [h_ref[p] for p in pairs]
    ah = [jnp.dot(lhs[p], h0[p].astype(BF16), preferred_element_type=F32) for p in pairs]
    x = [ah[p][:L2] + jnp.dot(a_ak[p], v_st[p].astype(BF16), preferred_element_type=F32) for p in pairs]
    u = [jnp.dot(t_inv[p], x[p].astype(BF16), preferred_element_type=F32) for p in pairs]
    uv = [jnp.concatenate([u[p], v_st[p]], axis=0).astype(BF16) for p in pairs]
    y_st = [ah[p][L2:] + jnp.dot(a_r[p], uv[p], preferred_element_type=F32) for p in pairs]
    y = [y_st[p][:L] + y_st[p][L:] for p in pairs]

    for p in pairs:
        sl = sls[p]
        upd_l = jnp.concatenate([stack(b_til[:, sl]), stack(k_til[:, sl])], axis=0).astype(BF16)
        upd = lax.dot_general(upd_l, uv[p], (((0,), (0,)), ((), ())), preferred_element_type=F32)
        w_col = jnp.sum(eye * w_end[:, sl], axis=1, keepdims=True)
        h_ref[p] = w_col * h0[p] + upd

    def head_sums(vals):
        parts = []
        for t in vals:
            hi = t.astype(BF16)
            parts += [hi, (t - hi.astype(F32)).astype(BF16)]
        res = jnp.dot(jnp.concatenate(parts, axis=0), hs, preferred_element_type=F32)
        return [res[2 * i * L:(2 * i + 1) * L] + res[(2 * i + 1) * L:(2 * i + 2) * L] for i in range(len(vals))]

    sums1 = [head_sums([y[p], rkk[:, sls[p]]]) for p in pairs]
    d = [y[p] - sums1[p][0] * (1.0 / RWKV_HEAD_DIM) for p in pairs]
    var = [head_sums([d[p] * d[p]])[0] * (1.0 / RWKV_HEAD_DIM) for p in pairs]
    for p in pairs:
        sl = sls[p]
        yn = d[p] * lax.rsqrt(var[p] + GN_EPS) * lnw_ref[:, sl] + lnb_ref[:, sl]
        o_ref[:, sl] = ((yn + sums1[p][1] * vx[:, sl]) * g_ref[:, sl]).astype(o_ref.dtype)


def rwkv_scan(r, k, v, kk, b, lw, g, r_k, ln_w, ln_b):
    B, S, C = r.shape
    blk = pl.BlockSpec((None, CHUNK, C), lambda bb, c: (bb, c, 0))
    full = pl.BlockSpec((1, C), lambda bb, c: (0, 0))
    return pl.pallas_call(
        _rwkv_scan_kernel,
        grid=(B, S // CHUNK),
        in_specs=[blk] * 7 + [full] * 3,
        out_specs=blk,
        out_shape=jax.ShapeDtypeStruct((B, S, C), BF16),
        scratch_shapes=[pltpu.VMEM((C // LANES, LANES, LANES), F32)],
        compiler_params=_cparams(("parallel", "arbitrary")),
        name="rwkv_scan",
    )(r, k, v, kk, b, lw, g, r_k, ln_w, ln_b)


def _pad_rows(w, n):
    return jnp.pad(w, ((0, n - w.shape[0]), (0, 0)))


def rwkv_branch(p_pad, mu, w0, w_up, a0, a_up, g_up, k_k, k_a, r_k, ln_w, ln_b):
    row = lambda t: t.reshape(1, -1)
    mu_p = jnp.concatenate([mu[:RW_WD], jnp.pad(mu[RW_WD:RW_WD + DECAY_LORA], (0, LORA_PAD - DECAY_LORA)),
                            jnp.pad(mu[RW_WD + DECAY_LORA:RW_WD + DECAY_LORA + AAA_LORA], (0, LORA_PAD - AAA_LORA)),
                            mu[RW_WD + DECAY_LORA + AAA_LORA:]])
    outs = rwkv_prep(p_pad, row(mu_p), row(w0), _pad_rows(w_up, LORA_PAD).astype(BF16), row(a0),
                     _pad_rows(a_up, LORA_PAD).astype(BF16), g_up.astype(BF16), row(k_k), row(k_a))
    return rwkv_scan(*outs, row(r_k), row(ln_w), row(ln_b))


def _mlstm_kernel(p_ref, convw_ref, ifb_ref, nw_ref, o_ref, carry_ref, c_ref, n_ref, m_ref):
    ci = pl.program_id(1)

    @pl.when(ci == 0)
    def _():
        carry_ref[...] = jnp.zeros_like(carry_ref)
        c_ref[...] = jnp.zeros_like(c_ref)
        n_ref[...] = jnp.zeros_like(n_ref)
        m_ref[...] = jnp.zeros_like(m_ref)

    L = CHUNK
    u = p_ref[:, 0:ML_V]
    ext = jnp.concatenate([carry_ref[...], u], axis=0)
    carry_ref[...] = u[L - SUBLANES:L, :]
    conv = convw_ref[CONV_WIDTH - 1:CONV_WIDTH, :] * u
    for j in range(1, CONV_WIDTH):
        conv = conv + convw_ref[CONV_WIDTH - 1 - j:CONV_WIDTH - j, :] * pltpu.roll(ext, j, 0)[SUBLANES:, :]
    qk = conv * _sigmoid(conv)

    pre = GATE_SOFTCAP * jnp.tanh((p_ref[:, ML_IF:ML_O] + ifb_ref[...]) * (1.0 / GATE_SOFTCAP))
    lane = lax.broadcasted_iota(jnp.int32, (1, LANES), 1)
    comb = jnp.where(lane < MLSTM_HEADS, pre, -_softplus(-pre))
    comb_t = comb.T
    ri = lax.broadcasted_iota(jnp.int32, (L, L), 0)
    cj = lax.broadcasted_iota(jnp.int32, (L, L), 1)
    causal = ri >= cj
    tril = jnp.where(causal, 1.0, 0.0).astype(BF16)
    triu = jnp.where(ri <= cj, 1.0, 0.0).astype(BF16)
    b_col = _dot_exact_lhs(tril, comb)
    b_row = _dot_exact_rhs(comb_t, triu)

    H = range(MLSTM_HEADS)
    dk, dv = MLSTM_QK_DIM, MLSTM_V_DIM
    qh = [qk[:, h * dk:(h + 1) * dk] * (dk ** -0.5) for h in H]
    kh = [qk[:, MLSTM_QK + h * dk:MLSTM_QK + (h + 1) * dk] for h in H]
    vh = [p_ref[:, ML_V + h * dv:ML_V + (h + 1) * dv].astype(BF16) for h in H]
    qb = [qh[h].astype(BF16) for h in H]
    bcol = [b_col[:, MLSTM_HEADS + h:MLSTM_HEADS + h + 1] for h in H]
    brow = [b_row[MLSTM_HEADS + h:MLSTM_HEADS + h + 1, :] for h in H]
    m_prev = [m_ref[h][0:1, 0:1] for h in H]
    n_prev = [n_ref[h][0:1, :] for h in H]
    c_prev = [c_ref[h] for h in H]

    qk_t = [lax.dot_general(qb[h], kh[h].astype(BF16), (((1,), (1,)), ((), ())), preferred_element_type=F32)
            for h in H]
    qc = [jnp.dot(qb[h], c_prev[h].astype(BF16), preferred_element_type=F32) for h in H]
    dm = [jnp.where(causal, bcol[h] - brow[h] + comb_t[h:h + 1, :], -jnp.inf) for h in H]
    inter = [bcol[h] + m_prev[h] for h in H]
    m_t = [jnp.maximum(inter[h], jnp.max(dm[h], axis=-1, keepdims=True)) for h in H]
    s = [qk_t[h] * jnp.exp(dm[h] - m_t[h]) for h in H]
    w_inter = [jnp.exp(inter[h] - m_t[h]) for h in H]
    num = [jnp.dot(s[h].astype(BF16), vh[h], preferred_element_type=F32) + w_inter[h] * qc[h] for h in H]
    den = [jnp.sum(s[h], axis=-1, keepdims=True) + w_inter[h] * jnp.sum(qh[h] * n_prev[h], axis=-1, keepdims=True)
           for h in H]
    hh = [num[h] / jnp.maximum(jnp.abs(den[h]), jnp.exp(-m_t[h])) for h in H]

    g_tot = [bcol[h][L - 1:L, :] for h in H]
    a = [comb[:, h:h + 1] + g_tot[h] - bcol[h] for h in H]
    m_new = [jnp.maximum(g_tot[h] + m_prev[h], jnp.max(a[h], axis=0, keepdims=True)) for h in H]
    dec = [jnp.exp(g_tot[h] + m_prev[h] - m_new[h]) for h in H]
    wkk = [jnp.exp(a[h] - m_new[h]) * kh[h] for h in H]
    for h in H:
        c_ref[h] = dec[h] * c_prev[h] + lax.dot_general(wkk[h].astype(BF16), vh[h], (((0,), (0,)), ((), ())),
                                                        preferred_element_type=F32)
        n_ref[h] = jnp.broadcast_to(dec[h] * n_prev[h] + jnp.sum(wkk[h], axis=0, keepdims=True),
                                    (SUBLANES, LANES))
        m_ref[h] = jnp.broadcast_to(m_new[h], (SUBLANES, LANES))
    for h in H:
        vs = slice(h * dv, (h + 1) * dv)
        hn = hh[h] * lax.rsqrt(jnp.mean(hh[h] * hh[h], axis=-1, keepdims=True) + NORM_EPS)
        o_raw = p_ref[:, ML_O + h * dv:ML_O + (h + 1) * dv]
        o_ref[:, vs] = (hn * nw_ref[:, vs] * _sigmoid(o_raw)).astype(o_ref.dtype)


def mlstm_branch(p_pad, conv_w, i_b, f_b, norm_w):
    B, S, C = p_pad.shape
    ifb = jnp.pad(jnp.concatenate([i_b, f_b]), (0, LANES - 2 * MLSTM_HEADS)).reshape(1, LANES)
    full = lambda a: pl.BlockSpec(a.shape, lambda b, c: (0,) * a.ndim)
    nw = norm_w.reshape(1, MLSTM_DIM)
    return pl.pallas_call(
        _mlstm_kernel,
        grid=(B, S // CHUNK),
        in_specs=[pl.BlockSpec((None, CHUNK, C), lambda b, c: (b, c, 0)), full(conv_w), full(ifb), full(nw)],
        out_specs=pl.BlockSpec((None, CHUNK, MLSTM_DIM), lambda b, c: (b, c, 0)),
        out_shape=jax.ShapeDtypeStruct((B, S, MLSTM_DIM), BF16),
        scratch_shapes=[pltpu.VMEM((SUBLANES, ML_V), F32),
                        pltpu.VMEM((MLSTM_HEADS, MLSTM_QK_DIM, MLSTM_V_DIM), F32),
                        pltpu.VMEM((MLSTM_HEADS, SUBLANES, LANES), F32),
                        pltpu.VMEM((MLSTM_HEADS, SUBLANES, LANES), F32)],
        compiler_params=_cparams(("parallel", "arbitrary")),
        name="mlstm_scan",
    )(p_pad, conv_w, ifb, nw)


HI_MASK = 0xFFFF0000


def _pack_bf16_pairs(hb_f32):
    c = hb_f32.shape[1] // 2
    u = pltpu.bitcast(hb_f32, jnp.uint32)
    return u[:, c:] | (u[:, :c] >> 16)


def _unpack_bf16_pairs(xu):
    lo = pltpu.bitcast(xu << 16, F32).astype(BF16)
    hi = pltpu.bitcast(xu & jnp.uint32(HI_MASK), F32).astype(BF16)
    return lo, hi


def _merge_kernel(oa_ref, ob_ref, pg_ref, bg_ref, x_ref, wa_ref, wb_ref, wo_ref, n2_ref, rwh_ref, rwl_ref,
                  rb_ref, x1_ref, xp_ref, lg_ref):
    ya = jnp.dot(oa_ref[...], wa_ref[...], preferred_element_type=F32)
    yb = jnp.dot(ob_ref[...], wb_ref[...], preferred_element_type=F32)
    gates = _sigmoid(pg_ref[...] + bg_ref[...])
    merged = gates[:, :D_MODEL] * ya + gates[:, D_MODEL:] * yb
    x1 = x_ref[...] + jnp.dot(merged.astype(BF16), wo_ref[...], preferred_element_type=F32)
    x1_ref[...] = x1
    hn = x1 * lax.rsqrt(jnp.mean(x1 * x1, axis=-1, keepdims=True) + NORM_EPS) * n2_ref[...]
    hi = hn.astype(BF16)
    xp_ref[...] = _pack_bf16_pairs(hi.astype(F32))
    lo = (hn - hi.astype(F32)).astype(BF16)
    lg_ref[...] = (jnp.dot(hi, rwh_ref[...], preferred_element_type=F32)
                   + jnp.dot(hi, rwl_ref[...], preferred_element_type=F32)
                   + jnp.dot(lo, rwh_ref[...], preferred_element_type=F32) + rb_ref[...])


def merge_project(o_a, o_b, p_gate, b_gate, x, w_a, w_b, w_out, norm2_w, router_w, router_b, tm=256):
    T, D = x.shape
    rw = jnp.pad(router_w, ((0, 0), (0, ROUTER_PAD - N_EXPERTS)))
    rw_hi = rw.astype(BF16)
    rw_lo = (rw - rw_hi.astype(F32)).astype(BF16)
    rb = jnp.pad(router_b, (0, ROUTER_PAD - N_EXPERTS)).reshape(1, ROUTER_PAD)
    rows = lambda n: pl.BlockSpec((tm, n), lambda i: (i, 0))
    full = lambda a: pl.BlockSpec(a.shape, lambda i: (0, 0))
    params = (w_a.astype(BF16), w_b.astype(BF16), w_out.astype(BF16), norm2_w.reshape(1, D), rw_hi, rw_lo, rb)
    bg = b_gate.reshape(1, 2 * D)
    return pl.pallas_call(
        _merge_kernel,
        grid=(T // tm,),
        in_specs=[rows(o_a.shape[1]), rows(o_b.shape[1]), rows(2 * D), full(bg), rows(D)] + [full(a) for a in params],
        out_specs=[rows(D), rows(D // 2), rows(ROUTER_PAD)],
        out_shape=[jax.ShapeDtypeStruct((T, D), F32), jax.ShapeDtypeStruct((T, D // 2), jnp.uint32),
                   jax.ShapeDtypeStruct((T, ROUTER_PAD), F32)],
        compiler_params=_cparams(("parallel",)),
        name="merge_project",
    )(o_a, o_b, p_gate, bg, x, *params)


UP_TILE = 1024
DOWN_TILE = 1024
RANK_GROUP = 256
DISPATCH_TOKENS = 256
COMBINE_TOKENS = 128
DMA_UNROLL = 8


def _row_copy(src, dst, src_row, dst_row, sem):
    return pltpu.make_async_copy(src.at[pl.ds(src_row, 1), :], dst.at[pl.ds(dst_row, 1), :], sem)


def _dispatch_kernel(zrow_ref, dest_ref, xp_ref, xs_hbm, zbuf, zsem, sem):
    i = pl.program_id(0)
    tq = dest_ref.shape[2] // TOP_K

    def zero_copy(k):
        row = pl.multiple_of(zrow_ref[k], ROW_BLOCK)
        return pltpu.make_async_copy(zbuf, xs_hbm.at[pl.ds(row, ROW_BLOCK), :], zsem)

    @pl.when(i == 0)
    def _():
        zbuf[...] = jnp.zeros_like(zbuf)
        for k in range(zrow_ref.shape[0]):
            pl.when(zrow_ref[k] >= 0)(lambda k=k: zero_copy(k).start())
        for k in range(zrow_ref.shape[0]):
            pl.when(zrow_ref[k] >= 0)(lambda k=k: zero_copy(k).wait())

    def body(t, carry):
        for j in range(TOP_K):
            _row_copy(xp_ref, xs_hbm, t, dest_ref[0, 0, TOP_K * t + j], sem).start()
        return carry

    lax.fori_loop(0, tq, body, 0, unroll=DMA_UNROLL // TOP_K)
    for j in range(TOP_K):
        pltpu.make_async_copy(xp_ref, xs_hbm.at[pl.ds(0, tq), :], sem).wait()


def moe_dispatch(xp, dest, zrow, n_rows):
    T, C = xp.shape
    tq = DISPATCH_TOKENS
    dest_blk = dest.reshape(T // tq, 1, TOP_K * tq)
    return pl.pallas_call(
        _dispatch_kernel,
        grid_spec=pltpu.PrefetchScalarGridSpec(
            num_scalar_prefetch=1,
            grid=(T // tq,),
            in_specs=[pl.BlockSpec((1, 1, TOP_K * tq), lambda i, zr: (i, 0, 0), memory_space=pltpu.SMEM),
                      pl.BlockSpec((tq, C), lambda i, zr: (i, 0))],
            out_specs=pl.BlockSpec(memory_space=pl.ANY),
            scratch_shapes=[pltpu.VMEM((ROW_BLOCK, C), xp.dtype), pltpu.SemaphoreType.DMA(()),
                            pltpu.SemaphoreType.DMA(())]),
        out_shape=jax.ShapeDtypeStruct((n_rows, C), xp.dtype),
        compiler_params=_cparams(("arbitrary",)),
        name="moe_dispatch",
    )(zrow, dest_blk, xp)


SCHED_NV, SCHED_SG, SCHED_SB, SCHED_SO, SCHED_FULL, SCHED_GE, SCHED_GF, SCHED_NG = range(8)


def _stream_weights(s, sched, copies, on_arrival):
    sg_ref = sched[SCHED_SG]
    g = sg_ref[s]
    first = (s < sched[SCHED_NV][0]) & ((s == 0) | (g != sg_ref[jnp.maximum(s - 1, 0)]))

    @pl.when(first)
    def _():
        @pl.when(s == 0)
        def _():
            for c in copies(g):
                c.start()

        for c in copies(g):
            c.wait()
        on_arrival()

        @pl.when(g + 1 < sched[SCHED_NG][0])
        def _():
            for c in copies(g + 1):
                c.start()


def _for_used_rows(s, sched, out_ref, compute):
    valid = s < sched[SCHED_NV][0]
    full = sched[SCHED_FULL][s] > 0

    @pl.when(jnp.logical_not(valid))
    def _():
        out_ref[...] = jnp.zeros_like(out_ref)

    @pl.when(valid & full)
    def _():
        compute(ROW_BLOCK)

    @pl.when(valid & jnp.logical_not(full))
    def _():
        compute(ROW_HALF)
        out_ref[ROW_HALF:, :] = jnp.zeros((ROW_BLOCK - ROW_HALF, out_ref.shape[1]), out_ref.dtype)


def _moe_up_kernel(*refs):
    sched = refs[:8]
    xs_ref, w_hbm, bg_ref, bu_ref, h_ref, wbuf, wgb, wub, sems = refs[8:]
    s = pl.program_id(0)
    tf = wgb.shape[1]
    n_ff = w_hbm.shape[2] // 2

    def copies(g):
        e = sched[SCHED_GE][g]
        col = pl.multiple_of(sched[SCHED_GF][g] * tf, tf)
        return [pltpu.make_async_copy(w_hbm.at[e, :, pl.ds(half * n_ff + col, tf)], wbuf.at[half], sems.at[half])
                for half in range(2)]

    def on_arrival():
        wgb[...] = wbuf[0].astype(BF16)
        wub[...] = wbuf[1].astype(BF16)

    _stream_weights(s, sched, copies, on_arrival)

    def compute(rows):
        lo, hi = _unpack_bf16_pairs(xs_ref[:rows, :])
        half = lo.shape[1]

        def proj(wb, b_ref):
            return (jnp.dot(lo, wb[:half, :], preferred_element_type=F32)
                    + jnp.dot(hi, wb[half:, :], preferred_element_type=F32) + b_ref[...])

        gate = jnp.minimum(proj(wgb, bg_ref), SWIGLU_LIMIT)
        up = jnp.clip(proj(wub, bu_ref), -SWIGLU_LIMIT, SWIGLU_LIMIT)
        h_ref[:rows, :] = ((up + 1.0) * gate * _sigmoid(SWIGLU_ALPHA * gate)).astype(h_ref.dtype)

    _for_used_rows(s, sched, h_ref, compute)


def _step_expert(s, r):
    return r[SCHED_GE][r[SCHED_SG][s]]


def _step_tile(s, r):
    return r[SCHED_GF][r[SCHED_SG][s]]


def moe_up(xs, w_gu, b_gu, sched):
    P = xs.shape[0]
    E, D, F2 = w_gu.shape
    F = F2 // 2
    tf = UP_TILE
    nf = F // tf
    b3 = b_gu.reshape(E, 1, F2)
    return pl.pallas_call(
        _moe_up_kernel,
        grid_spec=pltpu.PrefetchScalarGridSpec(
            num_scalar_prefetch=len(sched),
            grid=(sched[SCHED_SG].shape[0],),
            in_specs=[pl.BlockSpec((ROW_BLOCK, D // 2), lambda s, *r: (r[SCHED_SB][s], 0)),
                      pl.BlockSpec(memory_space=pl.ANY),
                      pl.BlockSpec((None, 1, tf), lambda s, *r: (_step_expert(s, r), 0, _step_tile(s, r))),
                      pl.BlockSpec((None, 1, tf), lambda s, *r: (_step_expert(s, r), 0, nf + _step_tile(s, r)))],
            out_specs=pl.BlockSpec((ROW_BLOCK, tf), lambda s, *r: (r[SCHED_SB][s], r[SCHED_SO][s])),
            scratch_shapes=[pltpu.VMEM((2, D, tf), F32), pltpu.VMEM((D, tf), BF16), pltpu.VMEM((D, tf), BF16),
                            pltpu.SemaphoreType.DMA((2,))]),
        out_shape=jax.ShapeDtypeStruct((P, F), BF16),
        compiler_params=_cparams(("arbitrary",)),
        name="moe_up",
    )(*sched, xs, w_gu, b3, b3)


def _moe_down_kernel(*refs):
    sched = refs[:8]
    h_ref, w_hbm, bd_ref, y_ref, wbuf, wdb, sem = refs[8:]
    s = pl.program_id(0)
    tn = wdb.shape[1]

    def copies(g):
        col = pl.multiple_of(sched[SCHED_GF][g] * tn, tn)
        return [pltpu.make_async_copy(w_hbm.at[sched[SCHED_GE][g], :, pl.ds(col, tn)], wbuf, sem)]

    def on_arrival():
        wdb[...] = wbuf[...].astype(BF16)

    _stream_weights(s, sched, copies, on_arrival)

    def compute(rows):
        y_ref[:rows, :] = jnp.dot(h_ref[:rows, :], wdb[...], preferred_element_type=F32) + bd_ref[...]

    _for_used_rows(s, sched, y_ref, compute)


def moe_down(h, w_down, b_down, sched):
    P, F = h.shape
    E, _, D = w_down.shape
    tn = DOWN_TILE
    b3 = b_down.reshape(E, 1, D)
    return pl.pallas_call(
        _moe_down_kernel,
        grid_spec=pltpu.PrefetchScalarGridSpec(
            num_scalar_prefetch=len(sched),
            grid=(sched[SCHED_SG].shape[0],),
            in_specs=[pl.BlockSpec((ROW_BLOCK, F), lambda s, *r: (r[SCHED_SB][s], 0)),
                      pl.BlockSpec(memory_space=pl.ANY),
                      pl.BlockSpec((None, 1, tn), lambda s, *r: (_step_expert(s, r), 0, _step_tile(s, r)))],
            out_specs=pl.BlockSpec((ROW_BLOCK, tn), lambda s, *r: (r[SCHED_SB][s], r[SCHED_SO][s])),
            scratch_shapes=[pltpu.VMEM((F, tn), F32), pltpu.VMEM((F, tn), BF16), pltpu.SemaphoreType.DMA(())]),
        out_shape=jax.ShapeDtypeStruct((P, D), F32),
        compiler_params=_cparams(("arbitrary",)),
        name="moe_down",
    )(*sched, h, w_down, b3)


def _combine_kernel(dcur_ref, dnxt_ref, x1_ref, w_ref, fw_ref, ys_hbm, o_ref, buf, sems):
    i = pl.program_id(0)
    n_steps = pl.num_programs(0)
    tq = x1_ref.shape[0]
    n = TOP_K * tq
    slot = lax.rem(i, 2)

    def issue(idx_ref, sl):
        def body(t, carry):
            _row_copy(ys_hbm, buf.at[sl], idx_ref[0, 0, t], t, sems.at[sl]).start()
            return carry

        lax.fori_loop(0, n, body, 0, unroll=DMA_UNROLL)

    pl.when(i == 0)(lambda: issue(dcur_ref, 0))
    pl.when(i + 1 < n_steps)(lambda: issue(dnxt_ref, 1 - slot))
    pltpu.make_async_copy(ys_hbm.at[pl.ds(0, n), :], buf.at[slot], sems.at[slot]).wait()

    acc = x1_ref[...]
    for j in range(TOP_K):
        acc = acc + w_ref[:, j:j + 1] * buf[slot, j * tq:(j + 1) * tq, :]
    o_ref[...] = acc * lax.rsqrt(jnp.mean(acc * acc, axis=-1, keepdims=True) + NORM_EPS) * fw_ref[...]


def combine(x1, ys, dest, top_w, final_w):
    T, D = x1.shape
    tq = COMBINE_TOKENS
    nblk = T // tq
    dest_blk = dest.reshape(nblk, tq, TOP_K).transpose(0, 2, 1).reshape(nblk, 1, TOP_K * tq)
    idx_spec = lambda f: pl.BlockSpec((1, 1, TOP_K * tq), f, memory_space=pltpu.SMEM)
    return pl.pallas_call(
        _combine_kernel,
        grid=(nblk,),
        in_specs=[idx_spec(lambda i: (i, 0, 0)),
                  idx_spec(lambda i: (jnp.minimum(i + 1, nblk - 1), 0, 0)),
                  pl.BlockSpec((tq, D), lambda i: (i, 0)),
                  pl.BlockSpec((tq, TOP_K), lambda i: (i, 0)),
                  pl.BlockSpec((1, D), lambda i: (0, 0)),
                  pl.BlockSpec(memory_space=pl.ANY)],
        out_specs=pl.BlockSpec((tq, D), lambda i: (i, 0)),
        out_shape=jax.ShapeDtypeStruct((T, D), F32),
        scratch_shapes=[pltpu.VMEM((2, TOP_K * tq, D), F32), pltpu.SemaphoreType.DMA((2,))],
        compiler_params=_cparams(("arbitrary",)),
        name="moe_combine",
    )(dest_blk, dest_blk, x1, top_w, final_w.reshape(1, D), ys)


def _routing(logits):
    T = logits.shape[0]
    TK = T * TOP_K
    NB = TK // ROW_BLOCK + N_EXPERTS
    top_logits, top_idx = lax.top_k(logits[:, :N_EXPERTS], TOP_K)
    top_w = jax.nn.softmax(top_logits, axis=-1)
    flat_e = top_idx.reshape(TK).astype(jnp.int32)
    onehot = (flat_e[:, None] == jnp.arange(N_EXPERTS, dtype=jnp.int32)[None, :]).astype(F32)
    oh = onehot.reshape(TK // RANK_GROUP, RANK_GROUP, N_EXPERTS)
    local = jnp.einsum("ts,gse->gte", jnp.tril(jnp.ones((RANK_GROUP, RANK_GROUP), F32)), oh)
    tot = local[:, -1, :]
    offs = jnp.cumsum(tot, axis=0) - tot
    rank = jnp.sum(oh * (local + offs[:, None, :]), axis=-1).reshape(TK).astype(jnp.int32) - 1
    counts = (offs[-1] + tot[-1]).astype(jnp.int32)
    padded = (counts + ROW_BLOCK - 1) // ROW_BLOCK * ROW_BLOCK
    pad_end = jnp.cumsum(padded).astype(jnp.int32)
    dest = ((pad_end - padded)[flat_e] + rank).astype(jnp.int32)
    block_start = jnp.arange(NB, dtype=jnp.int32) * ROW_BLOCK
    block_e = jnp.minimum(jnp.sum((pad_end[None, :] <= block_start[:, None]).astype(jnp.int32), axis=1),
                          N_EXPERTS - 1)
    nb_used = pad_end[-1] // ROW_BLOCK
    blk = jnp.arange(NB, dtype=jnp.int32)
    rows_left = counts[block_e] - (blk - ((pad_end - padded) // ROW_BLOCK)[block_e]) * ROW_BLOCK
    block_full = ((blk < nb_used) & (rows_left > ROW_HALF)).astype(jnp.int32)
    tail = nb_used + jnp.arange(N_EXPERTS, dtype=jnp.int32)
    zrow = jnp.concatenate([jnp.where(padded > 0, pad_end - ROW_BLOCK, -1),
                            jnp.where(tail < NB, tail * ROW_BLOCK, -1)]).astype(jnp.int32)
    return dest.reshape(T, TOP_K), top_w, block_e, block_full, padded, nb_used, zrow


def _schedule(block_e, block_full, padded, nb_used, n_tiles):
    nb = block_e.shape[0]
    b = jnp.tile(jnp.arange(nb, dtype=jnp.int32), n_tiles)
    f = jnp.repeat(jnp.arange(n_tiles, dtype=jnp.int32), nb)
    key = jnp.where(b < nb_used, (block_e[b] * n_tiles + f) * nb + b, (N_EXPERTS * n_tiles + f) * nb + b)
    order = jnp.argsort(key)
    used = padded > 0
    n_groups = (jnp.sum(used) * n_tiles).astype(jnp.int32)
    expert_pos = jnp.cumsum(used) - 1
    group = jnp.minimum(expert_pos[block_e[b]] * n_tiles + f, n_groups - 1).astype(jnp.int32)
    n_valid = (nb_used * n_tiles).astype(jnp.int32)
    sg = jnp.where(jnp.arange(nb * n_tiles) < n_valid, group[order], n_groups - 1)
    experts = jnp.arange(N_EXPERTS, dtype=jnp.int32)
    used_first = jnp.argsort(jnp.where(used, experts, N_EXPERTS + experts)).astype(jnp.int32)
    ge = jnp.repeat(used_first, n_tiles)
    gf = jnp.tile(jnp.arange(n_tiles, dtype=jnp.int32), N_EXPERTS)
    return (n_valid.reshape(1), sg, b[order], f[order], block_full[b][order], ge, gf, n_groups.reshape(1))


def _repack_w_in(w_in):
    z = lambda n: jnp.zeros((w_in.shape[0], n), w_in.dtype)
    o = RWKV_COLS
    w_r = jnp.concatenate([w_in[:, :RW_WD], w_in[:, RW_WD:RW_WD + DECAY_LORA], z(LORA_PAD - DECAY_LORA),
                           w_in[:, RW_WD + DECAY_LORA:RW_WD + DECAY_LORA + AAA_LORA], z(LORA_PAD - AAA_LORA),
                           w_in[:, RW_WD + DECAY_LORA + AAA_LORA:o]], axis=1)
    w_m = jnp.concatenate([w_in[:, o:o + ML_IF], w_in[:, o + ML_IF:o + ML_IF + 2 * MLSTM_HEADS],
                           z(LANES - 2 * MLSTM_HEADS), w_in[:, o + ML_IF + 2 * MLSTM_HEADS:o + MLSTM_COLS]], axis=1)
    w_g = w_in[:, o + MLSTM_COLS:]
    return w_r.astype(BF16), w_m.astype(BF16), w_g.astype(BF16)


def kernel(x, norm1_w, w_in, b_gate, rwkv_mu, rwkv_w0, rwkv_w_up, rwkv_a0, rwkv_a_up, rwkv_g_up, rwkv_k_k,
           rwkv_k_a, rwkv_r_k, rwkv_ln_w, rwkv_ln_b, mlstm_conv_w, mlstm_i_b, mlstm_f_b, mlstm_norm_w,
           w_branch_a, w_branch_b, w_out, norm2_w, router_w, router_b, w_gu, b_gu, w_down, b_down,
           final_norm_w):
    B, S, D = x.shape
    T = B * S
    xt = x.reshape(T, D)
    assert norm1_w.shape[0] == 1, "single-layer block: the final rmsnorm is fused into the MoE combine"
    for l in range(1):
        hn = rmsnorm_rows(xt, norm1_w[l])
        w_r, w_m, w_g = _repack_w_in(w_in[l])
        p_r = matmul(hn, w_r, 512, RW_COLS_P // 2, F32, "proj_rwkv").reshape(B, S, RW_COLS_P)
        p_m = matmul(hn, w_m, 256, ML_COLS_P, F32, "proj_mlstm").reshape(B, S, ML_COLS_P)
        p_g = matmul(hn, w_g, 512, D, F32, "proj_gate")
        o_a = rwkv_branch(p_r, rwkv_mu[l], rwkv_w0[l], rwkv_w_up[l], rwkv_a0[l], rwkv_a_up[l], rwkv_g_up[l],
                          rwkv_k_k[l], rwkv_k_a[l], rwkv_r_k[l].reshape(-1), rwkv_ln_w[l], rwkv_ln_b[l])
        o_b = mlstm_branch(p_m, mlstm_conv_w[l], mlstm_i_b[l], mlstm_f_b[l], mlstm_norm_w[l])
        x1, xp, logits = merge_project(o_a.reshape(T, RWKV_DIM), o_b.reshape(T, MLSTM_DIM), p_g, b_gate[l], xt,
                                       w_branch_a[l], w_branch_b[l], w_out[l], norm2_w[l], router_w[l],
                                       router_b[l])
        dest, top_w, block_e, block_full, padded, nb_used, zrow = _routing(logits)
        xs = moe_dispatch(xp, dest, zrow, block_e.shape[0] * ROW_BLOCK)
        assert EXPERT_FF // UP_TILE == D // DOWN_TILE, "up and down share one step order"
        sched = _schedule(block_e, block_full, padded, nb_used, EXPERT_FF // UP_TILE)
        h = moe_up(xs, w_gu[l], b_gu[l], sched)
        ys = moe_down(h, w_down[l], b_down[l], sched)
        xt = combine(x1, ys, dest, top_w, final_norm_w)
    return xt.reshape(B, S, D)
```

```python
import functools

import jax
import jax.numpy as jnp
import numpy as np
from jax import lax
from jax.experimental import pallas as pl
from jax.experimental.pallas import tpu as pltpu

F32 = jnp.float32
BF16 = jnp.bfloat16

D_MODEL = 2048
CHUNK = 64
NORM_EPS = 1e-6
RWKV_HEADS = 16
RWKV_HEAD_DIM = 64
RWKV_DIM = 1024
DECAY_LORA = 96
AAA_LORA = 96
GATE_LORA = 256
GN_EPS = 64e-5
RWKV_COLS = 3 * RWKV_DIM + DECAY_LORA + AAA_LORA + GATE_LORA
MLSTM_HEADS = 4
MLSTM_QK_DIM = 128
MLSTM_V_DIM = 256
MLSTM_QK = 512
MLSTM_DIM = 1024
CONV_WIDTH = 4
GATE_SOFTCAP = 15.0
MLSTM_COLS = 2 * MLSTM_QK + 2 * MLSTM_DIM + 2 * MLSTM_HEADS
N_EXPERTS = 32
TOP_K = 4
EXPERT_FF = 2048
SWIGLU_LIMIT = 7.0
SWIGLU_ALPHA = 1.702

LANES = 128
SUBLANES = 8
VMEM_LIMIT = 56 * 1024 * 1024

LORA_PAD = 128
RW_WD = 3 * RWKV_DIM
RW_AD = RW_WD + LORA_PAD
RW_GD = RW_AD + LORA_PAD
RW_COLS_P = RW_GD + GATE_LORA
ML_V = 2 * MLSTM_QK
ML_IF = ML_V + MLSTM_DIM
ML_O = ML_IF + LANES
ML_COLS_P = ML_O + MLSTM_DIM
ROUTER_PAD = 128
SCAN_SEQS = 4

ROW_BLOCK = 512
ROW_HALF = ROW_BLOCK // 2


def _cparams(sem):
    return pltpu.CompilerParams(dimension_semantics=sem, vmem_limit_bytes=VMEM_LIMIT)


def _bdot(a, b):
    return jnp.dot(a.astype(BF16), b.astype(BF16), preferred_element_type=F32)


def _split3(x):
    hi = x.astype(BF16)
    r1 = x - hi.astype(F32)
    mid = r1.astype(BF16)
    lo = (r1 - mid.astype(F32)).astype(BF16)
    return hi, mid, lo


def _dot_exact_lhs(mat_bf16, x):
    hi, mid, lo = _split3(x)
    return (jnp.dot(mat_bf16, hi, preferred_element_type=F32)
            + jnp.dot(mat_bf16, mid, preferred_element_type=F32)
            + jnp.dot(mat_bf16, lo, preferred_element_type=F32))


def _dot_exact_rhs(x, mat_bf16):
    hi, mid, lo = _split3(x)
    return (jnp.dot(hi, mat_bf16, preferred_element_type=F32)
            + jnp.dot(mid, mat_bf16, preferred_element_type=F32)
            + jnp.dot(lo, mat_bf16, preferred_element_type=F32))


def _sigmoid(x):
    return 1.0 / (1.0 + jnp.exp(-x))


def _softplus(x):
    return jnp.maximum(x, 0.0) + jnp.log(1.0 + jnp.exp(-jnp.abs(x)))


def _rmsnorm_kernel(x_ref, w_ref, o_ref):
    x = x_ref[...]
    y = x * lax.rsqrt(jnp.mean(x * x, axis=-1, keepdims=True) + NORM_EPS)
    o_ref[...] = (y * w_ref[...]).astype(o_ref.dtype)


def rmsnorm_rows(x, w, tm=512):
    T, D = x.shape
    return pl.pallas_call(
        _rmsnorm_kernel,
        grid=(T // tm,),
        in_specs=[pl.BlockSpec((tm, D), lambda i: (i, 0)),
                  pl.BlockSpec((1, D), lambda i: (0, 0))],
        out_specs=pl.BlockSpec((tm, D), lambda i: (i, 0)),
        out_shape=jax.ShapeDtypeStruct((T, D), BF16),
        compiler_params=_cparams(("parallel",)),
        name="rmsnorm1",
    )(x, w.reshape(1, D))


def _mm_kernel(a_ref, b_ref, o_ref):
    o_ref[...] = jnp.dot(a_ref[...], b_ref[...], preferred_element_type=F32).astype(o_ref.dtype)


def matmul(a, b, tm, tn, out_dtype, name):
    M, K = a.shape
    N = b.shape[1]
    return pl.pallas_call(
        _mm_kernel,
        grid=(N // tn, M // tm),
        in_specs=[pl.BlockSpec((tm, K), lambda j, i: (i, 0)),
                  pl.BlockSpec((K, tn), lambda j, i: (0, j))],
        out_specs=pl.BlockSpec((tm, tn), lambda j, i: (i, j)),
        out_shape=jax.ShapeDtypeStruct((M, N), out_dtype),
        compiler_params=_cparams(("parallel", "parallel")),
        name=name,
    )(a, b)


def _head_sum_mat():
    r = lax.broadcasted_iota(jnp.int32, (LANES, LANES), 0) // RWKV_HEAD_DIM
    c = lax.broadcasted_iota(jnp.int32, (LANES, LANES), 1) // RWKV_HEAD_DIM
    return jnp.where(r == c, 1.0, 0.0).astype(BF16)


def _rwkv_prep_kernel(p_ref, mu_ref, w0_ref, wup_ref, a0_ref, aup_ref, gup_ref, kk_ref, ka_ref,
                      r_out, k_out, v_out, kk_out, b_out, lw_out, g_out, carry_ref):
    i = pl.program_id(1)

    @pl.when(i == 0)
    def _():
        carry_ref[...] = jnp.zeros_like(carry_ref)

    p = p_ref[...]
    tq = p.shape[0]
    row = lax.broadcasted_iota(jnp.int32, (tq, 1), 0)
    prev = jnp.where(row == 0, carry_ref[0:1, :], pltpu.roll(p, 1, 0))
    carry_ref[0:1, :] = p[tq - 1:tq, :]
    p = p + (prev - p) * mu_ref[...]

    r = p[:, 0:RWKV_DIM]
    k = p[:, RWKV_DIM:2 * RWKV_DIM]
    v = p[:, 2 * RWKV_DIM:3 * RWKV_DIM]
    wd = p[:, RW_WD:RW_AD]
    ad = p[:, RW_AD:RW_GD]
    gd = p[:, RW_GD:RW_COLS_P]

    w_log = -_softplus(-(w0_ref[...] + _bdot(jnp.tanh(wd), wup_ref[...]))) - 0.5
    lw = -jnp.exp(w_log)
    a = _sigmoid(a0_ref[...] + _bdot(ad, aup_ref[...]))
    g = _bdot(_sigmoid(gd), gup_ref[...])

    kk = k * kk_ref[...]
    hs = _head_sum_mat()
    nrm2 = jnp.concatenate(
        [_dot_exact_rhs(kk[:, c:c + LANES] * kk[:, c:c + LANES], hs) for c in range(0, RWKV_DIM, LANES)],
        axis=1)
    kk = kk / jnp.maximum(jnp.sqrt(nrm2), 1e-12)
    k = k * (1.0 + (a - 1.0) * ka_ref[...])

    r_out[...] = r.astype(r_out.dtype)
    k_out[...] = k.astype(k_out.dtype)
    v_out[...] = v.astype(v_out.dtype)
    kk_out[...] = kk.astype(kk_out.dtype)
    b_out[...] = (kk * a).astype(b_out.dtype)
    lw_out[...] = lw
    g_out[...] = g.astype(g_out.dtype)


def rwkv_prep(p_rwkv, mu, w0, w_up, a0, a_up, g_up, k_k, k_a, tq=256):
    B, S, C = p_rwkv.shape
    blk = lambda n: pl.BlockSpec((None, tq, n), lambda b, i: (b, i, 0))
    full = lambda a: pl.BlockSpec(a.shape, lambda b, i: (0,) * a.ndim)
    params = (mu, w0, w_up, a0, a_up, g_up, k_k, k_a)
    out = lambda dt: jax.ShapeDtypeStruct((B, S, RWKV_DIM), dt)
    return pl.pallas_call(
        _rwkv_prep_kernel,
        grid=(B, S // tq),
        in_specs=[blk(C)] + [full(a) for a in params],
        out_specs=[blk(RWKV_DIM)] * 7,
        out_shape=[out(BF16)] * 5 + [out(F32), out(BF16)],
        scratch_shapes=[pltpu.VMEM((SUBLANES, C), F32)],
        compiler_params=_cparams(("parallel", "arbitrary")),
        name="rwkv_prep",
    )(p_rwkv, *params)


def _rwkv_scan_kernel(r_ref, k_ref, v_ref, kk_ref, b_ref, lw_ref, g_ref, rk_ref, lnw_ref, lnb_ref,
                      o_ref, h_ref):
    c = pl.program_id(1)

    @pl.when(c == 0)
    def _():
        h_ref[...] = jnp.zeros_like(h_ref)

    L = CHUNK
    L2 = 2 * L
    ri = lax.broadcasted_iota(jnp.int32, (L, L), 0)
    ci = lax.broadcasted_iota(jnp.int32, (L, L), 1)
    tril = jnp.where(ri >= ci, 1.0, 0.0).astype(BF16)

    n_seq = lw_ref.shape[0]
    cat = lambda ref: jnp.concatenate([ref[i] for i in range(n_seq)], axis=1)
    rep = lambda ref: jnp.concatenate([ref[...]] * n_seq, axis=1)
    lw = cat(lw_ref)
    cum = _dot_exact_lhs(tril, lw)
    cum_end = cum[L - 1:L, :]
    w_in = jnp.exp(cum)
    w_prev = jnp.exp(cum - lw)
    w_inv = jnp.exp(-cum)
    w_tail = jnp.exp(cum_end - cum)
    w_end = jnp.exp(cum_end)

    kk = cat(kk_ref).astype(F32)
    bb = cat(b_ref).astype(F32)
    kx = cat(k_ref).astype(F32)
    rx = cat(r_ref).astype(F32)
    vx = cat(v_ref).astype(F32)
    gx = cat(g_ref).astype(F32)
    a_hat = -kk * w_prev
    r_hat = rx * w_in
    b_hat = bb * w_inv
    k_hat = kx * w_inv
    b_til = bb * w_tail
    k_til = kx * w_tail
    rkk = rx * kx * rep(rk_ref)

    lane = lax.broadcasted_iota(jnp.int32, (1, LANES), 1)
    m_lo = jnp.where(lane < RWKV_HEAD_DIM, 1.0, 0.0)
    m_hi = 1.0 - m_lo

    def stack(x):
        return jnp.concatenate([x * m_lo, x * m_hi], axis=0)

    r2 = lax.broadcasted_iota(jnp.int32, (L2, L2), 0)
    c2 = lax.broadcasted_iota(jnp.int32, (L2, L2), 1)
    same_head = (r2 // L) == (c2 // L)
    strict = same_head & (r2 > c2)
    incl = same_head & (r2 >= c2)
    diag16 = (r2 // 16) == (c2 // 16)
    eye = jnp.where(r2 == c2, 1.0, 0.0)
    hs = _head_sum_mat()

    pairs_per_seq = RWKV_DIM // LANES
    pairs = range(n_seq * pairs_per_seq)
    sls = [slice(p * LANES, (p + 1) * LANES) for p in pairs]
    v_st = [stack(vx[:, sl]) for sl in sls]
    lhs = [jnp.concatenate([stack(a_hat[:, sl]), stack(r_hat[:, sl])], axis=0).astype(BF16) for sl in sls]
    rhs = [jnp.concatenate([stack(b_hat[:, sl]), stack(k_hat[:, sl])], axis=0).astype(BF16) for sl in sls]
    sc = [lax.dot_general(lhs[p], rhs[p], (((1,), (1,)), ((), ())), preferred_element_type=F32) for p in pairs]
    n_ab = [jnp.where(strict, sc[p][:L2, :L2], 0.0) for p in pairs]
    a_ak = [jnp.where(strict, sc[p][:L2, L2:], 0.0).astype(BF16) for p in pairs]
    a_r = [jnp.concatenate([jnp.where(incl, sc[p][L2:, :L2], 0.0), jnp.where(incl, sc[p][L2:, L2:], 0.0)],
                           axis=1).astype(BF16) for p in pairs]

    nd = [jnp.where(diag16, n_ab[p], 0.0) for p in pairs]
    noff = [(n_ab[p] - nd[p]).astype(BF16) for p in pairs]
    ndb = [nd[p].astype(BF16) for p in pairs]
    s2 = [jnp.dot(ndb[p], ndb[p], preferred_element_type=F32).astype(BF16) for p in pairs]
    s4 = [jnp.dot(s2[p], s2[p], preferred_element_type=F32).astype(BF16) for p in pairs]
    s8 = [jnp.dot(s4[p], s4[p], preferred_element_type=F32).astype(BF16) for p in pairs]
    x1 = [eye + nd[p] for p in pairs]
    x2 = [x1[p] + _bdot(x1[p], s2[p]) for p in pairs]
    x3 = [x2[p] + _bdot(x2[p], s4[p]) for p in pairs]
    t_d = [(x3[p] + _bdot(x3[p], s8[p])).astype(BF16) for p in pairs]
    m1 = [jnp.dot(t_d[p], noff[p], preferred_element_type=F32) for p in pairs]
    m1b = [m1[p].astype(BF16) for p in pairs]
    m2 = [jnp.dot(m1b[p], m1b[p], preferred_element_type=F32) for p in pairs]
    m3 = [jnp.dot(m1b[p], m2[p].astype(BF16), preferred_element_type=F32) for p in pairs]
    t_inv = [jnp.dot((eye + m1[p] + m2[p] + m3[p]).astype(BF16), t_d[p], preferred_element_type=F32).astype(BF16)
             for p in pairs]

    h0 = [h_ref[p] for p in pairs]
    ah = [jnp.dot(lhs[p], h0[p].astype(BF16), preferred_element_type=F32) for p in pairs]
    x = [ah[p][:L2] + jnp.dot(a_ak[p], v_st[p].astype(BF16), preferred_element_type=F32) for p in pairs]
    u = [jnp.dot(t_inv[p], x[p].astype(BF16), preferred_element_type=F32) for p in pairs]
    uv = [jnp.concatenate([u[p], v_st[p]], axis=0).astype(BF16) for p in pairs]
    y_st = [ah[p][L2:] + jnp.dot(a_r[p], uv[p], preferred_element_type=F32) for p in pairs]
    y = [y_st[p][:L] + y_st[p][L:] for p in pairs]

    for p in pairs:
        sl = sls[p]
        upd_l = jnp.concatenate([stack(b_til[:, sl]), stack(k_til[:, sl])], axis=0).astype(BF16)
        upd = lax.dot_general(upd_l, uv[p], (((0,), (0,)), ((), ())), preferred_element_type=F32)
        w_col = jnp.sum(eye * w_end[:, sl], axis=1, keepdims=True)
        h_ref[p] = w_col * h0[p] + upd

    def head_sums(vals):
        parts = []
        for t in vals:
            hi = t.astype(BF16)
            parts += [hi, (t - hi.astype(F32)).astype(BF16)]
        res = jnp.dot(jnp.concatenate(parts, axis=0), hs, preferred_element_type=F32)
        return [res[2 * i * L:(2 * i + 1) * L] + res[(2 * i + 1) * L:(2 * i + 2) * L] for i in range(len(vals))]

    sums1 = [head_sums([y[p], rkk[:, sls[p]]]) for p in pairs]
    d = [y[p] - sums1[p][0] * (1.0 / RWKV_HEAD_DIM) for p in pairs]
    var = [head_sums([d[p] * d[p]])[0] * (1.0 / RWKV_HEAD_DIM) for p in pairs]
    for p in pairs:
        sl = sls[p]
        psl = sls[p % pairs_per_seq]
        yn = d[p] * lax.rsqrt(var[p] + GN_EPS) * lnw_ref[:, psl] + lnb_ref[:, psl]
        o_ref[p // pairs_per_seq, :, psl] = ((yn + sums1[p][1] * vx[:, sl]) * gx[:, sl]).astype(o_ref.dtype)


def rwkv_scan(r, k, v, kk, b, lw, g, r_k, ln_w, ln_b):
    B, S, C = r.shape
    blk = pl.BlockSpec((SCAN_SEQS, CHUNK, C), lambda bb, c: (bb, c, 0))
    full = pl.BlockSpec((1, C), lambda bb, c: (0, 0))
    return pl.pallas_call(
        _rwkv_scan_kernel,
        grid=(B // SCAN_SEQS, S // CHUNK),
        in_specs=[blk] * 7 + [full] * 3,
        out_specs=blk,
        out_shape=jax.ShapeDtypeStruct((B, S, C), BF16),
        scratch_shapes=[pltpu.VMEM((SCAN_SEQS * C // LANES, LANES, LANES), F32)],
        compiler_params=_cparams(("parallel", "arbitrary")),
        name="rwkv_scan",
    )(r, k, v, kk, b, lw, g, r_k, ln_w, ln_b)


def _pad_rows(w, n):
    return jnp.pad(w, ((0, n - w.shape[0]), (0, 0)))


def rwkv_branch(p_pad, mu, w0, w_up, a0, a_up, g_up, k_k, k_a, r_k, ln_w, ln_b):
    row = lambda t: t.reshape(1, -1)
    mu_p = jnp.concatenate([mu[:RW_WD], jnp.pad(mu[RW_WD:RW_WD + DECAY_LORA], (0, LORA_PAD - DECAY_LORA)),
                            jnp.pad(mu[RW_WD + DECAY_LORA:RW_WD + DECAY_LORA + AAA_LORA], (0, LORA_PAD - AAA_LORA)),
                            mu[RW_WD + DECAY_LORA + AAA_LORA:]])
    outs = rwkv_prep(p_pad, row(mu_p), row(w0), _pad_rows(w_up, LORA_PAD).astype(BF16), row(a0),
                     _pad_rows(a_up, LORA_PAD).astype(BF16), g_up.astype(BF16), row(k_k), row(k_a))
    return rwkv_scan(*outs, row(r_k), row(ln_w), row(ln_b))


def _mlstm_kernel(p_ref, convw_ref, ifb_ref, nw_ref, o_ref, carry_ref, c_ref, n_ref, m_ref):
    ci = pl.program_id(1)

    @pl.when(ci == 0)
    def _():
        carry_ref[...] = jnp.zeros_like(carry_ref)
        c_ref[...] = jnp.zeros_like(c_ref)
        n_ref[...] = jnp.zeros_like(n_ref)
        m_ref[...] = jnp.zeros_like(m_ref)

    L = CHUNK
    u = p_ref[:, 0:ML_V]
    ext = jnp.concatenate([carry_ref[...], u], axis=0)
    carry_ref[...] = u[L - SUBLANES:L, :]
    conv = convw_ref[CONV_WIDTH - 1:CONV_WIDTH, :] * u
    for j in range(1, CONV_WIDTH):
        conv = conv + convw_ref[CONV_WIDTH - 1 - j:CONV_WIDTH - j, :] * pltpu.roll(ext, j, 0)[SUBLANES:, :]
    qk = conv * _sigmoid(conv)

    pre = GATE_SOFTCAP * jnp.tanh((p_ref[:, ML_IF:ML_O] + ifb_ref[...]) * (1.0 / GATE_SOFTCAP))
    lane = lax.broadcasted_iota(jnp.int32, (1, LANES), 1)
    comb = jnp.where(lane < MLSTM_HEADS, pre, -_softplus(-pre))
    comb_t = comb.T
    ri = lax.broadcasted_iota(jnp.int32, (L, L), 0)
    cj = lax.broadcasted_iota(jnp.int32, (L, L), 1)
    causal = ri >= cj
    tril = jnp.where(causal, 1.0, 0.0).astype(BF16)
    triu = jnp.where(ri <= cj, 1.0, 0.0).astype(BF16)
    b_col = _dot_exact_lhs(tril, comb)
    b_row = _dot_exact_rhs(comb_t, triu)

    H = range(MLSTM_HEADS)
    dk, dv = MLSTM_QK_DIM, MLSTM_V_DIM
    qh = [qk[:, h * dk:(h + 1) * dk] * (dk ** -0.5) for h in H]
    kh = [qk[:, MLSTM_QK + h * dk:MLSTM_QK + (h + 1) * dk] for h in H]
    vh = [p_ref[:, ML_V + h * dv:ML_V + (h + 1) * dv].astype(BF16) for h in H]
    qb = [qh[h].astype(BF16) for h in H]
    bcol = [b_col[:, MLSTM_HEADS + h:MLSTM_HEADS + h + 1] for h in H]
    brow = [b_row[MLSTM_HEADS + h:MLSTM_HEADS + h + 1, :] for h in H]
    m_prev = [m_ref[h][0:1, 0:1] for h in H]
    n_prev = [n_ref[h][0:1, :] for h in H]
    c_prev = [c_ref[h] for h in H]

    qk_t = [lax.dot_general(qb[h], kh[h].astype(BF16), (((1,), (1,)), ((), ())), preferred_element_type=F32)
            for h in H]
    qc = [jnp.dot(qb[h], c_prev[h].astype(BF16), preferred_element_type=F32) for h in H]
    dm = [jnp.where(causal, bcol[h] - brow[h] + comb_t[h:h + 1, :], -jnp.inf) for h in H]
    inter = [bcol[h] + m_prev[h] for h in H]
    m_t = [jnp.maximum(inter[h], jnp.max(dm[h], axis=-1, keepdims=True)) for h in H]
    s = [qk_t[h] * jnp.exp(dm[h] - m_t[h]) for h in H]
    w_inter = [jnp.exp(inter[h] - m_t[h]) for h in H]
    num = [jnp.dot(s[h].astype(BF16), vh[h], preferred_element_type=F32) + w_inter[h] * qc[h] for h in H]
    den = [jnp.sum(s[h], axis=-1, keepdims=True) + w_inter[h] * jnp.sum(qh[h] * n_prev[h], axis=-1, keepdims=True)
           for h in H]
    hh = [num[h] / jnp.maximum(jnp.abs(den[h]), jnp.exp(-m_t[h])) for h in H]

    g_tot = [bcol[h][L - 1:L, :] for h in H]
    a = [comb[:, h:h + 1] + g_tot[h] - bcol[h] for h in H]
    m_new = [jnp.maximum(g_tot[h] + m_prev[h], jnp.max(a[h], axis=0, keepdims=True)) for h in H]
    dec = [jnp.exp(g_tot[h] + m_prev[h] - m_new[h]) for h in H]
    wkk = [jnp.exp(a[h] - m_new[h]) * kh[h] for h in H]
    for h in H:
        c_ref[h] = dec[h] * c_prev[h] + lax.dot_general(wkk[h].astype(BF16), vh[h], (((0,), (0,)), ((), ())),
                                                        preferred_element_type=F32)
        n_ref[h] = jnp.broadcast_to(dec[h] * n_prev[h] + jnp.sum(wkk[h], axis=0, keepdims=True),
                                    (SUBLANES, LANES))
        m_ref[h] = jnp.broadcast_to(m_new[h], (SUBLANES, LANES))
    for h in H:
        vs = slice(h * dv, (h + 1) * dv)
        hn = hh[h] * lax.rsqrt(jnp.mean(hh[h] * hh[h], axis=-1, keepdims=True) + NORM_EPS)
        o_raw = p_ref[:, ML_O + h * dv:ML_O + (h + 1) * dv]
        o_ref[:, vs] = (hn * nw_ref[:, vs] * _sigmoid(o_raw)).astype(o_ref.dtype)


def mlstm_branch(p_pad, conv_w, i_b, f_b, norm_w):
    B, S, C = p_pad.shape
    ifb = jnp.pad(jnp.concatenate([i_b, f_b]), (0, LANES - 2 * MLSTM_HEADS)).reshape(1, LANES)
    full = lambda a: pl.BlockSpec(a.shape, lambda b, c: (0,) * a.ndim)
    nw = norm_w.reshape(1, MLSTM_DIM)
    return pl.pallas_call(
        _mlstm_kernel,
        grid=(B, S // CHUNK),
        in_specs=[pl.BlockSpec((None, CHUNK, C), lambda b, c: (b, c, 0)), full(conv_w), full(ifb), full(nw)],
        out_specs=pl.BlockSpec((None, CHUNK, MLSTM_DIM), lambda b, c: (b, c, 0)),
        out_shape=jax.ShapeDtypeStruct((B, S, MLSTM_DIM), BF16),
        scratch_shapes=[pltpu.VMEM((SUBLANES, ML_V), F32),
                        pltpu.VMEM((MLSTM_HEADS, MLSTM_QK_DIM, MLSTM_V_DIM), F32),
                        pltpu.VMEM((MLSTM_HEADS, SUBLANES, LANES), F32),
                        pltpu.VMEM((MLSTM_HEADS, SUBLANES, LANES), F32)],
        compiler_params=_cparams(("parallel", "arbitrary")),
        name="mlstm_scan",
    )(p_pad, conv_w, ifb, nw)


HI_MASK = 0xFFFF0000


def _pack_bf16_pairs(hb_f32):
    c = hb_f32.shape[1] // 2
    u = pltpu.bitcast(hb_f32, jnp.uint32)
    return u[:, c:] | (u[:, :c] >> 16)


def _unpack_bf16_pairs(xu):
    lo = pltpu.bitcast(xu << 16, F32).astype(BF16)
    hi = pltpu.bitcast(xu & jnp.uint32(HI_MASK), F32).astype(BF16)
    return lo, hi


def _merge_kernel(oa_ref, ob_ref, hn_ref, bg_ref, x_ref, wa_ref, wb_ref, wg_ref, wo_ref, n2_ref, rwh_ref, rwl_ref,
                  rb_ref, x1_ref, xp_ref, lg_ref):
    hn1 = hn_ref[...]

    def gated(o_ref, w_ref, cols):
        gate = _sigmoid(jnp.dot(hn1, wg_ref[:, cols], preferred_element_type=F32) + bg_ref[:, cols])
        return gate * jnp.dot(o_ref[...], w_ref[...], preferred_element_type=F32)

    merged = gated(oa_ref, wa_ref, slice(0, D_MODEL)) + gated(ob_ref, wb_ref, slice(D_MODEL, 2 * D_MODEL))
    x1 = x_ref[...] + jnp.dot(merged.astype(BF16), wo_ref[...], preferred_element_type=F32)
    x1_ref[...] = x1
    hn = x1 * lax.rsqrt(jnp.mean(x1 * x1, axis=-1, keepdims=True) + NORM_EPS) * n2_ref[...]
    hi = hn.astype(BF16)
    xp_ref[...] = _pack_bf16_pairs(hi.astype(F32))
    lo = (hn - hi.astype(F32)).astype(BF16)
    lg_ref[...] = (jnp.dot(hi, rwh_ref[...], preferred_element_type=F32)
                   + jnp.dot(hi, rwl_ref[...], preferred_element_type=F32)
                   + jnp.dot(lo, rwh_ref[...], preferred_element_type=F32) + rb_ref[...])


def merge_project(o_a, o_b, hn, w_gate, b_gate, x, w_a, w_b, w_out, norm2_w, router_w, router_b, tm=256):
    T, D = x.shape
    rw = jnp.pad(router_w, ((0, 0), (0, ROUTER_PAD - N_EXPERTS)))
    rw_hi = rw.astype(BF16)
    rw_lo = (rw - rw_hi.astype(F32)).astype(BF16)
    rb = jnp.pad(router_b, (0, ROUTER_PAD - N_EXPERTS)).reshape(1, ROUTER_PAD)
    rows = lambda n: pl.BlockSpec((tm, n), lambda i: (i, 0))
    full = lambda a: pl.BlockSpec(a.shape, lambda i: (0, 0), pipeline_mode=pl.Buffered(1))
    params = (w_a.astype(BF16), w_b.astype(BF16), w_gate, w_out.astype(BF16), norm2_w.reshape(1, D), rw_hi, rw_lo,
              rb)
    bg = b_gate.reshape(1, 2 * D)
    return pl.pallas_call(
        _merge_kernel,
        grid=(T // tm,),
        in_specs=[rows(o_a.shape[1]), rows(o_b.shape[1]), rows(D), full(bg), rows(D)] + [full(a) for a in params],
        out_specs=[rows(D), rows(D // 2), rows(ROUTER_PAD)],
        out_shape=[jax.ShapeDtypeStruct((T, D), F32), jax.ShapeDtypeStruct((T, D // 2), jnp.uint32),
                   jax.ShapeDtypeStruct((T, ROUTER_PAD), F32)],
        compiler_params=_cparams(("parallel",)),
        name="merge_project",
    )(o_a, o_b, hn, bg, x, *params)


UP_TILE = 1024
DOWN_TILE = 1024
RANK_GROUP = 256
DISPATCH_TOKENS = 256
COMBINE_TOKENS = 128
DMA_UNROLL = 8


def _row_copy(src, dst, src_row, dst_row, sem):
    return pltpu.make_async_copy(src.at[pl.ds(src_row, 1), :], dst.at[pl.ds(dst_row, 1), :], sem)


def _dispatch_kernel(zrow_ref, dest_ref, xp_ref, xs_hbm, zbuf, zsem, sem):
    i = pl.program_id(0)
    tq = dest_ref.shape[2] // TOP_K

    def zero_copy(k):
        row = pl.multiple_of(zrow_ref[k], ROW_BLOCK)
        return pltpu.make_async_copy(zbuf, xs_hbm.at[pl.ds(row, ROW_BLOCK), :], zsem)

    @pl.when(i == 0)
    def _():
        zbuf[...] = jnp.zeros_like(zbuf)
        for k in range(zrow_ref.shape[0]):
            pl.when(zrow_ref[k] >= 0)(lambda k=k: zero_copy(k).start())
        for k in range(zrow_ref.shape[0]):
            pl.when(zrow_ref[k] >= 0)(lambda k=k: zero_copy(k).wait())

    def body(t, carry):
        for j in range(TOP_K):
            _row_copy(xp_ref, xs_hbm, t, dest_ref[0, 0, TOP_K * t + j], sem).start()
        return carry

    lax.fori_loop(0, tq, body, 0, unroll=DMA_UNROLL // TOP_K)
    for j in range(TOP_K):
        pltpu.make_async_copy(xp_ref, xs_hbm.at[pl.ds(0, tq), :], sem).wait()


def moe_dispatch(xp, dest, zrow, n_rows):
    T, C = xp.shape
    tq = DISPATCH_TOKENS
    dest_blk = dest.reshape(T // tq, 1, TOP_K * tq)
    return pl.pallas_call(
        _dispatch_kernel,
        grid_spec=pltpu.PrefetchScalarGridSpec(
            num_scalar_prefetch=1,
            grid=(T // tq,),
            in_specs=[pl.BlockSpec((1, 1, TOP_K * tq), lambda i, zr: (i, 0, 0), memory_space=pltpu.SMEM),
                      pl.BlockSpec((tq, C), lambda i, zr: (i, 0))],
            out_specs=pl.BlockSpec(memory_space=pl.ANY),
            scratch_shapes=[pltpu.VMEM((ROW_BLOCK, C), xp.dtype), pltpu.SemaphoreType.DMA(()),
                            pltpu.SemaphoreType.DMA(())]),
        out_shape=jax.ShapeDtypeStruct((n_rows, C), xp.dtype),
        compiler_params=_cparams(("arbitrary",)),
        name="moe_dispatch",
    )(zrow, dest_blk, xp)


SCHED_NV, SCHED_SG, SCHED_SB, SCHED_SO, SCHED_FULL, SCHED_GE, SCHED_GF, SCHED_NG = range(8)


def _stream_weights(s, sched, copies, on_arrival):
    sg_ref = sched[SCHED_SG]
    g = sg_ref[s]
    first = (s < sched[SCHED_NV][0]) & ((s == 0) | (g != sg_ref[jnp.maximum(s - 1, 0)]))

    @pl.when(first)
    def _():
        @pl.when(s == 0)
        def _():
            for c in copies(g):
                c.start()

        for c in copies(g):
            c.wait()
        on_arrival()

        @pl.when(g + 1 < sched[SCHED_NG][0])
        def _():
            for c in copies(g + 1):
                c.start()


def _for_used_rows(s, sched, out_ref, compute):
    valid = s < sched[SCHED_NV][0]
    full = sched[SCHED_FULL][s] > 0

    @pl.when(jnp.logical_not(valid))
    def _():
        out_ref[...] = jnp.zeros_like(out_ref)

    @pl.when(valid & full)
    def _():
        compute(ROW_BLOCK)

    @pl.when(valid & jnp.logical_not(full))
    def _():
        compute(ROW_HALF)
        out_ref[ROW_HALF:, :] = jnp.zeros((ROW_BLOCK - ROW_HALF, out_ref.shape[1]), out_ref.dtype)


def _moe_up_kernel(*refs):
    sched = refs[:8]
    xs_ref, w_hbm, bg_ref, bu_ref, h_ref, wbuf, wgb, wub, sems = refs[8:]
    s = pl.program_id(0)
    tf = wgb.shape[1]
    n_ff = w_hbm.shape[2] // 2

    def copies(g):
        e = sched[SCHED_GE][g]
        col = pl.multiple_of(sched[SCHED_GF][g] * tf, tf)
        return [pltpu.make_async_copy(w_hbm.at[e, :, pl.ds(half * n_ff + col, tf)], wbuf.at[half], sems.at[half])
                for half in range(2)]

    def on_arrival():
        wgb[...] = wbuf[0].astype(BF16)
        wub[...] = wbuf[1].astype(BF16)

    _stream_weights(s, sched, copies, on_arrival)

    def compute(rows):
        lo, hi = _unpack_bf16_pairs(xs_ref[:rows, :])
        half = lo.shape[1]

        def proj(wb, b_ref):
            return (jnp.dot(lo, wb[:half, :], preferred_element_type=F32)
                    + jnp.dot(hi, wb[half:, :], preferred_element_type=F32) + b_ref[...])

        gate = jnp.minimum(proj(wgb, bg_ref), SWIGLU_LIMIT)
        up = jnp.clip(proj(wub, bu_ref), -SWIGLU_LIMIT, SWIGLU_LIMIT)
        h_ref[:rows, :] = ((up + 1.0) * gate * _sigmoid(SWIGLU_ALPHA * gate)).astype(h_ref.dtype)

    _for_used_rows(s, sched, h_ref, compute)


def _step_expert(s, r):
    return r[SCHED_GE][r[SCHED_SG][s]]


def _step_tile(s, r):
    return r[SCHED_GF][r[SCHED_SG][s]]


def moe_up(xs, w_gu, b_gu, sched):
    P = xs.shape[0]
    E, D, F2 = w_gu.shape
    F = F2 // 2
    tf = UP_TILE
    nf = F // tf
    b3 = b_gu.reshape(E, 1, F2)
    return pl.pallas_call(
        _moe_up_kernel,
        grid_spec=pltpu.PrefetchScalarGridSpec(
            num_scalar_prefetch=len(sched),
            grid=(sched[SCHED_SG].shape[0],),
            in_specs=[pl.BlockSpec((ROW_BLOCK, D // 2), lambda s, *r: (r[SCHED_SB][s], 0)),
                      pl.BlockSpec(memory_space=pl.ANY),
                      pl.BlockSpec((None, 1, tf), lambda s, *r: (_step_expert(s, r), 0, _step_tile(s, r))),
                      pl.BlockSpec((None, 1, tf), lambda s, *r: (_step_expert(s, r), 0, nf + _step_tile(s, r)))],
            out_specs=pl.BlockSpec((ROW_BLOCK, tf), lambda s, *r: (r[SCHED_SB][s], r[SCHED_SO][s])),
            scratch_shapes=[pltpu.VMEM((2, D, tf), F32), pltpu.VMEM((D, tf), BF16), pltpu.VMEM((D, tf), BF16),
                            pltpu.SemaphoreType.DMA((2,))]),
        out_shape=jax.ShapeDtypeStruct((P, F), BF16),
        compiler_params=_cparams(("arbitrary",)),
        name="moe_up",
    )(*sched, xs, w_gu, b3, b3)


def _moe_down_kernel(*refs):
    sched = refs[:8]
    h_ref, w_hbm, bd_ref, y_ref, wbuf, wdb, sem = refs[8:]
    s = pl.program_id(0)
    tn = wdb.shape[1]

    def copies(g):
        col = pl.multiple_of(sched[SCHED_GF][g] * tn, tn)
        return [pltpu.make_async_copy(w_hbm.at[sched[SCHED_GE][g], :, pl.ds(col, tn)], wbuf, sem)]

    def on_arrival():
        wdb[...] = wbuf[...].astype(BF16)

    _stream_weights(s, sched, copies, on_arrival)

    def compute(rows):
        y_ref[:rows, :] = jnp.dot(h_ref[:rows, :], wdb[...], preferred_element_type=F32) + bd_ref[...]

    _for_used_rows(s, sched, y_ref, compute)


def moe_down(h, w_down, b_down, sched):
    P, F = h.shape
    E, _, D = w_down.shape
    tn = DOWN_TILE
    b3 = b_down.reshape(E, 1, D)
    return pl.pallas_call(
        _moe_down_kernel,
        grid_spec=pltpu.PrefetchScalarGridSpec(
            num_scalar_prefetch=len(sched),
            grid=(sched[SCHED_SG].shape[0],),
            in_specs=[pl.BlockSpec((ROW_BLOCK, F), lambda s, *r: (r[SCHED_SB][s], 0)),
                      pl.BlockSpec(memory_space=pl.ANY),
                      pl.BlockSpec((None, 1, tn), lambda s, *r: (_step_expert(s, r), 0, _step_tile(s, r)))],
            out_specs=pl.BlockSpec((ROW_BLOCK, tn), lambda s, *r: (r[SCHED_SB][s], r[SCHED_SO][s])),
            scratch_shapes=[pltpu.VMEM((F, tn), F32), pltpu.VMEM((F, tn), BF16), pltpu.SemaphoreType.DMA(())]),
        out_shape=jax.ShapeDtypeStruct((P, D), F32),
        compiler_params=_cparams(("arbitrary",)),
        name="moe_down",
    )(*sched, h, w_down, b3)


def _combine_kernel(dcur_ref, dnxt_ref, x1_ref, w_ref, fw_ref, ys_hbm, o_ref, buf, sems):
    i = pl.program_id(0)
    n_steps = pl.num_programs(0)
    tq = x1_ref.shape[0]
    n = TOP_K * tq
    slot = lax.rem(i, 2)

    def issue(idx_ref, sl):
        def body(t, carry):
            _row_copy(ys_hbm, buf.at[sl], idx_ref[0, 0, t], t, sems.at[sl]).start()
            return carry

        lax.fori_loop(0, n, body, 0, unroll=DMA_UNROLL)

    pl.when(i == 0)(lambda: issue(dcur_ref, 0))
    pl.when(i + 1 < n_steps)(lambda: issue(dnxt_ref, 1 - slot))
    pltpu.make_async_copy(ys_hbm.at[pl.ds(0, n), :], buf.at[slot], sems.at[slot]).wait()

    acc = x1_ref[...]
    for j in range(TOP_K):
        acc = acc + w_ref[:, j:j + 1] * buf[slot, j * tq:(j + 1) * tq, :]
    o_ref[...] = acc * lax.rsqrt(jnp.mean(acc * acc, axis=-1, keepdims=True) + NORM_EPS) * fw_ref[...]


def combine(x1, ys, dest, top_w, final_w):
    T, D = x1.shape
    tq = COMBINE_TOKENS
    nblk = T // tq
    dest_blk = dest.reshape(nblk, tq, TOP_K).transpose(0, 2, 1).reshape(nblk, 1, TOP_K * tq)
    idx_spec = lambda f: pl.BlockSpec((1, 1, TOP_K * tq), f, memory_space=pltpu.SMEM)
    return pl.pallas_call(
        _combine_kernel,
        grid=(nblk,),
        in_specs=[idx_spec(lambda i: (i, 0, 0)),
                  idx_spec(lambda i: (jnp.minimum(i + 1, nblk - 1), 0, 0)),
                  pl.BlockSpec((tq, D), lambda i: (i, 0)),
                  pl.BlockSpec((tq, TOP_K), lambda i: (i, 0)),
                  pl.BlockSpec((1, D), lambda i: (0, 0)),
                  pl.BlockSpec(memory_space=pl.ANY)],
        out_specs=pl.BlockSpec((tq, D), lambda i: (i, 0)),
        out_shape=jax.ShapeDtypeStruct((T, D), F32),
        scratch_shapes=[pltpu.VMEM((2, TOP_K * tq, D), F32), pltpu.SemaphoreType.DMA((2,))],
        compiler_params=_cparams(("arbitrary",)),
        name="moe_combine",
    )(dest_blk, dest_blk, x1, top_w, final_w.reshape(1, D), ys)


def _routing(logits):
    T = logits.shape[0]
    TK = T * TOP_K
    NB = TK // ROW_BLOCK + N_EXPERTS
    top_logits, top_idx = lax.top_k(logits[:, :N_EXPERTS], TOP_K)
    top_w = jax.nn.softmax(top_logits, axis=-1)
    flat_e = top_idx.reshape(TK).astype(jnp.int32)
    onehot = (flat_e[:, None] == jnp.arange(N_EXPERTS, dtype=jnp.int32)[None, :]).astype(F32)
    oh = onehot.reshape(TK // RANK_GROUP, RANK_GROUP, N_EXPERTS)
    local = jnp.einsum("ts,gse->gte", jnp.tril(jnp.ones((RANK_GROUP, RANK_GROUP), F32)), oh)
    tot = local[:, -1, :]
    offs = jnp.cumsum(tot, axis=0) - tot
    rank = jnp.sum(oh * (local + offs[:, None, :]), axis=-1).reshape(TK).astype(jnp.int32) - 1
    counts = (offs[-1] + tot[-1]).astype(jnp.int32)
    padded = (counts + ROW_BLOCK - 1) // ROW_BLOCK * ROW_BLOCK
    pad_end = jnp.cumsum(padded).astype(jnp.int32)
    dest = ((pad_end - padded)[flat_e] + rank).astype(jnp.int32)
    block_start = jnp.arange(NB, dtype=jnp.int32) * ROW_BLOCK
    block_e = jnp.minimum(jnp.sum((pad_end[None, :] <= block_start[:, None]).astype(jnp.int32), axis=1),
                          N_EXPERTS - 1)
    nb_used = pad_end[-1] // ROW_BLOCK
    blk = jnp.arange(NB, dtype=jnp.int32)
    rows_left = counts[block_e] - (blk - ((pad_end - padded) // ROW_BLOCK)[block_e]) * ROW_BLOCK
    block_full = ((blk < nb_used) & (rows_left > ROW_HALF)).astype(jnp.int32)
    tail = nb_used + jnp.arange(N_EXPERTS, dtype=jnp.int32)
    zrow = jnp.concatenate([jnp.where(padded > 0, pad_end - ROW_BLOCK, -1),
                            jnp.where(tail < NB, tail * ROW_BLOCK, -1)]).astype(jnp.int32)
    return dest.reshape(T, TOP_K), top_w, block_e, block_full, padded, nb_used, zrow


def _schedule(block_e, block_full, padded, nb_used, n_tiles):
    nb = block_e.shape[0]
    b = jnp.tile(jnp.arange(nb, dtype=jnp.int32), n_tiles)
    f = jnp.repeat(jnp.arange(n_tiles, dtype=jnp.int32), nb)
    key = jnp.where(b < nb_used, (block_e[b] * n_tiles + f) * nb + b, (N_EXPERTS * n_tiles + f) * nb + b)
    order = jnp.argsort(key)
    used = padded > 0
    n_groups = (jnp.sum(used) * n_tiles).astype(jnp.int32)
    expert_pos = jnp.cumsum(used) - 1
    group = jnp.minimum(expert_pos[block_e[b]] * n_tiles + f, n_groups - 1).astype(jnp.int32)
    n_valid = (nb_used * n_tiles).astype(jnp.int32)
    sg = jnp.where(jnp.arange(nb * n_tiles) < n_valid, group[order], n_groups - 1)
    experts = jnp.arange(N_EXPERTS, dtype=jnp.int32)
    used_first = jnp.argsort(jnp.where(used, experts, N_EXPERTS + experts)).astype(jnp.int32)
    ge = jnp.repeat(used_first, n_tiles)
    gf = jnp.tile(jnp.arange(n_tiles, dtype=jnp.int32), N_EXPERTS)
    return (n_valid.reshape(1), sg, b[order], f[order], block_full[b][order], ge, gf, n_groups.reshape(1))


def _repack_w_in(w_in):
    w_in = w_in.astype(BF16)
    z = lambda n: jnp.zeros((w_in.shape[0], n), w_in.dtype)
    o = RWKV_COLS
    w_r = jnp.concatenate([w_in[:, :RW_WD], w_in[:, RW_WD:RW_WD + DECAY_LORA], z(LORA_PAD - DECAY_LORA),
                           w_in[:, RW_WD + DECAY_LORA:RW_WD + DECAY_LORA + AAA_LORA], z(LORA_PAD - AAA_LORA),
                           w_in[:, RW_WD + DECAY_LORA + AAA_LORA:o]], axis=1)
    w_m = jnp.concatenate([w_in[:, o:o + ML_IF], w_in[:, o + ML_IF:o + ML_IF + 2 * MLSTM_HEADS],
                           z(LANES - 2 * MLSTM_HEADS), w_in[:, o + ML_IF + 2 * MLSTM_HEADS:o + MLSTM_COLS]], axis=1)
    w_g = w_in[:, o + MLSTM_COLS:]
    return w_r, w_m, w_g


def kernel(x, norm1_w, w_in, b_gate, rwkv_mu, rwkv_w0, rwkv_w_up, rwkv_a0, rwkv_a_up, rwkv_g_up, rwkv_k_k,
           rwkv_k_a, rwkv_r_k, rwkv_ln_w, rwkv_ln_b, mlstm_conv_w, mlstm_i_b, mlstm_f_b, mlstm_norm_w,
           w_branch_a, w_branch_b, w_out, norm2_w, router_w, router_b, w_gu, b_gu, w_down, b_down,
           final_norm_w):
    B, S, D = x.shape
    T = B * S
    xt = x.reshape(T, D)
    assert norm1_w.shape[0] == 1, "single-layer block: the final rmsnorm is fused into the MoE combine"
    for l in range(1):
        hn = rmsnorm_rows(xt, norm1_w[l])
        w_r, w_m, w_g = _repack_w_in(w_in[l])
        p_r = matmul(hn, w_r, 512, RW_COLS_P // 2, F32, "proj_rwkv").reshape(B, S, RW_COLS_P)
        p_m = matmul(hn, w_m, 256, ML_COLS_P, F32, "proj_mlstm").reshape(B, S, ML_COLS_P)
        o_a = rwkv_branch(p_r, rwkv_mu[l], rwkv_w0[l], rwkv_w_up[l], rwkv_a0[l], rwkv_a_up[l], rwkv_g_up[l],
                          rwkv_k_k[l], rwkv_k_a[l], rwkv_r_k[l].reshape(-1), rwkv_ln_w[l], rwkv_ln_b[l])
        o_b = mlstm_branch(p_m, mlstm_conv_w[l], mlstm_i_b[l], mlstm_f_b[l], mlstm_norm_w[l])
        x1, xp, logits = merge_project(o_a.reshape(T, RWKV_DIM), o_b.reshape(T, MLSTM_DIM), hn, w_g, b_gate[l], xt,
                                       w_branch_a[l], w_branch_b[l], w_out[l], norm2_w[l], router_w[l],
                                       router_b[l])
        dest, top_w, block_e, block_full, padded, nb_used, zrow = _routing(logits)
        xs = moe_dispatch(xp, dest, zrow, block_e.shape[0] * ROW_BLOCK)
        assert EXPERT_FF // UP_TILE == D // DOWN_TILE, "up and down share one step order"
        sched = _schedule(block_e, block_full, padded, nb_used, EXPERT_FF // UP_TILE)
        h = moe_up(xs, w_gu[l], b_gu[l], sched)
        ys = moe_down(h, w_down[l], b_down[l], sched)
        xt = combine(x1, ys, dest, top_w, final_norm_w)
    return xt.reshape(B, S, D)
```

```python
import functools

import jax
import jax.numpy as jnp
import numpy as np
from jax import lax
from jax.experimental import pallas as pl
from jax.experimental.pallas import tpu as pltpu

F32 = jnp.float32
BF16 = jnp.bfloat16

D_MODEL = 2048
CHUNK = 64
NORM_EPS = 1e-6
RWKV_HEADS = 16
RWKV_HEAD_DIM = 64
RWKV_DIM = 1024
DECAY_LORA = 96
AAA_LORA = 96
GATE_LORA = 256
GN_EPS = 64e-5
RWKV_COLS = 3 * RWKV_DIM + DECAY_LORA + AAA_LORA + GATE_LORA
MLSTM_HEADS = 4
MLSTM_QK_DIM = 128
MLSTM_V_DIM = 256
MLSTM_QK = 512
MLSTM_DIM = 1024
CONV_WIDTH = 4
GATE_SOFTCAP = 15.0
MLSTM_COLS = 2 * MLSTM_QK + 2 * MLSTM_DIM + 2 * MLSTM_HEADS
N_EXPERTS = 32
TOP_K = 4
EXPERT_FF = 2048
SWIGLU_LIMIT = 7.0
SWIGLU_ALPHA = 1.702

LANES = 128
SUBLANES = 8
VMEM_LIMIT = 56 * 1024 * 1024

LORA_PAD = 128
RW_WD = 3 * RWKV_DIM
RW_AD = RW_WD + LORA_PAD
RW_GD = RW_AD + LORA_PAD
RW_COLS_P = RW_GD + GATE_LORA
ML_V = 2 * MLSTM_QK
ML_IF = ML_V + MLSTM_DIM
ML_O = ML_IF + LANES
ML_COLS_P = ML_O + MLSTM_DIM
ROUTER_PAD = 128
SCAN_SEQS = 4
MLSTM_SEQS = 1

ROW_BLOCK = 512
ROW_HALF = ROW_BLOCK // 2


def _cparams(sem):
    return pltpu.CompilerParams(dimension_semantics=sem, vmem_limit_bytes=VMEM_LIMIT)


def _bdot(a, b):
    return jnp.dot(a.astype(BF16), b.astype(BF16), preferred_element_type=F32)


def _split3(x):
    hi = x.astype(BF16)
    r1 = x - hi.astype(F32)
    mid = r1.astype(BF16)
    lo = (r1 - mid.astype(F32)).astype(BF16)
    return hi, mid, lo


def _dot_exact_lhs(mat_bf16, x):
    hi, mid, lo = _split3(x)
    return (jnp.dot(mat_bf16, hi, preferred_element_type=F32)
            + jnp.dot(mat_bf16, mid, preferred_element_type=F32)
            + jnp.dot(mat_bf16, lo, preferred_element_type=F32))


def _dot_exact_rhs(x, mat_bf16):
    hi, mid, lo = _split3(x)
    return (jnp.dot(hi, mat_bf16, preferred_element_type=F32)
            + jnp.dot(mid, mat_bf16, preferred_element_type=F32)
            + jnp.dot(lo, mat_bf16, preferred_element_type=F32))


def _sigmoid(x):
    return 1.0 / (1.0 + jnp.exp(-x))


def _softplus(x):
    return jnp.maximum(x, 0.0) + jnp.log(1.0 + jnp.exp(-jnp.abs(x)))


def _rmsnorm_kernel(x_ref, w_ref, o_ref):
    x = x_ref[...]
    y = x * lax.rsqrt(jnp.mean(x * x, axis=-1, keepdims=True) + NORM_EPS)
    o_ref[...] = (y * w_ref[...]).astype(o_ref.dtype)


def rmsnorm_rows(x, w, tm=512):
    T, D = x.shape
    return pl.pallas_call(
        _rmsnorm_kernel,
        grid=(T // tm,),
        in_specs=[pl.BlockSpec((tm, D), lambda i: (i, 0)),
                  pl.BlockSpec((1, D), lambda i: (0, 0))],
        out_specs=pl.BlockSpec((tm, D), lambda i: (i, 0)),
        out_shape=jax.ShapeDtypeStruct((T, D), BF16),
        compiler_params=_cparams(("parallel",)),
        name="rmsnorm1",
    )(x, w.reshape(1, D))


def _mm_kernel(a_ref, b_ref, o_ref):
    o_ref[...] = jnp.dot(a_ref[...], b_ref[...], preferred_element_type=F32).astype(o_ref.dtype)


def matmul(a, b, tm, tn, out_dtype, name):
    M, K = a.shape
    N = b.shape[1]
    return pl.pallas_call(
        _mm_kernel,
        grid=(N // tn, M // tm),
        in_specs=[pl.BlockSpec((tm, K), lambda j, i: (i, 0)),
                  pl.BlockSpec((K, tn), lambda j, i: (0, j))],
        out_specs=pl.BlockSpec((tm, tn), lambda j, i: (i, j)),
        out_shape=jax.ShapeDtypeStruct((M, N), out_dtype),
        compiler_params=_cparams(("parallel", "parallel")),
        name=name,
    )(a, b)


def _head_sum_mat():
    r = lax.broadcasted_iota(jnp.int32, (LANES, LANES), 0) // RWKV_HEAD_DIM
    c = lax.broadcasted_iota(jnp.int32, (LANES, LANES), 1) // RWKV_HEAD_DIM
    return jnp.where(r == c, 1.0, 0.0).astype(BF16)


def _rwkv_prep_kernel(p_ref, mu_ref, w0_ref, wup_ref, a0_ref, aup_ref, gup_ref, kk_ref, ka_ref,
                      r_out, k_out, v_out, kk_out, b_out, lw_out, g_out, carry_ref):
    i = pl.program_id(1)

    @pl.when(i == 0)
    def _():
        carry_ref[...] = jnp.zeros_like(carry_ref)

    p = p_ref[...]
    tq = p.shape[0]
    row = lax.broadcasted_iota(jnp.int32, (tq, 1), 0)
    prev = jnp.where(row == 0, carry_ref[0:1, :], pltpu.roll(p, 1, 0))
    carry_ref[0:1, :] = p[tq - 1:tq, :]
    p = p + (prev - p) * mu_ref[...]

    r = p[:, 0:RWKV_DIM]
    k = p[:, RWKV_DIM:2 * RWKV_DIM]
    v = p[:, 2 * RWKV_DIM:3 * RWKV_DIM]
    wd = p[:, RW_WD:RW_AD]
    ad = p[:, RW_AD:RW_GD]
    gd = p[:, RW_GD:RW_COLS_P]

    w_log = -_softplus(-(w0_ref[...] + _bdot(jnp.tanh(wd), wup_ref[...]))) - 0.5
    lw = -jnp.exp(w_log)
    a = _sigmoid(a0_ref[...] + _bdot(ad, aup_ref[...]))
    g = _bdot(_sigmoid(gd), gup_ref[...])

    kk = k * kk_ref[...]
    hs = _head_sum_mat()
    nrm2 = jnp.concatenate(
        [_dot_exact_rhs(kk[:, c:c + LANES] * kk[:, c:c + LANES], hs) for c in range(0, RWKV_DIM, LANES)],
        axis=1)
    kk = kk / jnp.maximum(jnp.sqrt(nrm2), 1e-12)
    k = k * (1.0 + (a - 1.0) * ka_ref[...])

    r_out[...] = r.astype(r_out.dtype)
    k_out[...] = k.astype(k_out.dtype)
    v_out[...] = v.astype(v_out.dtype)
    kk_out[...] = kk.astype(kk_out.dtype)
    b_out[...] = (kk * a).astype(b_out.dtype)
    lw_out[...] = lw
    g_out[...] = g.astype(g_out.dtype)


def rwkv_prep(p_rwkv, mu, w0, w_up, a0, a_up, g_up, k_k, k_a, tq=256):
    B, S, C = p_rwkv.shape
    blk = lambda n: pl.BlockSpec((None, tq, n), lambda b, i: (b, i, 0))
    full = lambda a: pl.BlockSpec(a.shape, lambda b, i: (0,) * a.ndim)
    params = (mu, w0, w_up, a0, a_up, g_up, k_k, k_a)
    out = lambda dt: jax.ShapeDtypeStruct((B, S, RWKV_DIM), dt)
    return pl.pallas_call(
        _rwkv_prep_kernel,
        grid=(B, S // tq),
        in_specs=[blk(C)] + [full(a) for a in params],
        out_specs=[blk(RWKV_DIM)] * 7,
        out_shape=[out(BF16)] * 5 + [out(F32), out(BF16)],
        scratch_shapes=[pltpu.VMEM((SUBLANES, C), F32)],
        compiler_params=_cparams(("parallel", "arbitrary")),
        name="rwkv_prep",
    )(p_rwkv, *params)


def _rwkv_scan_kernel(r_ref, k_ref, v_ref, kk_ref, b_ref, lw_ref, g_ref, rk_ref, lnw_ref, lnb_ref,
                      o_ref, h_ref):
    c = pl.program_id(1)

    @pl.when(c == 0)
    def _():
        h_ref[...] = jnp.zeros_like(h_ref)

    L = CHUNK
    L2 = 2 * L
    ri = lax.broadcasted_iota(jnp.int32, (L, L), 0)
    ci = lax.broadcasted_iota(jnp.int32, (L, L), 1)
    tril = jnp.where(ri >= ci, 1.0, 0.0).astype(BF16)

    n_seq = lw_ref.shape[0]
    cat = lambda ref: jnp.concatenate([ref[i] for i in range(n_seq)], axis=1)
    rep = lambda ref: jnp.concatenate([ref[...]] * n_seq, axis=1)
    lw = cat(lw_ref)
    cum = _dot_exact_lhs(tril, lw)
    cum_end = cum[L - 1:L, :]
    w_in = jnp.exp(cum)
    w_prev = jnp.exp(cum - lw)
    w_inv = jnp.exp(-cum)
    w_tail = jnp.exp(cum_end - cum)
    w_end = jnp.exp(cum_end)

    kk = cat(kk_ref).astype(F32)
    bb = cat(b_ref).astype(F32)
    kx = cat(k_ref).astype(F32)
    rx = cat(r_ref).astype(F32)
    vx = cat(v_ref).astype(F32)
    gx = cat(g_ref).astype(F32)
    a_hat = -kk * w_prev
    r_hat = rx * w_in
    b_hat = bb * w_inv
    k_hat = kx * w_inv
    b_til = bb * w_tail
    k_til = kx * w_tail
    rkk = rx * kx * rep(rk_ref)

    lane = lax.broadcasted_iota(jnp.int32, (1, LANES), 1)
    m_lo = jnp.where(lane < RWKV_HEAD_DIM, 1.0, 0.0)
    m_hi = 1.0 - m_lo

    def stack(x):
        return jnp.concatenate([x * m_lo, x * m_hi], axis=0)

    r2 = lax.broadcasted_iota(jnp.int32, (L2, L2), 0)
    c2 = lax.broadcasted_iota(jnp.int32, (L2, L2), 1)
    same_head = (r2 // L) == (c2 // L)
    strict = same_head & (r2 > c2)
    incl = same_head & (r2 >= c2)
    diag16 = (r2 // 16) == (c2 // 16)
    eye = jnp.where(r2 == c2, 1.0, 0.0)
    hs = _head_sum_mat()

    pairs_per_seq = RWKV_DIM // LANES
    pairs = range(n_seq * pairs_per_seq)
    sls = [slice(p * LANES, (p + 1) * LANES) for p in pairs]
    v_st = [stack(vx[:, sl]) for sl in sls]
    lhs = [jnp.concatenate([stack(a_hat[:, sl]), stack(r_hat[:, sl])], axis=0).astype(BF16) for sl in sls]
    rhs = [jnp.concatenate([stack(b_hat[:, sl]), stack(k_hat[:, sl])], axis=0).astype(BF16) for sl in sls]
    sc = [lax.dot_general(lhs[p], rhs[p], (((1,), (1,)), ((), ())), preferred_element_type=F32) for p in pairs]
    n_ab = [jnp.where(strict, sc[p][:L2, :L2], 0.0) for p in pairs]
    a_ak = [jnp.where(strict, sc[p][:L2, L2:], 0.0).astype(BF16) for p in pairs]
    a_r = [jnp.concatenate([jnp.where(incl, sc[p][L2:, :L2], 0.0), jnp.where(incl, sc[p][L2:, L2:], 0.0)],
                           axis=1).astype(BF16) for p in pairs]

    nd = [jnp.where(diag16, n_ab[p], 0.0) for p in pairs]
    noff = [(n_ab[p] - nd[p]).astype(BF16) for p in pairs]
    ndb = [nd[p].astype(BF16) for p in pairs]
    s2 = [jnp.dot(ndb[p], ndb[p], preferred_element_type=F32).astype(BF16) for p in pairs]
    s4 = [jnp.dot(s2[p], s2[p], preferred_element_type=F32).astype(BF16) for p in pairs]
    s8 = [jnp.dot(s4[p], s4[p], preferred_element_type=F32).astype(BF16) for p in pairs]
    x1 = [eye + nd[p] for p in pairs]
    x2 = [x1[p] + _bdot(x1[p], s2[p]) for p in pairs]
    x3 = [x2[p] + _bdot(x2[p], s4[p]) for p in pairs]
    t_d = [(x3[p] + _bdot(x3[p], s8[p])).astype(BF16) for p in pairs]
    m1 = [jnp.dot(t_d[p], noff[p], preferred_element_type=F32) for p in pairs]
    m1b = [m1[p].astype(BF16) for p in pairs]
    m2 = [jnp.dot(m1b[p], m1b[p], preferred_element_type=F32) for p in pairs]
    m3 = [jnp.dot(m1b[p], m2[p].astype(BF16), preferred_element_type=F32) for p in pairs]
    t_inv = [jnp.dot((eye + m1[p] + m2[p] + m3[p]).astype(BF16), t_d[p], preferred_element_type=F32).astype(BF16)
             for p in pairs]

    h0 = [h_ref[p] for p in pairs]
    ah = [jnp.dot(lhs[p], h0[p].astype(BF16), preferred_element_type=F32) for p in pairs]
    x = [ah[p][:L2] + jnp.dot(a_ak[p], v_st[p].astype(BF16), preferred_element_type=F32) for p in pairs]
    u = [jnp.dot(t_inv[p], x[p].astype(BF16), preferred_element_type=F32) for p in pairs]
    uv = [jnp.concatenate([u[p], v_st[p]], axis=0).astype(BF16) for p in pairs]
    y_st = [ah[p][L2:] + jnp.dot(a_r[p], uv[p], preferred_element_type=F32) for p in pairs]
    y = [y_st[p][:L] + y_st[p][L:] for p in pairs]

    for p in pairs:
        sl = sls[p]
        upd_l = jnp.concatenate([stack(b_til[:, sl]), stack(k_til[:, sl])], axis=0).astype(BF16)
        upd = lax.dot_general(upd_l, uv[p], (((0,), (0,)), ((), ())), preferred_element_type=F32)
        w_col = jnp.sum(eye * w_end[:, sl], axis=1, keepdims=True)
        h_ref[p] = w_col * h0[p] + upd

    def head_sums(vals):
        parts = []
        for t in vals:
            hi = t.astype(BF16)
            parts += [hi, (t - hi.astype(F32)).astype(BF16)]
        res = jnp.dot(jnp.concatenate(parts, axis=0), hs, preferred_element_type=F32)
        return [res[2 * i * L:(2 * i + 1) * L] + res[(2 * i + 1) * L:(2 * i + 2) * L] for i in range(len(vals))]

    sums1 = [head_sums([y[p], rkk[:, sls[p]]]) for p in pairs]
    d = [y[p] - sums1[p][0] * (1.0 / RWKV_HEAD_DIM) for p in pairs]
    var = [head_sums([d[p] * d[p]])[0] * (1.0 / RWKV_HEAD_DIM) for p in pairs]
    for p in pairs:
        sl = sls[p]
        psl = sls[p % pairs_per_seq]
        yn = d[p] * lax.rsqrt(var[p] + GN_EPS) * lnw_ref[:, psl] + lnb_ref[:, psl]
        o_ref[p // pairs_per_seq, :, psl] = ((yn + sums1[p][1] * vx[:, sl]) * gx[:, sl]).astype(o_ref.dtype)


def rwkv_scan(r, k, v, kk, b, lw, g, r_k, ln_w, ln_b):
    B, S, C = r.shape
    blk = pl.BlockSpec((SCAN_SEQS, CHUNK, C), lambda bb, c: (bb, c, 0))
    full = pl.BlockSpec((1, C), lambda bb, c: (0, 0))
    return pl.pallas_call(
        _rwkv_scan_kernel,
        grid=(B // SCAN_SEQS, S // CHUNK),
        in_specs=[blk] * 7 + [full] * 3,
        out_specs=blk,
        out_shape=jax.ShapeDtypeStruct((B, S, C), BF16),
        scratch_shapes=[pltpu.VMEM((SCAN_SEQS * C // LANES, LANES, LANES), F32)],
        compiler_params=_cparams(("parallel", "arbitrary")),
        name="rwkv_scan",
    )(r, k, v, kk, b, lw, g, r_k, ln_w, ln_b)


def _pad_rows(w, n):
    return jnp.pad(w, ((0, n - w.shape[0]), (0, 0)))


def rwkv_branch(p_pad, mu, w0, w_up, a0, a_up, g_up, k_k, k_a, r_k, ln_w, ln_b):
    row = lambda t: t.reshape(1, -1)
    mu_p = jnp.concatenate([mu[:RW_WD], jnp.pad(mu[RW_WD:RW_WD + DECAY_LORA], (0, LORA_PAD - DECAY_LORA)),
                            jnp.pad(mu[RW_WD + DECAY_LORA:RW_WD + DECAY_LORA + AAA_LORA], (0, LORA_PAD - AAA_LORA)),
                            mu[RW_WD + DECAY_LORA + AAA_LORA:]])
    outs = rwkv_prep(p_pad, row(mu_p), row(w0), _pad_rows(w_up, LORA_PAD).astype(BF16), row(a0),
                     _pad_rows(a_up, LORA_PAD).astype(BF16), g_up.astype(BF16), row(k_k), row(k_a))
    return rwkv_scan(*outs, row(r_k), row(ln_w), row(ln_b))


def _mlstm_kernel(p_ref, convw_ref, ifb_ref, nw_ref, o_ref, carry_ref, c_ref, n_ref, m_ref):
    ci = pl.program_id(1)

    @pl.when(ci == 0)
    def _():
        carry_ref[...] = jnp.zeros_like(carry_ref)
        c_ref[...] = jnp.zeros_like(c_ref)
        n_ref[...] = jnp.zeros_like(n_ref)
        m_ref[...] = jnp.zeros_like(m_ref)

    L = CHUNK
    n_seq = p_ref.shape[0]
    lane = lax.broadcasted_iota(jnp.int32, (1, LANES), 1)
    ri = lax.broadcasted_iota(jnp.int32, (L, L), 0)
    cj = lax.broadcasted_iota(jnp.int32, (L, L), 1)
    causal = ri >= cj
    tril = jnp.where(causal, 1.0, 0.0).astype(BF16)
    triu = jnp.where(ri <= cj, 1.0, 0.0).astype(BF16)

    qk, comb, comb_t, b_col, b_row = [], [], [], [], []
    for i in range(n_seq):
        u = p_ref[i, :, 0:ML_V]
        ext = jnp.concatenate([carry_ref[i], u], axis=0)
        carry_ref[i] = u[L - SUBLANES:L, :]
        conv = convw_ref[CONV_WIDTH - 1:CONV_WIDTH, :] * u
        for j in range(1, CONV_WIDTH):
            conv = conv + convw_ref[CONV_WIDTH - 1 - j:CONV_WIDTH - j, :] * pltpu.roll(ext, j, 0)[SUBLANES:, :]
        qk.append(conv * _sigmoid(conv))
        pre = GATE_SOFTCAP * jnp.tanh((p_ref[i, :, ML_IF:ML_O] + ifb_ref[...]) * (1.0 / GATE_SOFTCAP))
        cb = jnp.where(lane < MLSTM_HEADS, pre, -_softplus(-pre))
        comb.append(cb)
        comb_t.append(cb.T)
        b_col.append(_dot_exact_lhs(tril, cb))
        b_row.append(_dot_exact_rhs(comb_t[i], triu))

    H = range(n_seq * MLSTM_HEADS)
    sq = [v // MLSTM_HEADS for v in H]
    hd = [v % MLSTM_HEADS for v in H]
    dk, dv = MLSTM_QK_DIM, MLSTM_V_DIM
    qh = [qk[sq[v]][:, hd[v] * dk:(hd[v] + 1) * dk] * (dk ** -0.5) for v in H]
    kh = [qk[sq[v]][:, MLSTM_QK + hd[v] * dk:MLSTM_QK + (hd[v] + 1) * dk] for v in H]
    vh = [p_ref[sq[v], :, ML_V + hd[v] * dv:ML_V + (hd[v] + 1) * dv].astype(BF16) for v in H]
    qb = [qh[h].astype(BF16) for h in H]
    bcol = [b_col[sq[v]][:, MLSTM_HEADS + hd[v]:MLSTM_HEADS + hd[v] + 1] for v in H]
    brow = [b_row[sq[v]][MLSTM_HEADS + hd[v]:MLSTM_HEADS + hd[v] + 1, :] for v in H]
    m_prev = [m_ref[h][0:1, 0:1] for h in H]
    n_prev = [n_ref[h][0:1, :] for h in H]
    c_prev = [c_ref[h] for h in H]

    qk_t = [lax.dot_general(qb[h], kh[h].astype(BF16), (((1,), (1,)), ((), ())), preferred_element_type=F32)
            for h in H]
    qc = [jnp.dot(qb[h], c_prev[h].astype(BF16), preferred_element_type=F32) for h in H]
    dm = [jnp.where(causal, bcol[h] - brow[h] + comb_t[sq[h]][hd[h]:hd[h] + 1, :], -jnp.inf) for h in H]
    inter = [bcol[h] + m_prev[h] for h in H]
    m_t = [jnp.maximum(inter[h], jnp.max(dm[h], axis=-1, keepdims=True)) for h in H]
    s = [qk_t[h] * jnp.exp(dm[h] - m_t[h]) for h in H]
    w_inter = [jnp.exp(inter[h] - m_t[h]) for h in H]
    num = [jnp.dot(s[h].astype(BF16), vh[h], preferred_element_type=F32) + w_inter[h] * qc[h] for h in H]
    den = [jnp.sum(s[h], axis=-1, keepdims=True) + w_inter[h] * jnp.sum(qh[h] * n_prev[h], axis=-1, keepdims=True)
           for h in H]
    hh = [num[h] / jnp.maximum(jnp.abs(den[h]), jnp.exp(-m_t[h])) for h in H]

    g_tot = [bcol[h][L - 1:L, :] for h in H]
    a = [comb[sq[h]][:, hd[h]:hd[h] + 1] + g_tot[h] - bcol[h] for h in H]
    m_new = [jnp.maximum(g_tot[h] + m_prev[h], jnp.max(a[h], axis=0, keepdims=True)) for h in H]
    dec = [jnp.exp(g_tot[h] + m_prev[h] - m_new[h]) for h in H]
    wkk = [jnp.exp(a[h] - m_new[h]) * kh[h] for h in H]
    for h in H:
        c_ref[h] = dec[h] * c_prev[h] + lax.dot_general(wkk[h].astype(BF16), vh[h], (((0,), (0,)), ((), ())),
                                                        preferred_element_type=F32)
        n_ref[h] = jnp.broadcast_to(dec[h] * n_prev[h] + jnp.sum(wkk[h], axis=0, keepdims=True),
                                    (SUBLANES, LANES))
        m_ref[h] = jnp.broadcast_to(m_new[h], (SUBLANES, LANES))
    for h in H:
        vs = slice(hd[h] * dv, (hd[h] + 1) * dv)
        hn = hh[h] * lax.rsqrt(jnp.mean(hh[h] * hh[h], axis=-1, keepdims=True) + NORM_EPS)
        o_raw = p_ref[sq[h], :, ML_O + hd[h] * dv:ML_O + (hd[h] + 1) * dv]
        o_ref[sq[h], :, vs] = (hn * nw_ref[:, vs] * _sigmoid(o_raw)).astype(o_ref.dtype)


def mlstm_branch(p_pad, conv_w, i_b, f_b, norm_w):
    B, S, C = p_pad.shape
    ifb = jnp.pad(jnp.concatenate([i_b, f_b]), (0, LANES - 2 * MLSTM_HEADS)).reshape(1, LANES)
    full = lambda a: pl.BlockSpec(a.shape, lambda b, c: (0,) * a.ndim)
    nw = norm_w.reshape(1, MLSTM_DIM)
    return pl.pallas_call(
        _mlstm_kernel,
        grid=(B // MLSTM_SEQS, S // CHUNK),
        in_specs=[pl.BlockSpec((MLSTM_SEQS, CHUNK, C), lambda b, c: (b, c, 0)), full(conv_w), full(ifb), full(nw)],
        out_specs=pl.BlockSpec((MLSTM_SEQS, CHUNK, MLSTM_DIM), lambda b, c: (b, c, 0)),
        out_shape=jax.ShapeDtypeStruct((B, S, MLSTM_DIM), BF16),
        scratch_shapes=[pltpu.VMEM((MLSTM_SEQS, SUBLANES, ML_V), F32),
                        pltpu.VMEM((MLSTM_SEQS * MLSTM_HEADS, MLSTM_QK_DIM, MLSTM_V_DIM), F32),
                        pltpu.VMEM((MLSTM_SEQS * MLSTM_HEADS, SUBLANES, LANES), F32),
                        pltpu.VMEM((MLSTM_SEQS * MLSTM_HEADS, SUBLANES, LANES), F32)],
        compiler_params=_cparams(("parallel", "arbitrary")),
        name="mlstm_scan",
    )(p_pad, conv_w, ifb, nw)


HI_MASK = 0xFFFF0000


def _pack_bf16_pairs(hb_f32):
    c = hb_f32.shape[1] // 2
    u = pltpu.bitcast(hb_f32, jnp.uint32)
    return u[:, c:] | (u[:, :c] >> 16)


def _unpack_bf16_pairs(xu):
    lo = pltpu.bitcast(xu << 16, F32).astype(BF16)
    hi = pltpu.bitcast(xu & jnp.uint32(HI_MASK), F32).astype(BF16)
    return lo, hi


def _merge_kernel(oa_ref, ob_ref, hn_ref, bg_ref, x_ref, wa_ref, wb_ref, wg_ref, wo_ref, n2_ref, rwh_ref, rwl_ref,
                  rb_ref, x1_ref, xp_ref, lg_ref):
    hn1 = hn_ref[...]

    def gated(o_ref, w_ref, cols):
        gate = _sigmoid(jnp.dot(hn1, wg_ref[:, cols], preferred_element_type=F32) + bg_ref[:, cols])
        return gate * jnp.dot(o_ref[...], w_ref[...], preferred_element_type=F32)

    merged = gated(oa_ref, wa_ref, slice(0, D_MODEL)) + gated(ob_ref, wb_ref, slice(D_MODEL, 2 * D_MODEL))
    x1 = x_ref[...] + jnp.dot(merged.astype(BF16), wo_ref[...], preferred_element_type=F32)
    x1_ref[...] = x1
    hn = x1 * lax.rsqrt(jnp.mean(x1 * x1, axis=-1, keepdims=True) + NORM_EPS) * n2_ref[...]
    hi = hn.astype(BF16)
    xp_ref[...] = _pack_bf16_pairs(hi.astype(F32))
    lo = (hn - hi.astype(F32)).astype(BF16)
    lg_ref[...] = (jnp.dot(hi, rwh_ref[...], preferred_element_type=F32)
                   + jnp.dot(hi, rwl_ref[...], preferred_element_type=F32)
                   + jnp.dot(lo, rwh_ref[...], preferred_element_type=F32) + rb_ref[...])


def merge_project(o_a, o_b, hn, w_gate, b_gate, x, w_a, w_b, w_out, norm2_w, router_w, router_b, tm=256):
    T, D = x.shape
    rw = jnp.pad(router_w, ((0, 0), (0, ROUTER_PAD - N_EXPERTS)))
    rw_hi = rw.astype(BF16)
    rw_lo = (rw - rw_hi.astype(F32)).astype(BF16)
    rb = jnp.pad(router_b, (0, ROUTER_PAD - N_EXPERTS)).reshape(1, ROUTER_PAD)
    rows = lambda n: pl.BlockSpec((tm, n), lambda i: (i, 0))
    full = lambda a: pl.BlockSpec(a.shape, lambda i: (0, 0), pipeline_mode=pl.Buffered(1))
    params = (w_a.astype(BF16), w_b.astype(BF16), w_gate, w_out.astype(BF16), norm2_w.reshape(1, D), rw_hi, rw_lo,
              rb)
    bg = b_gate.reshape(1, 2 * D)
    return pl.pallas_call(
        _merge_kernel,
        grid=(T // tm,),
        in_specs=[rows(o_a.shape[1]), rows(o_b.shape[1]), rows(D), full(bg), rows(D)] + [full(a) for a in params],
        out_specs=[rows(D), rows(D // 2), rows(ROUTER_PAD)],
        out_shape=[jax.ShapeDtypeStruct((T, D), F32), jax.ShapeDtypeStruct((T, D // 2), jnp.uint32),
                   jax.ShapeDtypeStruct((T, ROUTER_PAD), F32)],
        compiler_params=_cparams(("parallel",)),
        name="merge_project",
    )(o_a, o_b, hn, bg, x, *params)


UP_TILE = 1024
DOWN_TILE = 2048
RANK_GROUP = 256
DISPATCH_TOKENS = 256
COMBINE_TOKENS = 128
DMA_UNROLL = 8


def _row_copy(src, dst, src_row, dst_row, sem):
    return pltpu.make_async_copy(src.at[pl.ds(src_row, 1), :], dst.at[pl.ds(dst_row, 1), :], sem)


def _dispatch_kernel(zrow_ref, dest_ref, xp_ref, xs_hbm, zbuf, zsem, sem):
    i = pl.program_id(0)
    tq = dest_ref.shape[2] // TOP_K

    def zero_copy(k):
        row = pl.multiple_of(zrow_ref[k], ROW_BLOCK)
        return pltpu.make_async_copy(zbuf, xs_hbm.at[pl.ds(row, ROW_BLOCK), :], zsem)

    @pl.when(i == 0)
    def _():
        zbuf[...] = jnp.zeros_like(zbuf)
        for k in range(zrow_ref.shape[0]):
            pl.when(zrow_ref[k] >= 0)(lambda k=k: zero_copy(k).start())
        for k in range(zrow_ref.shape[0]):
            pl.when(zrow_ref[k] >= 0)(lambda k=k: zero_copy(k).wait())

    def body(t, carry):
        for j in range(TOP_K):
            _row_copy(xp_ref, xs_hbm, t, dest_ref[0, 0, TOP_K * t + j], sem).start()
        return carry

    lax.fori_loop(0, tq, body, 0, unroll=DMA_UNROLL // TOP_K)
    for j in range(TOP_K):
        pltpu.make_async_copy(xp_ref, xs_hbm.at[pl.ds(0, tq), :], sem).wait()


def moe_dispatch(xp, dest, zrow, n_rows):
    T, C = xp.shape
    tq = DISPATCH_TOKENS
    dest_blk = dest.reshape(T // tq, 1, TOP_K * tq)
    return pl.pallas_call(
        _dispatch_kernel,
        grid_spec=pltpu.PrefetchScalarGridSpec(
            num_scalar_prefetch=1,
            grid=(T // tq,),
            in_specs=[pl.BlockSpec((1, 1, TOP_K * tq), lambda i, zr: (i, 0, 0), memory_space=pltpu.SMEM),
                      pl.BlockSpec((tq, C), lambda i, zr: (i, 0))],
            out_specs=pl.BlockSpec(memory_space=pl.ANY),
            scratch_shapes=[pltpu.VMEM((ROW_BLOCK, C), xp.dtype), pltpu.SemaphoreType.DMA(()),
                            pltpu.SemaphoreType.DMA(())]),
        out_shape=jax.ShapeDtypeStruct((n_rows, C), xp.dtype),
        compiler_params=_cparams(("arbitrary",)),
        name="moe_dispatch",
    )(zrow, dest_blk, xp)


SCHED_NV, SCHED_SG, SCHED_SB, SCHED_SO, SCHED_FULL, SCHED_GE, SCHED_GF, SCHED_NG = range(8)


def _stream_weights(s, sched, copies, on_arrival):
    sg_ref = sched[SCHED_SG]
    g = sg_ref[s]
    first = (s < sched[SCHED_NV][0]) & ((s == 0) | (g != sg_ref[jnp.maximum(s - 1, 0)]))

    @pl.when(first)
    def _():
        @pl.when(s == 0)
        def _():
            for c in copies(g):
                c.start()

        for c in copies(g):
            c.wait()
        on_arrival()

        @pl.when(g + 1 < sched[SCHED_NG][0])
        def _():
            for c in copies(g + 1):
                c.start()


def _for_used_rows(s, sched, out_ref, compute):
    valid = s < sched[SCHED_NV][0]
    full = sched[SCHED_FULL][s] > 0

    @pl.when(jnp.logical_not(valid))
    def _():
        out_ref[...] = jnp.zeros_like(out_ref)

    @pl.when(valid & full)
    def _():
        compute(ROW_BLOCK)

    @pl.when(valid & jnp.logical_not(full))
    def _():
        compute(ROW_HALF)
        out_ref[ROW_HALF:, :] = jnp.zeros((ROW_BLOCK - ROW_HALF, out_ref.shape[1]), out_ref.dtype)


def _moe_up_kernel(*refs):
    sched = refs[:8]
    xs_ref, w_hbm, bg_ref, bu_ref, h_ref, wbuf, wgb, wub, sems = refs[8:]
    s = pl.program_id(0)
    tf = wgb.shape[1]
    n_ff = w_hbm.shape[2] // 2

    def copies(g):
        e = sched[SCHED_GE][g]
        col = pl.multiple_of(sched[SCHED_GF][g] * tf, tf)
        return [pltpu.make_async_copy(w_hbm.at[e, :, pl.ds(half * n_ff + col, tf)], wbuf.at[half], sems.at[half])
                for half in range(2)]

    def on_arrival():
        wgb[...] = wbuf[0].astype(BF16)
        wub[...] = wbuf[1].astype(BF16)

    _stream_weights(s, sched, copies, on_arrival)

    def compute(rows):
        lo, hi = _unpack_bf16_pairs(xs_ref[:rows, :])
        half = lo.shape[1]

        def proj(wb, b_ref):
            return (jnp.dot(lo, wb[:half, :], preferred_element_type=F32)
                    + jnp.dot(hi, wb[half:, :], preferred_element_type=F32) + b_ref[...])

        gate = jnp.minimum(proj(wgb, bg_ref), SWIGLU_LIMIT)
        up = jnp.clip(proj(wub, bu_ref), -SWIGLU_LIMIT, SWIGLU_LIMIT)
        h_ref[:rows, :] = ((up + 1.0) * gate * _sigmoid(SWIGLU_ALPHA * gate)).astype(h_ref.dtype)

    _for_used_rows(s, sched, h_ref, compute)


def _step_expert(s, r):
    return r[SCHED_GE][r[SCHED_SG][s]]


def _step_tile(s, r):
    return r[SCHED_GF][r[SCHED_SG][s]]


def moe_up(xs, w_gu, b_gu, sched):
    P = xs.shape[0]
    E, D, F2 = w_gu.shape
    F = F2 // 2
    tf = UP_TILE
    nf = F // tf
    b3 = b_gu.reshape(E, 1, F2)
    return pl.pallas_call(
        _moe_up_kernel,
        grid_spec=pltpu.PrefetchScalarGridSpec(
            num_scalar_prefetch=len(sched),
            grid=(sched[SCHED_SG].shape[0],),
            in_specs=[pl.BlockSpec((ROW_BLOCK, D // 2), lambda s, *r: (r[SCHED_SB][s], 0)),
                      pl.BlockSpec(memory_space=pl.ANY),
                      pl.BlockSpec((None, 1, tf), lambda s, *r: (_step_expert(s, r), 0, _step_tile(s, r))),
                      pl.BlockSpec((None, 1, tf), lambda s, *r: (_step_expert(s, r), 0, nf + _step_tile(s, r)))],
            out_specs=pl.BlockSpec((ROW_BLOCK, tf), lambda s, *r: (r[SCHED_SB][s], r[SCHED_SO][s])),
            scratch_shapes=[pltpu.VMEM((2, D, tf), F32), pltpu.VMEM((D, tf), BF16), pltpu.VMEM((D, tf), BF16),
                            pltpu.SemaphoreType.DMA((2,))]),
        out_shape=jax.ShapeDtypeStruct((P, F), BF16),
        compiler_params=_cparams(("arbitrary",)),
        name="moe_up",
    )(*sched, xs, w_gu, b3, b3)


def _moe_down_kernel(*refs):
    sched = refs[:8]
    h_ref, w_hbm, bd_ref, y_ref, wbuf, wdb, sem = refs[8:]
    s = pl.program_id(0)
    tn = wdb.shape[1]

    def copies(g):
        col = pl.multiple_of(sched[SCHED_GF][g] * tn, tn)
        return [pltpu.make_async_copy(w_hbm.at[sched[SCHED_GE][g], :, pl.ds(col, tn)], wbuf, sem)]

    def on_arrival():
        wdb[...] = wbuf[...].astype(BF16)

    _stream_weights(s, sched, copies, on_arrival)

    def compute(rows):
        y = jnp.dot(h_ref[:rows, :], wdb[...], preferred_element_type=F32) + bd_ref[...]
        y_ref[:rows, :] = _pack_bf16_pairs(y.astype(BF16).astype(F32))

    _for_used_rows(s, sched, y_ref, compute)


def moe_down(h, w_down, b_down, sched):
    P, F = h.shape
    E, _, D = w_down.shape
    tn = DOWN_TILE
    assert tn == D, "the packed output pairs column c with column c + D/2"
    b3 = b_down.reshape(E, 1, D)
    return pl.pallas_call(
        _moe_down_kernel,
        grid_spec=pltpu.PrefetchScalarGridSpec(
            num_scalar_prefetch=len(sched),
            grid=(sched[SCHED_SG].shape[0],),
            in_specs=[pl.BlockSpec((ROW_BLOCK, F), lambda s, *r: (r[SCHED_SB][s], 0)),
                      pl.BlockSpec(memory_space=pl.ANY),
                      pl.BlockSpec((None, 1, tn), lambda s, *r: (_step_expert(s, r), 0, _step_tile(s, r)))],
            out_specs=pl.BlockSpec((ROW_BLOCK, tn // 2), lambda s, *r: (r[SCHED_SB][s], r[SCHED_SO][s])),
            scratch_shapes=[pltpu.VMEM((F, tn), F32), pltpu.VMEM((F, tn), BF16), pltpu.SemaphoreType.DMA(())]),
        out_shape=jax.ShapeDtypeStruct((P, D // 2), jnp.uint32),
        compiler_params=_cparams(("arbitrary",)),
        name="moe_down",
    )(*sched, h, w_down, b3)


def _combine_kernel(dcur_ref, dnxt_ref, x1_ref, w_ref, fw_ref, ys_hbm, o_ref, buf, sems):
    i = pl.program_id(0)
    n_steps = pl.num_programs(0)
    tq = x1_ref.shape[0]
    n = TOP_K * tq
    slot = lax.rem(i, 2)

    def issue(idx_ref, sl):
        def body(t, carry):
            _row_copy(ys_hbm, buf.at[sl], idx_ref[0, 0, t], t, sems.at[sl]).start()
            return carry

        lax.fori_loop(0, n, body, 0, unroll=DMA_UNROLL)

    pl.when(i == 0)(lambda: issue(dcur_ref, 0))
    pl.when(i + 1 < n_steps)(lambda: issue(dnxt_ref, 1 - slot))
    pltpu.make_async_copy(ys_hbm.at[pl.ds(0, n), :], buf.at[slot], sems.at[slot]).wait()

    half = buf.shape[2]
    acc_lo = x1_ref[:, :half]
    acc_hi = x1_ref[:, half:]
    for j in range(TOP_K):
        lo, hi = _unpack_bf16_pairs(buf[slot, j * tq:(j + 1) * tq, :])
        acc_lo = acc_lo + w_ref[:, j:j + 1] * lo.astype(F32)
        acc_hi = acc_hi + w_ref[:, j:j + 1] * hi.astype(F32)
    ssq = jnp.sum(acc_lo * acc_lo, axis=-1, keepdims=True) + jnp.sum(acc_hi * acc_hi, axis=-1, keepdims=True)
    scale = lax.rsqrt(ssq * (1.0 / (2 * half)) + NORM_EPS)
    o_ref[:, :half] = acc_lo * scale * fw_ref[:, :half]
    o_ref[:, half:] = acc_hi * scale * fw_ref[:, half:]


def combine(x1, ys, dest, top_w, final_w):
    T, D = x1.shape
    tq = COMBINE_TOKENS
    nblk = T // tq
    dest_blk = dest.reshape(nblk, tq, TOP_K).transpose(0, 2, 1).reshape(nblk, 1, TOP_K * tq)
    idx_spec = lambda f: pl.BlockSpec((1, 1, TOP_K * tq), f, memory_space=pltpu.SMEM)
    return pl.pallas_call(
        _combine_kernel,
        grid=(nblk,),
        in_specs=[idx_spec(lambda i: (i, 0, 0)),
                  idx_spec(lambda i: (jnp.minimum(i + 1, nblk - 1), 0, 0)),
                  pl.BlockSpec((tq, D), lambda i: (i, 0)),
                  pl.BlockSpec((tq, TOP_K), lambda i: (i, 0)),
                  pl.BlockSpec((1, D), lambda i: (0, 0)),
                  pl.BlockSpec(memory_space=pl.ANY)],
        out_specs=pl.BlockSpec((tq, D), lambda i: (i, 0)),
        out_shape=jax.ShapeDtypeStruct((T, D), F32),
        scratch_shapes=[pltpu.VMEM((2, TOP_K * tq, D // 2), ys.dtype), pltpu.SemaphoreType.DMA((2,))],
        compiler_params=_cparams(("arbitrary",)),
        name="moe_combine",
    )(dest_blk, dest_blk, x1, top_w, final_w.reshape(1, D), ys)


def _routing(logits):
    T = logits.shape[0]
    TK = T * TOP_K
    NB = TK // ROW_BLOCK + N_EXPERTS
    top_logits, top_idx = lax.top_k(logits[:, :N_EXPERTS], TOP_K)
    top_w = jax.nn.softmax(top_logits, axis=-1)
    flat_e = top_idx.reshape(TK).astype(jnp.int32)
    onehot = (flat_e[:, None] == jnp.arange(N_EXPERTS, dtype=jnp.int32)[None, :]).astype(F32)
    oh = onehot.reshape(TK // RANK_GROUP, RANK_GROUP, N_EXPERTS)
    local = jnp.einsum("ts,gse->gte", jnp.tril(jnp.ones((RANK_GROUP, RANK_GROUP), F32)), oh)
    tot = local[:, -1, :]
    offs = jnp.cumsum(tot, axis=0) - tot
    rank = jnp.sum(oh * (local + offs[:, None, :]), axis=-1).reshape(TK).astype(jnp.int32) - 1
    counts = (offs[-1] + tot[-1]).astype(jnp.int32)
    padded = (counts + ROW_BLOCK - 1) // ROW_BLOCK * ROW_BLOCK
    pad_end = jnp.cumsum(padded).astype(jnp.int32)
    dest = ((pad_end - padded)[flat_e] + rank).astype(jnp.int32)
    block_start = jnp.arange(NB, dtype=jnp.int32) * ROW_BLOCK
    block_e = jnp.minimum(jnp.sum((pad_end[None, :] <= block_start[:, None]).astype(jnp.int32), axis=1),
                          N_EXPERTS - 1)
    nb_used = pad_end[-1] // ROW_BLOCK
    blk = jnp.arange(NB, dtype=jnp.int32)
    rows_left = counts[block_e] - (blk - ((pad_end - padded) // ROW_BLOCK)[block_e]) * ROW_BLOCK
    block_full = ((blk < nb_used) & (rows_left > ROW_HALF)).astype(jnp.int32)
    tail = nb_used + jnp.arange(N_EXPERTS, dtype=jnp.int32)
    zrow = jnp.concatenate([jnp.where(padded > 0, pad_end - ROW_BLOCK, -1),
                            jnp.where(tail < NB, tail * ROW_BLOCK, -1)]).astype(jnp.int32)
    return dest.reshape(T, TOP_K), top_w, block_e, block_full, padded, nb_used, zrow


def _schedule(block_e, block_full, padded, nb_used, n_tiles):
    nb = block_e.shape[0]
    b = jnp.tile(jnp.arange(nb, dtype=jnp.int32), n_tiles)
    f = jnp.repeat(jnp.arange(n_tiles, dtype=jnp.int32), nb)
    key = jnp.where(b < nb_used, (block_e[b] * n_tiles + f) * nb + b, (N_EXPERTS * n_tiles + f) * nb + b)
    order = jnp.argsort(key)
    used = padded > 0
    n_groups = (jnp.sum(used) * n_tiles).astype(jnp.int32)
    expert_pos = jnp.cumsum(used) - 1
    group = jnp.minimum(expert_pos[block_e[b]] * n_tiles + f, n_groups - 1).astype(jnp.int32)
    n_valid = (nb_used * n_tiles).astype(jnp.int32)
    sg = jnp.where(jnp.arange(nb * n_tiles) < n_valid, group[order], n_groups - 1)
    experts = jnp.arange(N_EXPERTS, dtype=jnp.int32)
    used_first = jnp.argsort(jnp.where(used, experts, N_EXPERTS + experts)).astype(jnp.int32)
    ge = jnp.repeat(used_first, n_tiles)
    gf = jnp.tile(jnp.arange(n_tiles, dtype=jnp.int32), N_EXPERTS)
    return (n_valid.reshape(1), sg, b[order], f[order], block_full[b][order], ge, gf, n_groups.reshape(1))


def _repack_w_in(w_in):
    w_in = w_in.astype(BF16)
    z = lambda n: jnp.zeros((w_in.shape[0], n), w_in.dtype)
    o = RWKV_COLS
    w_r = jnp.concatenate([w_in[:, :RW_WD], w_in[:, RW_WD:RW_WD + DECAY_LORA], z(LORA_PAD - DECAY_LORA),
                           w_in[:, RW_WD + DECAY_LORA:RW_WD + DECAY_LORA + AAA_LORA], z(LORA_PAD - AAA_LORA),
                           w_in[:, RW_WD + DECAY_LORA + AAA_LORA:o]], axis=1)
    w_m = jnp.concatenate([w_in[:, o:o + ML_IF], w_in[:, o + ML_IF:o + ML_IF + 2 * MLSTM_HEADS],
                           z(LANES - 2 * MLSTM_HEADS), w_in[:, o + ML_IF + 2 * MLSTM_HEADS:o + MLSTM_COLS]], axis=1)
    w_g = w_in[:, o + MLSTM_COLS:]
    return w_r, w_m, w_g


def kernel(x, norm1_w, w_in, b_gate, rwkv_mu, rwkv_w0, rwkv_w_up, rwkv_a0, rwkv_a_up, rwkv_g_up, rwkv_k_k,
           rwkv_k_a, rwkv_r_k, rwkv_ln_w, rwkv_ln_b, mlstm_conv_w, mlstm_i_b, mlstm_f_b, mlstm_norm_w,
           w_branch_a, w_branch_b, w_out, norm2_w, router_w, router_b, w_gu, b_gu, w_down, b_down,
           final_norm_w):
    B, S, D = x.shape
    T = B * S
    xt = x.reshape(T, D)
    assert norm1_w.shape[0] == 1, "single-layer block: the final rmsnorm is fused into the MoE combine"
    for l in range(1):
        hn = rmsnorm_rows(xt, norm1_w[l])
        w_r, w_m, w_g = _repack_w_in(w_in[l])
        p_r = matmul(hn, w_r, 512, RW_COLS_P // 2, F32, "proj_rwkv").reshape(B, S, RW_COLS_P)
        p_m = matmul(hn, w_m, 256, ML_COLS_P, F32, "proj_mlstm").reshape(B, S, ML_COLS_P)
        o_a = rwkv_branch(p_r, rwkv_mu[l], rwkv_w0[l], rwkv_w_up[l], rwkv_a0[l], rwkv_a_up[l], rwkv_g_up[l],
                          rwkv_k_k[l], rwkv_k_a[l], rwkv_r_k[l].reshape(-1), rwkv_ln_w[l], rwkv_ln_b[l])
        o_b = mlstm_branch(p_m, mlstm_conv_w[l], mlstm_i_b[l], mlstm_f_b[l], mlstm_norm_w[l])
        x1, xp, logits = merge_project(o_a.reshape(T, RWKV_DIM), o_b.reshape(T, MLSTM_DIM), hn, w_g, b_gate[l], xt,
                                       w_branch_a[l], w_branch_b[l], w_out[l], norm2_w[l], router_w[l],
                                       router_b[l])
        dest, top_w, block_e, block_full, padded, nb_used, zrow = _routing(logits)
        xs = moe_dispatch(xp, dest, zrow, block_e.shape[0] * ROW_BLOCK)
        h = moe_up(xs, w_gu[l], b_gu[l], _schedule(block_e, block_full, padded, nb_used, EXPERT_FF // UP_TILE))
        ys = moe_down(h, w_down[l], b_down[l], _schedule(block_e, block_full, padded, nb_used, D // DOWN_TILE))
        xt = combine(x1, ys, dest, top_w, final_norm_w)
    return xt.reshape(B, S, D)
```

```python
import functools

import jax
import jax.numpy as jnp
import numpy as np
from jax import lax
from jax.experimental import pallas as pl
from jax.experimental.pallas import tpu as pltpu

F32 = jnp.float32
BF16 = jnp.bfloat16

D_MODEL = 2048
CHUNK = 64
NORM_EPS = 1e-6
RWKV_HEADS = 16
RWKV_HEAD_DIM = 64
RWKV_DIM = 1024
DECAY_LORA = 96
AAA_LORA = 96
GATE_LORA = 256
GN_EPS = 64e-5
RWKV_COLS = 3 * RWKV_DIM + DECAY_LORA + AAA_LORA + GATE_LORA
MLSTM_HEADS = 4
MLSTM_QK_DIM = 128
MLSTM_V_DIM = 256
MLSTM_QK = 512
MLSTM_DIM = 1024
CONV_WIDTH = 4
GATE_SOFTCAP = 15.0
MLSTM_COLS = 2 * MLSTM_QK + 2 * MLSTM_DIM + 2 * MLSTM_HEADS
N_EXPERTS = 32
TOP_K = 4
EXPERT_FF = 2048
SWIGLU_LIMIT = 7.0
SWIGLU_ALPHA = 1.702

LANES = 128
SUBLANES = 8
VMEM_LIMIT = 56 * 1024 * 1024

LORA_PAD = 128
RW_WD = 3 * RWKV_DIM
RW_AD = RW_WD + LORA_PAD
RW_GD = RW_AD + LORA_PAD
RW_COLS_P = RW_GD + GATE_LORA
ML_V = 2 * MLSTM_QK
ML_IF = ML_V + MLSTM_DIM
ML_O = ML_IF + LANES
ML_COLS_P = ML_O + MLSTM_DIM
ROUTER_PAD = 128
SCAN_SEQS = 4
MLSTM_SEQS = 1

ROW_BLOCK = 512
ROW_HALF = ROW_BLOCK // 2


def _cparams(sem):
    return pltpu.CompilerParams(dimension_semantics=sem, vmem_limit_bytes=VMEM_LIMIT)


def _bdot(a, b):
    return jnp.dot(a.astype(BF16), b.astype(BF16), preferred_element_type=F32)


def _split3(x):
    hi = x.astype(BF16)
    r1 = x - hi.astype(F32)
    mid = r1.astype(BF16)
    lo = (r1 - mid.astype(F32)).astype(BF16)
    return hi, mid, lo


def _dot_exact_lhs(mat_bf16, x):
    hi, mid, lo = _split3(x)
    return (jnp.dot(mat_bf16, hi, preferred_element_type=F32)
            + jnp.dot(mat_bf16, mid, preferred_element_type=F32)
            + jnp.dot(mat_bf16, lo, preferred_element_type=F32))


def _dot_exact_rhs(x, mat_bf16):
    hi, mid, lo = _split3(x)
    return (jnp.dot(hi, mat_bf16, preferred_element_type=F32)
            + jnp.dot(mid, mat_bf16, preferred_element_type=F32)
            + jnp.dot(lo, mat_bf16, preferred_element_type=F32))


def _sigmoid(x):
    return 1.0 / (1.0 + jnp.exp(-x))


def _softplus(x):
    return jnp.maximum(x, 0.0) + jnp.log(1.0 + jnp.exp(-jnp.abs(x)))


def _rmsnorm_kernel(x_ref, w_ref, o_ref):
    x = x_ref[...]
    y = x * lax.rsqrt(jnp.mean(x * x, axis=-1, keepdims=True) + NORM_EPS)
    o_ref[...] = (y * w_ref[...]).astype(o_ref.dtype)


def rmsnorm_rows(x, w, tm=512):
    T, D = x.shape
    return pl.pallas_call(
        _rmsnorm_kernel,
        grid=(T // tm,),
        in_specs=[pl.BlockSpec((tm, D), lambda i: (i, 0)),
                  pl.BlockSpec((1, D), lambda i: (0, 0))],
        out_specs=pl.BlockSpec((tm, D), lambda i: (i, 0)),
        out_shape=jax.ShapeDtypeStruct((T, D), BF16),
        compiler_params=_cparams(("parallel",)),
        name="rmsnorm1",
    )(x, w.reshape(1, D))


def _mm_kernel(a_ref, b_ref, o_ref):
    o_ref[...] = jnp.dot(a_ref[...], b_ref[...], preferred_element_type=F32).astype(o_ref.dtype)


def matmul(a, b, tm, tn, out_dtype, name):
    M, K = a.shape
    N = b.shape[1]
    return pl.pallas_call(
        _mm_kernel,
        grid=(N // tn, M // tm),
        in_specs=[pl.BlockSpec((tm, K), lambda j, i: (i, 0)),
                  pl.BlockSpec((K, tn), lambda j, i: (0, j))],
        out_specs=pl.BlockSpec((tm, tn), lambda j, i: (i, j)),
        out_shape=jax.ShapeDtypeStruct((M, N), out_dtype),
        compiler_params=_cparams(("parallel", "parallel")),
        name=name,
    )(a, b)


def _head_sum_mat():
    r = lax.broadcasted_iota(jnp.int32, (LANES, LANES), 0) // RWKV_HEAD_DIM
    c = lax.broadcasted_iota(jnp.int32, (LANES, LANES), 1) // RWKV_HEAD_DIM
    return jnp.where(r == c, 1.0, 0.0).astype(BF16)


def _rwkv_prep_kernel(p_ref, mu_ref, w0_ref, wup_ref, a0_ref, aup_ref, gup_ref, kk_ref, ka_ref,
                      r_out, k_out, v_out, kk_out, b_out, lw_out, g_out, carry_ref):
    i = pl.program_id(1)

    @pl.when(i == 0)
    def _():
        carry_ref[...] = jnp.zeros_like(carry_ref)

    p = p_ref[...]
    tq = p.shape[0]
    row = lax.broadcasted_iota(jnp.int32, (tq, 1), 0)
    prev = jnp.where(row == 0, carry_ref[0:1, :], pltpu.roll(p, 1, 0))
    carry_ref[0:1, :] = p[tq - 1:tq, :]
    p = p + (prev - p) * mu_ref[...]

    r = p[:, 0:RWKV_DIM]
    k = p[:, RWKV_DIM:2 * RWKV_DIM]
    v = p[:, 2 * RWKV_DIM:3 * RWKV_DIM]
    wd = p[:, RW_WD:RW_AD]
    ad = p[:, RW_AD:RW_GD]
    gd = p[:, RW_GD:RW_COLS_P]

    w_log = -_softplus(-(w0_ref[...] + _bdot(jnp.tanh(wd), wup_ref[...]))) - 0.5
    lw = -jnp.exp(w_log)
    a = _sigmoid(a0_ref[...] + _bdot(ad, aup_ref[...]))
    g = _bdot(_sigmoid(gd), gup_ref[...])

    kk = k * kk_ref[...]
    hs = _head_sum_mat()
    nrm2 = jnp.concatenate(
        [_dot_exact_rhs(kk[:, c:c + LANES] * kk[:, c:c + LANES], hs) for c in range(0, RWKV_DIM, LANES)],
        axis=1)
    kk = kk / jnp.maximum(jnp.sqrt(nrm2), 1e-12)
    k = k * (1.0 + (a - 1.0) * ka_ref[...])

    r_out[...] = r.astype(r_out.dtype)
    k_out[...] = k.astype(k_out.dtype)
    v_out[...] = v.astype(v_out.dtype)
    kk_out[...] = kk.astype(kk_out.dtype)
    b_out[...] = (kk * a).astype(b_out.dtype)
    lw_out[...] = lw
    g_out[...] = g.astype(g_out.dtype)


def rwkv_prep(p_rwkv, mu, w0, w_up, a0, a_up, g_up, k_k, k_a, tq=256):
    B, S, C = p_rwkv.shape
    blk = lambda n: pl.BlockSpec((None, tq, n), lambda b, i: (b, i, 0))
    full = lambda a: pl.BlockSpec(a.shape, lambda b, i: (0,) * a.ndim)
    params = (mu, w0, w_up, a0, a_up, g_up, k_k, k_a)
    out = lambda dt: jax.ShapeDtypeStruct((B, S, RWKV_DIM), dt)
    return pl.pallas_call(
        _rwkv_prep_kernel,
        grid=(B, S // tq),
        in_specs=[blk(C)] + [full(a) for a in params],
        out_specs=[blk(RWKV_DIM)] * 7,
        out_shape=[out(BF16)] * 5 + [out(F32), out(BF16)],
        scratch_shapes=[pltpu.VMEM((SUBLANES, C), F32)],
        compiler_params=_cparams(("parallel", "arbitrary")),
        name="rwkv_prep",
    )(p_rwkv, *params)


def _rwkv_scan_kernel(r_ref, k_ref, v_ref, kk_ref, b_ref, lw_ref, g_ref, rk_ref, lnw_ref, lnb_ref,
                      o_ref, h_ref):
    c = pl.program_id(1)

    @pl.when(c == 0)
    def _():
        h_ref[...] = jnp.zeros_like(h_ref)

    L = CHUNK
    L2 = 2 * L
    ri = lax.broadcasted_iota(jnp.int32, (L, L), 0)
    ci = lax.broadcasted_iota(jnp.int32, (L, L), 1)
    tril = jnp.where(ri >= ci, 1.0, 0.0).astype(BF16)

    n_seq = lw_ref.shape[0]
    cat = lambda ref: jnp.concatenate([ref[i] for i in range(n_seq)], axis=1)
    rep = lambda ref: jnp.concatenate([ref[...]] * n_seq, axis=1)
    lw = cat(lw_ref)
    cum = _dot_exact_lhs(tril, lw)
    cum_end = cum[L - 1:L, :]
    w_in = jnp.exp(cum)
    w_prev = jnp.exp(cum - lw)
    w_inv = jnp.exp(-cum)
    w_tail = jnp.exp(cum_end - cum)
    w_end = jnp.exp(cum_end)

    kk = cat(kk_ref).astype(F32)
    bb = cat(b_ref).astype(F32)
    kx = cat(k_ref).astype(F32)
    rx = cat(r_ref).astype(F32)
    vx = cat(v_ref).astype(F32)
    gx = cat(g_ref).astype(F32)
    a_hat = -kk * w_prev
    r_hat = rx * w_in
    b_hat = bb * w_inv
    k_hat = kx * w_inv
    b_til = bb * w_tail
    k_til = kx * w_tail
    rkk = rx * kx * rep(rk_ref)

    lane = lax.broadcasted_iota(jnp.int32, (1, LANES), 1)
    m_lo = jnp.where(lane < RWKV_HEAD_DIM, 1.0, 0.0)
    m_hi = 1.0 - m_lo

    def stack(x):
        return jnp.concatenate([x * m_lo, x * m_hi], axis=0)

    r2 = lax.broadcasted_iota(jnp.int32, (L2, L2), 0)
    c2 = lax.broadcasted_iota(jnp.int32, (L2, L2), 1)
    same_head = (r2 // L) == (c2 // L)
    strict = same_head & (r2 > c2)
    incl = same_head & (r2 >= c2)
    diag16 = (r2 // 16) == (c2 // 16)
    eye = jnp.where(r2 == c2, 1.0, 0.0)
    hs = _head_sum_mat()

    pairs_per_seq = RWKV_DIM // LANES
    pairs = range(n_seq * pairs_per_seq)
    sls = [slice(p * LANES, (p + 1) * LANES) for p in pairs]
    v_st = [stack(vx[:, sl]) for sl in sls]
    lhs = [jnp.concatenate([stack(a_hat[:, sl]), stack(r_hat[:, sl])], axis=0).astype(BF16) for sl in sls]
    rhs = [jnp.concatenate([stack(b_hat[:, sl]), stack(k_hat[:, sl])], axis=0).astype(BF16) for sl in sls]
    sc = [lax.dot_general(lhs[p], rhs[p], (((1,), (1,)), ((), ())), preferred_element_type=F32) for p in pairs]
    n_ab = [jnp.where(strict, sc[p][:L2, :L2], 0.0) for p in pairs]
    a_ak = [jnp.where(strict, sc[p][:L2, L2:], 0.0).astype(BF16) for p in pairs]
    a_r = [jnp.concatenate([jnp.where(incl, sc[p][L2:, :L2], 0.0), jnp.where(incl, sc[p][L2:, L2:], 0.0)],
                           axis=1).astype(BF16) for p in pairs]

    nd = [jnp.where(diag16, n_ab[p], 0.0) for p in pairs]
    noff = [(n_ab[p] - nd[p]).astype(BF16) for p in pairs]
    ndb = [nd[p].astype(BF16) for p in pairs]
    s2 = [jnp.dot(ndb[p], ndb[p], preferred_element_type=F32).astype(BF16) for p in pairs]
    s4 = [jnp.dot(s2[p], s2[p], preferred_element_type=F32).astype(BF16) for p in pairs]
    s8 = [jnp.dot(s4[p], s4[p], preferred_element_type=F32).astype(BF16) for p in pairs]
    x1 = [eye + nd[p] for p in pairs]
    x2 = [x1[p] + _bdot(x1[p], s2[p]) for p in pairs]
    x3 = [x2[p] + _bdot(x2[p], s4[p]) for p in pairs]
    t_d = [(x3[p] + _bdot(x3[p], s8[p])).astype(BF16) for p in pairs]
    m1 = [jnp.dot(t_d[p], noff[p], preferred_element_type=F32) for p in pairs]
    m1b = [m1[p].astype(BF16) for p in pairs]
    m2 = [jnp.dot(m1b[p], m1b[p], preferred_element_type=F32) for p in pairs]
    m3 = [jnp.dot(m1b[p], m2[p].astype(BF16), preferred_element_type=F32) for p in pairs]
    t_inv = [jnp.dot((eye + m1[p] + m2[p] + m3[p]).astype(BF16), t_d[p], preferred_element_type=F32).astype(BF16)
             for p in pairs]

    h0 = [h_ref[p] for p in pairs]
    ah = [jnp.dot(lhs[p], h0[p].astype(BF16), preferred_element_type=F32) for p in pairs]
    x = [ah[p][:L2] + jnp.dot(a_ak[p], v_st[p].astype(BF16), preferred_element_type=F32) for p in pairs]
    u = [jnp.dot(t_inv[p], x[p].astype(BF16), preferred_element_type=F32) for p in pairs]
    uv = [jnp.concatenate([u[p], v_st[p]], axis=0).astype(BF16) for p in pairs]
    y_st = [ah[p][L2:] + jnp.dot(a_r[p], uv[p], preferred_element_type=F32) for p in pairs]
    y = [y_st[p][:L] + y_st[p][L:] for p in pairs]

    for p in pairs:
        sl = sls[p]
        upd_l = jnp.concatenate([stack(b_til[:, sl]), stack(k_til[:, sl])], axis=0).astype(BF16)
        upd = lax.dot_general(upd_l, uv[p], (((0,), (0,)), ((), ())), preferred_element_type=F32)
        w_col = jnp.sum(eye * w_end[:, sl], axis=1, keepdims=True)
        h_ref[p] = w_col * h0[p] + upd

    def head_sums(vals):
        parts = []
        for t in vals:
            hi = t.astype(BF16)
            parts += [hi, (t - hi.astype(F32)).astype(BF16)]
        res = jnp.dot(jnp.concatenate(parts, axis=0), hs, preferred_element_type=F32)
        return [res[2 * i * L:(2 * i + 1) * L] + res[(2 * i + 1) * L:(2 * i + 2) * L] for i in range(len(vals))]

    sums1 = [head_sums([y[p], rkk[:, sls[p]]]) for p in pairs]
    d = [y[p] - sums1[p][0] * (1.0 / RWKV_HEAD_DIM) for p in pairs]
    var = [head_sums([d[p] * d[p]])[0] * (1.0 / RWKV_HEAD_DIM) for p in pairs]
    for p in pairs:
        sl = sls[p]
        psl = sls[p % pairs_per_seq]
        yn = d[p] * lax.rsqrt(var[p] + GN_EPS) * lnw_ref[:, psl] + lnb_ref[:, psl]
        o_ref[p // pairs_per_seq, :, psl] = ((yn + sums1[p][1] * vx[:, sl]) * gx[:, sl]).astype(o_ref.dtype)


def rwkv_scan(r, k, v, kk, b, lw, g, r_k, ln_w, ln_b):
    B, S, C = r.shape
    blk = pl.BlockSpec((SCAN_SEQS, CHUNK, C), lambda bb, c: (bb, c, 0))
    full = pl.BlockSpec((1, C), lambda bb, c: (0, 0))
    return pl.pallas_call(
        _rwkv_scan_kernel,
        grid=(B // SCAN_SEQS, S // CHUNK),
        in_specs=[blk] * 7 + [full] * 3,
        out_specs=blk,
        out_shape=jax.ShapeDtypeStruct((B, S, C), BF16),
        scratch_shapes=[pltpu.VMEM((SCAN_SEQS * C // LANES, LANES, LANES), F32)],
        compiler_params=_cparams(("parallel", "arbitrary")),
        name="rwkv_scan",
    )(r, k, v, kk, b, lw, g, r_k, ln_w, ln_b)


def _pad_rows(w, n):
    return jnp.pad(w, ((0, n - w.shape[0]), (0, 0)))


def rwkv_branch(p_pad, mu, w0, w_up, a0, a_up, g_up, k_k, k_a, r_k, ln_w, ln_b):
    row = lambda t: t.reshape(1, -1)
    mu_p = jnp.concatenate([mu[:RW_WD], jnp.pad(mu[RW_WD:RW_WD + DECAY_LORA], (0, LORA_PAD - DECAY_LORA)),
                            jnp.pad(mu[RW_WD + DECAY_LORA:RW_WD + DECAY_LORA + AAA_LORA], (0, LORA_PAD - AAA_LORA)),
                            mu[RW_WD + DECAY_LORA + AAA_LORA:]])
    outs = rwkv_prep(p_pad, row(mu_p), row(w0), _pad_rows(w_up, LORA_PAD).astype(BF16), row(a0),
                     _pad_rows(a_up, LORA_PAD).astype(BF16), g_up.astype(BF16), row(k_k), row(k_a))
    return rwkv_scan(*outs, row(r_k), row(ln_w), row(ln_b))


def _mlstm_kernel(p_ref, convw_ref, ifb_ref, nw_ref, o_ref, carry_ref, c_ref, n_ref, m_ref):
    ci = pl.program_id(1)

    @pl.when(ci == 0)
    def _():
        carry_ref[...] = jnp.zeros_like(carry_ref)
        c_ref[...] = jnp.zeros_like(c_ref)
        n_ref[...] = jnp.zeros_like(n_ref)
        m_ref[...] = jnp.zeros_like(m_ref)

    L = CHUNK
    n_seq = p_ref.shape[0]
    lane = lax.broadcasted_iota(jnp.int32, (1, LANES), 1)
    ri = lax.broadcasted_iota(jnp.int32, (L, L), 0)
    cj = lax.broadcasted_iota(jnp.int32, (L, L), 1)
    causal = ri >= cj
    tril = jnp.where(causal, 1.0, 0.0).astype(BF16)
    triu = jnp.where(ri <= cj, 1.0, 0.0).astype(BF16)

    qk, comb, comb_t, b_col, b_row = [], [], [], [], []
    for i in range(n_seq):
        u = p_ref[i, :, 0:ML_V]
        ext = jnp.concatenate([carry_ref[i], u], axis=0)
        carry_ref[i] = u[L - SUBLANES:L, :]
        conv = convw_ref[CONV_WIDTH - 1:CONV_WIDTH, :] * u
        for j in range(1, CONV_WIDTH):
            conv = conv + convw_ref[CONV_WIDTH - 1 - j:CONV_WIDTH - j, :] * pltpu.roll(ext, j, 0)[SUBLANES:, :]
        qk.append(conv * _sigmoid(conv))
        pre = GATE_SOFTCAP * jnp.tanh((p_ref[i, :, ML_IF:ML_O] + ifb_ref[...]) * (1.0 / GATE_SOFTCAP))
        cb = jnp.where(lane < MLSTM_HEADS, pre, -_softplus(-pre))
        comb.append(cb)
        comb_t.append(cb.T)
        b_col.append(_dot_exact_lhs(tril, cb))
        b_row.append(_dot_exact_rhs(comb_t[i], triu))

    H = range(n_seq * MLSTM_HEADS)
    sq = [v // MLSTM_HEADS for v in H]
    hd = [v % MLSTM_HEADS for v in H]
    dk, dv = MLSTM_QK_DIM, MLSTM_V_DIM
    qh = [qk[sq[v]][:, hd[v] * dk:(hd[v] + 1) * dk] * (dk ** -0.5) for v in H]
    kh = [qk[sq[v]][:, MLSTM_QK + hd[v] * dk:MLSTM_QK + (hd[v] + 1) * dk] for v in H]
    vh = [p_ref[sq[v], :, ML_V + hd[v] * dv:ML_V + (hd[v] + 1) * dv].astype(BF16) for v in H]
    qb = [qh[h].astype(BF16) for h in H]
    bcol = [b_col[sq[v]][:, MLSTM_HEADS + hd[v]:MLSTM_HEADS + hd[v] + 1] for v in H]
    brow = [b_row[sq[v]][MLSTM_HEADS + hd[v]:MLSTM_HEADS + hd[v] + 1, :] for v in H]
    m_prev = [m_ref[h][0:1, 0:1] for h in H]
    n_prev = [n_ref[h][0:1, :] for h in H]
    c_prev = [c_ref[h] for h in H]

    qk_t = [lax.dot_general(qb[h], kh[h].astype(BF16), (((1,), (1,)), ((), ())), preferred_element_type=F32)
            for h in H]
    qc = [jnp.dot(qb[h], c_prev[h].astype(BF16), preferred_element_type=F32) for h in H]
    dm = [jnp.where(causal, bcol[h] - brow[h] + comb_t[sq[h]][hd[h]:hd[h] + 1, :], -jnp.inf) for h in H]
    inter = [bcol[h] + m_prev[h] for h in H]
    m_t = [jnp.maximum(inter[h], jnp.max(dm[h], axis=-1, keepdims=True)) for h in H]
    s = [qk_t[h] * jnp.exp(dm[h] - m_t[h]) for h in H]
    w_inter = [jnp.exp(inter[h] - m_t[h]) for h in H]
    num = [jnp.dot(s[h].astype(BF16), vh[h], preferred_element_type=F32) + w_inter[h] * qc[h] for h in H]
    den = [jnp.sum(s[h], axis=-1, keepdims=True) + w_inter[h] * jnp.sum(qh[h] * n_prev[h], axis=-1, keepdims=True)
           for h in H]
    hh = [num[h] / jnp.maximum(jnp.abs(den[h]), jnp.exp(-m_t[h])) for h in H]

    g_tot = [bcol[h][L - 1:L, :] for h in H]
    a = [comb[sq[h]][:, hd[h]:hd[h] + 1] + g_tot[h] - bcol[h] for h in H]
    m_new = [jnp.maximum(g_tot[h] + m_prev[h], jnp.max(a[h], axis=0, keepdims=True)) for h in H]
    dec = [jnp.exp(g_tot[h] + m_prev[h] - m_new[h]) for h in H]
    wkk = [jnp.exp(a[h] - m_new[h]) * kh[h] for h in H]
    for h in H:
        c_ref[h] = dec[h] * c_prev[h] + lax.dot_general(wkk[h].astype(BF16), vh[h], (((0,), (0,)), ((), ())),
                                                        preferred_element_type=F32)
        n_ref[h] = jnp.broadcast_to(dec[h] * n_prev[h] + jnp.sum(wkk[h], axis=0, keepdims=True),
                                    (SUBLANES, LANES))
        m_ref[h] = jnp.broadcast_to(m_new[h], (SUBLANES, LANES))
    for h in H:
        vs = slice(hd[h] * dv, (hd[h] + 1) * dv)
        hn = hh[h] * lax.rsqrt(jnp.mean(hh[h] * hh[h], axis=-1, keepdims=True) + NORM_EPS)
        o_raw = p_ref[sq[h], :, ML_O + hd[h] * dv:ML_O + (hd[h] + 1) * dv]
        o_ref[sq[h], :, vs] = (hn * nw_ref[:, vs] * _sigmoid(o_raw)).astype(o_ref.dtype)


def mlstm_branch(p_pad, conv_w, i_b, f_b, norm_w):
    B, S, C = p_pad.shape
    ifb = jnp.pad(jnp.concatenate([i_b, f_b]), (0, LANES - 2 * MLSTM_HEADS)).reshape(1, LANES)
    full = lambda a: pl.BlockSpec(a.shape, lambda b, c: (0,) * a.ndim)
    nw = norm_w.reshape(1, MLSTM_DIM)
    return pl.pallas_call(
        _mlstm_kernel,
        grid=(B // MLSTM_SEQS, S // CHUNK),
        in_specs=[pl.BlockSpec((MLSTM_SEQS, CHUNK, C), lambda b, c: (b, c, 0)), full(conv_w), full(ifb), full(nw)],
        out_specs=pl.BlockSpec((MLSTM_SEQS, CHUNK, MLSTM_DIM), lambda b, c: (b, c, 0)),
        out_shape=jax.ShapeDtypeStruct((B, S, MLSTM_DIM), BF16),
        scratch_shapes=[pltpu.VMEM((MLSTM_SEQS, SUBLANES, ML_V), F32),
                        pltpu.VMEM((MLSTM_SEQS * MLSTM_HEADS, MLSTM_QK_DIM, MLSTM_V_DIM), F32),
                        pltpu.VMEM((MLSTM_SEQS * MLSTM_HEADS, SUBLANES, LANES), F32),
                        pltpu.VMEM((MLSTM_SEQS * MLSTM_HEADS, SUBLANES, LANES), F32)],
        compiler_params=_cparams(("parallel", "arbitrary")),
        name="mlstm_scan",
    )(p_pad, conv_w, ifb, nw)


HI_MASK = 0xFFFF0000


def _pack_bf16_pairs(hb_f32):
    c = hb_f32.shape[1] // 2
    u = pltpu.bitcast(hb_f32, jnp.uint32)
    return u[:, c:] | (u[:, :c] >> 16)


ROW_TILES = D_MODEL // 2 // LANES


def _store_tiled(ref, val, first_row=0):
    n = val.shape[0]
    for t in range(ROW_TILES):
        ref[pl.ds(first_row * ROW_TILES + t, n, stride=ROW_TILES), :] = val[:, t * LANES:(t + 1) * LANES]


def _load_tiled(ref, n, first_row=0):
    return jnp.concatenate([ref[pl.ds(first_row * ROW_TILES + t, n, stride=ROW_TILES), :] for t in range(ROW_TILES)],
                           axis=1)


def _unpack_bf16_pairs(xu):
    lo = pltpu.bitcast(xu << 16, F32).astype(BF16)
    hi = pltpu.bitcast(xu & jnp.uint32(HI_MASK), F32).astype(BF16)
    return lo, hi


def _merge_kernel(oa_ref, ob_ref, hn_ref, bg_ref, x_ref, wa_ref, wb_ref, wg_ref, wo_ref, n2_ref, rwh_ref, rwl_ref,
                  rb_ref, x1_ref, xp_ref, lg_ref):
    hn1 = hn_ref[...]

    def gated(o_ref, w_ref, cols):
        gate = _sigmoid(jnp.dot(hn1, wg_ref[:, cols], preferred_element_type=F32) + bg_ref[:, cols])
        return gate * jnp.dot(o_ref[...], w_ref[...], preferred_element_type=F32)

    merged = gated(oa_ref, wa_ref, slice(0, D_MODEL)) + gated(ob_ref, wb_ref, slice(D_MODEL, 2 * D_MODEL))
    x1 = x_ref[...] + jnp.dot(merged.astype(BF16), wo_ref[...], preferred_element_type=F32)
    x1_ref[...] = x1
    hn = x1 * lax.rsqrt(jnp.mean(x1 * x1, axis=-1, keepdims=True) + NORM_EPS) * n2_ref[...]
    hi = hn.astype(BF16)
    _store_tiled(xp_ref, _pack_bf16_pairs(hi.astype(F32)))
    lo = (hn - hi.astype(F32)).astype(BF16)
    lg_ref[...] = (jnp.dot(hi, rwh_ref[...], preferred_element_type=F32)
                   + jnp.dot(hi, rwl_ref[...], preferred_element_type=F32)
                   + jnp.dot(lo, rwh_ref[...], preferred_element_type=F32) + rb_ref[...])


def merge_project(o_a, o_b, hn, w_gate, b_gate, x, w_a, w_b, w_out, norm2_w, router_w, router_b, tm=256):
    T, D = x.shape
    rw = jnp.pad(router_w, ((0, 0), (0, ROUTER_PAD - N_EXPERTS)))
    rw_hi = rw.astype(BF16)
    rw_lo = (rw - rw_hi.astype(F32)).astype(BF16)
    rb = jnp.pad(router_b, (0, ROUTER_PAD - N_EXPERTS)).reshape(1, ROUTER_PAD)
    rows = lambda n: pl.BlockSpec((tm, n), lambda i: (i, 0))
    full = lambda a: pl.BlockSpec(a.shape, lambda i: (0, 0), pipeline_mode=pl.Buffered(1))
    params = (w_a.astype(BF16), w_b.astype(BF16), w_gate, w_out.astype(BF16), norm2_w.reshape(1, D), rw_hi, rw_lo,
              rb)
    bg = b_gate.reshape(1, 2 * D)
    return pl.pallas_call(
        _merge_kernel,
        grid=(T // tm,),
        in_specs=[rows(o_a.shape[1]), rows(o_b.shape[1]), rows(D), full(bg), rows(D)] + [full(a) for a in params],
        out_specs=[rows(D), pl.BlockSpec((tm * ROW_TILES, LANES), lambda i: (i, 0)), rows(ROUTER_PAD)],
        out_shape=[jax.ShapeDtypeStruct((T, D), F32), jax.ShapeDtypeStruct((T * ROW_TILES, LANES), jnp.uint32),
                   jax.ShapeDtypeStruct((T, ROUTER_PAD), F32)],
        compiler_params=_cparams(("parallel",)),
        name="merge_project",
    )(o_a, o_b, hn, bg, x, *params)


UP_TILE = 1024
DOWN_TILE = 2048
RANK_GROUP = 256
DISPATCH_TOKENS = 256
COMBINE_TOKENS = 128
DMA_UNROLL = 8


def _row_copy(src, dst, src_at, dst_at, sem):
    return pltpu.make_async_copy(src.at[pl.ds(pl.multiple_of(src_at, ROW_TILES), ROW_TILES), :],
                                 dst.at[pl.ds(pl.multiple_of(dst_at, ROW_TILES), ROW_TILES), :], sem)


def _dispatch_kernel(zrow_ref, dest_ref, xp_ref, xs_hbm, zbuf, zsem, sem):
    i = pl.program_id(0)
    tq = dest_ref.shape[2] // TOP_K

    def zero_copy(k):
        at = pl.multiple_of(zrow_ref[k], ROW_BLOCK * ROW_TILES)
        return pltpu.make_async_copy(zbuf, xs_hbm.at[pl.ds(at, ROW_BLOCK * ROW_TILES), :], zsem)

    @pl.when(i == 0)
    def _():
        zbuf[...] = jnp.zeros_like(zbuf)
        for k in range(zrow_ref.shape[0]):
            pl.when(zrow_ref[k] >= 0)(lambda k=k: zero_copy(k).start())
        for k in range(zrow_ref.shape[0]):
            pl.when(zrow_ref[k] >= 0)(lambda k=k: zero_copy(k).wait())

    def body(q, carry):
        base = pl.multiple_of(q * SUBLANES, SUBLANES)
        for r in range(SUBLANES):
            for j in range(TOP_K):
                _row_copy(xp_ref, xs_hbm, (base + r) * ROW_TILES, dest_ref[0, 0, TOP_K * (base + r) + j],
                          sem).start()
        return carry

    lax.fori_loop(0, tq // SUBLANES, body, 0)
    for j in range(TOP_K):
        pltpu.make_async_copy(xp_ref, xs_hbm.at[pl.ds(0, tq * ROW_TILES), :], sem).wait()


def moe_dispatch(xp, dest, zrow, n_rows):
    T = xp.shape[0] // ROW_TILES
    tq = DISPATCH_TOKENS
    dest_blk = (dest * ROW_TILES).reshape(T // tq, 1, TOP_K * tq)
    zrow = jnp.where(zrow >= 0, zrow * ROW_TILES, -1)
    return pl.pallas_call(
        _dispatch_kernel,
        grid_spec=pltpu.PrefetchScalarGridSpec(
            num_scalar_prefetch=1,
            grid=(T // tq,),
            in_specs=[pl.BlockSpec((1, 1, TOP_K * tq), lambda i, zr: (i, 0, 0), memory_space=pltpu.SMEM),
                      pl.BlockSpec((tq * ROW_TILES, LANES), lambda i, zr: (i, 0))],
            out_specs=pl.BlockSpec(memory_space=pl.ANY),
            scratch_shapes=[pltpu.VMEM((ROW_BLOCK * ROW_TILES, LANES), xp.dtype), pltpu.SemaphoreType.DMA(()),
                            pltpu.SemaphoreType.DMA(())]),
        out_shape=jax.ShapeDtypeStruct((n_rows * ROW_TILES, LANES), xp.dtype),
        compiler_params=_cparams(("arbitrary",)),
        name="moe_dispatch",
    )(zrow, dest_blk, xp)


SCHED_NV, SCHED_SG, SCHED_SB, SCHED_SO, SCHED_FULL, SCHED_GE, SCHED_GF, SCHED_NG = range(8)


def _stream_weights(s, sched, copies, on_arrival):
    sg_ref = sched[SCHED_SG]
    g = sg_ref[s]
    first = (s < sched[SCHED_NV][0]) & ((s == 0) | (g != sg_ref[jnp.maximum(s - 1, 0)]))

    @pl.when(first)
    def _():
        @pl.when(s == 0)
        def _():
            for c in copies(g):
                c.start()

        for c in copies(g):
            c.wait()
        on_arrival()

        @pl.when(g + 1 < sched[SCHED_NG][0])
        def _():
            for c in copies(g + 1):
                c.start()


def _for_used_rows(s, sched, out_ref, compute):
    valid = s < sched[SCHED_NV][0]
    full = sched[SCHED_FULL][s] > 0

    @pl.when(jnp.logical_not(valid))
    def _():
        out_ref[...] = jnp.zeros_like(out_ref)

    @pl.when(valid & full)
    def _():
        compute(ROW_BLOCK)

    @pl.when(valid & jnp.logical_not(full))
    def _():
        compute(ROW_HALF)
        rest = ROW_HALF * (out_ref.shape[0] // ROW_BLOCK)
        out_ref[rest:, :] = jnp.zeros((out_ref.shape[0] - rest, out_ref.shape[1]), out_ref.dtype)


def _moe_up_kernel(*refs):
    sched = refs[:8]
    xs_ref, w_hbm, bg_ref, bu_ref, h_ref, wbuf, wgb, wub, sems = refs[8:]
    s = pl.program_id(0)
    tf = wgb.shape[1]
    n_ff = w_hbm.shape[2] // 2

    def copies(g):
        e = sched[SCHED_GE][g]
        col = pl.multiple_of(sched[SCHED_GF][g] * tf, tf)
        return [pltpu.make_async_copy(w_hbm.at[e, :, pl.ds(half * n_ff + col, tf)], wbuf.at[half], sems.at[half])
                for half in range(2)]

    def on_arrival():
        wgb[...] = wbuf[0].astype(BF16)
        wub[...] = wbuf[1].astype(BF16)

    _stream_weights(s, sched, copies, on_arrival)

    def compute(rows):
        lo, hi = _unpack_bf16_pairs(_load_tiled(xs_ref, rows))
        half = lo.shape[1]

        def proj(wb, b_ref):
            return (jnp.dot(lo, wb[:half, :], preferred_element_type=F32)
                    + jnp.dot(hi, wb[half:, :], preferred_element_type=F32) + b_ref[...])

        gate = jnp.minimum(proj(wgb, bg_ref), SWIGLU_LIMIT)
        up = jnp.clip(proj(wub, bu_ref), -SWIGLU_LIMIT, SWIGLU_LIMIT)
        h_ref[:rows, :] = ((up + 1.0) * gate * _sigmoid(SWIGLU_ALPHA * gate)).astype(h_ref.dtype)

    _for_used_rows(s, sched, h_ref, compute)


def _step_expert(s, r):
    return r[SCHED_GE][r[SCHED_SG][s]]


def _step_tile(s, r):
    return r[SCHED_GF][r[SCHED_SG][s]]


def moe_up(xs, w_gu, b_gu, sched):
    P = xs.shape[0] // ROW_TILES
    E, D, F2 = w_gu.shape
    F = F2 // 2
    tf = UP_TILE
    nf = F // tf
    b3 = b_gu.reshape(E, 1, F2)
    return pl.pallas_call(
        _moe_up_kernel,
        grid_spec=pltpu.PrefetchScalarGridSpec(
            num_scalar_prefetch=len(sched),
            grid=(sched[SCHED_SG].shape[0],),
            in_specs=[pl.BlockSpec((ROW_BLOCK * ROW_TILES, LANES), lambda s, *r: (r[SCHED_SB][s], 0)),
                      pl.BlockSpec(memory_space=pl.ANY),
                      pl.BlockSpec((None, 1, tf), lambda s, *r: (_step_expert(s, r), 0, _step_tile(s, r))),
                      pl.BlockSpec((None, 1, tf), lambda s, *r: (_step_expert(s, r), 0, nf + _step_tile(s, r)))],
            out_specs=pl.BlockSpec((ROW_BLOCK, tf), lambda s, *r: (r[SCHED_SB][s], r[SCHED_SO][s])),
            scratch_shapes=[pltpu.VMEM((2, D, tf), F32), pltpu.VMEM((D, tf), BF16), pltpu.VMEM((D, tf), BF16),
                            pltpu.SemaphoreType.DMA((2,))]),
        out_shape=jax.ShapeDtypeStruct((P, F), BF16),
        compiler_params=_cparams(("arbitrary",)),
        name="moe_up",
    )(*sched, xs, w_gu, b3, b3)


def _moe_down_kernel(*refs):
    sched = refs[:8]
    h_ref, w_hbm, bd_ref, y_ref, wbuf, wdb, sem = refs[8:]
    s = pl.program_id(0)
    tn = wdb.shape[1]

    def copies(g):
        col = pl.multiple_of(sched[SCHED_GF][g] * tn, tn)
        return [pltpu.make_async_copy(w_hbm.at[sched[SCHED_GE][g], :, pl.ds(col, tn)], wbuf, sem)]

    def on_arrival():
        wdb[...] = wbuf[...].astype(BF16)

    _stream_weights(s, sched, copies, on_arrival)

    def compute(rows):
        y = jnp.dot(h_ref[:rows, :], wdb[...], preferred_element_type=F32) + bd_ref[...]
        _store_tiled(y_ref, _pack_bf16_pairs(y.astype(BF16).astype(F32)))

    _for_used_rows(s, sched, y_ref, compute)


def moe_down(h, w_down, b_down, sched):
    P, F = h.shape
    E, _, D = w_down.shape
    tn = DOWN_TILE
    assert tn == D, "the packed output pairs column c with column c + D/2"
    b3 = b_down.reshape(E, 1, D)
    return pl.pallas_call(
        _moe_down_kernel,
        grid_spec=pltpu.PrefetchScalarGridSpec(
            num_scalar_prefetch=len(sched),
            grid=(sched[SCHED_SG].shape[0],),
            in_specs=[pl.BlockSpec((ROW_BLOCK, F), lambda s, *r: (r[SCHED_SB][s], 0)),
                      pl.BlockSpec(memory_space=pl.ANY),
                      pl.BlockSpec((None, 1, tn), lambda s, *r: (_step_expert(s, r), 0, _step_tile(s, r)))],
            out_specs=pl.BlockSpec((ROW_BLOCK * ROW_TILES, LANES), lambda s, *r: (r[SCHED_SB][s], 0)),
            scratch_shapes=[pltpu.VMEM((F, tn), F32), pltpu.VMEM((F, tn), BF16), pltpu.SemaphoreType.DMA(())]),
        out_shape=jax.ShapeDtypeStruct((P * ROW_TILES, LANES), jnp.uint32),
        compiler_params=_cparams(("arbitrary",)),
        name="moe_down",
    )(*sched, h, w_down, b3)


def _combine_kernel(dcur_ref, dnxt_ref, x1_ref, w_ref, fw_ref, ys_hbm, o_ref, buf, sems):
    i = pl.program_id(0)
    n_steps = pl.num_programs(0)
    tq = x1_ref.shape[0]
    n = TOP_K * tq
    slot = lax.rem(i, 2)

    def issue(idx_ref, sl):
        def body(q, carry):
            base = pl.multiple_of(q * SUBLANES, SUBLANES)
            for r in range(SUBLANES):
                _row_copy(ys_hbm, buf.at[sl], idx_ref[0, 0, base + r], (base + r) * ROW_TILES, sems.at[sl]).start()
            return carry

        lax.fori_loop(0, n // SUBLANES, body, 0)

    pl.when(i == 0)(lambda: issue(dcur_ref, 0))
    pl.when(i + 1 < n_steps)(lambda: issue(dnxt_ref, 1 - slot))
    pltpu.make_async_copy(ys_hbm.at[pl.ds(0, n * ROW_TILES), :], buf.at[slot], sems.at[slot]).wait()

    half = ROW_TILES * LANES
    acc_lo = x1_ref[:, :half]
    acc_hi = x1_ref[:, half:]
    for j in range(TOP_K):
        lo, hi = _unpack_bf16_pairs(_load_tiled(buf.at[slot], tq, first_row=j * tq))
        acc_lo = acc_lo + w_ref[:, j:j + 1] * lo.astype(F32)
        acc_hi = acc_hi + w_ref[:, j:j + 1] * hi.astype(F32)
    ssq = jnp.sum(acc_lo * acc_lo, axis=-1, keepdims=True) + jnp.sum(acc_hi * acc_hi, axis=-1, keepdims=True)
    scale = lax.rsqrt(ssq * (1.0 / (2 * half)) + NORM_EPS)
    o_ref[:, :half] = acc_lo * scale * fw_ref[:, :half]
    o_ref[:, half:] = acc_hi * scale * fw_ref[:, half:]


def combine(x1, ys, dest, top_w, final_w):
    T, D = x1.shape
    tq = COMBINE_TOKENS
    nblk = T // tq
    dest_blk = (dest * ROW_TILES).reshape(nblk, tq, TOP_K).transpose(0, 2, 1).reshape(nblk, 1, TOP_K * tq)
    idx_spec = lambda f: pl.BlockSpec((1, 1, TOP_K * tq), f, memory_space=pltpu.SMEM)
    return pl.pallas_call(
        _combine_kernel,
        grid=(nblk,),
        in_specs=[idx_spec(lambda i: (i, 0, 0)),
                  idx_spec(lambda i: (jnp.minimum(i + 1, nblk - 1), 0, 0)),
                  pl.BlockSpec((tq, D), lambda i: (i, 0)),
                  pl.BlockSpec((tq, TOP_K), lambda i: (i, 0)),
                  pl.BlockSpec((1, D), lambda i: (0, 0)),
                  pl.BlockSpec(memory_space=pl.ANY)],
        out_specs=pl.BlockSpec((tq, D), lambda i: (i, 0)),
        out_shape=jax.ShapeDtypeStruct((T, D), F32),
        scratch_shapes=[pltpu.VMEM((2, TOP_K * tq * ROW_TILES, LANES), ys.dtype), pltpu.SemaphoreType.DMA((2,))],
        compiler_params=_cparams(("arbitrary",)),
        name="moe_combine",
    )(dest_blk, dest_blk, x1, top_w, final_w.reshape(1, D), ys)


def _routing(logits):
    T = logits.shape[0]
    TK = T * TOP_K
    NB = TK // ROW_BLOCK + N_EXPERTS
    top_logits, top_idx = lax.top_k(logits[:, :N_EXPERTS], TOP_K)
    top_w = jax.nn.softmax(top_logits, axis=-1)
    flat_e = top_idx.reshape(TK).astype(jnp.int32)
    onehot = (flat_e[:, None] == jnp.arange(N_EXPERTS, dtype=jnp.int32)[None, :]).astype(F32)
    oh = onehot.reshape(TK // RANK_GROUP, RANK_GROUP, N_EXPERTS)
    local = jnp.einsum("ts,gse->gte", jnp.tril(jnp.ones((RANK_GROUP, RANK_GROUP), F32)), oh)
    tot = local[:, -1, :]
    offs = jnp.cumsum(tot, axis=0) - tot
    rank = jnp.sum(oh * (local + offs[:, None, :]), axis=-1).reshape(TK).astype(jnp.int32) - 1
    counts = (offs[-1] + tot[-1]).astype(jnp.int32)
    padded = (counts + ROW_BLOCK - 1) // ROW_BLOCK * ROW_BLOCK
    pad_end = jnp.cumsum(padded).astype(jnp.int32)
    dest = ((pad_end - padded)[flat_e] + rank).astype(jnp.int32)
    block_start = jnp.arange(NB, dtype=jnp.int32) * ROW_BLOCK
    block_e = jnp.minimum(jnp.sum((pad_end[None, :] <= block_start[:, None]).astype(jnp.int32), axis=1),
                          N_EXPERTS - 1)
    nb_used = pad_end[-1] // ROW_BLOCK
    blk = jnp.arange(NB, dtype=jnp.int32)
    rows_left = counts[block_e] - (blk - ((pad_end - padded) // ROW_BLOCK)[block_e]) * ROW_BLOCK
    block_full = ((blk < nb_used) & (rows_left > ROW_HALF)).astype(jnp.int32)
    tail = nb_used + jnp.arange(N_EXPERTS, dtype=jnp.int32)
    zrow = jnp.concatenate([jnp.where(padded > 0, pad_end - ROW_BLOCK, -1),
                            jnp.where(tail < NB, tail * ROW_BLOCK, -1)]).astype(jnp.int32)
    return dest.reshape(T, TOP_K), top_w, block_e, block_full, padded, nb_used, zrow


def _schedule(block_e, block_full, padded, nb_used, n_tiles):
    nb = block_e.shape[0]
    b = jnp.tile(jnp.arange(nb, dtype=jnp.int32), n_tiles)
    f = jnp.repeat(jnp.arange(n_tiles, dtype=jnp.int32), nb)
    key = jnp.where(b < nb_used, (block_e[b] * n_tiles + f) * nb + b, (N_EXPERTS * n_tiles + f) * nb + b)
    order = jnp.argsort(key)
    used = padded > 0
    n_groups = (jnp.sum(used) * n_tiles).astype(jnp.int32)
    expert_pos = jnp.cumsum(used) - 1
    group = jnp.minimum(expert_pos[block_e[b]] * n_tiles + f, n_groups - 1).astype(jnp.int32)
    n_valid = (nb_used * n_tiles).astype(jnp.int32)
    sg = jnp.where(jnp.arange(nb * n_tiles) < n_valid, group[order], n_groups - 1)
    experts = jnp.arange(N_EXPERTS, dtype=jnp.int32)
    used_first = jnp.argsort(jnp.where(used, experts, N_EXPERTS + experts)).astype(jnp.int32)
    ge = jnp.repeat(used_first, n_tiles)
    gf = jnp.tile(jnp.arange(n_tiles, dtype=jnp.int32), N_EXPERTS)
    return (n_valid.reshape(1), sg, b[order], f[order], block_full[b][order], ge, gf, n_groups.reshape(1))


def _repack_w_in(w_in):
    w_in = w_in.astype(BF16)
    z = lambda n: jnp.zeros((w_in.shape[0], n), w_in.dtype)
    o = RWKV_COLS
    w_r = jnp.concatenate([w_in[:, :RW_WD], w_in[:, RW_WD:RW_WD + DECAY_LORA], z(LORA_PAD - DECAY_LORA),
                           w_in[:, RW_WD + DECAY_LORA:RW_WD + DECAY_LORA + AAA_LORA], z(LORA_PAD - AAA_LORA),
                           w_in[:, RW_WD + DECAY_LORA + AAA_LORA:o]], axis=1)
    w_m = jnp.concatenate([w_in[:, o:o + ML_IF], w_in[:, o + ML_IF:o + ML_IF + 2 * MLSTM_HEADS],
                           z(LANES - 2 * MLSTM_HEADS), w_in[:, o + ML_IF + 2 * MLSTM_HEADS:o + MLSTM_COLS]], axis=1)
    w_g = w_in[:, o + MLSTM_COLS:]
    return w_r, w_m, w_g


def kernel(x, norm1_w, w_in, b_gate, rwkv_mu, rwkv_w0, rwkv_w_up, rwkv_a0, rwkv_a_up, rwkv_g_up, rwkv_k_k,
           rwkv_k_a, rwkv_r_k, rwkv_ln_w, rwkv_ln_b, mlstm_conv_w, mlstm_i_b, mlstm_f_b, mlstm_norm_w,
           w_branch_a, w_branch_b, w_out, norm2_w, router_w, router_b, w_gu, b_gu, w_down, b_down,
           final_norm_w):
    B, S, D = x.shape
    T = B * S
    xt = x.reshape(T, D)
    assert norm1_w.shape[0] == 1, "single-layer block: the final rmsnorm is fused into the MoE combine"
    for l in range(1):
        hn = rmsnorm_rows(xt, norm1_w[l])
        w_r, w_m, w_g = _repack_w_in(w_in[l])
        p_r = matmul(hn, w_r, 512, RW_COLS_P // 2, F32, "proj_rwkv").reshape(B, S, RW_COLS_P)
        p_m = matmul(hn, w_m, 256, ML_COLS_P, F32, "proj_mlstm").reshape(B, S, ML_COLS_P)
        o_a = rwkv_branch(p_r, rwkv_mu[l], rwkv_w0[l], rwkv_w_up[l], rwkv_a0[l], rwkv_a_up[l], rwkv_g_up[l],
                          rwkv_k_k[l], rwkv_k_a[l], rwkv_r_k[l].reshape(-1), rwkv_ln_w[l], rwkv_ln_b[l])
        o_b = mlstm_branch(p_m, mlstm_conv_w[l], mlstm_i_b[l], mlstm_f_b[l], mlstm_norm_w[l])
        x1, xp, logits = merge_project(o_a.reshape(T, RWKV_DIM), o_b.reshape(T, MLSTM_DIM), hn, w_g, b_gate[l], xt,
                                       w_branch_a[l], w_branch_b[l], w_out[l], norm2_w[l], router_w[l],
                                       router_b[l])
        dest, top_w, block_e, block_full, padded, nb_used, zrow = _routing(logits)
        xs = moe_dispatch(xp, dest, zrow, block_e.shape[0] * ROW_BLOCK)
        h = moe_up(xs, w_gu[l], b_gu[l], _schedule(block_e, block_full, padded, nb_used, EXPERT_FF // UP_TILE))
        ys = moe_down(h, w_down[l], b_down[l], _schedule(block_e, block_full, padded, nb_used, D // DOWN_TILE))
        xt = combine(x1, ys, dest, top_w, final_norm_w)
    return xt.reshape(B, S, D)
```

```python
import functools

import jax
import jax.numpy as jnp
import numpy as np
from jax import lax
from jax.experimental import pallas as pl
from jax.experimental.pallas import tpu as pltpu

F32 = jnp.float32
BF16 = jnp.bfloat16

D_MODEL = 2048
CHUNK = 64
NORM_EPS = 1e-6
RWKV_HEADS = 16
RWKV_HEAD_DIM = 64
RWKV_DIM = 1024
DECAY_LORA = 96
AAA_LORA = 96
GATE_LORA = 256
GN_EPS = 64e-5
RWKV_COLS = 3 * RWKV_DIM + DECAY_LORA + AAA_LORA + GATE_LORA
MLSTM_HEADS = 4
MLSTM_QK_DIM = 128
MLSTM_V_DIM = 256
MLSTM_QK = 512
MLSTM_DIM = 1024
CONV_WIDTH = 4
GATE_SOFTCAP = 15.0
MLSTM_COLS = 2 * MLSTM_QK + 2 * MLSTM_DIM + 2 * MLSTM_HEADS
N_EXPERTS = 32
TOP_K = 4
EXPERT_FF = 2048
SWIGLU_LIMIT = 7.0
SWIGLU_ALPHA = 1.702

LANES = 128
SUBLANES = 8
VMEM_LIMIT = 56 * 1024 * 1024

LORA_PAD = 128
RW_WD = 3 * RWKV_DIM
RW_AD = RW_WD + LORA_PAD
RW_GD = RW_AD + LORA_PAD
RW_COLS_P = RW_GD + GATE_LORA
ML_V = 2 * MLSTM_QK
ML_IF = ML_V + MLSTM_DIM
ML_O = ML_IF + LANES
ML_COLS_P = ML_O + MLSTM_DIM
ROUTER_PAD = 128
SCAN_SEQS = 4
MLSTM_SEQS = 1

ROW_BLOCK = 512
ROW_HALF = ROW_BLOCK // 2


def _cparams(sem):
    return pltpu.CompilerParams(dimension_semantics=sem, vmem_limit_bytes=VMEM_LIMIT)


def _bdot(a, b):
    return jnp.dot(a.astype(BF16), b.astype(BF16), preferred_element_type=F32)


def _split3(x):
    hi = x.astype(BF16)
    r1 = x - hi.astype(F32)
    mid = r1.astype(BF16)
    lo = (r1 - mid.astype(F32)).astype(BF16)
    return hi, mid, lo


def _dot_exact_lhs(mat_bf16, x):
    hi, mid, lo = _split3(x)
    return (jnp.dot(mat_bf16, hi, preferred_element_type=F32)
            + jnp.dot(mat_bf16, mid, preferred_element_type=F32)
            + jnp.dot(mat_bf16, lo, preferred_element_type=F32))


def _dot_exact_rhs(x, mat_bf16):
    hi, mid, lo = _split3(x)
    return (jnp.dot(hi, mat_bf16, preferred_element_type=F32)
            + jnp.dot(mid, mat_bf16, preferred_element_type=F32)
            + jnp.dot(lo, mat_bf16, preferred_element_type=F32))


def _sigmoid(x):
    return 1.0 / (1.0 + jnp.exp(-x))


def _softplus(x):
    return jnp.maximum(x, 0.0) + jnp.log(1.0 + jnp.exp(-jnp.abs(x)))


def _mm_kernel(a_ref, b_ref, o_ref):
    o_ref[...] = jnp.dot(a_ref[...], b_ref[...], preferred_element_type=F32).astype(o_ref.dtype)


def matmul(a, b, tm, tn, out_dtype, name):
    M, K = a.shape
    N = b.shape[1]
    return pl.pallas_call(
        _mm_kernel,
        grid=(N // tn, M // tm),
        in_specs=[pl.BlockSpec((tm, K), lambda j, i: (i, 0)),
                  pl.BlockSpec((K, tn), lambda j, i: (0, j))],
        out_specs=pl.BlockSpec((tm, tn), lambda j, i: (i, j)),
        out_shape=jax.ShapeDtypeStruct((M, N), out_dtype),
        compiler_params=_cparams(("parallel", "parallel")),
        name=name,
    )(a, b)


def _head_sum_mat():
    r = lax.broadcasted_iota(jnp.int32, (LANES, LANES), 0) // RWKV_HEAD_DIM
    c = lax.broadcasted_iota(jnp.int32, (LANES, LANES), 1) // RWKV_HEAD_DIM
    return jnp.where(r == c, 1.0, 0.0).astype(BF16)


def _pad_rows(w, n):
    return jnp.pad(w, ((0, n - w.shape[0]), (0, 0)))


PREP_GROUP = 256


def _rwkv_in_kernel(x_ref, n1_ref, w_ref, mu_ref, w0_ref, wup_ref, a0_ref, aup_ref, gup_ref, kk_ref, ka_ref,
                    hn_out, r_out, k_out, v_out, kk_out, b_out, lw_out, g_out, carry_ref):
    i = pl.program_id(1)

    @pl.when(i == 0)
    def _():
        carry_ref[...] = jnp.zeros_like(carry_ref)

    x = x_ref[...]
    hn = (x * lax.rsqrt(jnp.mean(x * x, axis=-1, keepdims=True) + NORM_EPS) * n1_ref[...]).astype(BF16)
    hn_out[...] = hn
    tq = x.shape[0]
    row = lax.broadcasted_iota(jnp.int32, (tq, 1), 0)

    def shifted(cols):
        p = jnp.dot(hn, w_ref[:, cols], preferred_element_type=F32)
        prev = jnp.where(row == 0, carry_ref[0:1, cols], pltpu.roll(p, 1, 0))
        carry_ref[0:1, cols] = p[tq - 1:tq, :]
        return p + (prev - p) * mu_ref[:, cols]

    lora = shifted(slice(RW_WD, RW_COLS_P))
    wd_t = jnp.tanh(lora[:, 0:LORA_PAD]).astype(BF16)
    ad = lora[:, LORA_PAD:2 * LORA_PAD].astype(BF16)
    gd_s = _sigmoid(lora[:, 2 * LORA_PAD:]).astype(BF16)
    hs = _head_sum_mat()

    for c in range(0, RWKV_DIM, PREP_GROUP):
        cs = slice(c, c + PREP_GROUP)
        r = shifted(cs)
        k = shifted(slice(RWKV_DIM + c, RWKV_DIM + c + PREP_GROUP))
        v = shifted(slice(2 * RWKV_DIM + c, 2 * RWKV_DIM + c + PREP_GROUP))
        w_log = -_softplus(-(w0_ref[:, cs] + jnp.dot(wd_t, wup_ref[:, cs], preferred_element_type=F32))) - 0.5
        a = _sigmoid(a0_ref[:, cs] + jnp.dot(ad, aup_ref[:, cs], preferred_element_type=F32))
        kk = k * kk_ref[:, cs]
        nrm2 = jnp.concatenate(
            [_dot_exact_rhs(kk[:, t:t + LANES] * kk[:, t:t + LANES], hs) for t in range(0, PREP_GROUP, LANES)],
            axis=1)
        kk = kk / jnp.maximum(jnp.sqrt(nrm2), 1e-12)
        r_out[:, cs] = r.astype(r_out.dtype)
        k_out[:, cs] = (k * (1.0 + (a - 1.0) * ka_ref[:, cs])).astype(k_out.dtype)
        v_out[:, cs] = v.astype(v_out.dtype)
        kk_out[:, cs] = kk.astype(kk_out.dtype)
        b_out[:, cs] = (kk * a).astype(b_out.dtype)
        lw_out[:, cs] = -jnp.exp(w_log)
        g_out[:, cs] = jnp.dot(gd_s, gup_ref[:, cs], preferred_element_type=F32).astype(g_out.dtype)


def rwkv_in(x, norm1_w, w_r, mu, w0, w_up, a0, a_up, g_up, k_k, k_a, tq=256):
    B, S, D = x.shape
    blk = lambda n: pl.BlockSpec((None, tq, n), lambda b, i: (b, i, 0))
    full = lambda a: pl.BlockSpec(a.shape, lambda b, i: (0,) * a.ndim, pipeline_mode=pl.Buffered(1))
    row = lambda t: t.reshape(1, -1)
    mu_p = jnp.concatenate([mu[:RW_WD], jnp.pad(mu[RW_WD:RW_WD + DECAY_LORA], (0, LORA_PAD - DECAY_LORA)),
                            jnp.pad(mu[RW_WD + DECAY_LORA:RW_WD + DECAY_LORA + AAA_LORA], (0, LORA_PAD - AAA_LORA)),
                            mu[RW_WD + DECAY_LORA + AAA_LORA:]])
    params = (row(norm1_w), w_r, row(mu_p), row(w0), _pad_rows(w_up, LORA_PAD).astype(BF16), row(a0),
              _pad_rows(a_up, LORA_PAD).astype(BF16), g_up.astype(BF16), row(k_k), row(k_a))
    out = lambda n, dt: jax.ShapeDtypeStruct((B, S, n), dt)
    return pl.pallas_call(
        _rwkv_in_kernel,
        grid=(B, S // tq),
        in_specs=[blk(D)] + [full(a) for a in params],
        out_specs=[blk(D)] + [blk(RWKV_DIM)] * 7,
        out_shape=[out(D, BF16)] + [out(RWKV_DIM, BF16)] * 5 + [out(RWKV_DIM, F32), out(RWKV_DIM, BF16)],
        scratch_shapes=[pltpu.VMEM((SUBLANES, RW_COLS_P), F32)],
        compiler_params=_cparams(("parallel", "arbitrary")),
        name="rwkv_in",
    )(x, *params)


def _rwkv_scan_kernel(r_ref, k_ref, v_ref, kk_ref, b_ref, lw_ref, g_ref, rk_ref, lnw_ref, lnb_ref,
                      o_ref, h_ref):
    c = pl.program_id(1)

    @pl.when(c == 0)
    def _():
        h_ref[...] = jnp.zeros_like(h_ref)

    L = CHUNK
    L2 = 2 * L
    ri = lax.broadcasted_iota(jnp.int32, (L, L), 0)
    ci = lax.broadcasted_iota(jnp.int32, (L, L), 1)
    tril = jnp.where(ri >= ci, 1.0, 0.0).astype(BF16)

    n_seq = lw_ref.shape[0]
    cat = lambda ref: jnp.concatenate([ref[i] for i in range(n_seq)], axis=1)
    rep = lambda ref: jnp.concatenate([ref[...]] * n_seq, axis=1)
    lw = cat(lw_ref)
    cum = _dot_exact_lhs(tril, lw)
    cum_end = cum[L - 1:L, :]
    w_in = jnp.exp(cum)
    w_prev = jnp.exp(cum - lw)
    w_inv = jnp.exp(-cum)
    w_tail = jnp.exp(cum_end - cum)
    w_end = jnp.exp(cum_end)

    kk = cat(kk_ref).astype(F32)
    bb = cat(b_ref).astype(F32)
    kx = cat(k_ref).astype(F32)
    rx = cat(r_ref).astype(F32)
    vx = cat(v_ref).astype(F32)
    gx = cat(g_ref).astype(F32)
    a_hat = -kk * w_prev
    r_hat = rx * w_in
    b_hat = bb * w_inv
    k_hat = kx * w_inv
    b_til = bb * w_tail
    k_til = kx * w_tail
    rkk = rx * kx * rep(rk_ref)

    lane = lax.broadcasted_iota(jnp.int32, (1, LANES), 1)
    m_lo = jnp.where(lane < RWKV_HEAD_DIM, 1.0, 0.0)
    m_hi = 1.0 - m_lo

    def stack(x):
        return jnp.concatenate([x * m_lo, x * m_hi], axis=0)

    r2 = lax.broadcasted_iota(jnp.int32, (L2, L2), 0)
    c2 = lax.broadcasted_iota(jnp.int32, (L2, L2), 1)
    same_head = (r2 // L) == (c2 // L)
    strict = same_head & (r2 > c2)
    incl = same_head & (r2 >= c2)
    diag16 = (r2 // 16) == (c2 // 16)
    eye = jnp.where(r2 == c2, 1.0, 0.0)
    hs = _head_sum_mat()

    pairs_per_seq = RWKV_DIM // LANES
    pairs = range(n_seq * pairs_per_seq)
    sls = [slice(p * LANES, (p + 1) * LANES) for p in pairs]
    v_st = [stack(vx[:, sl]) for sl in sls]
    lhs = [jnp.concatenate([stack(a_hat[:, sl]), stack(r_hat[:, sl])], axis=0).astype(BF16) for sl in sls]
    rhs = [jnp.concatenate([stack(b_hat[:, sl]), stack(k_hat[:, sl])], axis=0).astype(BF16) for sl in sls]
    sc = [lax.dot_general(lhs[p], rhs[p], (((1,), (1,)), ((), ())), preferred_element_type=F32) for p in pairs]
    n_ab = [jnp.where(strict, sc[p][:L2, :L2], 0.0) for p in pairs]
    a_ak = [jnp.where(strict, sc[p][:L2, L2:], 0.0).astype(BF16) for p in pairs]
    a_r = [jnp.concatenate([jnp.where(incl, sc[p][L2:, :L2], 0.0), jnp.where(incl, sc[p][L2:, L2:], 0.0)],
                           axis=1).astype(BF16) for p in pairs]

    nd = [jnp.where(diag16, n_ab[p], 0.0) for p in pairs]
    noff = [(n_ab[p] - nd[p]).astype(BF16) for p in pairs]
    ndb = [nd[p].astype(BF16) for p in pairs]
    s2 = [jnp.dot(ndb[p], ndb[p], preferred_element_type=F32).astype(BF16) for p in pairs]
    s4 = [jnp.dot(s2[p], s2[p], preferred_element_type=F32).astype(BF16) for p in pairs]
    s8 = [jnp.dot(s4[p], s4[p], preferred_element_type=F32).astype(BF16) for p in pairs]
    x1 = [eye + nd[p] for p in pairs]
    x2 = [x1[p] + _bdot(x1[p], s2[p]) for p in pairs]
    x3 = [x2[p] + _bdot(x2[p], s4[p]) for p in pairs]
    t_d = [(x3[p] + _bdot(x3[p], s8[p])).astype(BF16) for p in pairs]
    m1 = [jnp.dot(t_d[p], noff[p], preferred_element_type=F32) for p in pairs]
    m1b = [m1[p].astype(BF16) for p in pairs]
    m2 = [jnp.dot(m1b[p], m1b[p], preferred_element_type=F32) for p in pairs]
    m3 = [jnp.dot(m1b[p], m2[p].astype(BF16), preferred_element_type=F32) for p in pairs]
    t_inv = [jnp.dot((eye + m1[p] + m2[p] + m3[p]).astype(BF16), t_d[p], preferred_element_type=F32).astype(BF16)
             for p in pairs]

    h0 = [h_ref[p] for p in pairs]
    ah = [jnp.dot(lhs[p], h0[p].astype(BF16), preferred_element_type=F32) for p in pairs]
    x = [ah[p][:L2] + jnp.dot(a_ak[p], v_st[p].astype(BF16), preferred_element_type=F32) for p in pairs]
    u = [jnp.dot(t_inv[p], x[p].astype(BF16), preferred_element_type=F32) for p in pairs]
    uv = [jnp.concatenate([u[p], v_st[p]], axis=0).astype(BF16) for p in pairs]
    y_st = [ah[p][L2:] + jnp.dot(a_r[p], uv[p], preferred_element_type=F32) for p in pairs]
    y = [y_st[p][:L] + y_st[p][L:] for p in pairs]

    for p in pairs:
        sl = sls[p]
        upd_l = jnp.concatenate([stack(b_til[:, sl]), stack(k_til[:, sl])], axis=0).astype(BF16)
        upd = lax.dot_general(upd_l, uv[p], (((0,), (0,)), ((), ())), preferred_element_type=F32)
        w_col = jnp.sum(eye * w_end[:, sl], axis=1, keepdims=True)
        h_ref[p] = w_col * h0[p] + upd

    def head_sums(vals):
        parts = []
        for t in vals:
            hi = t.astype(BF16)
            parts += [hi, (t - hi.astype(F32)).astype(BF16)]
        res = jnp.dot(jnp.concatenate(parts, axis=0), hs, preferred_element_type=F32)
        return [res[2 * i * L:(2 * i + 1) * L] + res[(2 * i + 1) * L:(2 * i + 2) * L] for i in range(len(vals))]

    sums1 = [head_sums([y[p], rkk[:, sls[p]]]) for p in pairs]
    d = [y[p] - sums1[p][0] * (1.0 / RWKV_HEAD_DIM) for p in pairs]
    var = [head_sums([d[p] * d[p]])[0] * (1.0 / RWKV_HEAD_DIM) for p in pairs]
    for p in pairs:
        sl = sls[p]
        psl = sls[p % pairs_per_seq]
        yn = d[p] * lax.rsqrt(var[p] + GN_EPS) * lnw_ref[:, psl] + lnb_ref[:, psl]
        o_ref[p // pairs_per_seq, :, psl] = ((yn + sums1[p][1] * vx[:, sl]) * gx[:, sl]).astype(o_ref.dtype)


def rwkv_scan(r, k, v, kk, b, lw, g, r_k, ln_w, ln_b):
    B, S, C = r.shape
    blk = pl.BlockSpec((SCAN_SEQS, CHUNK, C), lambda bb, c: (bb, c, 0))
    full = pl.BlockSpec((1, C), lambda bb, c: (0, 0))
    return pl.pallas_call(
        _rwkv_scan_kernel,
        grid=(B // SCAN_SEQS, S // CHUNK),
        in_specs=[blk] * 7 + [full] * 3,
        out_specs=blk,
        out_shape=jax.ShapeDtypeStruct((B, S, C), BF16),
        scratch_shapes=[pltpu.VMEM((SCAN_SEQS * C // LANES, LANES, LANES), F32)],
        compiler_params=_cparams(("parallel", "arbitrary")),
        name="rwkv_scan",
    )(r, k, v, kk, b, lw, g, r_k, ln_w, ln_b)


def _mlstm_kernel(p_ref, convw_ref, ifb_ref, nw_ref, o_ref, carry_ref, c_ref, n_ref, m_ref):
    ci = pl.program_id(1)

    @pl.when(ci == 0)
    def _():
        carry_ref[...] = jnp.zeros_like(carry_ref)
        c_ref[...] = jnp.zeros_like(c_ref)
        n_ref[...] = jnp.zeros_like(n_ref)
        m_ref[...] = jnp.zeros_like(m_ref)

    L = CHUNK
    n_seq = p_ref.shape[0]
    lane = lax.broadcasted_iota(jnp.int32, (1, LANES), 1)
    ri = lax.broadcasted_iota(jnp.int32, (L, L), 0)
    cj = lax.broadcasted_iota(jnp.int32, (L, L), 1)
    causal = ri >= cj
    tril = jnp.where(causal, 1.0, 0.0).astype(BF16)
    triu = jnp.where(ri <= cj, 1.0, 0.0).astype(BF16)

    qk, comb, comb_t, b_col, b_row = [], [], [], [], []
    for i in range(n_seq):
        u = p_ref[i, :, 0:ML_V]
        ext = jnp.concatenate([carry_ref[i], u], axis=0)
        carry_ref[i] = u[L - SUBLANES:L, :]
        conv = convw_ref[CONV_WIDTH - 1:CONV_WIDTH, :] * u
        for j in range(1, CONV_WIDTH):
            conv = conv + convw_ref[CONV_WIDTH - 1 - j:CONV_WIDTH - j, :] * pltpu.roll(ext, j, 0)[SUBLANES:, :]
        qk.append(conv * _sigmoid(conv))
        pre = GATE_SOFTCAP * jnp.tanh((p_ref[i, :, ML_IF:ML_O] + ifb_ref[...]) * (1.0 / GATE_SOFTCAP))
        cb = jnp.where(lane < MLSTM_HEADS, pre, -_softplus(-pre))
        comb.append(cb)
        comb_t.append(cb.T)
        b_col.append(_dot_exact_lhs(tril, cb))
        b_row.append(_dot_exact_rhs(comb_t[i], triu))

    H = range(n_seq * MLSTM_HEADS)
    sq = [v // MLSTM_HEADS for v in H]
    hd = [v % MLSTM_HEADS for v in H]
    dk, dv = MLSTM_QK_DIM, MLSTM_V_DIM
    qh = [qk[sq[v]][:, hd[v] * dk:(hd[v] + 1) * dk] * (dk ** -0.5) for v in H]
    kh = [qk[sq[v]][:, MLSTM_QK + hd[v] * dk:MLSTM_QK + (hd[v] + 1) * dk] for v in H]
    vh = [p_ref[sq[v], :, ML_V + hd[v] * dv:ML_V + (hd[v] + 1) * dv].astype(BF16) for v in H]
    qb = [qh[h].astype(BF16) for h in H]
    bcol = [b_col[sq[v]][:, MLSTM_HEADS + hd[v]:MLSTM_HEADS + hd[v] + 1] for v in H]
    brow = [b_row[sq[v]][MLSTM_HEADS + hd[v]:MLSTM_HEADS + hd[v] + 1, :] for v in H]
    m_prev = [m_ref[h][0:1, 0:1] for h in H]
    n_prev = [n_ref[h][0:1, :] for h in H]
    c_prev = [c_ref[h] for h in H]

    qk_t = [lax.dot_general(qb[h], kh[h].astype(BF16), (((1,), (1,)), ((), ())), preferred_element_type=F32)
            for h in H]
    qc = [jnp.dot(qb[h], c_prev[h].astype(BF16), preferred_element_type=F32) for h in H]
    dm = [jnp.where(causal, bcol[h] - brow[h] + comb_t[sq[h]][hd[h]:hd[h] + 1, :], -jnp.inf) for h in H]
    inter = [bcol[h] + m_prev[h] for h in H]
    m_t = [jnp.maximum(inter[h], jnp.max(dm[h], axis=-1, keepdims=True)) for h in H]
    s = [qk_t[h] * jnp.exp(dm[h] - m_t[h]) for h in H]
    w_inter = [jnp.exp(inter[h] - m_t[h]) for h in H]
    num = [jnp.dot(s[h].astype(BF16), vh[h], preferred_element_type=F32) + w_inter[h] * qc[h] for h in H]
    den = [jnp.sum(s[h], axis=-1, keepdims=True) + w_inter[h] * jnp.sum(qh[h] * n_prev[h], axis=-1, keepdims=True)
           for h in H]
    hh = [num[h] / jnp.maximum(jnp.abs(den[h]), jnp.exp(-m_t[h])) for h in H]

    g_tot = [bcol[h][L - 1:L, :] for h in H]
    a = [comb[sq[h]][:, hd[h]:hd[h] + 1] + g_tot[h] - bcol[h] for h in H]
    m_new = [jnp.maximum(g_tot[h] + m_prev[h], jnp.max(a[h], axis=0, keepdims=True)) for h in H]
    dec = [jnp.exp(g_tot[h] + m_prev[h] - m_new[h]) for h in H]
    wkk = [jnp.exp(a[h] - m_new[h]) * kh[h] for h in H]
    for h in H:
        c_ref[h] = dec[h] * c_prev[h] + lax.dot_general(wkk[h].astype(BF16), vh[h], (((0,), (0,)), ((), ())),
                                                        preferred_element_type=F32)
        n_ref[h] = jnp.broadcast_to(dec[h] * n_prev[h] + jnp.sum(wkk[h], axis=0, keepdims=True),
                                    (SUBLANES, LANES))
        m_ref[h] = jnp.broadcast_to(m_new[h], (SUBLANES, LANES))
    for h in H:
        vs = slice(hd[h] * dv, (hd[h] + 1) * dv)
        hn = hh[h] * lax.rsqrt(jnp.mean(hh[h] * hh[h], axis=-1, keepdims=True) + NORM_EPS)
        o_raw = p_ref[sq[h], :, ML_O + hd[h] * dv:ML_O + (hd[h] + 1) * dv]
        o_ref[sq[h], :, vs] = (hn * nw_ref[:, vs] * _sigmoid(o_raw)).astype(o_ref.dtype)


def mlstm_branch(p_pad, conv_w, i_b, f_b, norm_w):
    B, S, C = p_pad.shape
    ifb = jnp.pad(jnp.concatenate([i_b, f_b]), (0, LANES - 2 * MLSTM_HEADS)).reshape(1, LANES)
    full = lambda a: pl.BlockSpec(a.shape, lambda b, c: (0,) * a.ndim)
    nw = norm_w.reshape(1, MLSTM_DIM)
    return pl.pallas_call(
        _mlstm_kernel,
        grid=(B // MLSTM_SEQS, S // CHUNK),
        in_specs=[pl.BlockSpec((MLSTM_SEQS, CHUNK, C), lambda b, c: (b, c, 0)), full(conv_w), full(ifb), full(nw)],
        out_specs=pl.BlockSpec((MLSTM_SEQS, CHUNK, MLSTM_DIM), lambda b, c: (b, c, 0)),
        out_shape=jax.ShapeDtypeStruct((B, S, MLSTM_DIM), BF16),
        scratch_shapes=[pltpu.VMEM((MLSTM_SEQS, SUBLANES, ML_V), F32),
                        pltpu.VMEM((MLSTM_SEQS * MLSTM_HEADS, MLSTM_QK_DIM, MLSTM_V_DIM), F32),
                        pltpu.VMEM((MLSTM_SEQS * MLSTM_HEADS, SUBLANES, LANES), F32),
                        pltpu.VMEM((MLSTM_SEQS * MLSTM_HEADS, SUBLANES, LANES), F32)],
        compiler_params=_cparams(("parallel", "arbitrary")),
        name="mlstm_scan",
    )(p_pad, conv_w, ifb, nw)


HI_MASK = 0xFFFF0000


def _pack_bf16_pairs(hb_f32):
    c = hb_f32.shape[1] // 2
    u = pltpu.bitcast(hb_f32, jnp.uint32)
    return u[:, c:] | (u[:, :c] >> 16)


ROW_TILES = D_MODEL // 2 // LANES


def _store_tiled(ref, val, first_row=0):
    n = val.shape[0]
    for t in range(ROW_TILES):
        ref[pl.ds(first_row * ROW_TILES + t, n, stride=ROW_TILES), :] = val[:, t * LANES:(t + 1) * LANES]


def _load_tiled(ref, n, first_row=0):
    return jnp.concatenate([ref[pl.ds(first_row * ROW_TILES + t, n, stride=ROW_TILES), :] for t in range(ROW_TILES)],
                           axis=1)


def _unpack_bf16_pairs(xu):
    lo = pltpu.bitcast(xu << 16, F32).astype(BF16)
    hi = pltpu.bitcast(xu & jnp.uint32(HI_MASK), F32).astype(BF16)
    return lo, hi


def _merge_kernel(oa_ref, ob_ref, hn_ref, bg_ref, x_ref, wa_ref, wb_ref, wg_ref, wo_ref, n2_ref, rwh_ref, rwl_ref,
                  rb_ref, x1_ref, xp_ref, lg_ref):
    hn1 = hn_ref[...]

    def gated(o_ref, w_ref, cols):
        gate = _sigmoid(jnp.dot(hn1, wg_ref[:, cols], preferred_element_type=F32) + bg_ref[:, cols])
        return gate * jnp.dot(o_ref[...], w_ref[...], preferred_element_type=F32)

    merged = gated(oa_ref, wa_ref, slice(0, D_MODEL)) + gated(ob_ref, wb_ref, slice(D_MODEL, 2 * D_MODEL))
    x1 = x_ref[...] + jnp.dot(merged.astype(BF16), wo_ref[...], preferred_element_type=F32)
    x1_ref[...] = x1
    hn = x1 * lax.rsqrt(jnp.mean(x1 * x1, axis=-1, keepdims=True) + NORM_EPS) * n2_ref[...]
    hi = hn.astype(BF16)
    _store_tiled(xp_ref, _pack_bf16_pairs(hi.astype(F32)))
    lo = (hn - hi.astype(F32)).astype(BF16)
    lg_ref[...] = (jnp.dot(hi, rwh_ref[...], preferred_element_type=F32)
                   + jnp.dot(hi, rwl_ref[...], preferred_element_type=F32)
                   + jnp.dot(lo, rwh_ref[...], preferred_element_type=F32) + rb_ref[...])


def merge_project(o_a, o_b, hn, w_gate, b_gate, x, w_a, w_b, w_out, norm2_w, router_w, router_b, tm=256):
    T, D = x.shape
    rw = jnp.pad(router_w, ((0, 0), (0, ROUTER_PAD - N_EXPERTS)))
    rw_hi = rw.astype(BF16)
    rw_lo = (rw - rw_hi.astype(F32)).astype(BF16)
    rb = jnp.pad(router_b, (0, ROUTER_PAD - N_EXPERTS)).reshape(1, ROUTER_PAD)
    rows = lambda n: pl.BlockSpec((tm, n), lambda i: (i, 0))
    full = lambda a: pl.BlockSpec(a.shape, lambda i: (0, 0), pipeline_mode=pl.Buffered(1))
    params = (w_a.astype(BF16), w_b.astype(BF16), w_gate, w_out.astype(BF16), norm2_w.reshape(1, D), rw_hi, rw_lo,
              rb)
    bg = b_gate.reshape(1, 2 * D)
    return pl.pallas_call(
        _merge_kernel,
        grid=(T // tm,),
        in_specs=[rows(o_a.shape[1]), rows(o_b.shape[1]), rows(D), full(bg), rows(D)] + [full(a) for a in params],
        out_specs=[rows(D), pl.BlockSpec((tm * ROW_TILES, LANES), lambda i: (i, 0)), rows(ROUTER_PAD)],
        out_shape=[jax.ShapeDtypeStruct((T, D), F32), jax.ShapeDtypeStruct((T * ROW_TILES, LANES), jnp.uint32),
                   jax.ShapeDtypeStruct((T, ROUTER_PAD), F32)],
        compiler_params=_cparams(("parallel",)),
        name="merge_project",
    )(o_a, o_b, hn, bg, x, *params)


UP_TILE = 1024
DOWN_TILE = 2048
RANK_GROUP = 256
DISPATCH_TOKENS = 256
COMBINE_TOKENS = 128
DMA_UNROLL = 8


def _row_copy(src, dst, src_at, dst_at, sem):
    return pltpu.make_async_copy(src.at[pl.ds(pl.multiple_of(src_at, ROW_TILES), ROW_TILES), :],
                                 dst.at[pl.ds(pl.multiple_of(dst_at, ROW_TILES), ROW_TILES), :], sem)


def _dispatch_kernel(zrow_ref, dest_ref, xp_ref, xs_hbm, zbuf, zsem, sem):
    i = pl.program_id(0)
    tq = dest_ref.shape[2] // TOP_K

    def zero_copy(k):
        at = pl.multiple_of(zrow_ref[k], ROW_BLOCK * ROW_TILES)
        return pltpu.make_async_copy(zbuf, xs_hbm.at[pl.ds(at, ROW_BLOCK * ROW_TILES), :], zsem)

    @pl.when(i == 0)
    def _():
        zbuf[...] = jnp.zeros_like(zbuf)
        for k in range(zrow_ref.shape[0]):
            pl.when(zrow_ref[k] >= 0)(lambda k=k: zero_copy(k).start())
        for k in range(zrow_ref.shape[0]):
            pl.when(zrow_ref[k] >= 0)(lambda k=k: zero_copy(k).wait())

    def body(q, carry):
        base = pl.multiple_of(q * SUBLANES, SUBLANES)
        for r in range(SUBLANES):
            for j in range(TOP_K):
                _row_copy(xp_ref, xs_hbm, (base + r) * ROW_TILES, dest_ref[0, 0, TOP_K * (base + r) + j],
                          sem).start(priority=j % 2)
        return carry

    lax.fori_loop(0, tq // SUBLANES, body, 0)
    for j in range(TOP_K):
        pltpu.make_async_copy(xp_ref, xs_hbm.at[pl.ds(0, tq * ROW_TILES), :], sem).wait()


def moe_dispatch(xp, dest, zrow, n_rows):
    T = xp.shape[0] // ROW_TILES
    tq = DISPATCH_TOKENS
    dest_blk = (dest * ROW_TILES).reshape(T // tq, 1, TOP_K * tq)
    zrow = jnp.where(zrow >= 0, zrow * ROW_TILES, -1)
    return pl.pallas_call(
        _dispatch_kernel,
        grid_spec=pltpu.PrefetchScalarGridSpec(
            num_scalar_prefetch=1,
            grid=(T // tq,),
            in_specs=[pl.BlockSpec((1, 1, TOP_K * tq), lambda i, zr: (i, 0, 0), memory_space=pltpu.SMEM),
                      pl.BlockSpec((tq * ROW_TILES, LANES), lambda i, zr: (i, 0))],
            out_specs=pl.BlockSpec(memory_space=pl.ANY),
            scratch_shapes=[pltpu.VMEM((ROW_BLOCK * ROW_TILES, LANES), xp.dtype), pltpu.SemaphoreType.DMA(()),
                            pltpu.SemaphoreType.DMA(())]),
        out_shape=jax.ShapeDtypeStruct((n_rows * ROW_TILES, LANES), xp.dtype),
        compiler_params=_cparams(("arbitrary",)),
        name="moe_dispatch",
    )(zrow, dest_blk, xp)


SCHED_NV, SCHED_SG, SCHED_SB, SCHED_SO, SCHED_FULL, SCHED_GE, SCHED_GF, SCHED_NG = range(8)


def _stream_weights(s, sched, copies, on_arrival):
    sg_ref = sched[SCHED_SG]
    g = sg_ref[s]
    first = (s < sched[SCHED_NV][0]) & ((s == 0) | (g != sg_ref[jnp.maximum(s - 1, 0)]))

    @pl.when(first)
    def _():
        @pl.when(s == 0)
        def _():
            for c in copies(g):
                c.start()

        for c in copies(g):
            c.wait()
        on_arrival()

        @pl.when(g + 1 < sched[SCHED_NG][0])
        def _():
            for c in copies(g + 1):
                c.start()


def _for_used_rows(s, sched, out_ref, compute):
    valid = s < sched[SCHED_NV][0]
    full = sched[SCHED_FULL][s] > 0

    @pl.when(jnp.logical_not(valid))
    def _():
        out_ref[...] = jnp.zeros_like(out_ref)

    @pl.when(valid & full)
    def _():
        compute(ROW_BLOCK)

    @pl.when(valid & jnp.logical_not(full))
    def _():
        compute(ROW_HALF)
        rest = ROW_HALF * (out_ref.shape[0] // ROW_BLOCK)
        out_ref[rest:, :] = jnp.zeros((out_ref.shape[0] - rest, out_ref.shape[1]), out_ref.dtype)


def _moe_up_kernel(*refs):
    sched = refs[:8]
    xs_ref, w_hbm, bg_ref, bu_ref, h_ref, wbuf, wgb, wub, sems = refs[8:]
    s = pl.program_id(0)
    tf = wgb.shape[1]
    n_ff = w_hbm.shape[2] // 2

    def copies(g):
        e = sched[SCHED_GE][g]
        col = pl.multiple_of(sched[SCHED_GF][g] * tf, tf)
        return [pltpu.make_async_copy(w_hbm.at[e, :, pl.ds(half * n_ff + col, tf)], wbuf.at[half], sems.at[half])
                for half in range(2)]

    def on_arrival():
        wgb[...] = wbuf[0].astype(BF16)
        wub[...] = wbuf[1].astype(BF16)

    _stream_weights(s, sched, copies, on_arrival)

    def compute(rows):
        lo, hi = _unpack_bf16_pairs(_load_tiled(xs_ref, rows))
        half = lo.shape[1]

        def proj(wb, b_ref):
            return (jnp.dot(lo, wb[:half, :], preferred_element_type=F32)
                    + jnp.dot(hi, wb[half:, :], preferred_element_type=F32) + b_ref[...])

        gate = jnp.minimum(proj(wgb, bg_ref), SWIGLU_LIMIT)
        up = jnp.clip(proj(wub, bu_ref), -SWIGLU_LIMIT, SWIGLU_LIMIT)
        h_ref[:rows, :] = ((up + 1.0) * gate * _sigmoid(SWIGLU_ALPHA * gate)).astype(h_ref.dtype)

    _for_used_rows(s, sched, h_ref, compute)


def _step_expert(s, r):
    return r[SCHED_GE][r[SCHED_SG][s]]


def _step_tile(s, r):
    return r[SCHED_GF][r[SCHED_SG][s]]


def moe_up(xs, w_gu, b_gu, sched):
    P = xs.shape[0] // ROW_TILES
    E, D, F2 = w_gu.shape
    F = F2 // 2
    tf = UP_TILE
    nf = F // tf
    b3 = b_gu.reshape(E, 1, F2)
    return pl.pallas_call(
        _moe_up_kernel,
        grid_spec=pltpu.PrefetchScalarGridSpec(
            num_scalar_prefetch=len(sched),
            grid=(sched[SCHED_SG].shape[0],),
            in_specs=[pl.BlockSpec((ROW_BLOCK * ROW_TILES, LANES), lambda s, *r: (r[SCHED_SB][s], 0)),
                      pl.BlockSpec(memory_space=pl.ANY),
                      pl.BlockSpec((None, 1, tf), lambda s, *r: (_step_expert(s, r), 0, _step_tile(s, r))),
                      pl.BlockSpec((None, 1, tf), lambda s, *r: (_step_expert(s, r), 0, nf + _step_tile(s, r)))],
            out_specs=pl.BlockSpec((ROW_BLOCK, tf), lambda s, *r: (r[SCHED_SB][s], r[SCHED_SO][s])),
            scratch_shapes=[pltpu.VMEM((2, D, tf), F32), pltpu.VMEM((D, tf), BF16), pltpu.VMEM((D, tf), BF16),
                            pltpu.SemaphoreType.DMA((2,))]),
        out_shape=jax.ShapeDtypeStruct((P, F), BF16),
        compiler_params=_cparams(("arbitrary",)),
        name="moe_up",
    )(*sched, xs, w_gu, b3, b3)


def _moe_down_kernel(*refs):
    sched = refs[:8]
    h_ref, w_hbm, bd_ref, y_ref, wbuf, wdb, sem = refs[8:]
    s = pl.program_id(0)
    tn = wdb.shape[1]

    def copies(g):
        col = pl.multiple_of(sched[SCHED_GF][g] * tn, tn)
        return [pltpu.make_async_copy(w_hbm.at[sched[SCHED_GE][g], :, pl.ds(col, tn)], wbuf, sem)]

    def on_arrival():
        wdb[...] = wbuf[...].astype(BF16)

    _stream_weights(s, sched, copies, on_arrival)

    def compute(rows):
        y = jnp.dot(h_ref[:rows, :], wdb[...], preferred_element_type=F32) + bd_ref[...]
        _store_tiled(y_ref, _pack_bf16_pairs(y.astype(BF16).astype(F32)))

    _for_used_rows(s, sched, y_ref, compute)


def moe_down(h, w_down, b_down, sched):
    P, F = h.shape
    E, _, D = w_down.shape
    tn = DOWN_TILE
    assert tn == D, "the packed output pairs column c with column c + D/2"
    b3 = b_down.reshape(E, 1, D)
    return pl.pallas_call(
        _moe_down_kernel,
        grid_spec=pltpu.PrefetchScalarGridSpec(
            num_scalar_prefetch=len(sched),
            grid=(sched[SCHED_SG].shape[0],),
            in_specs=[pl.BlockSpec((ROW_BLOCK, F), lambda s, *r: (r[SCHED_SB][s], 0)),
                      pl.BlockSpec(memory_space=pl.ANY),
                      pl.BlockSpec((None, 1, tn), lambda s, *r: (_step_expert(s, r), 0, _step_tile(s, r)))],
            out_specs=pl.BlockSpec((ROW_BLOCK * ROW_TILES, LANES), lambda s, *r: (r[SCHED_SB][s], 0)),
            scratch_shapes=[pltpu.VMEM((F, tn), F32), pltpu.VMEM((F, tn), BF16), pltpu.SemaphoreType.DMA(())]),
        out_shape=jax.ShapeDtypeStruct((P * ROW_TILES, LANES), jnp.uint32),
        compiler_params=_cparams(("arbitrary",)),
        name="moe_down",
    )(*sched, h, w_down, b3)


def _combine_kernel(dcur_ref, dnxt_ref, x1_ref, w_ref, fw_ref, ys_hbm, o_ref, buf, sems):
    i = pl.program_id(0)
    n_steps = pl.num_programs(0)
    tq = x1_ref.shape[0]
    n = TOP_K * tq
    slot = lax.rem(i, 2)

    def issue(idx_ref, sl):
        def body(q, carry):
            base = pl.multiple_of(q * SUBLANES, SUBLANES)
            for r in range(SUBLANES):
                _row_copy(ys_hbm, buf.at[sl], idx_ref[0, 0, base + r], (base + r) * ROW_TILES,
                          sems.at[sl]).start(priority=r % 2)
            return carry

        lax.fori_loop(0, n // SUBLANES, body, 0)

    pl.when(i == 0)(lambda: issue(dcur_ref, 0))
    pl.when(i + 1 < n_steps)(lambda: issue(dnxt_ref, 1 - slot))
    pltpu.make_async_copy(ys_hbm.at[pl.ds(0, n * ROW_TILES), :], buf.at[slot], sems.at[slot]).wait()

    half = ROW_TILES * LANES
    acc_lo = x1_ref[:, :half]
    acc_hi = x1_ref[:, half:]
    for j in range(TOP_K):
        lo, hi = _unpack_bf16_pairs(_load_tiled(buf.at[slot], tq, first_row=j * tq))
        acc_lo = acc_lo + w_ref[:, j:j + 1] * lo.astype(F32)
        acc_hi = acc_hi + w_ref[:, j:j + 1] * hi.astype(F32)
    ssq = jnp.sum(acc_lo * acc_lo, axis=-1, keepdims=True) + jnp.sum(acc_hi * acc_hi, axis=-1, keepdims=True)
    scale = lax.rsqrt(ssq * (1.0 / (2 * half)) + NORM_EPS)
    o_ref[:, :half] = acc_lo * scale * fw_ref[:, :half]
    o_ref[:, half:] = acc_hi * scale * fw_ref[:, half:]


def combine(x1, ys, dest, top_w, final_w):
    T, D = x1.shape
    tq = COMBINE_TOKENS
    nblk = T // tq
    dest_blk = (dest * ROW_TILES).reshape(nblk, tq, TOP_K).transpose(0, 2, 1).reshape(nblk, 1, TOP_K * tq)
    idx_spec = lambda f: pl.BlockSpec((1, 1, TOP_K * tq), f, memory_space=pltpu.SMEM)
    return pl.pallas_call(
        _combine_kernel,
        grid=(nblk,),
        in_specs=[idx_spec(lambda i: (i, 0, 0)),
                  idx_spec(lambda i: (jnp.minimum(i + 1, nblk - 1), 0, 0)),
                  pl.BlockSpec((tq, D), lambda i: (i, 0)),
                  pl.BlockSpec((tq, TOP_K), lambda i: (i, 0)),
                  pl.BlockSpec((1, D), lambda i: (0, 0)),
                  pl.BlockSpec(memory_space=pl.ANY)],
        out_specs=pl.BlockSpec((tq, D), lambda i: (i, 0)),
        out_shape=jax.ShapeDtypeStruct((T, D), F32),
        scratch_shapes=[pltpu.VMEM((2, TOP_K * tq * ROW_TILES, LANES), ys.dtype), pltpu.SemaphoreType.DMA((2,))],
        compiler_params=_cparams(("arbitrary",)),
        name="moe_combine",
    )(dest_blk, dest_blk, x1, top_w, final_w.reshape(1, D), ys)


def _routing(logits):
    T = logits.shape[0]
    TK = T * TOP_K
    NB = TK // ROW_BLOCK + N_EXPERTS
    top_logits, top_idx = lax.top_k(logits[:, :N_EXPERTS], TOP_K)
    top_w = jax.nn.softmax(top_logits, axis=-1)
    flat_e = top_idx.reshape(TK).astype(jnp.int32)
    onehot = (flat_e[:, None] == jnp.arange(N_EXPERTS, dtype=jnp.int32)[None, :]).astype(F32)
    oh = onehot.reshape(TK // RANK_GROUP, RANK_GROUP, N_EXPERTS)
    local = jnp.einsum("ts,gse->gte", jnp.tril(jnp.ones((RANK_GROUP, RANK_GROUP), F32)), oh)
    tot = local[:, -1, :]
    offs = jnp.cumsum(tot, axis=0) - tot
    rank = jnp.sum(oh * (local + offs[:, None, :]), axis=-1).reshape(TK).astype(jnp.int32) - 1
    counts = (offs[-1] + tot[-1]).astype(jnp.int32)
    padded = (counts + ROW_BLOCK - 1) // ROW_BLOCK * ROW_BLOCK
    pad_end = jnp.cumsum(padded).astype(jnp.int32)
    dest = ((pad_end - padded)[flat_e] + rank).astype(jnp.int32)
    block_start = jnp.arange(NB, dtype=jnp.int32) * ROW_BLOCK
    block_e = jnp.minimum(jnp.sum((pad_end[None, :] <= block_start[:, None]).astype(jnp.int32), axis=1),
                          N_EXPERTS - 1)
    nb_used = pad_end[-1] // ROW_BLOCK
    blk = jnp.arange(NB, dtype=jnp.int32)
    rows_left = counts[block_e] - (blk - ((pad_end - padded) // ROW_BLOCK)[block_e]) * ROW_BLOCK
    block_full = ((blk < nb_used) & (rows_left > ROW_HALF)).astype(jnp.int32)
    tail = nb_used + jnp.arange(N_EXPERTS, dtype=jnp.int32)
    zrow = jnp.concatenate([jnp.where(padded > 0, pad_end - ROW_BLOCK, -1),
                            jnp.where(tail < NB, tail * ROW_BLOCK, -1)]).astype(jnp.int32)
    return dest.reshape(T, TOP_K), top_w, block_e, block_full, padded, nb_used, zrow


def _schedule(block_e, block_full, padded, nb_used, n_tiles):
    nb = block_e.shape[0]
    b = jnp.tile(jnp.arange(nb, dtype=jnp.int32), n_tiles)
    f = jnp.repeat(jnp.arange(n_tiles, dtype=jnp.int32), nb)
    key = jnp.where(b < nb_used, (block_e[b] * n_tiles + f) * nb + b, (N_EXPERTS * n_tiles + f) * nb + b)
    order = jnp.argsort(key)
    used = padded > 0
    n_groups = (jnp.sum(used) * n_tiles).astype(jnp.int32)
    expert_pos = jnp.cumsum(used) - 1
    group = jnp.minimum(expert_pos[block_e[b]] * n_tiles + f, n_groups - 1).astype(jnp.int32)
    n_valid = (nb_used * n_tiles).astype(jnp.int32)
    sg = jnp.where(jnp.arange(nb * n_tiles) < n_valid, group[order], n_groups - 1)
    experts = jnp.arange(N_EXPERTS, dtype=jnp.int32)
    used_first = jnp.argsort(jnp.where(used, experts, N_EXPERTS + experts)).astype(jnp.int32)
    ge = jnp.repeat(used_first, n_tiles)
    gf = jnp.tile(jnp.arange(n_tiles, dtype=jnp.int32), N_EXPERTS)
    return (n_valid.reshape(1), sg, b[order], f[order], block_full[b][order], ge, gf, n_groups.reshape(1))


def _repack_w_in(w_in):
    w_in = w_in.astype(BF16)
    z = lambda n: jnp.zeros((w_in.shape[0], n), w_in.dtype)
    o = RWKV_COLS
    w_r = jnp.concatenate([w_in[:, :RW_WD], w_in[:, RW_WD:RW_WD + DECAY_LORA], z(LORA_PAD - DECAY_LORA),
                           w_in[:, RW_WD + DECAY_LORA:RW_WD + DECAY_LORA + AAA_LORA], z(LORA_PAD - AAA_LORA),
                           w_in[:, RW_WD + DECAY_LORA + AAA_LORA:o]], axis=1)
    w_m = jnp.concatenate([w_in[:, o:o + ML_IF], w_in[:, o + ML_IF:o + ML_IF + 2 * MLSTM_HEADS],
                           z(LANES - 2 * MLSTM_HEADS), w_in[:, o + ML_IF + 2 * MLSTM_HEADS:o + MLSTM_COLS]], axis=1)
    w_g = w_in[:, o + MLSTM_COLS:]
    return w_r, w_m, w_g


def kernel(x, norm1_w, w_in, b_gate, rwkv_mu, rwkv_w0, rwkv_w_up, rwkv_a0, rwkv_a_up, rwkv_g_up, rwkv_k_k,
           rwkv_k_a, rwkv_r_k, rwkv_ln_w, rwkv_ln_b, mlstm_conv_w, mlstm_i_b, mlstm_f_b, mlstm_norm_w,
           w_branch_a, w_branch_b, w_out, norm2_w, router_w, router_b, w_gu, b_gu, w_down, b_down,
           final_norm_w):
    B, S, D = x.shape
    T = B * S
    xt = x.reshape(T, D)
    assert norm1_w.shape[0] == 1, "single-layer block: the final rmsnorm is fused into the MoE combine"
    for l in range(1):
        w_r, w_m, w_g = _repack_w_in(w_in[l])
        hn, *scan_in = rwkv_in(x, norm1_w[l], w_r, rwkv_mu[l], rwkv_w0[l], rwkv_w_up[l], rwkv_a0[l], rwkv_a_up[l],
                               rwkv_g_up[l], rwkv_k_k[l], rwkv_k_a[l])
        hn = hn.reshape(T, D)
        p_m = matmul(hn, w_m, 256, ML_COLS_P, F32, "proj_mlstm").reshape(B, S, ML_COLS_P)
        o_a = rwkv_scan(*scan_in, rwkv_r_k[l].reshape(1, -1), rwkv_ln_w[l].reshape(1, -1),
                        rwkv_ln_b[l].reshape(1, -1))
        o_b = mlstm_branch(p_m, mlstm_conv_w[l], mlstm_i_b[l], mlstm_f_b[l], mlstm_norm_w[l])
        x1, xp, logits = merge_project(o_a.reshape(T, RWKV_DIM), o_b.reshape(T, MLSTM_DIM), hn, w_g, b_gate[l], xt,
                                       w_branch_a[l], w_branch_b[l], w_out[l], norm2_w[l], router_w[l],
                                       router_b[l])
        dest, top_w, block_e, block_full, padded, nb_used, zrow = _routing(logits)
        xs = moe_dispatch(xp, dest, zrow, block_e.shape[0] * ROW_BLOCK)
        h = moe_up(xs, w_gu[l], b_gu[l], _schedule(block_e, block_full, padded, nb_used, EXPERT_FF // UP_TILE))
        ys = moe_down(h, w_down[l], b_down[l], _schedule(block_e, block_full, padded, nb_used, D // DOWN_TILE))
        xt = combine(x1, ys, dest, top_w, final_norm_w)
    return xt.reshape(B, S, D)
```

```python
import functools

import jax
import jax.numpy as jnp
import numpy as np
from jax import lax
from jax.experimental import pallas as pl
from jax.experimental.pallas import tpu as pltpu

F32 = jnp.float32
BF16 = jnp.bfloat16

D_MODEL = 2048
CHUNK = 64
NORM_EPS = 1e-6
RWKV_HEADS = 16
RWKV_HEAD_DIM = 64
RWKV_DIM = 1024
DECAY_LORA = 96
AAA_LORA = 96
GATE_LORA = 256
GN_EPS = 64e-5
RWKV_COLS = 3 * RWKV_DIM + DECAY_LORA + AAA_LORA + GATE_LORA
MLSTM_HEADS = 4
MLSTM_QK_DIM = 128
MLSTM_V_DIM = 256
MLSTM_QK = 512
MLSTM_DIM = 1024
CONV_WIDTH = 4
GATE_SOFTCAP = 15.0
MLSTM_COLS = 2 * MLSTM_QK + 2 * MLSTM_DIM + 2 * MLSTM_HEADS
N_EXPERTS = 32
TOP_K = 4
EXPERT_FF = 2048
SWIGLU_LIMIT = 7.0
SWIGLU_ALPHA = 1.702

LANES = 128
SUBLANES = 8
VMEM_LIMIT = 56 * 1024 * 1024

LORA_PAD = 128
RW_WD = 3 * RWKV_DIM
RW_AD = RW_WD + LORA_PAD
RW_GD = RW_AD + LORA_PAD
RW_COLS_P = RW_GD + GATE_LORA
ML_V = 2 * MLSTM_QK
ML_IF = ML_V + MLSTM_DIM
ML_O = ML_IF + LANES
ML_COLS_P = ML_O + MLSTM_DIM
ROUTER_PAD = 128
SCAN_SEQS = 4
MLSTM_SEQS = 1

ROW_BLOCK = 512
ROW_HALF = ROW_BLOCK // 2


def _cparams(sem):
    return pltpu.CompilerParams(dimension_semantics=sem, vmem_limit_bytes=VMEM_LIMIT)


def _bdot(a, b):
    return jnp.dot(a.astype(BF16), b.astype(BF16), preferred_element_type=F32)


def _split3(x):
    hi = x.astype(BF16)
    r1 = x - hi.astype(F32)
    mid = r1.astype(BF16)
    lo = (r1 - mid.astype(F32)).astype(BF16)
    return hi, mid, lo


def _dot_exact_lhs(mat_bf16, x):
    hi, mid, lo = _split3(x)
    return (jnp.dot(mat_bf16, hi, preferred_element_type=F32)
            + jnp.dot(mat_bf16, mid, preferred_element_type=F32)
            + jnp.dot(mat_bf16, lo, preferred_element_type=F32))


def _dot_exact_rhs(x, mat_bf16):
    hi, mid, lo = _split3(x)
    return (jnp.dot(hi, mat_bf16, preferred_element_type=F32)
            + jnp.dot(mid, mat_bf16, preferred_element_type=F32)
            + jnp.dot(lo, mat_bf16, preferred_element_type=F32))


def _sigmoid(x):
    return 1.0 / (1.0 + jnp.exp(-x))


def _softplus(x):
    return jnp.maximum(x, 0.0) + jnp.log(1.0 + jnp.exp(-jnp.abs(x)))


def _head_sum_mat():
    r = lax.broadcasted_iota(jnp.int32, (LANES, LANES), 0) // RWKV_HEAD_DIM
    c = lax.broadcasted_iota(jnp.int32, (LANES, LANES), 1) // RWKV_HEAD_DIM
    return jnp.where(r == c, 1.0, 0.0).astype(BF16)


def _pad_rows(w, n):
    return jnp.pad(w, ((0, n - w.shape[0]), (0, 0)))


PREP_GROUP = 256
MLSTM_PROJ_TILE = 640


def _rwkv_in_kernel(x_ref, n1_ref, w_ref, wm_ref, mu_ref, w0_ref, wup_ref, a0_ref, aup_ref, gup_ref, kk_ref,
                    ka_ref, hn_out, pm_out, r_out, k_out, v_out, kk_out, b_out, lw_out, g_out, carry_ref):
    i = pl.program_id(1)

    @pl.when(i == 0)
    def _():
        carry_ref[...] = jnp.zeros_like(carry_ref)

    x = x_ref[...]
    hn = (x * lax.rsqrt(jnp.mean(x * x, axis=-1, keepdims=True) + NORM_EPS) * n1_ref[...]).astype(BF16)
    hn_out[...] = hn
    tq = x.shape[0]
    row = lax.broadcasted_iota(jnp.int32, (tq, 1), 0)

    def shifted(cols):
        p = jnp.dot(hn, w_ref[:, cols], preferred_element_type=F32)
        prev = jnp.where(row == 0, carry_ref[0:1, cols], pltpu.roll(p, 1, 0))
        carry_ref[0:1, cols] = p[tq - 1:tq, :]
        return p + (prev - p) * mu_ref[:, cols]

    lora = shifted(slice(RW_WD, RW_COLS_P))
    wd_t = jnp.tanh(lora[:, 0:LORA_PAD]).astype(BF16)
    ad = lora[:, LORA_PAD:2 * LORA_PAD].astype(BF16)
    gd_s = _sigmoid(lora[:, 2 * LORA_PAD:]).astype(BF16)
    hs = _head_sum_mat()

    for c in range(0, RWKV_DIM, PREP_GROUP):
        cs = slice(c, c + PREP_GROUP)
        r = shifted(cs)
        k = shifted(slice(RWKV_DIM + c, RWKV_DIM + c + PREP_GROUP))
        v = shifted(slice(2 * RWKV_DIM + c, 2 * RWKV_DIM + c + PREP_GROUP))
        w_log = -_softplus(-(w0_ref[:, cs] + jnp.dot(wd_t, wup_ref[:, cs], preferred_element_type=F32))) - 0.5
        a = _sigmoid(a0_ref[:, cs] + jnp.dot(ad, aup_ref[:, cs], preferred_element_type=F32))
        kk = k * kk_ref[:, cs]
        nrm2 = jnp.concatenate(
            [_dot_exact_rhs(kk[:, t:t + LANES] * kk[:, t:t + LANES], hs) for t in range(0, PREP_GROUP, LANES)],
            axis=1)
        kk = kk / jnp.maximum(jnp.sqrt(nrm2), 1e-12)
        r_out[:, cs] = r.astype(r_out.dtype)
        k_out[:, cs] = (k * (1.0 + (a - 1.0) * ka_ref[:, cs])).astype(k_out.dtype)
        v_out[:, cs] = v.astype(v_out.dtype)
        kk_out[:, cs] = kk.astype(kk_out.dtype)
        b_out[:, cs] = (kk * a).astype(b_out.dtype)
        lw_out[:, cs] = -jnp.exp(w_log)
        g_out[:, cs] = jnp.dot(gd_s, gup_ref[:, cs], preferred_element_type=F32).astype(g_out.dtype)

    for c in range(0, ML_COLS_P, MLSTM_PROJ_TILE):
        cs = slice(c, c + MLSTM_PROJ_TILE)
        pm_out[:, cs] = jnp.dot(hn, wm_ref[:, cs], preferred_element_type=F32)


def rwkv_in(x, norm1_w, w_r, w_m, mu, w0, w_up, a0, a_up, g_up, k_k, k_a, tq=256):
    B, S, D = x.shape
    blk = lambda n: pl.BlockSpec((None, tq, n), lambda b, i: (b, i, 0))
    full = lambda a: pl.BlockSpec(a.shape, lambda b, i: (0,) * a.ndim, pipeline_mode=pl.Buffered(1))
    row = lambda t: t.reshape(1, -1)
    mu_p = jnp.concatenate([mu[:RW_WD], jnp.pad(mu[RW_WD:RW_WD + DECAY_LORA], (0, LORA_PAD - DECAY_LORA)),
                            jnp.pad(mu[RW_WD + DECAY_LORA:RW_WD + DECAY_LORA + AAA_LORA], (0, LORA_PAD - AAA_LORA)),
                            mu[RW_WD + DECAY_LORA + AAA_LORA:]])
    params = (row(norm1_w), w_r, w_m, row(mu_p), row(w0), _pad_rows(w_up, LORA_PAD).astype(BF16), row(a0),
              _pad_rows(a_up, LORA_PAD).astype(BF16), g_up.astype(BF16), row(k_k), row(k_a))
    out = lambda n, dt: jax.ShapeDtypeStruct((B, S, n), dt)
    return pl.pallas_call(
        _rwkv_in_kernel,
        grid=(B, S // tq),
        in_specs=[blk(D)] + [full(a) for a in params],
        out_specs=[blk(D), blk(ML_COLS_P)] + [blk(RWKV_DIM)] * 7,
        out_shape=[out(D, BF16), out(ML_COLS_P, F32)] + [out(RWKV_DIM, BF16)] * 5
        + [out(RWKV_DIM, F32), out(RWKV_DIM, BF16)],
        scratch_shapes=[pltpu.VMEM((SUBLANES, RW_COLS_P), F32)],
        compiler_params=_cparams(("parallel", "arbitrary")),
        name="rwkv_in",
    )(x, *params)


def _rwkv_scan_kernel(r_ref, k_ref, v_ref, kk_ref, b_ref, lw_ref, g_ref, rk_ref, lnw_ref, lnb_ref,
                      o_ref, h_ref):
    c = pl.program_id(1)

    @pl.when(c == 0)
    def _():
        h_ref[...] = jnp.zeros_like(h_ref)

    L = CHUNK
    L2 = 2 * L
    ri = lax.broadcasted_iota(jnp.int32, (L, L), 0)
    ci = lax.broadcasted_iota(jnp.int32, (L, L), 1)
    tril = jnp.where(ri >= ci, 1.0, 0.0).astype(BF16)

    n_seq = lw_ref.shape[0]
    cat = lambda ref: jnp.concatenate([ref[i] for i in range(n_seq)], axis=1)
    rep = lambda ref: jnp.concatenate([ref[...]] * n_seq, axis=1)
    lw = cat(lw_ref)
    cum = _dot_exact_lhs(tril, lw)
    cum_end = cum[L - 1:L, :]
    w_in = jnp.exp(cum)
    w_prev = jnp.exp(cum - lw)
    w_inv = jnp.exp(-cum)
    w_tail = jnp.exp(cum_end - cum)
    w_end = jnp.exp(cum_end)

    kk = cat(kk_ref).astype(F32)
    bb = cat(b_ref).astype(F32)
    kx = cat(k_ref).astype(F32)
    rx = cat(r_ref).astype(F32)
    vx = cat(v_ref).astype(F32)
    gx = cat(g_ref).astype(F32)
    a_hat = -kk * w_prev
    r_hat = rx * w_in
    b_hat = bb * w_inv
    k_hat = kx * w_inv
    b_til = bb * w_tail
    k_til = kx * w_tail
    rkk = rx * kx * rep(rk_ref)

    lane = lax.broadcasted_iota(jnp.int32, (1, LANES), 1)
    m_lo = jnp.where(lane < RWKV_HEAD_DIM, 1.0, 0.0)
    m_hi = 1.0 - m_lo

    def stack(x):
        return jnp.concatenate([x * m_lo, x * m_hi], axis=0)

    r2 = lax.broadcasted_iota(jnp.int32, (L2, L2), 0)
    c2 = lax.broadcasted_iota(jnp.int32, (L2, L2), 1)
    same_head = (r2 // L) == (c2 // L)
    strict = same_head & (r2 > c2)
    incl = same_head & (r2 >= c2)
    diag16 = (r2 // 16) == (c2 // 16)
    eye = jnp.where(r2 == c2, 1.0, 0.0)
    hs = _head_sum_mat()

    pairs_per_seq = RWKV_DIM // LANES
    pairs = range(n_seq * pairs_per_seq)
    sls = [slice(p * LANES, (p + 1) * LANES) for p in pairs]
    v_st = [stack(vx[:, sl]) for sl in sls]
    lhs = [jnp.concatenate([stack(a_hat[:, sl]), stack(r_hat[:, sl])], axis=0).astype(BF16) for sl in sls]
    rhs = [jnp.concatenate([stack(b_hat[:, sl]), stack(k_hat[:, sl])], axis=0).astype(BF16) for sl in sls]
    sc = [lax.dot_general(lhs[p], rhs[p], (((1,), (1,)), ((), ())), preferred_element_type=F32) for p in pairs]
    n_ab = [jnp.where(strict, sc[p][:L2, :L2], 0.0) for p in pairs]
    a_ak = [jnp.where(strict, sc[p][:L2, L2:], 0.0).astype(BF16) for p in pairs]
    a_r = [jnp.concatenate([jnp.where(incl, sc[p][L2:, :L2], 0.0), jnp.where(incl, sc[p][L2:, L2:], 0.0)],
                           axis=1).astype(BF16) for p in pairs]

    nd = [jnp.where(diag16, n_ab[p], 0.0) for p in pairs]
    noff = [(n_ab[p] - nd[p]).astype(BF16) for p in pairs]
    ndb = [nd[p].astype(BF16) for p in pairs]
    s2 = [jnp.dot(ndb[p], ndb[p], preferred_element_type=F32).astype(BF16) for p in pairs]
    s4 = [jnp.dot(s2[p], s2[p], preferred_element_type=F32).astype(BF16) for p in pairs]
    s8 = [jnp.dot(s4[p], s4[p], preferred_element_type=F32).astype(BF16) for p in pairs]
    x1 = [eye + nd[p] for p in pairs]
    x2 = [x1[p] + _bdot(x1[p], s2[p]) for p in pairs]
    x3 = [x2[p] + _bdot(x2[p], s4[p]) for p in pairs]
    t_d = [(x3[p] + _bdot(x3[p], s8[p])).astype(BF16) for p in pairs]
    m1 = [jnp.dot(t_d[p], noff[p], preferred_element_type=F32) for p in pairs]
    m1b = [m1[p].astype(BF16) for p in pairs]
    m2 = [jnp.dot(m1b[p], m1b[p], preferred_element_type=F32) for p in pairs]
    m3 = [jnp.dot(m1b[p], m2[p].astype(BF16), preferred_element_type=F32) for p in pairs]
    t_inv = [jnp.dot((eye + m1[p] + m2[p] + m3[p]).astype(BF16), t_d[p], preferred_element_type=F32).astype(BF16)
             for p in pairs]

    h0 = [h_ref[p] for p in pairs]
    ah = [jnp.dot(lhs[p], h0[p].astype(BF16), preferred_element_type=F32) for p in pairs]
    x = [ah[p][:L2] + jnp.dot(a_ak[p], v_st[p].astype(BF16), preferred_element_type=F32) for p in pairs]
    u = [jnp.dot(t_inv[p], x[p].astype(BF16), preferred_element_type=F32) for p in pairs]
    uv = [jnp.concatenate([u[p], v_st[p]], axis=0).astype(BF16) for p in pairs]
    y_st = [ah[p][L2:] + jnp.dot(a_r[p], uv[p], preferred_element_type=F32) for p in pairs]
    y = [y_st[p][:L] + y_st[p][L:] for p in pairs]

    for p in pairs:
        sl = sls[p]
        upd_l = jnp.concatenate([stack(b_til[:, sl]), stack(k_til[:, sl])], axis=0).astype(BF16)
        upd = lax.dot_general(upd_l, uv[p], (((0,), (0,)), ((), ())), preferred_element_type=F32)
        w_col = jnp.sum(eye * w_end[:, sl], axis=1, keepdims=True)
        h_ref[p] = w_col * h0[p] + upd

    def head_sums(vals):
        parts = []
        for t in vals:
            hi = t.astype(BF16)
            parts += [hi, (t - hi.astype(F32)).astype(BF16)]
        res = jnp.dot(jnp.concatenate(parts, axis=0), hs, preferred_element_type=F32)
        return [res[2 * i * L:(2 * i + 1) * L] + res[(2 * i + 1) * L:(2 * i + 2) * L] for i in range(len(vals))]

    sums1 = [head_sums([y[p], rkk[:, sls[p]]]) for p in pairs]
    d = [y[p] - sums1[p][0] * (1.0 / RWKV_HEAD_DIM) for p in pairs]
    var = [head_sums([d[p] * d[p]])[0] * (1.0 / RWKV_HEAD_DIM) for p in pairs]
    for p in pairs:
        sl = sls[p]
        psl = sls[p % pairs_per_seq]
        yn = d[p] * lax.rsqrt(var[p] + GN_EPS) * lnw_ref[:, psl] + lnb_ref[:, psl]
        o_ref[p // pairs_per_seq, :, psl] = ((yn + sums1[p][1] * vx[:, sl]) * gx[:, sl]).astype(o_ref.dtype)


def rwkv_scan(r, k, v, kk, b, lw, g, r_k, ln_w, ln_b):
    B, S, C = r.shape
    blk = pl.BlockSpec((SCAN_SEQS, CHUNK, C), lambda bb, c: (bb, c, 0))
    full = pl.BlockSpec((1, C), lambda bb, c: (0, 0))
    return pl.pallas_call(
        _rwkv_scan_kernel,
        grid=(B // SCAN_SEQS, S // CHUNK),
        in_specs=[blk] * 7 + [full] * 3,
        out_specs=blk,
        out_shape=jax.ShapeDtypeStruct((B, S, C), BF16),
        scratch_shapes=[pltpu.VMEM((SCAN_SEQS * C // LANES, LANES, LANES), F32)],
        compiler_params=_cparams(("parallel", "arbitrary")),
        name="rwkv_scan",
    )(r, k, v, kk, b, lw, g, r_k, ln_w, ln_b)


def _mlstm_kernel(p_ref, convw_ref, ifb_ref, nw_ref, o_ref, carry_ref, c_ref, n_ref, m_ref):
    ci = pl.program_id(1)

    @pl.when(ci == 0)
    def _():
        carry_ref[...] = jnp.zeros_like(carry_ref)
        c_ref[...] = jnp.zeros_like(c_ref)
        n_ref[...] = jnp.zeros_like(n_ref)
        m_ref[...] = jnp.zeros_like(m_ref)

    L = CHUNK
    n_seq = p_ref.shape[0]
    lane = lax.broadcasted_iota(jnp.int32, (1, LANES), 1)
    ri = lax.broadcasted_iota(jnp.int32, (L, L), 0)
    cj = lax.broadcasted_iota(jnp.int32, (L, L), 1)
    causal = ri >= cj
    tril = jnp.where(causal, 1.0, 0.0).astype(BF16)
    triu = jnp.where(ri <= cj, 1.0, 0.0).astype(BF16)

    qk, comb, comb_t, b_col, b_row = [], [], [], [], []
    for i in range(n_seq):
        u = p_ref[i, :, 0:ML_V]
        ext = jnp.concatenate([carry_ref[i], u], axis=0)
        carry_ref[i] = u[L - SUBLANES:L, :]
        conv = convw_ref[CONV_WIDTH - 1:CONV_WIDTH, :] * u
        for j in range(1, CONV_WIDTH):
            conv = conv + convw_ref[CONV_WIDTH - 1 - j:CONV_WIDTH - j, :] * pltpu.roll(ext, j, 0)[SUBLANES:, :]
        qk.append(conv * _sigmoid(conv))
        pre = GATE_SOFTCAP * jnp.tanh((p_ref[i, :, ML_IF:ML_O] + ifb_ref[...]) * (1.0 / GATE_SOFTCAP))
        cb = jnp.where(lane < MLSTM_HEADS, pre, -_softplus(-pre))
        comb.append(cb)
        comb_t.append(cb.T)
        b_col.append(_dot_exact_lhs(tril, cb))
        b_row.append(_dot_exact_rhs(comb_t[i], triu))

    H = range(n_seq * MLSTM_HEADS)
    sq = [v // MLSTM_HEADS for v in H]
    hd = [v % MLSTM_HEADS for v in H]
    dk, dv = MLSTM_QK_DIM, MLSTM_V_DIM
    qh = [qk[sq[v]][:, hd[v] * dk:(hd[v] + 1) * dk] * (dk ** -0.5) for v in H]
    kh = [qk[sq[v]][:, MLSTM_QK + hd[v] * dk:MLSTM_QK + (hd[v] + 1) * dk] for v in H]
    vh = [p_ref[sq[v], :, ML_V + hd[v] * dv:ML_V + (hd[v] + 1) * dv].astype(BF16) for v in H]
    qb = [qh[h].astype(BF16) for h in H]
    bcol = [b_col[sq[v]][:, MLSTM_HEADS + hd[v]:MLSTM_HEADS + hd[v] + 1] for v in H]
    brow = [b_row[sq[v]][MLSTM_HEADS + hd[v]:MLSTM_HEADS + hd[v] + 1, :] for v in H]
    m_prev = [m_ref[h][0:1, 0:1] for h in H]
    n_prev = [n_ref[h][0:1, :] for h in H]
    c_prev = [c_ref[h] for h in H]

    qk_t = [lax.dot_general(qb[h], kh[h].astype(BF16), (((1,), (1,)), ((), ())), preferred_element_type=F32)
            for h in H]
    qc = [jnp.dot(qb[h], c_prev[h].astype(BF16), preferred_element_type=F32) for h in H]
    dm = [jnp.where(causal, bcol[h] - brow[h] + comb_t[sq[h]][hd[h]:hd[h] + 1, :], -jnp.inf) for h in H]
    inter = [bcol[h] + m_prev[h] for h in H]
    m_t = [jnp.maximum(inter[h], jnp.max(dm[h], axis=-1, keepdims=True)) for h in H]
    s = [qk_t[h] * jnp.exp(dm[h] - m_t[h]) for h in H]
    w_inter = [jnp.exp(inter[h] - m_t[h]) for h in H]
    num = [jnp.dot(s[h].astype(BF16), vh[h], preferred_element_type=F32) + w_inter[h] * qc[h] for h in H]
    den = [jnp.sum(s[h], axis=-1, keepdims=True) + w_inter[h] * jnp.sum(qh[h] * n_prev[h], axis=-1, keepdims=True)
           for h in H]
    hh = [num[h] / jnp.maximum(jnp.abs(den[h]), jnp.exp(-m_t[h])) for h in H]

    g_tot = [bcol[h][L - 1:L, :] for h in H]
    a = [comb[sq[h]][:, hd[h]:hd[h] + 1] + g_tot[h] - bcol[h] for h in H]
    m_new = [jnp.maximum(g_tot[h] + m_prev[h], jnp.max(a[h], axis=0, keepdims=True)) for h in H]
    dec = [jnp.exp(g_tot[h] + m_prev[h] - m_new[h]) for h in H]
    wkk = [jnp.exp(a[h] - m_new[h]) * kh[h] for h in H]
    for h in H:
        c_ref[h] = dec[h] * c_prev[h] + lax.dot_general(wkk[h].astype(BF16), vh[h], (((0,), (0,)), ((), ())),
                                                        preferred_element_type=F32)
        n_ref[h] = jnp.broadcast_to(dec[h] * n_prev[h] + jnp.sum(wkk[h], axis=0, keepdims=True),
                                    (SUBLANES, LANES))
        m_ref[h] = jnp.broadcast_to(m_new[h], (SUBLANES, LANES))
    for h in H:
        vs = slice(hd[h] * dv, (hd[h] + 1) * dv)
        hn = hh[h] * lax.rsqrt(jnp.mean(hh[h] * hh[h], axis=-1, keepdims=True) + NORM_EPS)
        o_raw = p_ref[sq[h], :, ML_O + hd[h] * dv:ML_O + (hd[h] + 1) * dv]
        o_ref[sq[h], :, vs] = (hn * nw_ref[:, vs] * _sigmoid(o_raw)).astype(o_ref.dtype)


def mlstm_branch(p_pad, conv_w, i_b, f_b, norm_w):
    B, S, C = p_pad.shape
    ifb = jnp.pad(jnp.concatenate([i_b, f_b]), (0, LANES - 2 * MLSTM_HEADS)).reshape(1, LANES)
    full = lambda a: pl.BlockSpec(a.shape, lambda b, c: (0,) * a.ndim)
    nw = norm_w.reshape(1, MLSTM_DIM)
    return pl.pallas_call(
        _mlstm_kernel,
        grid=(B // MLSTM_SEQS, S // CHUNK),
        in_specs=[pl.BlockSpec((MLSTM_SEQS, CHUNK, C), lambda b, c: (b, c, 0)), full(conv_w), full(ifb), full(nw)],
        out_specs=pl.BlockSpec((MLSTM_SEQS, CHUNK, MLSTM_DIM), lambda b, c: (b, c, 0)),
        out_shape=jax.ShapeDtypeStruct((B, S, MLSTM_DIM), BF16),
        scratch_shapes=[pltpu.VMEM((MLSTM_SEQS, SUBLANES, ML_V), F32),
                        pltpu.VMEM((MLSTM_SEQS * MLSTM_HEADS, MLSTM_QK_DIM, MLSTM_V_DIM), F32),
                        pltpu.VMEM((MLSTM_SEQS * MLSTM_HEADS, SUBLANES, LANES), F32),
                        pltpu.VMEM((MLSTM_SEQS * MLSTM_HEADS, SUBLANES, LANES), F32)],
        compiler_params=_cparams(("parallel", "arbitrary")),
        name="mlstm_scan",
    )(p_pad, conv_w, ifb, nw)


HI_MASK = 0xFFFF0000


def _pack_bf16_pairs(hb_f32):
    c = hb_f32.shape[1] // 2
    u = pltpu.bitcast(hb_f32, jnp.uint32)
    return u[:, c:] | (u[:, :c] >> 16)


ROW_TILES = D_MODEL // 2 // LANES


def _store_tiled(ref, val, first_row=0):
    n = val.shape[0]
    for t in range(ROW_TILES):
        ref[pl.ds(first_row * ROW_TILES + t, n, stride=ROW_TILES), :] = val[:, t * LANES:(t + 1) * LANES]


def _load_tiled(ref, n, first_row=0):
    return jnp.concatenate([ref[pl.ds(first_row * ROW_TILES + t, n, stride=ROW_TILES), :] for t in range(ROW_TILES)],
                           axis=1)


def _unpack_bf16_pairs(xu):
    lo = pltpu.bitcast(xu << 16, F32).astype(BF16)
    hi = pltpu.bitcast(xu & jnp.uint32(HI_MASK), F32).astype(BF16)
    return lo, hi


ROUTE_IDX, ROUTE_RANK, ROUTE_W = 0, TOP_K, 2 * TOP_K


def _route(logits, run_ref):
    tm = logits.shape[0]
    lane = lax.broadcasted_iota(jnp.int32, (1, ROUTER_PAD), 1)
    cur = jnp.where(lane < N_EXPERTS, logits, -jnp.inf)
    vals, hots = [], []
    for _ in range(TOP_K):
        m = jnp.max(cur, axis=-1, keepdims=True)
        idx = jnp.min(jnp.where(cur == m, lane, ROUTER_PAD), axis=-1, keepdims=True)
        hot = lane == idx
        vals.append(m)
        hots.append(hot)
        cur = jnp.where(hot, -jnp.inf, cur)
    ex = [jnp.exp(v - vals[0]) for v in vals]
    den = ex[0] + ex[1] + ex[2] + ex[3]
    chosen = jnp.where(hots[0] | hots[1] | hots[2] | hots[3], 1.0, 0.0)
    ri = lax.broadcasted_iota(jnp.int32, (tm, tm), 0)
    ci = lax.broadcasted_iota(jnp.int32, (tm, tm), 1)
    before = jnp.dot(jnp.where(ri > ci, 1.0, 0.0).astype(BF16), chosen.astype(BF16),
                     preferred_element_type=F32) + run_ref[0:1, :]
    run_ref[0:1, :] = run_ref[0:1, :] + jnp.sum(chosen, axis=0, keepdims=True)
    lanef = lane.astype(F32)
    rec = jnp.zeros((tm, ROUTER_PAD), F32)
    for j in range(TOP_K):
        e_j = jnp.sum(jnp.where(hots[j], lanef, 0.0), axis=-1, keepdims=True)
        rank_j = jnp.sum(jnp.where(hots[j], before, 0.0), axis=-1, keepdims=True)
        rec = jnp.where(lane == ROUTE_IDX + j, e_j, rec)
        rec = jnp.where(lane == ROUTE_RANK + j, rank_j, rec)
        rec = jnp.where(lane == ROUTE_W + j, ex[j] / den, rec)
    return rec


def _merge_kernel(oa_ref, ob_ref, hn_ref, bg_ref, x_ref, wa_ref, wb_ref, wg_ref, wo_ref, n2_ref, rwh_ref, rwl_ref,
                  rb_ref, x1_ref, xp_ref, rt_ref, cnt_ref, run_ref):
    @pl.when(pl.program_id(0) == 0)
    def _():
        run_ref[...] = jnp.zeros_like(run_ref)

    hn1 = hn_ref[...]

    def gated(o_ref, w_ref, cols):
        gate = _sigmoid(jnp.dot(hn1, wg_ref[:, cols], preferred_element_type=F32) + bg_ref[:, cols])
        return gate * jnp.dot(o_ref[...], w_ref[...], preferred_element_type=F32)

    merged = gated(oa_ref, wa_ref, slice(0, D_MODEL)) + gated(ob_ref, wb_ref, slice(D_MODEL, 2 * D_MODEL))
    x1 = x_ref[...] + jnp.dot(merged.astype(BF16), wo_ref[...], preferred_element_type=F32)
    x1_ref[...] = x1
    hn = x1 * lax.rsqrt(jnp.mean(x1 * x1, axis=-1, keepdims=True) + NORM_EPS) * n2_ref[...]
    hi = hn.astype(BF16)
    _store_tiled(xp_ref, _pack_bf16_pairs(hi.astype(F32)))
    lo = (hn - hi.astype(F32)).astype(BF16)
    logits = (jnp.dot(hi, rwh_ref[...], preferred_element_type=F32)
              + jnp.dot(hi, rwl_ref[...], preferred_element_type=F32)
              + jnp.dot(lo, rwh_ref[...], preferred_element_type=F32) + rb_ref[...])
    rt_ref[...] = _route(logits, run_ref)
    cnt_ref[...] = run_ref[...]


def merge_project(o_a, o_b, hn, w_gate, b_gate, x, w_a, w_b, w_out, norm2_w, router_w, router_b, tm=256):
    T, D = x.shape
    rw = jnp.pad(router_w, ((0, 0), (0, ROUTER_PAD - N_EXPERTS)))
    rw_hi = rw.astype(BF16)
    rw_lo = (rw - rw_hi.astype(F32)).astype(BF16)
    rb = jnp.pad(router_b, (0, ROUTER_PAD - N_EXPERTS)).reshape(1, ROUTER_PAD)
    rows = lambda n: pl.BlockSpec((tm, n), lambda i: (i, 0))
    full = lambda a: pl.BlockSpec(a.shape, lambda i: (0, 0), pipeline_mode=pl.Buffered(1))
    params = (w_a.astype(BF16), w_b.astype(BF16), w_gate, w_out.astype(BF16), norm2_w.reshape(1, D), rw_hi, rw_lo,
              rb)
    bg = b_gate.reshape(1, 2 * D)
    return pl.pallas_call(
        _merge_kernel,
        grid=(T // tm,),
        in_specs=[rows(o_a.shape[1]), rows(o_b.shape[1]), rows(D), full(bg), rows(D)] + [full(a) for a in params],
        out_specs=[rows(D), pl.BlockSpec((tm * ROW_TILES, LANES), lambda i: (i, 0)), rows(ROUTER_PAD),
                   pl.BlockSpec((SUBLANES, ROUTER_PAD), lambda i: (0, 0))],
        out_shape=[jax.ShapeDtypeStruct((T, D), F32), jax.ShapeDtypeStruct((T * ROW_TILES, LANES), jnp.uint32),
                   jax.ShapeDtypeStruct((T, ROUTER_PAD), F32), jax.ShapeDtypeStruct((SUBLANES, ROUTER_PAD), F32)],
        scratch_shapes=[pltpu.VMEM((SUBLANES, ROUTER_PAD), F32)],
        compiler_params=_cparams(("arbitrary",)),
        name="merge_project",
    )(o_a, o_b, hn, bg, x, *params)


UP_TILE = 1024
DOWN_TILE = 2048
DISPATCH_TOKENS = 256
COMBINE_TOKENS = 128


def _row_copy(src, dst, src_at, dst_at, sem):
    return pltpu.make_async_copy(src.at[pl.ds(pl.multiple_of(src_at, ROW_TILES), ROW_TILES), :],
                                 dst.at[pl.ds(pl.multiple_of(dst_at, ROW_TILES), ROW_TILES), :], sem)


def _dispatch_kernel(zrow_ref, dest_ref, xp_ref, xs_hbm, zbuf, zsem, sem):
    i = pl.program_id(0)
    tq = dest_ref.shape[2] // TOP_K

    def zero_copy(k):
        at = pl.multiple_of(zrow_ref[k], ROW_BLOCK * ROW_TILES)
        return pltpu.make_async_copy(zbuf, xs_hbm.at[pl.ds(at, ROW_BLOCK * ROW_TILES), :], zsem)

    @pl.when(i == 0)
    def _():
        zbuf[...] = jnp.zeros_like(zbuf)
        for k in range(zrow_ref.shape[0]):
            pl.when(zrow_ref[k] >= 0)(lambda k=k: zero_copy(k).start())
        for k in range(zrow_ref.shape[0]):
            pl.when(zrow_ref[k] >= 0)(lambda k=k: zero_copy(k).wait())

    def body(q, carry):
        base = pl.multiple_of(q * SUBLANES, SUBLANES)
        for r in range(SUBLANES):
            for j in range(TOP_K):
                _row_copy(xp_ref, xs_hbm, (base + r) * ROW_TILES, dest_ref[0, 0, TOP_K * (base + r) + j],
                          sem).start(priority=j % 2)
        return carry

    lax.fori_loop(0, tq // SUBLANES, body, 0)
    for j in range(TOP_K):
        pltpu.make_async_copy(xp_ref, xs_hbm.at[pl.ds(0, tq * ROW_TILES), :], sem).wait()


def moe_dispatch(xp, dest, zrow, n_rows):
    T = xp.shape[0] // ROW_TILES
    tq = DISPATCH_TOKENS
    dest_blk = (dest * ROW_TILES).reshape(T // tq, 1, TOP_K * tq)
    zrow = jnp.where(zrow >= 0, zrow * ROW_TILES, -1)
    return pl.pallas_call(
        _dispatch_kernel,
        grid_spec=pltpu.PrefetchScalarGridSpec(
            num_scalar_prefetch=1,
            grid=(T // tq,),
            in_specs=[pl.BlockSpec((1, 1, TOP_K * tq), lambda i, zr: (i, 0, 0), memory_space=pltpu.SMEM),
                      pl.BlockSpec((tq * ROW_TILES, LANES), lambda i, zr: (i, 0))],
            out_specs=pl.BlockSpec(memory_space=pl.ANY),
            scratch_shapes=[pltpu.VMEM((ROW_BLOCK * ROW_TILES, LANES), xp.dtype), pltpu.SemaphoreType.DMA(()),
                            pltpu.SemaphoreType.DMA(())]),
        out_shape=jax.ShapeDtypeStruct((n_rows * ROW_TILES, LANES), xp.dtype),
        compiler_params=_cparams(("arbitrary",)),
        name="moe_dispatch",
    )(zrow, dest_blk, xp)


SCHED_NV, SCHED_SG, SCHED_SB, SCHED_SO, SCHED_FULL, SCHED_GE, SCHED_GF, SCHED_NG = range(8)


def _stream_weights(s, sched, copies, on_arrival):
    sg_ref = sched[SCHED_SG]
    g = sg_ref[s]
    first = (s < sched[SCHED_NV][0]) & ((s == 0) | (g != sg_ref[jnp.maximum(s - 1, 0)]))

    @pl.when(first)
    def _():
        @pl.when(s == 0)
        def _():
            for c in copies(g):
                c.start()

        for c in copies(g):
            c.wait()
        on_arrival()

        @pl.when(g + 1 < sched[SCHED_NG][0])
        def _():
            for c in copies(g + 1):
                c.start()


def _for_used_rows(s, sched, out_ref, compute):
    valid = s < sched[SCHED_NV][0]
    full = sched[SCHED_FULL][s] > 0

    @pl.when(jnp.logical_not(valid))
    def _():
        out_ref[...] = jnp.zeros_like(out_ref)

    @pl.when(valid & full)
    def _():
        compute(ROW_BLOCK)

    @pl.when(valid & jnp.logical_not(full))
    def _():
        compute(ROW_HALF)
        rest = ROW_HALF * (out_ref.shape[0] // ROW_BLOCK)
        out_ref[rest:, :] = jnp.zeros((out_ref.shape[0] - rest, out_ref.shape[1]), out_ref.dtype)


def _moe_up_kernel(*refs):
    sched = refs[:8]
    xs_ref, w_hbm, bg_ref, bu_ref, h_ref, wbuf, wgb, wub, sems = refs[8:]
    s = pl.program_id(0)
    tf = wgb.shape[1]
    n_ff = w_hbm.shape[2] // 2

    def copies(g):
        e = sched[SCHED_GE][g]
        col = pl.multiple_of(sched[SCHED_GF][g] * tf, tf)
        return [pltpu.make_async_copy(w_hbm.at[e, :, pl.ds(half * n_ff + col, tf)], wbuf.at[half], sems.at[half])
                for half in range(2)]

    def on_arrival():
        wgb[...] = wbuf[0].astype(BF16)
        wub[...] = wbuf[1].astype(BF16)

    _stream_weights(s, sched, copies, on_arrival)

    def compute(rows):
        lo, hi = _unpack_bf16_pairs(_load_tiled(xs_ref, rows))
        half = lo.shape[1]

        def proj(wb, b_ref):
            return (jnp.dot(lo, wb[:half, :], preferred_element_type=F32)
                    + jnp.dot(hi, wb[half:, :], preferred_element_type=F32) + b_ref[...])

        gate = jnp.minimum(proj(wgb, bg_ref), SWIGLU_LIMIT)
        up = jnp.clip(proj(wub, bu_ref), -SWIGLU_LIMIT, SWIGLU_LIMIT)
        h_ref[:rows, :] = ((up + 1.0) * gate * _sigmoid(SWIGLU_ALPHA * gate)).astype(h_ref.dtype)

    _for_used_rows(s, sched, h_ref, compute)


def _step_expert(s, r):
    return r[SCHED_GE][r[SCHED_SG][s]]


def _step_tile(s, r):
    return r[SCHED_GF][r[SCHED_SG][s]]


def moe_up(xs, w_gu, b_gu, sched):
    P = xs.shape[0] // ROW_TILES
    E, D, F2 = w_gu.shape
    F = F2 // 2
    tf = UP_TILE
    nf = F // tf
    b3 = b_gu.reshape(E, 1, F2)
    return pl.pallas_call(
        _moe_up_kernel,
        grid_spec=pltpu.PrefetchScalarGridSpec(
            num_scalar_prefetch=len(sched),
            grid=(sched[SCHED_SG].shape[0],),
            in_specs=[pl.BlockSpec((ROW_BLOCK * ROW_TILES, LANES), lambda s, *r: (r[SCHED_SB][s], 0)),
                      pl.BlockSpec(memory_space=pl.ANY),
                      pl.BlockSpec((None, 1, tf), lambda s, *r: (_step_expert(s, r), 0, _step_tile(s, r))),
                      pl.BlockSpec((None, 1, tf), lambda s, *r: (_step_expert(s, r), 0, nf + _step_tile(s, r)))],
            out_specs=pl.BlockSpec((ROW_BLOCK, tf), lambda s, *r: (r[SCHED_SB][s], r[SCHED_SO][s])),
            scratch_shapes=[pltpu.VMEM((2, D, tf), F32), pltpu.VMEM((D, tf), BF16), pltpu.VMEM((D, tf), BF16),
                            pltpu.SemaphoreType.DMA((2,))]),
        out_shape=jax.ShapeDtypeStruct((P, F), BF16),
        compiler_params=_cparams(("arbitrary",)),
        name="moe_up",
    )(*sched, xs, w_gu, b3, b3)


def _moe_down_kernel(*refs):
    sched = refs[:8]
    h_ref, w_hbm, bd_ref, y_ref, wbuf, wdb, sem = refs[8:]
    s = pl.program_id(0)
    tn = wdb.shape[1]

    def copies(g):
        col = pl.multiple_of(sched[SCHED_GF][g] * tn, tn)
        return [pltpu.make_async_copy(w_hbm.at[sched[SCHED_GE][g], :, pl.ds(col, tn)], wbuf, sem)]

    def on_arrival():
        wdb[...] = wbuf[...].astype(BF16)

    _stream_weights(s, sched, copies, on_arrival)

    def compute(rows):
        y = jnp.dot(h_ref[:rows, :], wdb[...], preferred_element_type=F32) + bd_ref[...]
        _store_tiled(y_ref, _pack_bf16_pairs(y.astype(BF16).astype(F32)))

    _for_used_rows(s, sched, y_ref, compute)


def moe_down(h, w_down, b_down, sched):
    P, F = h.shape
    E, _, D = w_down.shape
    tn = DOWN_TILE
    assert tn == D, "the packed output pairs column c with column c + D/2"
    b3 = b_down.reshape(E, 1, D)
    return pl.pallas_call(
        _moe_down_kernel,
        grid_spec=pltpu.PrefetchScalarGridSpec(
            num_scalar_prefetch=len(sched),
            grid=(sched[SCHED_SG].shape[0],),
            in_specs=[pl.BlockSpec((ROW_BLOCK, F), lambda s, *r: (r[SCHED_SB][s], 0)),
                      pl.BlockSpec(memory_space=pl.ANY),
                      pl.BlockSpec((None, 1, tn), lambda s, *r: (_step_expert(s, r), 0, _step_tile(s, r)))],
            out_specs=pl.BlockSpec((ROW_BLOCK * ROW_TILES, LANES), lambda s, *r: (r[SCHED_SB][s], 0)),
            scratch_shapes=[pltpu.VMEM((F, tn), F32), pltpu.VMEM((F, tn), BF16), pltpu.SemaphoreType.DMA(())]),
        out_shape=jax.ShapeDtypeStruct((P * ROW_TILES, LANES), jnp.uint32),
        compiler_params=_cparams(("arbitrary",)),
        name="moe_down",
    )(*sched, h, w_down, b3)


def _combine_kernel(dcur_ref, dnxt_ref, x1_ref, w_ref, fw_ref, ys_hbm, o_ref, buf, sems):
    i = pl.program_id(0)
    n_steps = pl.num_programs(0)
    tq = x1_ref.shape[0]
    n = TOP_K * tq
    slot = lax.rem(i, 2)

    def issue(idx_ref, sl):
        def body(q, carry):
            base = pl.multiple_of(q * SUBLANES, SUBLANES)
            for r in range(SUBLANES):
                _row_copy(ys_hbm, buf.at[sl], idx_ref[0, 0, base + r], (base + r) * ROW_TILES,
                          sems.at[sl]).start(priority=r % 2)
            return carry

        lax.fori_loop(0, n // SUBLANES, body, 0)

    pl.when(i == 0)(lambda: issue(dcur_ref, 0))
    pl.when(i + 1 < n_steps)(lambda: issue(dnxt_ref, 1 - slot))
    pltpu.make_async_copy(ys_hbm.at[pl.ds(0, n * ROW_TILES), :], buf.at[slot], sems.at[slot]).wait()

    half = ROW_TILES * LANES
    acc_lo = x1_ref[:, :half]
    acc_hi = x1_ref[:, half:]
    for j in range(TOP_K):
        lo, hi = _unpack_bf16_pairs(_load_tiled(buf.at[slot], tq, first_row=j * tq))
        acc_lo = acc_lo + w_ref[:, j:j + 1] * lo.astype(F32)
        acc_hi = acc_hi + w_ref[:, j:j + 1] * hi.astype(F32)
    ssq = jnp.sum(acc_lo * acc_lo, axis=-1, keepdims=True) + jnp.sum(acc_hi * acc_hi, axis=-1, keepdims=True)
    scale = lax.rsqrt(ssq * (1.0 / (2 * half)) + NORM_EPS)
    o_ref[:, :half] = acc_lo * scale * fw_ref[:, :half]
    o_ref[:, half:] = acc_hi * scale * fw_ref[:, half:]


def combine(x1, ys, dest, top_w, final_w):
    T, D = x1.shape
    tq = COMBINE_TOKENS
    nblk = T // tq
    dest_blk = (dest * ROW_TILES).reshape(nblk, tq, TOP_K).transpose(0, 2, 1).reshape(nblk, 1, TOP_K * tq)
    idx_spec = lambda f: pl.BlockSpec((1, 1, TOP_K * tq), f, memory_space=pltpu.SMEM)
    return pl.pallas_call(
        _combine_kernel,
        grid=(nblk,),
        in_specs=[idx_spec(lambda i: (i, 0, 0)),
                  idx_spec(lambda i: (jnp.minimum(i + 1, nblk - 1), 0, 0)),
                  pl.BlockSpec((tq, D), lambda i: (i, 0)),
                  pl.BlockSpec((tq, TOP_K), lambda i: (i, 0)),
                  pl.BlockSpec((1, D), lambda i: (0, 0)),
                  pl.BlockSpec(memory_space=pl.ANY)],
        out_specs=pl.BlockSpec((tq, D), lambda i: (i, 0)),
        out_shape=jax.ShapeDtypeStruct((T, D), F32),
        scratch_shapes=[pltpu.VMEM((2, TOP_K * tq * ROW_TILES, LANES), ys.dtype), pltpu.SemaphoreType.DMA((2,))],
        compiler_params=_cparams(("arbitrary",)),
        name="moe_combine",
    )(dest_blk, dest_blk, x1, top_w, final_w.reshape(1, D), ys)


def _routing(route, counts):
    T = route.shape[0]
    TK = T * TOP_K
    NB = TK // ROW_BLOCK + N_EXPERTS
    flat_e = route[:, ROUTE_IDX:ROUTE_IDX + TOP_K].astype(jnp.int32).reshape(TK)
    rank = route[:, ROUTE_RANK:ROUTE_RANK + TOP_K].astype(jnp.int32).reshape(TK)
    top_w = route[:, ROUTE_W:ROUTE_W + TOP_K]
    counts = counts[0, :N_EXPERTS].astype(jnp.int32)
    padded = (counts + ROW_BLOCK - 1) // ROW_BLOCK * ROW_BLOCK
    pad_end = jnp.cumsum(padded).astype(jnp.int32)
    dest = ((pad_end - padded)[flat_e] + rank).astype(jnp.int32)
    block_start = jnp.arange(NB, dtype=jnp.int32) * ROW_BLOCK
    block_e = jnp.minimum(jnp.sum((pad_end[None, :] <= block_start[:, None]).astype(jnp.int32), axis=1),
                          N_EXPERTS - 1)
    nb_used = pad_end[-1] // ROW_BLOCK
    blk = jnp.arange(NB, dtype=jnp.int32)
    rows_left = counts[block_e] - (blk - ((pad_end - padded) // ROW_BLOCK)[block_e]) * ROW_BLOCK
    block_full = ((blk < nb_used) & (rows_left > ROW_HALF)).astype(jnp.int32)
    tail = nb_used + jnp.arange(N_EXPERTS, dtype=jnp.int32)
    zrow = jnp.concatenate([jnp.where(padded > 0, pad_end - ROW_BLOCK, -1),
                            jnp.where(tail < NB, tail * ROW_BLOCK, -1)]).astype(jnp.int32)
    return dest.reshape(T, TOP_K), top_w, block_e, block_full, padded, nb_used, zrow


def _schedule(block_e, block_full, padded, nb_used, n_tiles):
    nb = block_e.shape[0]
    b = jnp.tile(jnp.arange(nb, dtype=jnp.int32), n_tiles)
    f = jnp.repeat(jnp.arange(n_tiles, dtype=jnp.int32), nb)
    key = jnp.where(b < nb_used, (block_e[b] * n_tiles + f) * nb + b, (N_EXPERTS * n_tiles + f) * nb + b)
    order = jnp.argsort(key)
    used = padded > 0
    n_groups = (jnp.sum(used) * n_tiles).astype(jnp.int32)
    expert_pos = jnp.cumsum(used) - 1
    group = jnp.minimum(expert_pos[block_e[b]] * n_tiles + f, n_groups - 1).astype(jnp.int32)
    n_valid = (nb_used * n_tiles).astype(jnp.int32)
    sg = jnp.where(jnp.arange(nb * n_tiles) < n_valid, group[order], n_groups - 1)
    experts = jnp.arange(N_EXPERTS, dtype=jnp.int32)
    used_first = jnp.argsort(jnp.where(used, experts, N_EXPERTS + experts)).astype(jnp.int32)
    ge = jnp.repeat(used_first, n_tiles)
    gf = jnp.tile(jnp.arange(n_tiles, dtype=jnp.int32), N_EXPERTS)
    return (n_valid.reshape(1), sg, b[order], f[order], block_full[b][order], ge, gf, n_groups.reshape(1))


def _repack_w_in(w_in):
    w_in = w_in.astype(BF16)
    z = lambda n: jnp.zeros((w_in.shape[0], n), w_in.dtype)
    o = RWKV_COLS
    w_r = jnp.concatenate([w_in[:, :RW_WD], w_in[:, RW_WD:RW_WD + DECAY_LORA], z(LORA_PAD - DECAY_LORA),
                           w_in[:, RW_WD + DECAY_LORA:RW_WD + DECAY_LORA + AAA_LORA], z(LORA_PAD - AAA_LORA),
                           w_in[:, RW_WD + DECAY_LORA + AAA_LORA:o]], axis=1)
    w_m = jnp.concatenate([w_in[:, o:o + ML_IF], w_in[:, o + ML_IF:o + ML_IF + 2 * MLSTM_HEADS],
                           z(LANES - 2 * MLSTM_HEADS), w_in[:, o + ML_IF + 2 * MLSTM_HEADS:o + MLSTM_COLS]], axis=1)
    w_g = w_in[:, o + MLSTM_COLS:]
    return w_r, w_m, w_g


def kernel(x, norm1_w, w_in, b_gate, rwkv_mu, rwkv_w0, rwkv_w_up, rwkv_a0, rwkv_a_up, rwkv_g_up, rwkv_k_k,
           rwkv_k_a, rwkv_r_k, rwkv_ln_w, rwkv_ln_b, mlstm_conv_w, mlstm_i_b, mlstm_f_b, mlstm_norm_w,
           w_branch_a, w_branch_b, w_out, norm2_w, router_w, router_b, w_gu, b_gu, w_down, b_down,
           final_norm_w):
    B, S, D = x.shape
    T = B * S
    xt = x.reshape(T, D)
    assert norm1_w.shape[0] == 1, "single-layer block: the final rmsnorm is fused into the MoE combine"
    for l in range(1):
        w_r, w_m, w_g = _repack_w_in(w_in[l])
        hn, p_m, *scan_in = rwkv_in(x, norm1_w[l], w_r, w_m, rwkv_mu[l], rwkv_w0[l], rwkv_w_up[l], rwkv_a0[l],
                                    rwkv_a_up[l], rwkv_g_up[l], rwkv_k_k[l], rwkv_k_a[l])
        hn = hn.reshape(T, D)
        o_a = rwkv_scan(*scan_in, rwkv_r_k[l].reshape(1, -1), rwkv_ln_w[l].reshape(1, -1),
                        rwkv_ln_b[l].reshape(1, -1))
        o_b = mlstm_branch(p_m, mlstm_conv_w[l], mlstm_i_b[l], mlstm_f_b[l], mlstm_norm_w[l])
        x1, xp, route, counts = merge_project(o_a.reshape(T, RWKV_DIM), o_b.reshape(T, MLSTM_DIM), hn, w_g, b_gate[l], xt,
                                       w_branch_a[l], w_branch_b[l], w_out[l], norm2_w[l], router_w[l],
                                       router_b[l])
        dest, top_w, block_e, block_full, padded, nb_used, zrow = _routing(route, counts)
        xs = moe_dispatch(xp, dest, zrow, block_e.shape[0] * ROW_BLOCK)
        h = moe_up(xs, w_gu[l], b_gu[l], _schedule(block_e, block_full, padded, nb_used, EXPERT_FF // UP_TILE))
        ys = moe_down(h, w_down[l], b_down[l], _schedule(block_e, block_full, padded, nb_used, D // DOWN_TILE))
        xt = combine(x1, ys, dest, top_w, final_norm_w)
    return xt.reshape(B, S, D)
```

```python
import functools

import jax
import jax.numpy as jnp
import numpy as np
from jax import lax
from jax.experimental import pallas as pl
from jax.experimental.pallas import tpu as pltpu

F32 = jnp.float32
BF16 = jnp.bfloat16

D_MODEL = 2048
CHUNK = 64
NORM_EPS = 1e-6
RWKV_HEADS = 16
RWKV_HEAD_DIM = 64
RWKV_DIM = 1024
DECAY_LORA = 96
AAA_LORA = 96
GATE_LORA = 256
GN_EPS = 64e-5
RWKV_COLS = 3 * RWKV_DIM + DECAY_LORA + AAA_LORA + GATE_LORA
MLSTM_HEADS = 4
MLSTM_QK_DIM = 128
MLSTM_V_DIM = 256
MLSTM_QK = 512
MLSTM_DIM = 1024
CONV_WIDTH = 4
GATE_SOFTCAP = 15.0
MLSTM_COLS = 2 * MLSTM_QK + 2 * MLSTM_DIM + 2 * MLSTM_HEADS
N_EXPERTS = 32
TOP_K = 4
EXPERT_FF = 2048
SWIGLU_LIMIT = 7.0
SWIGLU_ALPHA = 1.702

LANES = 128
SUBLANES = 8
VMEM_LIMIT = 56 * 1024 * 1024

LORA_PAD = 128
RW_WD = 3 * RWKV_DIM
RW_AD = RW_WD + LORA_PAD
RW_GD = RW_AD + LORA_PAD
RW_COLS_P = RW_GD + GATE_LORA
ML_V = 2 * MLSTM_QK
ML_IF = ML_V + MLSTM_DIM
ML_O = ML_IF + LANES
ML_COLS_P = ML_O + MLSTM_DIM
ROUTER_PAD = 128
SCAN_SEQS = 4
MLSTM_SEQS = 1

ROW_BLOCK = 512
ROW_HALF = ROW_BLOCK // 2


def _cparams(sem):
    return pltpu.CompilerParams(dimension_semantics=sem, vmem_limit_bytes=VMEM_LIMIT)


def _bdot(a, b):
    return jnp.dot(a.astype(BF16), b.astype(BF16), preferred_element_type=F32)


def _split3(x):
    hi = x.astype(BF16)
    r1 = x - hi.astype(F32)
    mid = r1.astype(BF16)
    lo = (r1 - mid.astype(F32)).astype(BF16)
    return hi, mid, lo


def _dot_exact_lhs(mat_bf16, x):
    hi, mid, lo = _split3(x)
    return (jnp.dot(mat_bf16, hi, preferred_element_type=F32)
            + jnp.dot(mat_bf16, mid, preferred_element_type=F32)
            + jnp.dot(mat_bf16, lo, preferred_element_type=F32))


def _dot_exact_rhs(x, mat_bf16):
    hi, mid, lo = _split3(x)
    return (jnp.dot(hi, mat_bf16, preferred_element_type=F32)
            + jnp.dot(mid, mat_bf16, preferred_element_type=F32)
            + jnp.dot(lo, mat_bf16, preferred_element_type=F32))


def _sigmoid(x):
    return 1.0 / (1.0 + jnp.exp(-x))


def _softplus(x):
    return jnp.maximum(x, 0.0) + jnp.log(1.0 + jnp.exp(-jnp.abs(x)))


def _head_sum_mat():
    r = lax.broadcasted_iota(jnp.int32, (LANES, LANES), 0) // RWKV_HEAD_DIM
    c = lax.broadcasted_iota(jnp.int32, (LANES, LANES), 1) // RWKV_HEAD_DIM
    return jnp.where(r == c, 1.0, 0.0).astype(BF16)


def _pad_rows(w, n):
    return jnp.pad(w, ((0, n - w.shape[0]), (0, 0)))


PREP_GROUP = 256
MLSTM_PROJ_TILE = 640


def _rwkv_in_kernel(x_ref, n1_ref, w_ref, wm_ref, mu_ref, w0_ref, wup_ref, a0_ref, aup_ref, gup_ref, kk_ref,
                    ka_ref, hn_out, pm_out, r_out, k_out, v_out, kk_out, b_out, lw_out, g_out, carry_ref):
    i = pl.program_id(1)

    @pl.when(i == 0)
    def _():
        carry_ref[...] = jnp.zeros_like(carry_ref)

    x = x_ref[...]
    hn = (x * lax.rsqrt(jnp.mean(x * x, axis=-1, keepdims=True) + NORM_EPS) * n1_ref[...]).astype(BF16)
    hn_out[...] = hn
    tq = x.shape[0]
    row = lax.broadcasted_iota(jnp.int32, (tq, 1), 0)

    def shifted(cols):
        p = jnp.dot(hn, w_ref[:, cols], preferred_element_type=F32)
        prev = jnp.where(row == 0, carry_ref[0:1, cols], pltpu.roll(p, 1, 0))
        carry_ref[0:1, cols] = p[tq - 1:tq, :]
        return p + (prev - p) * mu_ref[:, cols]

    lora = shifted(slice(RW_WD, RW_COLS_P))
    wd_t = jnp.tanh(lora[:, 0:LORA_PAD]).astype(BF16)
    ad = lora[:, LORA_PAD:2 * LORA_PAD].astype(BF16)
    gd_s = _sigmoid(lora[:, 2 * LORA_PAD:]).astype(BF16)
    hs = _head_sum_mat()

    for c in range(0, RWKV_DIM, PREP_GROUP):
        cs = slice(c, c + PREP_GROUP)
        r = shifted(cs)
        k = shifted(slice(RWKV_DIM + c, RWKV_DIM + c + PREP_GROUP))
        v = shifted(slice(2 * RWKV_DIM + c, 2 * RWKV_DIM + c + PREP_GROUP))
        w_log = -_softplus(-(w0_ref[:, cs] + jnp.dot(wd_t, wup_ref[:, cs], preferred_element_type=F32))) - 0.5
        a = _sigmoid(a0_ref[:, cs] + jnp.dot(ad, aup_ref[:, cs], preferred_element_type=F32))
        kk = k * kk_ref[:, cs]
        nrm2 = jnp.concatenate(
            [_dot_exact_rhs(kk[:, t:t + LANES] * kk[:, t:t + LANES], hs) for t in range(0, PREP_GROUP, LANES)],
            axis=1)
        kk = kk / jnp.maximum(jnp.sqrt(nrm2), 1e-12)
        r_out[:, cs] = r.astype(r_out.dtype)
        k_out[:, cs] = (k * (1.0 + (a - 1.0) * ka_ref[:, cs])).astype(k_out.dtype)
        v_out[:, cs] = v.astype(v_out.dtype)
        kk_out[:, cs] = kk.astype(kk_out.dtype)
        b_out[:, cs] = (kk * a).astype(b_out.dtype)
        lw_out[:, cs] = -jnp.exp(w_log)
        g_out[:, cs] = jnp.dot(gd_s, gup_ref[:, cs], preferred_element_type=F32).astype(g_out.dtype)

    for c in range(0, ML_COLS_P, MLSTM_PROJ_TILE):
        cs = slice(c, c + MLSTM_PROJ_TILE)
        pm_out[:, cs] = jnp.dot(hn, wm_ref[:, cs], preferred_element_type=F32)


def rwkv_in(x, norm1_w, w_r, w_m, mu, w0, w_up, a0, a_up, g_up, k_k, k_a, tq=256):
    B, S, D = x.shape
    blk = lambda n: pl.BlockSpec((None, tq, n), lambda b, i: (b, i, 0))
    full = lambda a: pl.BlockSpec(a.shape, lambda b, i: (0,) * a.ndim, pipeline_mode=pl.Buffered(1))
    row = lambda t: t.reshape(1, -1)
    mu_p = jnp.concatenate([mu[:RW_WD], jnp.pad(mu[RW_WD:RW_WD + DECAY_LORA], (0, LORA_PAD - DECAY_LORA)),
                            jnp.pad(mu[RW_WD + DECAY_LORA:RW_WD + DECAY_LORA + AAA_LORA], (0, LORA_PAD - AAA_LORA)),
                            mu[RW_WD + DECAY_LORA + AAA_LORA:]])
    params = (row(norm1_w), w_r, w_m, row(mu_p), row(w0), _pad_rows(w_up, LORA_PAD).astype(BF16), row(a0),
              _pad_rows(a_up, LORA_PAD).astype(BF16), g_up.astype(BF16), row(k_k), row(k_a))
    out = lambda n, dt: jax.ShapeDtypeStruct((B, S, n), dt)
    return pl.pallas_call(
        _rwkv_in_kernel,
        grid=(B, S // tq),
        in_specs=[blk(D)] + [full(a) for a in params],
        out_specs=[blk(D), blk(ML_COLS_P)] + [blk(RWKV_DIM)] * 7,
        out_shape=[out(D, BF16), out(ML_COLS_P, F32)] + [out(RWKV_DIM, BF16)] * 5
        + [out(RWKV_DIM, F32), out(RWKV_DIM, BF16)],
        scratch_shapes=[pltpu.VMEM((SUBLANES, RW_COLS_P), F32)],
        compiler_params=_cparams(("parallel", "arbitrary")),
        name="rwkv_in",
    )(x, *params)


def _rwkv_scan_kernel(r_ref, k_ref, v_ref, kk_ref, b_ref, lw_ref, g_ref, rk_ref, lnw_ref, lnb_ref,
                      o_ref, h_ref):
    c = pl.program_id(1)

    @pl.when(c == 0)
    def _():
        h_ref[...] = jnp.zeros_like(h_ref)

    L = CHUNK
    L2 = 2 * L
    ri = lax.broadcasted_iota(jnp.int32, (L, L), 0)
    ci = lax.broadcasted_iota(jnp.int32, (L, L), 1)
    tril = jnp.where(ri >= ci, 1.0, 0.0).astype(BF16)

    n_seq = lw_ref.shape[0]
    cat = lambda ref: jnp.concatenate([ref[i] for i in range(n_seq)], axis=1)
    rep = lambda ref: jnp.concatenate([ref[...]] * n_seq, axis=1)
    lw = cat(lw_ref)
    cum = _dot_exact_lhs(tril, lw)
    cum_end = cum[L - 1:L, :]
    w_in = jnp.exp(cum)
    w_prev = jnp.exp(cum - lw)
    w_inv = jnp.exp(-cum)
    w_tail = jnp.exp(cum_end - cum)
    w_end = jnp.exp(cum_end)

    kk = cat(kk_ref).astype(F32)
    bb = cat(b_ref).astype(F32)
    kx = cat(k_ref).astype(F32)
    rx = cat(r_ref).astype(F32)
    vx = cat(v_ref).astype(F32)
    gx = cat(g_ref).astype(F32)
    a_hat = -kk * w_prev
    r_hat = rx * w_in
    b_hat = bb * w_inv
    k_hat = kx * w_inv
    b_til = bb * w_tail
    k_til = kx * w_tail
    rkk = rx * kx * rep(rk_ref)

    lane = lax.broadcasted_iota(jnp.int32, (1, LANES), 1)
    m_lo = jnp.where(lane < RWKV_HEAD_DIM, 1.0, 0.0)
    m_hi = 1.0 - m_lo

    def stack(x):
        return jnp.concatenate([x * m_lo, x * m_hi], axis=0)

    r2 = lax.broadcasted_iota(jnp.int32, (L2, L2), 0)
    c2 = lax.broadcasted_iota(jnp.int32, (L2, L2), 1)
    same_head = (r2 // L) == (c2 // L)
    strict = same_head & (r2 > c2)
    incl = same_head & (r2 >= c2)
    diag16 = (r2 // 16) == (c2 // 16)
    eye = jnp.where(r2 == c2, 1.0, 0.0)
    hs = _head_sum_mat()

    pairs_per_seq = RWKV_DIM // LANES
    pairs = range(n_seq * pairs_per_seq)
    sls = [slice(p * LANES, (p + 1) * LANES) for p in pairs]
    v_st = [stack(vx[:, sl]) for sl in sls]
    lhs = [jnp.concatenate([stack(a_hat[:, sl]), stack(r_hat[:, sl])], axis=0).astype(BF16) for sl in sls]
    rhs = [jnp.concatenate([stack(b_hat[:, sl]), stack(k_hat[:, sl])], axis=0).astype(BF16) for sl in sls]
    sc = [lax.dot_general(lhs[p], rhs[p], (((1,), (1,)), ((), ())), preferred_element_type=F32) for p in pairs]
    n_ab = [jnp.where(strict, sc[p][:L2, :L2], 0.0) for p in pairs]
    a_ak = [jnp.where(strict, sc[p][:L2, L2:], 0.0).astype(BF16) for p in pairs]
    a_r = [jnp.concatenate([jnp.where(incl, sc[p][L2:, :L2], 0.0), jnp.where(incl, sc[p][L2:, L2:], 0.0)],
                           axis=1).astype(BF16) for p in pairs]

    nd = [jnp.where(diag16, n_ab[p], 0.0) for p in pairs]
    noff = [(n_ab[p] - nd[p]).astype(BF16) for p in pairs]
    ndb = [nd[p].astype(BF16) for p in pairs]
    s2 = [jnp.dot(ndb[p], ndb[p], preferred_element_type=F32).astype(BF16) for p in pairs]
    s4 = [jnp.dot(s2[p], s2[p], preferred_element_type=F32).astype(BF16) for p in pairs]
    s8 = [jnp.dot(s4[p], s4[p], preferred_element_type=F32).astype(BF16) for p in pairs]
    x1 = [eye + nd[p] for p in pairs]
    x2 = [x1[p] + _bdot(x1[p], s2[p]) for p in pairs]
    x3 = [x2[p] + _bdot(x2[p], s4[p]) for p in pairs]
    t_d = [(x3[p] + _bdot(x3[p], s8[p])).astype(BF16) for p in pairs]
    m1 = [jnp.dot(t_d[p], noff[p], preferred_element_type=F32) for p in pairs]
    m1b = [m1[p].astype(BF16) for p in pairs]
    m2 = [jnp.dot(m1b[p], m1b[p], preferred_element_type=F32) for p in pairs]
    m3 = [jnp.dot(m1b[p], m2[p].astype(BF16), preferred_element_type=F32) for p in pairs]
    t_inv = [jnp.dot((eye + m1[p] + m2[p] + m3[p]).astype(BF16), t_d[p], preferred_element_type=F32).astype(BF16)
             for p in pairs]

    h0 = [h_ref[p] for p in pairs]
    ah = [jnp.dot(lhs[p], h0[p].astype(BF16), preferred_element_type=F32) for p in pairs]
    x = [ah[p][:L2] + jnp.dot(a_ak[p], v_st[p].astype(BF16), preferred_element_type=F32) for p in pairs]
    u = [jnp.dot(t_inv[p], x[p].astype(BF16), preferred_element_type=F32) for p in pairs]
    uv = [jnp.concatenate([u[p], v_st[p]], axis=0).astype(BF16) for p in pairs]
    y_st = [ah[p][L2:] + jnp.dot(a_r[p], uv[p], preferred_element_type=F32) for p in pairs]
    y = [y_st[p][:L] + y_st[p][L:] for p in pairs]

    for p in pairs:
        sl = sls[p]
        upd_l = jnp.concatenate([stack(b_til[:, sl]), stack(k_til[:, sl])], axis=0).astype(BF16)
        upd = lax.dot_general(upd_l, uv[p], (((0,), (0,)), ((), ())), preferred_element_type=F32)
        w_col = jnp.sum(eye * w_end[:, sl], axis=1, keepdims=True)
        h_ref[p] = w_col * h0[p] + upd

    def head_sums(vals):
        parts = []
        for t in vals:
            hi = t.astype(BF16)
            parts += [hi, (t - hi.astype(F32)).astype(BF16)]
        res = jnp.dot(jnp.concatenate(parts, axis=0), hs, preferred_element_type=F32)
        return [res[2 * i * L:(2 * i + 1) * L] + res[(2 * i + 1) * L:(2 * i + 2) * L] for i in range(len(vals))]

    sums1 = [head_sums([y[p], rkk[:, sls[p]]]) for p in pairs]
    d = [y[p] - sums1[p][0] * (1.0 / RWKV_HEAD_DIM) for p in pairs]
    var = [head_sums([d[p] * d[p]])[0] * (1.0 / RWKV_HEAD_DIM) for p in pairs]
    for p in pairs:
        sl = sls[p]
        psl = sls[p % pairs_per_seq]
        yn = d[p] * lax.rsqrt(var[p] + GN_EPS) * lnw_ref[:, psl] + lnb_ref[:, psl]
        o_ref[p // pairs_per_seq, :, psl] = ((yn + sums1[p][1] * vx[:, sl]) * gx[:, sl]).astype(o_ref.dtype)


def rwkv_scan(r, k, v, kk, b, lw, g, r_k, ln_w, ln_b):
    B, S, C = r.shape
    blk = pl.BlockSpec((SCAN_SEQS, CHUNK, C), lambda bb, c: (bb, c, 0))
    full = pl.BlockSpec((1, C), lambda bb, c: (0, 0))
    return pl.pallas_call(
        _rwkv_scan_kernel,
        grid=(B // SCAN_SEQS, S // CHUNK),
        in_specs=[blk] * 7 + [full] * 3,
        out_specs=blk,
        out_shape=jax.ShapeDtypeStruct((B, S, C), BF16),
        scratch_shapes=[pltpu.VMEM((SCAN_SEQS * C // LANES, LANES, LANES), F32)],
        compiler_params=_cparams(("parallel", "arbitrary")),
        name="rwkv_scan",
    )(r, k, v, kk, b, lw, g, r_k, ln_w, ln_b)


def _mlstm_kernel(p_ref, convw_ref, ifb_ref, nw_ref, o_ref, carry_ref, c_ref, n_ref, m_ref):
    ci = pl.program_id(1)

    @pl.when(ci == 0)
    def _():
        carry_ref[...] = jnp.zeros_like(carry_ref)
        c_ref[...] = jnp.zeros_like(c_ref)
        n_ref[...] = jnp.zeros_like(n_ref)
        m_ref[...] = jnp.zeros_like(m_ref)

    L = CHUNK
    n_seq = p_ref.shape[0]
    lane = lax.broadcasted_iota(jnp.int32, (1, LANES), 1)
    ri = lax.broadcasted_iota(jnp.int32, (L, L), 0)
    cj = lax.broadcasted_iota(jnp.int32, (L, L), 1)
    causal = ri >= cj
    tril = jnp.where(causal, 1.0, 0.0).astype(BF16)
    triu = jnp.where(ri <= cj, 1.0, 0.0).astype(BF16)

    qk, comb, comb_t, b_col, b_row = [], [], [], [], []
    for i in range(n_seq):
        u = p_ref[i, :, 0:ML_V]
        ext = jnp.concatenate([carry_ref[i], u], axis=0)
        carry_ref[i] = u[L - SUBLANES:L, :]
        conv = convw_ref[CONV_WIDTH - 1:CONV_WIDTH, :] * u
        for j in range(1, CONV_WIDTH):
            conv = conv + convw_ref[CONV_WIDTH - 1 - j:CONV_WIDTH - j, :] * pltpu.roll(ext, j, 0)[SUBLANES:, :]
        qk.append(conv * _sigmoid(conv))
        pre = GATE_SOFTCAP * jnp.tanh((p_ref[i, :, ML_IF:ML_O] + ifb_ref[...]) * (1.0 / GATE_SOFTCAP))
        cb = jnp.where(lane < MLSTM_HEADS, pre, -_softplus(-pre))
        comb.append(cb)
        comb_t.append(cb.T)
        b_col.append(_dot_exact_lhs(tril, cb))
        b_row.append(_dot_exact_rhs(comb_t[i], triu))

    H = range(n_seq * MLSTM_HEADS)
    sq = [v // MLSTM_HEADS for v in H]
    hd = [v % MLSTM_HEADS for v in H]
    dk, dv = MLSTM_QK_DIM, MLSTM_V_DIM
    qh = [qk[sq[v]][:, hd[v] * dk:(hd[v] + 1) * dk] * (dk ** -0.5) for v in H]
    kh = [qk[sq[v]][:, MLSTM_QK + hd[v] * dk:MLSTM_QK + (hd[v] + 1) * dk] for v in H]
    vh = [p_ref[sq[v], :, ML_V + hd[v] * dv:ML_V + (hd[v] + 1) * dv].astype(BF16) for v in H]
    qb = [qh[h].astype(BF16) for h in H]
    bcol = [b_col[sq[v]][:, MLSTM_HEADS + hd[v]:MLSTM_HEADS + hd[v] + 1] for v in H]
    brow = [b_row[sq[v]][MLSTM_HEADS + hd[v]:MLSTM_HEADS + hd[v] + 1, :] for v in H]
    m_prev = [m_ref[h][0:1, 0:1] for h in H]
    n_prev = [n_ref[h][0:1, :] for h in H]
    c_prev = [c_ref[h] for h in H]

    qk_t = [lax.dot_general(qb[h], kh[h].astype(BF16), (((1,), (1,)), ((), ())), preferred_element_type=F32)
            for h in H]
    qc = [jnp.dot(qb[h], c_prev[h].astype(BF16), preferred_element_type=F32) for h in H]
    dm = [jnp.where(causal, bcol[h] - brow[h] + comb_t[sq[h]][hd[h]:hd[h] + 1, :], -jnp.inf) for h in H]
    inter = [bcol[h] + m_prev[h] for h in H]
    m_t = [jnp.maximum(inter[h], jnp.max(dm[h], axis=-1, keepdims=True)) for h in H]
    s = [qk_t[h] * jnp.exp(dm[h] - m_t[h]) for h in H]
    w_inter = [jnp.exp(inter[h] - m_t[h]) for h in H]
    num = [jnp.dot(s[h].astype(BF16), vh[h], preferred_element_type=F32) + w_inter[h] * qc[h] for h in H]
    den = [jnp.sum(s[h], axis=-1, keepdims=True) + w_inter[h] * jnp.sum(qh[h] * n_prev[h], axis=-1, keepdims=True)
           for h in H]
    hh = [num[h] / jnp.maximum(jnp.abs(den[h]), jnp.exp(-m_t[h])) for h in H]

    g_tot = [bcol[h][L - 1:L, :] for h in H]
    a = [comb[sq[h]][:, hd[h]:hd[h] + 1] + g_tot[h] - bcol[h] for h in H]
    m_new = [jnp.maximum(g_tot[h] + m_prev[h], jnp.max(a[h], axis=0, keepdims=True)) for h in H]
    dec = [jnp.exp(g_tot[h] + m_prev[h] - m_new[h]) for h in H]
    wkk = [jnp.exp(a[h] - m_new[h]) * kh[h] for h in H]
    for h in H:
        c_ref[h] = dec[h] * c_prev[h] + lax.dot_general(wkk[h].astype(BF16), vh[h], (((0,), (0,)), ((), ())),
                                                        preferred_element_type=F32)
        n_ref[h] = jnp.broadcast_to(dec[h] * n_prev[h] + jnp.sum(wkk[h], axis=0, keepdims=True),
                                    (SUBLANES, LANES))
        m_ref[h] = jnp.broadcast_to(m_new[h], (SUBLANES, LANES))
    for h in H:
        vs = slice(hd[h] * dv, (hd[h] + 1) * dv)
        hn = hh[h] * lax.rsqrt(jnp.mean(hh[h] * hh[h], axis=-1, keepdims=True) + NORM_EPS)
        o_raw = p_ref[sq[h], :, ML_O + hd[h] * dv:ML_O + (hd[h] + 1) * dv]
        o_ref[sq[h], :, vs] = (hn * nw_ref[:, vs] * _sigmoid(o_raw)).astype(o_ref.dtype)


def mlstm_branch(p_pad, conv_w, i_b, f_b, norm_w):
    B, S, C = p_pad.shape
    ifb = jnp.pad(jnp.concatenate([i_b, f_b]), (0, LANES - 2 * MLSTM_HEADS)).reshape(1, LANES)
    full = lambda a: pl.BlockSpec(a.shape, lambda b, c: (0,) * a.ndim)
    nw = norm_w.reshape(1, MLSTM_DIM)
    return pl.pallas_call(
        _mlstm_kernel,
        grid=(B // MLSTM_SEQS, S // CHUNK),
        in_specs=[pl.BlockSpec((MLSTM_SEQS, CHUNK, C), lambda b, c: (b, c, 0)), full(conv_w), full(ifb), full(nw)],
        out_specs=pl.BlockSpec((MLSTM_SEQS, CHUNK, MLSTM_DIM), lambda b, c: (b, c, 0)),
        out_shape=jax.ShapeDtypeStruct((B, S, MLSTM_DIM), BF16),
        scratch_shapes=[pltpu.VMEM((MLSTM_SEQS, SUBLANES, ML_V), F32),
                        pltpu.VMEM((MLSTM_SEQS * MLSTM_HEADS, MLSTM_QK_DIM, MLSTM_V_DIM), F32),
                        pltpu.VMEM((MLSTM_SEQS * MLSTM_HEADS, SUBLANES, LANES), F32),
                        pltpu.VMEM((MLSTM_SEQS * MLSTM_HEADS, SUBLANES, LANES), F32)],
        compiler_params=_cparams(("parallel", "arbitrary")),
        name="mlstm_scan",
    )(p_pad, conv_w, ifb, nw)


HI_MASK = 0xFFFF0000


def _pack_bf16_pairs(hb_f32):
    c = hb_f32.shape[1] // 2
    u = pltpu.bitcast(hb_f32, jnp.uint32)
    return u[:, c:] | (u[:, :c] >> 16)


ROW_TILES = D_MODEL // 2 // LANES


def _store_tiled(ref, val, first_row=0):
    n = val.shape[0]
    for t in range(ROW_TILES):
        ref[pl.ds(first_row * ROW_TILES + t, n, stride=ROW_TILES), :] = val[:, t * LANES:(t + 1) * LANES]


def _load_tiled(ref, n, first_row=0):
    return jnp.concatenate([ref[pl.ds(first_row * ROW_TILES + t, n, stride=ROW_TILES), :] for t in range(ROW_TILES)],
                           axis=1)


def _unpack_bf16_pairs(xu):
    lo = pltpu.bitcast(xu << 16, F32).astype(BF16)
    hi = pltpu.bitcast(xu & jnp.uint32(HI_MASK), F32).astype(BF16)
    return lo, hi


def _merge_kernel(oa_ref, ob_ref, hn_ref, bg_ref, x_ref, wa_ref, wb_ref, wg_ref, wo_ref, n2_ref, rwh_ref, rwl_ref,
                  rb_ref, x1_ref, xp_ref, lg_ref):
    hn1 = hn_ref[...]

    def gated(o_ref, w_ref, cols):
        gate = _sigmoid(jnp.dot(hn1, wg_ref[:, cols], preferred_element_type=F32) + bg_ref[:, cols])
        return gate * jnp.dot(o_ref[...], w_ref[...], preferred_element_type=F32)

    merged = gated(oa_ref, wa_ref, slice(0, D_MODEL)) + gated(ob_ref, wb_ref, slice(D_MODEL, 2 * D_MODEL))
    x1 = x_ref[...] + jnp.dot(merged.astype(BF16), wo_ref[...], preferred_element_type=F32)
    x1_ref[...] = x1
    hn = x1 * lax.rsqrt(jnp.mean(x1 * x1, axis=-1, keepdims=True) + NORM_EPS) * n2_ref[...]
    hi = hn.astype(BF16)
    _store_tiled(xp_ref, _pack_bf16_pairs(hi.astype(F32)))
    lo = (hn - hi.astype(F32)).astype(BF16)
    lg_ref[...] = (jnp.dot(hi, rwh_ref[...], preferred_element_type=F32)
                   + jnp.dot(hi, rwl_ref[...], preferred_element_type=F32)
                   + jnp.dot(lo, rwh_ref[...], preferred_element_type=F32) + rb_ref[...])


def merge_project(o_a, o_b, hn, w_gate, b_gate, x, w_a, w_b, w_out, norm2_w, router_w, router_b, tm=256):
    T, D = x.shape
    rw = jnp.pad(router_w, ((0, 0), (0, ROUTER_PAD - N_EXPERTS)))
    rw_hi = rw.astype(BF16)
    rw_lo = (rw - rw_hi.astype(F32)).astype(BF16)
    rb = jnp.pad(router_b, (0, ROUTER_PAD - N_EXPERTS)).reshape(1, ROUTER_PAD)
    rows = lambda n: pl.BlockSpec((tm, n), lambda i: (i, 0))
    full = lambda a: pl.BlockSpec(a.shape, lambda i: (0, 0), pipeline_mode=pl.Buffered(1))
    params = (w_a.astype(BF16), w_b.astype(BF16), w_gate, w_out.astype(BF16), norm2_w.reshape(1, D), rw_hi, rw_lo,
              rb)
    bg = b_gate.reshape(1, 2 * D)
    return pl.pallas_call(
        _merge_kernel,
        grid=(T // tm,),
        in_specs=[rows(o_a.shape[1]), rows(o_b.shape[1]), rows(D), full(bg), rows(D)] + [full(a) for a in params],
        out_specs=[rows(D), pl.BlockSpec((tm * ROW_TILES, LANES), lambda i: (i, 0)), rows(ROUTER_PAD)],
        out_shape=[jax.ShapeDtypeStruct((T, D), F32), jax.ShapeDtypeStruct((T * ROW_TILES, LANES), jnp.uint32),
                   jax.ShapeDtypeStruct((T, ROUTER_PAD), F32)],
        compiler_params=_cparams(("parallel",)),
        name="merge_project",
    )(o_a, o_b, hn, bg, x, *params)


UP_TILE = 1024
DOWN_TILE = 2048
RANK_GROUP = 256
DISPATCH_TOKENS = 256
COMBINE_TOKENS = 128


def _row_copy(src, dst, src_at, dst_at, sem):
    return pltpu.make_async_copy(src.at[pl.ds(pl.multiple_of(src_at, ROW_TILES), ROW_TILES), :],
                                 dst.at[pl.ds(pl.multiple_of(dst_at, ROW_TILES), ROW_TILES), :], sem)


def _dispatch_kernel(zrow_ref, dest_ref, xp_ref, xs_hbm, zbuf, zsem, sem):
    i = pl.program_id(0)
    tq = dest_ref.shape[2] // TOP_K

    def zero_copy(k):
        at = pl.multiple_of(zrow_ref[k], ROW_BLOCK * ROW_TILES)
        return pltpu.make_async_copy(zbuf, xs_hbm.at[pl.ds(at, ROW_BLOCK * ROW_TILES), :], zsem)

    @pl.when(i == 0)
    def _():
        zbuf[...] = jnp.zeros_like(zbuf)
        for k in range(zrow_ref.shape[0]):
            pl.when(zrow_ref[k] >= 0)(lambda k=k: zero_copy(k).start())
        for k in range(zrow_ref.shape[0]):
            pl.when(zrow_ref[k] >= 0)(lambda k=k: zero_copy(k).wait())

    def body(q, carry):
        base = pl.multiple_of(q * SUBLANES, SUBLANES)
        for r in range(SUBLANES):
            for j in range(TOP_K):
                _row_copy(xp_ref, xs_hbm, (base + r) * ROW_TILES, dest_ref[0, 0, TOP_K * (base + r) + j],
                          sem).start(priority=j % 2)
        return carry

    lax.fori_loop(0, tq // SUBLANES, body, 0)
    for j in range(TOP_K):
        pltpu.make_async_copy(xp_ref, xs_hbm.at[pl.ds(0, tq * ROW_TILES), :], sem).wait()


def moe_dispatch(xp, dest, zrow, n_rows):
    T = xp.shape[0] // ROW_TILES
    tq = DISPATCH_TOKENS
    dest_blk = (dest * ROW_TILES).reshape(T // tq, 1, TOP_K * tq)
    zrow = jnp.where(zrow >= 0, zrow * ROW_TILES, -1)
    return pl.pallas_call(
        _dispatch_kernel,
        grid_spec=pltpu.PrefetchScalarGridSpec(
            num_scalar_prefetch=1,
            grid=(T // tq,),
            in_specs=[pl.BlockSpec((1, 1, TOP_K * tq), lambda i, zr: (i, 0, 0), memory_space=pltpu.SMEM),
                      pl.BlockSpec((tq * ROW_TILES, LANES), lambda i, zr: (i, 0))],
            out_specs=pl.BlockSpec(memory_space=pl.ANY),
            scratch_shapes=[pltpu.VMEM((ROW_BLOCK * ROW_TILES, LANES), xp.dtype), pltpu.SemaphoreType.DMA(()),
                            pltpu.SemaphoreType.DMA(())]),
        out_shape=jax.ShapeDtypeStruct((n_rows * ROW_TILES, LANES), xp.dtype),
        compiler_params=_cparams(("arbitrary",)),
        name="moe_dispatch",
    )(zrow, dest_blk, xp)


SCHED_NV, SCHED_SG, SCHED_SB, SCHED_SO, SCHED_FULL, SCHED_GE, SCHED_GF, SCHED_NG = range(8)


def _stream_weights(s, sched, copies, on_arrival):
    sg_ref = sched[SCHED_SG]
    g = sg_ref[s]
    first = (s < sched[SCHED_NV][0]) & ((s == 0) | (g != sg_ref[jnp.maximum(s - 1, 0)]))

    @pl.when(first)
    def _():
        @pl.when(s == 0)
        def _():
            for c in copies(g):
                c.start()

        for c in copies(g):
            c.wait()
        on_arrival()

        @pl.when(g + 1 < sched[SCHED_NG][0])
        def _():
            for c in copies(g + 1):
                c.start()


def _for_used_rows(s, sched, out_ref, compute):
    valid = s < sched[SCHED_NV][0]
    full = sched[SCHED_FULL][s] > 0

    @pl.when(jnp.logical_not(valid))
    def _():
        out_ref[...] = jnp.zeros_like(out_ref)

    @pl.when(valid & full)
    def _():
        compute(ROW_BLOCK)

    @pl.when(valid & jnp.logical_not(full))
    def _():
        compute(ROW_HALF)
        rest = ROW_HALF * (out_ref.shape[0] // ROW_BLOCK)
        out_ref[rest:, :] = jnp.zeros((out_ref.shape[0] - rest, out_ref.shape[1]), out_ref.dtype)


def _moe_up_kernel(*refs):
    sched = refs[:8]
    xs_ref, w_hbm, bg_ref, bu_ref, h_ref, wbuf, wgb, wub, sems = refs[8:]
    s = pl.program_id(0)
    tf = wgb.shape[1]
    n_ff = w_hbm.shape[2] // 2

    def copies(g):
        e = sched[SCHED_GE][g]
        col = pl.multiple_of(sched[SCHED_GF][g] * tf, tf)
        return [pltpu.make_async_copy(w_hbm.at[e, :, pl.ds(half * n_ff + col, tf)], wbuf.at[half], sems.at[half])
                for half in range(2)]

    def on_arrival():
        wgb[...] = wbuf[0].astype(BF16)
        wub[...] = wbuf[1].astype(BF16)

    _stream_weights(s, sched, copies, on_arrival)

    def compute(rows):
        lo, hi = _unpack_bf16_pairs(_load_tiled(xs_ref, rows))
        half = lo.shape[1]

        def proj(wb, b_ref):
            return (jnp.dot(lo, wb[:half, :], preferred_element_type=F32)
                    + jnp.dot(hi, wb[half:, :], preferred_element_type=F32) + b_ref[...])

        gate = jnp.minimum(proj(wgb, bg_ref), SWIGLU_LIMIT)
        up = jnp.clip(proj(wub, bu_ref), -SWIGLU_LIMIT, SWIGLU_LIMIT)
        h_ref[:rows, :] = ((up + 1.0) * gate * _sigmoid(SWIGLU_ALPHA * gate)).astype(h_ref.dtype)

    _for_used_rows(s, sched, h_ref, compute)


def _step_expert(s, r):
    return r[SCHED_GE][r[SCHED_SG][s]]


def _step_tile(s, r):
    return r[SCHED_GF][r[SCHED_SG][s]]


def moe_up(xs, w_gu, b_gu, sched):
    P = xs.shape[0] // ROW_TILES
    E, D, F2 = w_gu.shape
    F = F2 // 2
    tf = UP_TILE
    nf = F // tf
    b3 = b_gu.reshape(E, 1, F2)
    return pl.pallas_call(
        _moe_up_kernel,
        grid_spec=pltpu.PrefetchScalarGridSpec(
            num_scalar_prefetch=len(sched),
            grid=(sched[SCHED_SG].shape[0],),
            in_specs=[pl.BlockSpec((ROW_BLOCK * ROW_TILES, LANES), lambda s, *r: (r[SCHED_SB][s], 0)),
                      pl.BlockSpec(memory_space=pl.ANY),
                      pl.BlockSpec((None, 1, tf), lambda s, *r: (_step_expert(s, r), 0, _step_tile(s, r))),
                      pl.BlockSpec((None, 1, tf), lambda s, *r: (_step_expert(s, r), 0, nf + _step_tile(s, r)))],
            out_specs=pl.BlockSpec((ROW_BLOCK, tf), lambda s, *r: (r[SCHED_SB][s], r[SCHED_SO][s])),
            scratch_shapes=[pltpu.VMEM((2, D, tf), F32), pltpu.VMEM((D, tf), BF16), pltpu.VMEM((D, tf), BF16),
                            pltpu.SemaphoreType.DMA((2,))]),
        out_shape=jax.ShapeDtypeStruct((P, F), BF16),
        compiler_params=_cparams(("arbitrary",)),
        name="moe_up",
    )(*sched, xs, w_gu, b3, b3)


def _moe_down_kernel(*refs):
    sched = refs[:8]
    h_ref, w_hbm, bd_ref, y_ref, wbuf, wdb, sem = refs[8:]
    s = pl.program_id(0)
    tn = wdb.shape[1]

    def copies(g):
        col = pl.multiple_of(sched[SCHED_GF][g] * tn, tn)
        return [pltpu.make_async_copy(w_hbm.at[sched[SCHED_GE][g], :, pl.ds(col, tn)], wbuf, sem)]

    def on_arrival():
        wdb[...] = wbuf[...].astype(BF16)

    _stream_weights(s, sched, copies, on_arrival)

    def compute(rows):
        y = jnp.dot(h_ref[:rows, :], wdb[...], preferred_element_type=F32) + bd_ref[...]
        _store_tiled(y_ref, _pack_bf16_pairs(y.astype(BF16).astype(F32)))

    _for_used_rows(s, sched, y_ref, compute)


def moe_down(h, w_down, b_down, sched):
    P, F = h.shape
    E, _, D = w_down.shape
    tn = DOWN_TILE
    assert tn == D, "the packed output pairs column c with column c + D/2"
    b3 = b_down.reshape(E, 1, D)
    return pl.pallas_call(
        _moe_down_kernel,
        grid_spec=pltpu.PrefetchScalarGridSpec(
            num_scalar_prefetch=len(sched),
            grid=(sched[SCHED_SG].shape[0],),
            in_specs=[pl.BlockSpec((ROW_BLOCK, F), lambda s, *r: (r[SCHED_SB][s], 0)),
                      pl.BlockSpec(memory_space=pl.ANY),
                      pl.BlockSpec((None, 1, tn), lambda s, *r: (_step_expert(s, r), 0, _step_tile(s, r)))],
            out_specs=pl.BlockSpec((ROW_BLOCK * ROW_TILES, LANES), lambda s, *r: (r[SCHED_SB][s], 0)),
            scratch_shapes=[pltpu.VMEM((F, tn), F32), pltpu.VMEM((F, tn), BF16), pltpu.SemaphoreType.DMA(())]),
        out_shape=jax.ShapeDtypeStruct((P * ROW_TILES, LANES), jnp.uint32),
        compiler_params=_cparams(("arbitrary",)),
        name="moe_down",
    )(*sched, h, w_down, b3)


def _combine_kernel(dcur_ref, dnxt_ref, x1_ref, w_ref, fw_ref, ys_hbm, o_ref, buf, sems):
    i = pl.program_id(0)
    n_steps = pl.num_programs(0)
    tq = x1_ref.shape[0]
    n = TOP_K * tq
    slot = lax.rem(i, 2)

    def issue(idx_ref, sl):
        def body(q, carry):
            base = pl.multiple_of(q * SUBLANES, SUBLANES)
            for r in range(SUBLANES):
                _row_copy(ys_hbm, buf.at[sl], idx_ref[0, 0, base + r], (base + r) * ROW_TILES,
                          sems.at[sl]).start(priority=r % 2)
            return carry

        lax.fori_loop(0, n // SUBLANES, body, 0)

    pl.when(i == 0)(lambda: issue(dcur_ref, 0))
    pl.when(i + 1 < n_steps)(lambda: issue(dnxt_ref, 1 - slot))
    pltpu.make_async_copy(ys_hbm.at[pl.ds(0, n * ROW_TILES), :], buf.at[slot], sems.at[slot]).wait()

    half = ROW_TILES * LANES
    acc_lo = x1_ref[:, :half]
    acc_hi = x1_ref[:, half:]
    for j in range(TOP_K):
        lo, hi = _unpack_bf16_pairs(_load_tiled(buf.at[slot], tq, first_row=j * tq))
        acc_lo = acc_lo + w_ref[:, j:j + 1] * lo.astype(F32)
        acc_hi = acc_hi + w_ref[:, j:j + 1] * hi.astype(F32)
    ssq = jnp.sum(acc_lo * acc_lo, axis=-1, keepdims=True) + jnp.sum(acc_hi * acc_hi, axis=-1, keepdims=True)
    scale = lax.rsqrt(ssq * (1.0 / (2 * half)) + NORM_EPS)
    o_ref[:, :half] = acc_lo * scale * fw_ref[:, :half]
    o_ref[:, half:] = acc_hi * scale * fw_ref[:, half:]


def combine(x1, ys, dest, top_w, final_w):
    T, D = x1.shape
    tq = COMBINE_TOKENS
    nblk = T // tq
    dest_blk = (dest * ROW_TILES).reshape(nblk, tq, TOP_K).transpose(0, 2, 1).reshape(nblk, 1, TOP_K * tq)
    idx_spec = lambda f: pl.BlockSpec((1, 1, TOP_K * tq), f, memory_space=pltpu.SMEM)
    return pl.pallas_call(
        _combine_kernel,
        grid=(nblk,),
        in_specs=[idx_spec(lambda i: (i, 0, 0)),
                  idx_spec(lambda i: (jnp.minimum(i + 1, nblk - 1), 0, 0)),
                  pl.BlockSpec((tq, D), lambda i: (i, 0)),
                  pl.BlockSpec((tq, TOP_K), lambda i: (i, 0)),
                  pl.BlockSpec((1, D), lambda i: (0, 0)),
                  pl.BlockSpec(memory_space=pl.ANY)],
        out_specs=pl.BlockSpec((tq, D), lambda i: (i, 0)),
        out_shape=jax.ShapeDtypeStruct((T, D), F32),
        scratch_shapes=[pltpu.VMEM((2, TOP_K * tq * ROW_TILES, LANES), ys.dtype), pltpu.SemaphoreType.DMA((2,))],
        compiler_params=_cparams(("arbitrary",)),
        name="moe_combine",
    )(dest_blk, dest_blk, x1, top_w, final_w.reshape(1, D), ys)


def _routing(logits):
    T = logits.shape[0]
    TK = T * TOP_K
    NB = TK // ROW_BLOCK + N_EXPERTS
    top_logits, top_idx = lax.top_k(logits[:, :N_EXPERTS], TOP_K)
    top_w = jax.nn.softmax(top_logits, axis=-1)
    experts = jnp.arange(N_EXPERTS, dtype=jnp.int32)
    onehot = (top_idx.reshape(TK, 1) == experts[None, :]).astype(F32)
    oh = onehot.reshape(TK // RANK_GROUP, RANK_GROUP, N_EXPERTS)
    local = jnp.einsum("ts,gse->gte", jnp.tril(jnp.ones((RANK_GROUP, RANK_GROUP), F32)), oh)
    tot = local[:, -1, :]
    offs = jnp.cumsum(tot, axis=0) - tot
    counts = (offs[-1] + tot[-1]).astype(jnp.int32)
    padded = (counts + ROW_BLOCK - 1) // ROW_BLOCK * ROW_BLOCK
    pad_end = jnp.cumsum(padded).astype(jnp.int32)
    pad_start = pad_end - padded
    dest = jnp.sum(oh * (local + (offs + (pad_start.astype(F32) - 1.0)[None, :])[:, None, :]), axis=-1)
    dest = dest.reshape(T, TOP_K).astype(jnp.int32)

    blk = jnp.arange(NB, dtype=jnp.int32)
    block_e = jnp.minimum(jnp.sum((pad_end[None, :] <= (blk * ROW_BLOCK)[:, None]).astype(jnp.int32), axis=1),
                          N_EXPERTS - 1)
    of_block = lambda table: jnp.sum(jnp.where(block_e[:, None] == experts[None, :], table[None, :], 0), axis=1)
    nb_used = pad_end[-1] // ROW_BLOCK
    rows_left = of_block(counts) - (blk - of_block(pad_start // ROW_BLOCK)) * ROW_BLOCK
    block_full = ((blk < nb_used) & (rows_left > ROW_HALF)).astype(jnp.int32)
    used = padded > 0
    block_pos = of_block(jnp.cumsum(used.astype(jnp.int32)) - 1)
    tail = nb_used + experts
    zrow = jnp.concatenate([jnp.where(used, pad_end - ROW_BLOCK, -1),
                            jnp.where(tail < NB, tail * ROW_BLOCK, -1)]).astype(jnp.int32)
    return dest, top_w, (block_e, block_full, block_pos, used, nb_used), zrow


def _schedule(blocks, n_tiles):
    block_e, block_full, block_pos, used, nb_used = blocks
    nb = block_e.shape[0]
    b = jnp.asarray(np.tile(np.arange(nb, dtype=np.int32), n_tiles))
    f = jnp.asarray(np.repeat(np.arange(n_tiles, dtype=np.int32), nb))
    rep = lambda a: jnp.tile(a, n_tiles)
    n_groups = (jnp.sum(used) * n_tiles).astype(jnp.int32)
    n_valid = (nb_used * n_tiles).astype(jnp.int32)
    group = jnp.minimum(rep(block_pos) * n_tiles + f, n_groups - 1)
    key = jnp.where(b < nb_used, (rep(block_e) * n_tiles + f) * nb + b, (N_EXPERTS * n_tiles + f) * nb + b)
    _, group, sb, so, full = lax.sort((key, group, b, f, rep(block_full)), num_keys=1)
    sg = jnp.where(jnp.arange(nb * n_tiles) < n_valid, group, n_groups - 1)
    experts = jnp.arange(N_EXPERTS, dtype=jnp.int32)
    used_first = jnp.argsort(jnp.where(used, experts, N_EXPERTS + experts)).astype(jnp.int32)
    ge = jnp.repeat(used_first, n_tiles)
    gf = jnp.tile(jnp.arange(n_tiles, dtype=jnp.int32), N_EXPERTS)
    return (n_valid.reshape(1), sg, sb, so, full, ge, gf, n_groups.reshape(1))


def _repack_w_in(w_in):
    w_in = w_in.astype(BF16)
    z = lambda n: jnp.zeros((w_in.shape[0], n), w_in.dtype)
    o = RWKV_COLS
    w_r = jnp.concatenate([w_in[:, :RW_WD], w_in[:, RW_WD:RW_WD + DECAY_LORA], z(LORA_PAD - DECAY_LORA),
                           w_in[:, RW_WD + DECAY_LORA:RW_WD + DECAY_LORA + AAA_LORA], z(LORA_PAD - AAA_LORA),
                           w_in[:, RW_WD + DECAY_LORA + AAA_LORA:o]], axis=1)
    w_m = jnp.concatenate([w_in[:, o:o + ML_IF], w_in[:, o + ML_IF:o + ML_IF + 2 * MLSTM_HEADS],
                           z(LANES - 2 * MLSTM_HEADS), w_in[:, o + ML_IF + 2 * MLSTM_HEADS:o + MLSTM_COLS]], axis=1)
    w_g = w_in[:, o + MLSTM_COLS:]
    return w_r, w_m, w_g


def kernel(x, norm1_w, w_in, b_gate, rwkv_mu, rwkv_w0, rwkv_w_up, rwkv_a0, rwkv_a_up, rwkv_g_up, rwkv_k_k,
           rwkv_k_a, rwkv_r_k, rwkv_ln_w, rwkv_ln_b, mlstm_conv_w, mlstm_i_b, mlstm_f_b, mlstm_norm_w,
           w_branch_a, w_branch_b, w_out, norm2_w, router_w, router_b, w_gu, b_gu, w_down, b_down,
           final_norm_w):
    B, S, D = x.shape
    T = B * S
    xt = x.reshape(T, D)
    assert norm1_w.shape[0] == 1, "single-layer block: the final rmsnorm is fused into the MoE combine"
    for l in range(1):
        w_r, w_m, w_g = _repack_w_in(w_in[l])
        hn, p_m, *scan_in = rwkv_in(x, norm1_w[l], w_r, w_m, rwkv_mu[l], rwkv_w0[l], rwkv_w_up[l], rwkv_a0[l],
                                    rwkv_a_up[l], rwkv_g_up[l], rwkv_k_k[l], rwkv_k_a[l])
        hn = hn.reshape(T, D)
        o_a = rwkv_scan(*scan_in, rwkv_r_k[l].reshape(1, -1), rwkv_ln_w[l].reshape(1, -1),
                        rwkv_ln_b[l].reshape(1, -1))
        o_b = mlstm_branch(p_m, mlstm_conv_w[l], mlstm_i_b[l], mlstm_f_b[l], mlstm_norm_w[l])
        x1, xp, logits = merge_project(o_a.reshape(T, RWKV_DIM), o_b.reshape(T, MLSTM_DIM), hn, w_g, b_gate[l], xt,
                                       w_branch_a[l], w_branch_b[l], w_out[l], norm2_w[l], router_w[l],
                                       router_b[l])
        dest, top_w, blocks, zrow = _routing(logits)
        xs = moe_dispatch(xp, dest, zrow, blocks[0].shape[0] * ROW_BLOCK)
        h = moe_up(xs, w_gu[l], b_gu[l], _schedule(blocks, EXPERT_FF // UP_TILE))
        ys = moe_down(h, w_down[l], b_down[l], _schedule(blocks, D // DOWN_TILE))
        xt = combine(x1, ys, dest, top_w, final_norm_w)
    return xt.reshape(B, S, D)
```

```python
import functools

import jax
import jax.numpy as jnp
import numpy as np
from jax import lax
from jax.experimental import pallas as pl
from jax.experimental.pallas import tpu as pltpu

F32 = jnp.float32
BF16 = jnp.bfloat16

D_MODEL = 2048
CHUNK = 64
NORM_EPS = 1e-6
RWKV_HEADS = 16
RWKV_HEAD_DIM = 64
RWKV_DIM = 1024
DECAY_LORA = 96
AAA_LORA = 96
GATE_LORA = 256
GN_EPS = 64e-5
RWKV_COLS = 3 * RWKV_DIM + DECAY_LORA + AAA_LORA + GATE_LORA
MLSTM_HEADS = 4
MLSTM_QK_DIM = 128
MLSTM_V_DIM = 256
MLSTM_QK = 512
MLSTM_DIM = 1024
CONV_WIDTH = 4
GATE_SOFTCAP = 15.0
MLSTM_COLS = 2 * MLSTM_QK + 2 * MLSTM_DIM + 2 * MLSTM_HEADS
N_EXPERTS = 32
TOP_K = 4
EXPERT_FF = 2048
SWIGLU_LIMIT = 7.0
SWIGLU_ALPHA = 1.702

LANES = 128
SUBLANES = 8
VMEM_LIMIT = 56 * 1024 * 1024

LORA_PAD = 128
RW_WD = 3 * RWKV_DIM
RW_AD = RW_WD + LORA_PAD
RW_GD = RW_AD + LORA_PAD
RW_COLS_P = RW_GD + GATE_LORA
ML_V = 2 * MLSTM_QK
ML_IF = ML_V + MLSTM_DIM
ML_O = ML_IF + LANES
ML_COLS_P = ML_O + MLSTM_DIM
ROUTER_PAD = 128
SCAN_SEQS = 4
MLSTM_SEQS = 1

ROW_BLOCK = 512
ROW_PART = 128


def _cparams(sem):
    return pltpu.CompilerParams(dimension_semantics=sem, vmem_limit_bytes=VMEM_LIMIT)


def _bdot(a, b):
    return jnp.dot(a.astype(BF16), b.astype(BF16), preferred_element_type=F32)


def _split3(x):
    hi = x.astype(BF16)
    r1 = x - hi.astype(F32)
    mid = r1.astype(BF16)
    lo = (r1 - mid.astype(F32)).astype(BF16)
    return hi, mid, lo


def _dot_exact_lhs(mat_bf16, x):
    hi, mid, lo = _split3(x)
    return (jnp.dot(mat_bf16, hi, preferred_element_type=F32)
            + jnp.dot(mat_bf16, mid, preferred_element_type=F32)
            + jnp.dot(mat_bf16, lo, preferred_element_type=F32))


def _dot_exact_rhs(x, mat_bf16):
    hi, mid, lo = _split3(x)
    return (jnp.dot(hi, mat_bf16, preferred_element_type=F32)
            + jnp.dot(mid, mat_bf16, preferred_element_type=F32)
            + jnp.dot(lo, mat_bf16, preferred_element_type=F32))


def _sigmoid(x):
    return 1.0 / (1.0 + jnp.exp(-x))


def _softplus(x):
    return jnp.maximum(x, 0.0) + jnp.log(1.0 + jnp.exp(-jnp.abs(x)))


def _head_sum_mat():
    r = lax.broadcasted_iota(jnp.int32, (LANES, LANES), 0) // RWKV_HEAD_DIM
    c = lax.broadcasted_iota(jnp.int32, (LANES, LANES), 1) // RWKV_HEAD_DIM
    return jnp.where(r == c, 1.0, 0.0).astype(BF16)


def _pad_rows(w, n):
    return jnp.pad(w, ((0, n - w.shape[0]), (0, 0)))


PREP_GROUP = 256
MLSTM_PROJ_TILE = 640


def _rwkv_in_kernel(x_ref, n1_ref, w_ref, wm_ref, mu_ref, w0_ref, wup_ref, a0_ref, aup_ref, gup_ref, kk_ref,
                    ka_ref, hn_out, pm_out, r_out, k_out, v_out, kk_out, b_out, lw_out, g_out, carry_ref):
    i = pl.program_id(1)

    @pl.when(i == 0)
    def _():
        carry_ref[...] = jnp.zeros_like(carry_ref)

    x = x_ref[...]
    hn = (x * lax.rsqrt(jnp.mean(x * x, axis=-1, keepdims=True) + NORM_EPS) * n1_ref[...]).astype(BF16)
    hn_out[...] = hn
    tq = x.shape[0]
    row = lax.broadcasted_iota(jnp.int32, (tq, 1), 0)

    def shifted(cols):
        p = jnp.dot(hn, w_ref[:, cols], preferred_element_type=F32)
        prev = jnp.where(row == 0, carry_ref[0:1, cols], pltpu.roll(p, 1, 0))
        carry_ref[0:1, cols] = p[tq - 1:tq, :]
        return p + (prev - p) * mu_ref[:, cols]

    lora = shifted(slice(RW_WD, RW_COLS_P))
    wd_t = jnp.tanh(lora[:, 0:LORA_PAD]).astype(BF16)
    ad = lora[:, LORA_PAD:2 * LORA_PAD].astype(BF16)
    gd_s = _sigmoid(lora[:, 2 * LORA_PAD:]).astype(BF16)
    hs = _head_sum_mat()

    for c in range(0, RWKV_DIM, PREP_GROUP):
        cs = slice(c, c + PREP_GROUP)
        r = shifted(cs)
        k = shifted(slice(RWKV_DIM + c, RWKV_DIM + c + PREP_GROUP))
        v = shifted(slice(2 * RWKV_DIM + c, 2 * RWKV_DIM + c + PREP_GROUP))
        w_log = -_softplus(-(w0_ref[:, cs] + jnp.dot(wd_t, wup_ref[:, cs], preferred_element_type=F32))) - 0.5
        a = _sigmoid(a0_ref[:, cs] + jnp.dot(ad, aup_ref[:, cs], preferred_element_type=F32))
        kk = k * kk_ref[:, cs]
        nrm2 = jnp.concatenate(
            [_dot_exact_rhs(kk[:, t:t + LANES] * kk[:, t:t + LANES], hs) for t in range(0, PREP_GROUP, LANES)],
            axis=1)
        kk = kk / jnp.maximum(jnp.sqrt(nrm2), 1e-12)
        r_out[:, cs] = r.astype(r_out.dtype)
        k_out[:, cs] = (k * (1.0 + (a - 1.0) * ka_ref[:, cs])).astype(k_out.dtype)
        v_out[:, cs] = v.astype(v_out.dtype)
        kk_out[:, cs] = kk.astype(kk_out.dtype)
        b_out[:, cs] = (kk * a).astype(b_out.dtype)
        lw_out[:, cs] = -jnp.exp(w_log)
        g_out[:, cs] = jnp.dot(gd_s, gup_ref[:, cs], preferred_element_type=F32).astype(g_out.dtype)

    for c in range(0, ML_COLS_P, MLSTM_PROJ_TILE):
        cs = slice(c, c + MLSTM_PROJ_TILE)
        pm_out[:, cs] = jnp.dot(hn, wm_ref[:, cs], preferred_element_type=F32)


def rwkv_in(x, norm1_w, w_r, w_m, mu, w0, w_up, a0, a_up, g_up, k_k, k_a, tq=256):
    B, S, D = x.shape
    blk = lambda n: pl.BlockSpec((None, tq, n), lambda b, i: (b, i, 0))
    full = lambda a: pl.BlockSpec(a.shape, lambda b, i: (0,) * a.ndim, pipeline_mode=pl.Buffered(1))
    row = lambda t: t.reshape(1, -1)
    mu_p = jnp.concatenate([mu[:RW_WD], jnp.pad(mu[RW_WD:RW_WD + DECAY_LORA], (0, LORA_PAD - DECAY_LORA)),
                            jnp.pad(mu[RW_WD + DECAY_LORA:RW_WD + DECAY_LORA + AAA_LORA], (0, LORA_PAD - AAA_LORA)),
                            mu[RW_WD + DECAY_LORA + AAA_LORA:]])
    params = (row(norm1_w), w_r, w_m, row(mu_p), row(w0), _pad_rows(w_up, LORA_PAD).astype(BF16), row(a0),
              _pad_rows(a_up, LORA_PAD).astype(BF16), g_up.astype(BF16), row(k_k), row(k_a))
    out = lambda n, dt: jax.ShapeDtypeStruct((B, S, n), dt)
    return pl.pallas_call(
        _rwkv_in_kernel,
        grid=(B, S // tq),
        in_specs=[blk(D)] + [full(a) for a in params],
        out_specs=[blk(D), blk(ML_COLS_P)] + [blk(RWKV_DIM)] * 7,
        out_shape=[out(D, BF16), out(ML_COLS_P, F32)] + [out(RWKV_DIM, BF16)] * 5
        + [out(RWKV_DIM, F32), out(RWKV_DIM, BF16)],
        scratch_shapes=[pltpu.VMEM((SUBLANES, RW_COLS_P), F32)],
        compiler_params=_cparams(("parallel", "arbitrary")),
        name="rwkv_in",
    )(x, *params)


def _rwkv_scan_kernel(r_ref, k_ref, v_ref, kk_ref, b_ref, lw_ref, g_ref, rk_ref, lnw_ref, lnb_ref,
                      o_ref, h_ref):
    c = pl.program_id(1)

    @pl.when(c == 0)
    def _():
        h_ref[...] = jnp.zeros_like(h_ref)

    L = CHUNK
    L2 = 2 * L
    ri = lax.broadcasted_iota(jnp.int32, (L, L), 0)
    ci = lax.broadcasted_iota(jnp.int32, (L, L), 1)
    tril = jnp.where(ri >= ci, 1.0, 0.0).astype(BF16)

    n_seq = lw_ref.shape[0]
    cat = lambda ref: jnp.concatenate([ref[i] for i in range(n_seq)], axis=1)
    rep = lambda ref: jnp.concatenate([ref[...]] * n_seq, axis=1)
    lw = cat(lw_ref)
    cum = _dot_exact_lhs(tril, lw)
    cum_end = cum[L - 1:L, :]
    w_in = jnp.exp(cum)
    w_prev = jnp.exp(cum - lw)
    w_inv = jnp.exp(-cum)
    w_tail = jnp.exp(cum_end - cum)
    w_end = jnp.exp(cum_end)

    kk = cat(kk_ref).astype(F32)
    bb = cat(b_ref).astype(F32)
    kx = cat(k_ref).astype(F32)
    rx = cat(r_ref).astype(F32)
    vx = cat(v_ref).astype(F32)
    gx = cat(g_ref).astype(F32)
    a_hat = -kk * w_prev
    r_hat = rx * w_in
    b_hat = bb * w_inv
    k_hat = kx * w_inv
    b_til = bb * w_tail
    k_til = kx * w_tail
    rkk = rx * kx * rep(rk_ref)

    lane = lax.broadcasted_iota(jnp.int32, (1, LANES), 1)
    m_lo = jnp.where(lane < RWKV_HEAD_DIM, 1.0, 0.0)
    m_hi = 1.0 - m_lo

    def stack(x):
        return jnp.concatenate([x * m_lo, x * m_hi], axis=0)

    r2 = lax.broadcasted_iota(jnp.int32, (L2, L2), 0)
    c2 = lax.broadcasted_iota(jnp.int32, (L2, L2), 1)
    same_head = (r2 // L) == (c2 // L)
    strict = same_head & (r2 > c2)
    incl = same_head & (r2 >= c2)
    diag16 = (r2 // 16) == (c2 // 16)
    eye = jnp.where(r2 == c2, 1.0, 0.0)
    hs = _head_sum_mat()

    pairs_per_seq = RWKV_DIM // LANES
    pairs = range(n_seq * pairs_per_seq)
    sls = [slice(p * LANES, (p + 1) * LANES) for p in pairs]
    v_st = [stack(vx[:, sl]) for sl in sls]
    lhs = [jnp.concatenate([stack(a_hat[:, sl]), stack(r_hat[:, sl])], axis=0).astype(BF16) for sl in sls]
    rhs = [jnp.concatenate([stack(b_hat[:, sl]), stack(k_hat[:, sl])], axis=0).astype(BF16) for sl in sls]
    sc = [lax.dot_general(lhs[p], rhs[p], (((1,), (1,)), ((), ())), preferred_element_type=F32) for p in pairs]
    n_ab = [jnp.where(strict, sc[p][:L2, :L2], 0.0) for p in pairs]
    a_ak = [jnp.where(strict, sc[p][:L2, L2:], 0.0).astype(BF16) for p in pairs]
    a_r = [jnp.concatenate([jnp.where(incl, sc[p][L2:, :L2], 0.0), jnp.where(incl, sc[p][L2:, L2:], 0.0)],
                           axis=1).astype(BF16) for p in pairs]

    nd = [jnp.where(diag16, n_ab[p], 0.0) for p in pairs]
    noff = [(n_ab[p] - nd[p]).astype(BF16) for p in pairs]
    ndb = [nd[p].astype(BF16) for p in pairs]
    s2 = [jnp.dot(ndb[p], ndb[p], preferred_element_type=F32).astype(BF16) for p in pairs]
    s4 = [jnp.dot(s2[p], s2[p], preferred_element_type=F32).astype(BF16) for p in pairs]
    s8 = [jnp.dot(s4[p], s4[p], preferred_element_type=F32).astype(BF16) for p in pairs]
    x1 = [eye + nd[p] for p in pairs]
    x2 = [x1[p] + _bdot(x1[p], s2[p]) for p in pairs]
    x3 = [x2[p] + _bdot(x2[p], s4[p]) for p in pairs]
    t_d = [(x3[p] + _bdot(x3[p], s8[p])).astype(BF16) for p in pairs]
    m1 = [jnp.dot(t_d[p], noff[p], preferred_element_type=F32) for p in pairs]
    m1b = [m1[p].astype(BF16) for p in pairs]
    m2 = [jnp.dot(m1b[p], m1b[p], preferred_element_type=F32) for p in pairs]
    m3 = [jnp.dot(m1b[p], m2[p].astype(BF16), preferred_element_type=F32) for p in pairs]
    t_inv = [jnp.dot((eye + m1[p] + m2[p] + m3[p]).astype(BF16), t_d[p], preferred_element_type=F32).astype(BF16)
             for p in pairs]

    h0 = [h_ref[p] for p in pairs]
    ah = [jnp.dot(lhs[p], h0[p].astype(BF16), preferred_element_type=F32) for p in pairs]
    x = [ah[p][:L2] + jnp.dot(a_ak[p], v_st[p].astype(BF16), preferred_element_type=F32) for p in pairs]
    u = [jnp.dot(t_inv[p], x[p].astype(BF16), preferred_element_type=F32) for p in pairs]
    uv = [jnp.concatenate([u[p], v_st[p]], axis=0).astype(BF16) for p in pairs]
    y_st = [ah[p][L2:] + jnp.dot(a_r[p], uv[p], preferred_element_type=F32) for p in pairs]
    y = [y_st[p][:L] + y_st[p][L:] for p in pairs]

    for p in pairs:
        sl = sls[p]
        upd_l = jnp.concatenate([stack(b_til[:, sl]), stack(k_til[:, sl])], axis=0).astype(BF16)
        upd = lax.dot_general(upd_l, uv[p], (((0,), (0,)), ((), ())), preferred_element_type=F32)
        w_col = jnp.sum(eye * w_end[:, sl], axis=1, keepdims=True)
        h_ref[p] = w_col * h0[p] + upd

    def head_sums(vals):
        parts = []
        for t in vals:
            hi = t.astype(BF16)
            parts += [hi, (t - hi.astype(F32)).astype(BF16)]
        res = jnp.dot(jnp.concatenate(parts, axis=0), hs, preferred_element_type=F32)
        return [res[2 * i * L:(2 * i + 1) * L] + res[(2 * i + 1) * L:(2 * i + 2) * L] for i in range(len(vals))]

    sums1 = [head_sums([y[p], rkk[:, sls[p]]]) for p in pairs]
    d = [y[p] - sums1[p][0] * (1.0 / RWKV_HEAD_DIM) for p in pairs]
    var = [head_sums([d[p] * d[p]])[0] * (1.0 / RWKV_HEAD_DIM) for p in pairs]
    for p in pairs:
        sl = sls[p]
        psl = sls[p % pairs_per_seq]
        yn = d[p] * lax.rsqrt(var[p] + GN_EPS) * lnw_ref[:, psl] + lnb_ref[:, psl]
        o_ref[p // pairs_per_seq, :, psl] = ((yn + sums1[p][1] * vx[:, sl]) * gx[:, sl]).astype(o_ref.dtype)


def rwkv_scan(r, k, v, kk, b, lw, g, r_k, ln_w, ln_b):
    B, S, C = r.shape
    blk = pl.BlockSpec((SCAN_SEQS, CHUNK, C), lambda bb, c: (bb, c, 0))
    full = pl.BlockSpec((1, C), lambda bb, c: (0, 0))
    return pl.pallas_call(
        _rwkv_scan_kernel,
        grid=(B // SCAN_SEQS, S // CHUNK),
        in_specs=[blk] * 7 + [full] * 3,
        out_specs=blk,
        out_shape=jax.ShapeDtypeStruct((B, S, C), BF16),
        scratch_shapes=[pltpu.VMEM((SCAN_SEQS * C // LANES, LANES, LANES), F32)],
        compiler_params=_cparams(("parallel", "arbitrary")),
        name="rwkv_scan",
    )(r, k, v, kk, b, lw, g, r_k, ln_w, ln_b)


def _mlstm_kernel(p_ref, convw_ref, ifb_ref, nw_ref, o_ref, carry_ref, c_ref, n_ref, m_ref):
    ci = pl.program_id(1)

    @pl.when(ci == 0)
    def _():
        carry_ref[...] = jnp.zeros_like(carry_ref)
        c_ref[...] = jnp.zeros_like(c_ref)
        n_ref[...] = jnp.zeros_like(n_ref)
        m_ref[...] = jnp.zeros_like(m_ref)

    L = CHUNK
    n_seq = p_ref.shape[0]
    lane = lax.broadcasted_iota(jnp.int32, (1, LANES), 1)
    ri = lax.broadcasted_iota(jnp.int32, (L, L), 0)
    cj = lax.broadcasted_iota(jnp.int32, (L, L), 1)
    causal = ri >= cj
    tril = jnp.where(causal, 1.0, 0.0).astype(BF16)
    triu = jnp.where(ri <= cj, 1.0, 0.0).astype(BF16)

    qk, comb, comb_t, b_col, b_row = [], [], [], [], []
    for i in range(n_seq):
        u = p_ref[i, :, 0:ML_V]
        ext = jnp.concatenate([carry_ref[i], u], axis=0)
        carry_ref[i] = u[L - SUBLANES:L, :]
        conv = convw_ref[CONV_WIDTH - 1:CONV_WIDTH, :] * u
        for j in range(1, CONV_WIDTH):
            conv = conv + convw_ref[CONV_WIDTH - 1 - j:CONV_WIDTH - j, :] * pltpu.roll(ext, j, 0)[SUBLANES:, :]
        qk.append(conv * _sigmoid(conv))
        pre = GATE_SOFTCAP * jnp.tanh((p_ref[i, :, ML_IF:ML_O] + ifb_ref[...]) * (1.0 / GATE_SOFTCAP))
        cb = jnp.where(lane < MLSTM_HEADS, pre, -_softplus(-pre))
        comb.append(cb)
        comb_t.append(cb.T)
        b_col.append(_dot_exact_lhs(tril, cb))
        b_row.append(_dot_exact_rhs(comb_t[i], triu))

    H = range(n_seq * MLSTM_HEADS)
    sq = [v // MLSTM_HEADS for v in H]
    hd = [v % MLSTM_HEADS for v in H]
    dk, dv = MLSTM_QK_DIM, MLSTM_V_DIM
    qh = [qk[sq[v]][:, hd[v] * dk:(hd[v] + 1) * dk] * (dk ** -0.5) for v in H]
    kh = [qk[sq[v]][:, MLSTM_QK + hd[v] * dk:MLSTM_QK + (hd[v] + 1) * dk] for v in H]
    vh = [p_ref[sq[v], :, ML_V + hd[v] * dv:ML_V + (hd[v] + 1) * dv].astype(BF16) for v in H]
    qb = [qh[h].astype(BF16) for h in H]
    bcol = [b_col[sq[v]][:, MLSTM_HEADS + hd[v]:MLSTM_HEADS + hd[v] + 1] for v in H]
    brow = [b_row[sq[v]][MLSTM_HEADS + hd[v]:MLSTM_HEADS + hd[v] + 1, :] for v in H]
    m_prev = [m_ref[h][0:1, 0:1] for h in H]
    n_prev = [n_ref[h][0:1, :] for h in H]
    c_prev = [c_ref[h] for h in H]

    qk_t = [lax.dot_general(qb[h], kh[h].astype(BF16), (((1,), (1,)), ((), ())), preferred_element_type=F32)
            for h in H]
    qc = [jnp.dot(qb[h], c_prev[h].astype(BF16), preferred_element_type=F32) for h in H]
    dm = [jnp.where(causal, bcol[h] - brow[h] + comb_t[sq[h]][hd[h]:hd[h] + 1, :], -jnp.inf) for h in H]
    inter = [bcol[h] + m_prev[h] for h in H]
    m_t = [jnp.maximum(inter[h], jnp.max(dm[h], axis=-1, keepdims=True)) for h in H]
    s = [qk_t[h] * jnp.exp(dm[h] - m_t[h]) for h in H]
    w_inter = [jnp.exp(inter[h] - m_t[h]) for h in H]
    num = [jnp.dot(s[h].astype(BF16), vh[h], preferred_element_type=F32) + w_inter[h] * qc[h] for h in H]
    den = [jnp.sum(s[h], axis=-1, keepdims=True) + w_inter[h] * jnp.sum(qh[h] * n_prev[h], axis=-1, keepdims=True)
           for h in H]
    hh = [num[h] / jnp.maximum(jnp.abs(den[h]), jnp.exp(-m_t[h])) for h in H]

    g_tot = [bcol[h][L - 1:L, :] for h in H]
    a = [comb[sq[h]][:, hd[h]:hd[h] + 1] + g_tot[h] - bcol[h] for h in H]
    m_new = [jnp.maximum(g_tot[h] + m_prev[h], jnp.max(a[h], axis=0, keepdims=True)) for h in H]
    dec = [jnp.exp(g_tot[h] + m_prev[h] - m_new[h]) for h in H]
    wkk = [jnp.exp(a[h] - m_new[h]) * kh[h] for h in H]
    for h in H:
        c_ref[h] = dec[h] * c_prev[h] + lax.dot_general(wkk[h].astype(BF16), vh[h], (((0,), (0,)), ((), ())),
                                                        preferred_element_type=F32)
        n_ref[h] = jnp.broadcast_to(dec[h] * n_prev[h] + jnp.sum(wkk[h], axis=0, keepdims=True),
                                    (SUBLANES, LANES))
        m_ref[h] = jnp.broadcast_to(m_new[h], (SUBLANES, LANES))
    for h in H:
        vs = slice(hd[h] * dv, (hd[h] + 1) * dv)
        hn = hh[h] * lax.rsqrt(jnp.mean(hh[h] * hh[h], axis=-1, keepdims=True) + NORM_EPS)
        o_raw = p_ref[sq[h], :, ML_O + hd[h] * dv:ML_O + (hd[h] + 1) * dv]
        o_ref[sq[h], :, vs] = (hn * nw_ref[:, vs] * _sigmoid(o_raw)).astype(o_ref.dtype)


def mlstm_branch(p_pad, conv_w, i_b, f_b, norm_w):
    B, S, C = p_pad.shape
    ifb = jnp.pad(jnp.concatenate([i_b, f_b]), (0, LANES - 2 * MLSTM_HEADS)).reshape(1, LANES)
    full = lambda a: pl.BlockSpec(a.shape, lambda b, c: (0,) * a.ndim)
    nw = norm_w.reshape(1, MLSTM_DIM)
    return pl.pallas_call(
        _mlstm_kernel,
        grid=(B // MLSTM_SEQS, S // CHUNK),
        in_specs=[pl.BlockSpec((MLSTM_SEQS, CHUNK, C), lambda b, c: (b, c, 0)), full(conv_w), full(ifb), full(nw)],
        out_specs=pl.BlockSpec((MLSTM_SEQS, CHUNK, MLSTM_DIM), lambda b, c: (b, c, 0)),
        out_shape=jax.ShapeDtypeStruct((B, S, MLSTM_DIM), BF16),
        scratch_shapes=[pltpu.VMEM((MLSTM_SEQS, SUBLANES, ML_V), F32),
                        pltpu.VMEM((MLSTM_SEQS * MLSTM_HEADS, MLSTM_QK_DIM, MLSTM_V_DIM), F32),
                        pltpu.VMEM((MLSTM_SEQS * MLSTM_HEADS, SUBLANES, LANES), F32),
                        pltpu.VMEM((MLSTM_SEQS * MLSTM_HEADS, SUBLANES, LANES), F32)],
        compiler_params=_cparams(("parallel", "arbitrary")),
        name="mlstm_scan",
    )(p_pad, conv_w, ifb, nw)


HI_MASK = 0xFFFF0000


def _pack_bf16_pairs(hb_f32):
    c = hb_f32.shape[1] // 2
    u = pltpu.bitcast(hb_f32, jnp.uint32)
    return u[:, c:] | (u[:, :c] >> 16)


ROW_TILES = D_MODEL // 2 // LANES


def _store_tiled(ref, val, first_row=0):
    n = val.shape[0]
    for t in range(ROW_TILES):
        ref[pl.ds(first_row * ROW_TILES + t, n, stride=ROW_TILES), :] = val[:, t * LANES:(t + 1) * LANES]


def _load_tiled(ref, n, first_row=0):
    return jnp.concatenate([ref[pl.ds(first_row * ROW_TILES + t, n, stride=ROW_TILES), :] for t in range(ROW_TILES)],
                           axis=1)


def _unpack_bf16_pairs(xu):
    lo = pltpu.bitcast(xu << 16, F32).astype(BF16)
    hi = pltpu.bitcast(xu & jnp.uint32(HI_MASK), F32).astype(BF16)
    return lo, hi


def _merge_kernel(oa_ref, ob_ref, hn_ref, bg_ref, x_ref, wa_ref, wb_ref, wg_ref, wo_ref, n2_ref, rwh_ref, rwl_ref,
                  rb_ref, x1_ref, xp_ref, lg_ref):
    hn1 = hn_ref[...]

    def gated(o_ref, w_ref, cols):
        gate = _sigmoid(jnp.dot(hn1, wg_ref[:, cols], preferred_element_type=F32) + bg_ref[:, cols])
        return gate * jnp.dot(o_ref[...], w_ref[...], preferred_element_type=F32)

    merged = gated(oa_ref, wa_ref, slice(0, D_MODEL)) + gated(ob_ref, wb_ref, slice(D_MODEL, 2 * D_MODEL))
    x1 = x_ref[...] + jnp.dot(merged.astype(BF16), wo_ref[...], preferred_element_type=F32)
    x1_ref[...] = x1
    hn = x1 * lax.rsqrt(jnp.mean(x1 * x1, axis=-1, keepdims=True) + NORM_EPS) * n2_ref[...]
    hi = hn.astype(BF16)
    _store_tiled(xp_ref, _pack_bf16_pairs(hi.astype(F32)))
    lo = (hn - hi.astype(F32)).astype(BF16)
    lg_ref[...] = (jnp.dot(hi, rwh_ref[...], preferred_element_type=F32)
                   + jnp.dot(hi, rwl_ref[...], preferred_element_type=F32)
                   + jnp.dot(lo, rwh_ref[...], preferred_element_type=F32) + rb_ref[...])


def merge_project(o_a, o_b, hn, w_gate, b_gate, x, w_a, w_b, w_out, norm2_w, router_w, router_b, tm=256):
    T, D = x.shape
    rw = jnp.pad(router_w, ((0, 0), (0, ROUTER_PAD - N_EXPERTS)))
    rw_hi = rw.astype(BF16)
    rw_lo = (rw - rw_hi.astype(F32)).astype(BF16)
    rb = jnp.pad(router_b, (0, ROUTER_PAD - N_EXPERTS)).reshape(1, ROUTER_PAD)
    rows = lambda n: pl.BlockSpec((tm, n), lambda i: (i, 0))
    full = lambda a: pl.BlockSpec(a.shape, lambda i: (0, 0), pipeline_mode=pl.Buffered(1))
    params = (w_a.astype(BF16), w_b.astype(BF16), w_gate, w_out.astype(BF16), norm2_w.reshape(1, D), rw_hi, rw_lo,
              rb)
    bg = b_gate.reshape(1, 2 * D)
    return pl.pallas_call(
        _merge_kernel,
        grid=(T // tm,),
        in_specs=[rows(o_a.shape[1]), rows(o_b.shape[1]), rows(D), full(bg), rows(D)] + [full(a) for a in params],
        out_specs=[rows(D), pl.BlockSpec((tm * ROW_TILES, LANES), lambda i: (i, 0)), rows(ROUTER_PAD)],
        out_shape=[jax.ShapeDtypeStruct((T, D), F32), jax.ShapeDtypeStruct((T * ROW_TILES, LANES), jnp.uint32),
                   jax.ShapeDtypeStruct((T, ROUTER_PAD), F32)],
        compiler_params=_cparams(("parallel",)),
        name="merge_project",
    )(o_a, o_b, hn, bg, x, *params)


UP_TILE = 1024
DOWN_TILE = 2048
RANK_GROUP = 256
DISPATCH_TOKENS = 256
COMBINE_TOKENS = 128


def _row_copy(src, dst, src_at, dst_at, sem):
    return pltpu.make_async_copy(src.at[pl.ds(pl.multiple_of(src_at, ROW_TILES), ROW_TILES), :],
                                 dst.at[pl.ds(pl.multiple_of(dst_at, ROW_TILES), ROW_TILES), :], sem)


def _dispatch_kernel(zrow_ref, dest_ref, xp_ref, xs_hbm, zbuf, zsem, sem):
    i = pl.program_id(0)
    tq = dest_ref.shape[2] // TOP_K

    def zero_copy(k):
        at = pl.multiple_of(zrow_ref[k], ROW_BLOCK * ROW_TILES)
        return pltpu.make_async_copy(zbuf, xs_hbm.at[pl.ds(at, ROW_BLOCK * ROW_TILES), :], zsem)

    @pl.when(i == 0)
    def _():
        zbuf[...] = jnp.zeros_like(zbuf)
        for k in range(zrow_ref.shape[0]):
            pl.when(zrow_ref[k] >= 0)(lambda k=k: zero_copy(k).start())
        for k in range(zrow_ref.shape[0]):
            pl.when(zrow_ref[k] >= 0)(lambda k=k: zero_copy(k).wait())

    def body(q, carry):
        base = pl.multiple_of(q * SUBLANES, SUBLANES)
        for r in range(SUBLANES):
            for j in range(TOP_K):
                _row_copy(xp_ref, xs_hbm, (base + r) * ROW_TILES, dest_ref[0, 0, TOP_K * (base + r) + j],
                          sem).start(priority=j % 2)
        return carry

    lax.fori_loop(0, tq // SUBLANES, body, 0)
    for j in range(TOP_K):
        pltpu.make_async_copy(xp_ref, xs_hbm.at[pl.ds(0, tq * ROW_TILES), :], sem).wait()


def moe_dispatch(xp, dest, zrow, n_rows):
    T = xp.shape[0] // ROW_TILES
    tq = DISPATCH_TOKENS
    dest_blk = (dest * ROW_TILES).reshape(T // tq, 1, TOP_K * tq)
    zrow = jnp.where(zrow >= 0, zrow * ROW_TILES, -1)
    return pl.pallas_call(
        _dispatch_kernel,
        grid_spec=pltpu.PrefetchScalarGridSpec(
            num_scalar_prefetch=1,
            grid=(T // tq,),
            in_specs=[pl.BlockSpec((1, 1, TOP_K * tq), lambda i, zr: (i, 0, 0), memory_space=pltpu.SMEM),
                      pl.BlockSpec((tq * ROW_TILES, LANES), lambda i, zr: (i, 0))],
            out_specs=pl.BlockSpec(memory_space=pl.ANY),
            scratch_shapes=[pltpu.VMEM((ROW_BLOCK * ROW_TILES, LANES), xp.dtype), pltpu.SemaphoreType.DMA(()),
                            pltpu.SemaphoreType.DMA(())]),
        out_shape=jax.ShapeDtypeStruct((n_rows * ROW_TILES, LANES), xp.dtype),
        compiler_params=_cparams(("arbitrary",)),
        name="moe_dispatch",
    )(zrow, dest_blk, xp)


SCHED_NV, SCHED_SG, SCHED_SB, SCHED_SO, SCHED_PARTS, SCHED_GE, SCHED_GF, SCHED_NG = range(8)


def _stream_weights(s, sched, copies, on_arrival):
    sg_ref = sched[SCHED_SG]
    g = sg_ref[s]
    first = (s < sched[SCHED_NV][0]) & ((s == 0) | (g != sg_ref[jnp.maximum(s - 1, 0)]))

    @pl.when(first)
    def _():
        @pl.when(s == 0)
        def _():
            for c in copies(g):
                c.start()

        for c in copies(g):
            c.wait()
        on_arrival()

        @pl.when(g + 1 < sched[SCHED_NG][0])
        def _():
            for c in copies(g + 1):
                c.start()


def _for_used_rows(s, sched, out_ref, compute):
    parts = sched[SCHED_PARTS][s]
    rows_per_row = out_ref.shape[0] // ROW_BLOCK

    @pl.when(s >= sched[SCHED_NV][0])
    def _():
        out_ref[...] = jnp.zeros_like(out_ref)

    def variant(rows):
        compute(rows)
        if rows < ROW_BLOCK:
            rest = rows * rows_per_row
            out_ref[rest:, :] = jnp.zeros((out_ref.shape[0] - rest, out_ref.shape[1]), out_ref.dtype)

    for p in range(1, ROW_BLOCK // ROW_PART + 1):
        pl.when((s < sched[SCHED_NV][0]) & (parts == p))(functools.partial(variant, p * ROW_PART))


def _moe_up_kernel(*refs):
    sched = refs[:8]
    xs_ref, w_hbm, bg_ref, bu_ref, h_ref, wbuf, wgb, wub, sems = refs[8:]
    s = pl.program_id(0)
    tf = wgb.shape[1]
    n_ff = w_hbm.shape[2] // 2

    def copies(g):
        e = sched[SCHED_GE][g]
        col = pl.multiple_of(sched[SCHED_GF][g] * tf, tf)
        return [pltpu.make_async_copy(w_hbm.at[e, :, pl.ds(half * n_ff + col, tf)], wbuf.at[half], sems.at[half])
                for half in range(2)]

    def on_arrival():
        wgb[...] = wbuf[0].astype(BF16)
        wub[...] = wbuf[1].astype(BF16)

    _stream_weights(s, sched, copies, on_arrival)

    def compute(rows):
        lo, hi = _unpack_bf16_pairs(_load_tiled(xs_ref, rows))
        half = lo.shape[1]

        def proj(wb, b_ref):
            return (jnp.dot(lo, wb[:half, :], preferred_element_type=F32)
                    + jnp.dot(hi, wb[half:, :], preferred_element_type=F32) + b_ref[...])

        gate = jnp.minimum(proj(wgb, bg_ref), SWIGLU_LIMIT)
        up = jnp.clip(proj(wub, bu_ref), -SWIGLU_LIMIT, SWIGLU_LIMIT)
        h_ref[:rows, :] = ((up + 1.0) * gate * _sigmoid(SWIGLU_ALPHA * gate)).astype(h_ref.dtype)

    _for_used_rows(s, sched, h_ref, compute)


def _step_expert(s, r):
    return r[SCHED_GE][r[SCHED_SG][s]]


def _step_tile(s, r):
    return r[SCHED_GF][r[SCHED_SG][s]]


def moe_up(xs, w_gu, b_gu, sched):
    P = xs.shape[0] // ROW_TILES
    E, D, F2 = w_gu.shape
    F = F2 // 2
    tf = UP_TILE
    nf = F // tf
    b3 = b_gu.reshape(E, 1, F2)
    return pl.pallas_call(
        _moe_up_kernel,
        grid_spec=pltpu.PrefetchScalarGridSpec(
            num_scalar_prefetch=len(sched),
            grid=(sched[SCHED_SG].shape[0],),
            in_specs=[pl.BlockSpec((ROW_BLOCK * ROW_TILES, LANES), lambda s, *r: (r[SCHED_SB][s], 0)),
                      pl.BlockSpec(memory_space=pl.ANY),
                      pl.BlockSpec((None, 1, tf), lambda s, *r: (_step_expert(s, r), 0, _step_tile(s, r))),
                      pl.BlockSpec((None, 1, tf), lambda s, *r: (_step_expert(s, r), 0, nf + _step_tile(s, r)))],
            out_specs=pl.BlockSpec((ROW_BLOCK, tf), lambda s, *r: (r[SCHED_SB][s], r[SCHED_SO][s])),
            scratch_shapes=[pltpu.VMEM((2, D, tf), F32), pltpu.VMEM((D, tf), BF16), pltpu.VMEM((D, tf), BF16),
                            pltpu.SemaphoreType.DMA((2,))]),
        out_shape=jax.ShapeDtypeStruct((P, F), BF16),
        compiler_params=_cparams(("arbitrary",)),
        name="moe_up",
    )(*sched, xs, w_gu, b3, b3)


def _moe_down_kernel(*refs):
    sched = refs[:8]
    h_ref, w_hbm, bd_ref, y_ref, wbuf, wdb, sem = refs[8:]
    s = pl.program_id(0)
    tn = wdb.shape[1]

    def copies(g):
        col = pl.multiple_of(sched[SCHED_GF][g] * tn, tn)
        return [pltpu.make_async_copy(w_hbm.at[sched[SCHED_GE][g], :, pl.ds(col, tn)], wbuf, sem)]

    def on_arrival():
        wdb[...] = wbuf[...].astype(BF16)

    _stream_weights(s, sched, copies, on_arrival)

    def compute(rows):
        y = jnp.dot(h_ref[:rows, :], wdb[...], preferred_element_type=F32) + bd_ref[...]
        _store_tiled(y_ref, _pack_bf16_pairs(y.astype(BF16).astype(F32)))

    _for_used_rows(s, sched, y_ref, compute)


def moe_down(h, w_down, b_down, sched):
    P, F = h.shape
    E, _, D = w_down.shape
    tn = DOWN_TILE
    assert tn == D, "the packed output pairs column c with column c + D/2"
    b3 = b_down.reshape(E, 1, D)
    return pl.pallas_call(
        _moe_down_kernel,
        grid_spec=pltpu.PrefetchScalarGridSpec(
            num_scalar_prefetch=len(sched),
            grid=(sched[SCHED_SG].shape[0],),
            in_specs=[pl.BlockSpec((ROW_BLOCK, F), lambda s, *r: (r[SCHED_SB][s], 0)),
                      pl.BlockSpec(memory_space=pl.ANY),
                      pl.BlockSpec((None, 1, tn), lambda s, *r: (_step_expert(s, r), 0, _step_tile(s, r)))],
            out_specs=pl.BlockSpec((ROW_BLOCK * ROW_TILES, LANES), lambda s, *r: (r[SCHED_SB][s], 0)),
            scratch_shapes=[pltpu.VMEM((F, tn), F32), pltpu.VMEM((F, tn), BF16), pltpu.SemaphoreType.DMA(())]),
        out_shape=jax.ShapeDtypeStruct((P * ROW_TILES, LANES), jnp.uint32),
        compiler_params=_cparams(("arbitrary",)),
        name="moe_down",
    )(*sched, h, w_down, b3)


def _combine_kernel(dcur_ref, dnxt_ref, x1_ref, w_ref, fw_ref, ys_hbm, o_ref, buf, sems):
    i = pl.program_id(0)
    n_steps = pl.num_programs(0)
    tq = x1_ref.shape[0]
    n = TOP_K * tq
    slot = lax.rem(i, 2)

    def issue(idx_ref, sl):
        def body(q, carry):
            base = pl.multiple_of(q * SUBLANES, SUBLANES)
            for r in range(SUBLANES):
                _row_copy(ys_hbm, buf.at[sl], idx_ref[0, 0, base + r], (base + r) * ROW_TILES,
                          sems.at[sl]).start(priority=r % 2)
            return carry

        lax.fori_loop(0, n // SUBLANES, body, 0)

    def wait_slot(sl):
        pltpu.make_async_copy(ys_hbm.at[pl.ds(0, n * ROW_TILES), :], buf.at[sl], sems.at[sl]).wait()

    pl.when(i == 0)(lambda: issue(dcur_ref, 0))
    wait_slot(slot)
    for t in range(n):
        _row_copy(ys_hbm, buf.at[1 - slot], dnxt_ref[0, 0, t], t * ROW_TILES,
                  sems.at[1 - slot]).start(priority=t % 2)

    half = ROW_TILES * LANES
    acc_lo = x1_ref[:, :half]
    acc_hi = x1_ref[:, half:]
    for j in range(TOP_K):
        lo, hi = _unpack_bf16_pairs(_load_tiled(buf.at[slot], tq, first_row=j * tq))
        acc_lo = acc_lo + w_ref[:, j:j + 1] * lo.astype(F32)
        acc_hi = acc_hi + w_ref[:, j:j + 1] * hi.astype(F32)
    ssq = jnp.sum(acc_lo * acc_lo, axis=-1, keepdims=True) + jnp.sum(acc_hi * acc_hi, axis=-1, keepdims=True)
    scale = lax.rsqrt(ssq * (1.0 / (2 * half)) + NORM_EPS)
    o_ref[:, :half] = acc_lo * scale * fw_ref[:, :half]
    o_ref[:, half:] = acc_hi * scale * fw_ref[:, half:]
    pl.when(i == n_steps - 1)(lambda: wait_slot(1 - slot))


def combine(x1, ys, dest, top_w, final_w):
    T, D = x1.shape
    tq = COMBINE_TOKENS
    nblk = T // tq
    dest_blk = (dest * ROW_TILES).reshape(nblk, tq, TOP_K).transpose(0, 2, 1).reshape(nblk, 1, TOP_K * tq)
    idx_spec = lambda f: pl.BlockSpec((1, 1, TOP_K * tq), f, memory_space=pltpu.SMEM)
    return pl.pallas_call(
        _combine_kernel,
        grid=(nblk,),
        in_specs=[idx_spec(lambda i: (i, 0, 0)),
                  idx_spec(lambda i: (jnp.minimum(i + 1, nblk - 1), 0, 0)),
                  pl.BlockSpec((tq, D), lambda i: (i, 0)),
                  pl.BlockSpec((tq, TOP_K), lambda i: (i, 0)),
                  pl.BlockSpec((1, D), lambda i: (0, 0)),
                  pl.BlockSpec(memory_space=pl.ANY)],
        out_specs=pl.BlockSpec((tq, D), lambda i: (i, 0)),
        out_shape=jax.ShapeDtypeStruct((T, D), F32),
        scratch_shapes=[pltpu.VMEM((2, TOP_K * tq * ROW_TILES, LANES), ys.dtype), pltpu.SemaphoreType.DMA((2,))],
        compiler_params=_cparams(("arbitrary",)),
        name="moe_combine",
    )(dest_blk, dest_blk, x1, top_w, final_w.reshape(1, D), ys)


def _routing(logits):
    T = logits.shape[0]
    TK = T * TOP_K
    NB = TK // ROW_BLOCK + N_EXPERTS
    top_logits, top_idx = lax.top_k(logits[:, :N_EXPERTS], TOP_K)
    top_w = jax.nn.softmax(top_logits, axis=-1)
    experts = jnp.arange(N_EXPERTS, dtype=jnp.int32)
    onehot = (top_idx.reshape(TK, 1) == experts[None, :]).astype(F32)
    oh = onehot.reshape(TK // RANK_GROUP, RANK_GROUP, N_EXPERTS)
    local = jnp.einsum("ts,gse->gte", jnp.tril(jnp.ones((RANK_GROUP, RANK_GROUP), F32)), oh)
    tot = local[:, -1, :]
    offs = jnp.cumsum(tot, axis=0) - tot
    counts = (offs[-1] + tot[-1]).astype(jnp.int32)
    padded = (counts + ROW_BLOCK - 1) // ROW_BLOCK * ROW_BLOCK
    pad_end = jnp.cumsum(padded).astype(jnp.int32)
    pad_start = pad_end - padded
    dest = jnp.sum(oh * (local + (offs + (pad_start.astype(F32) - 1.0)[None, :])[:, None, :]), axis=-1)
    dest = dest.reshape(T, TOP_K).astype(jnp.int32)

    blk = jnp.arange(NB, dtype=jnp.int32)
    block_e = jnp.minimum(jnp.sum((pad_end[None, :] <= (blk * ROW_BLOCK)[:, None]).astype(jnp.int32), axis=1),
                          N_EXPERTS - 1)
    of_block = lambda table: jnp.sum(jnp.where(block_e[:, None] == experts[None, :], table[None, :], 0), axis=1)
    nb_used = pad_end[-1] // ROW_BLOCK
    rows_left = of_block(counts) - (blk - of_block(pad_start // ROW_BLOCK)) * ROW_BLOCK
    block_parts = jnp.where(blk < nb_used, jnp.clip((rows_left + ROW_PART - 1) // ROW_PART, 1, ROW_BLOCK // ROW_PART), 0)
    used = padded > 0
    block_pos = of_block(jnp.cumsum(used.astype(jnp.int32)) - 1)
    tail = nb_used + experts
    zrow = jnp.concatenate([jnp.where(used, pad_end - ROW_BLOCK, -1),
                            jnp.where(tail < NB, tail * ROW_BLOCK, -1)]).astype(jnp.int32)
    return dest, top_w, (block_e, block_parts, block_pos, used, nb_used), zrow


def _schedule(blocks, n_tiles):
    block_e, block_parts, block_pos, used, nb_used = blocks
    nb = block_e.shape[0]
    b = jnp.asarray(np.tile(np.arange(nb, dtype=np.int32), n_tiles))
    f = jnp.asarray(np.repeat(np.arange(n_tiles, dtype=np.int32), nb))
    rep = lambda a: jnp.tile(a, n_tiles)
    n_groups = (jnp.sum(used) * n_tiles).astype(jnp.int32)
    n_valid = (nb_used * n_tiles).astype(jnp.int32)
    group = jnp.minimum(rep(block_pos) * n_tiles + f, n_groups - 1)
    key = jnp.where(b < nb_used, (rep(block_e) * n_tiles + f) * nb + b, (N_EXPERTS * n_tiles + f) * nb + b)
    _, group, sb, so, parts = lax.sort((key, group, b, f, rep(block_parts)), num_keys=1)
    sg = jnp.where(jnp.arange(nb * n_tiles) < n_valid, group, n_groups - 1)
    experts = jnp.arange(N_EXPERTS, dtype=jnp.int32)
    used_first = jnp.argsort(jnp.where(used, experts, N_EXPERTS + experts)).astype(jnp.int32)
    ge = jnp.repeat(used_first, n_tiles)
    gf = jnp.tile(jnp.arange(n_tiles, dtype=jnp.int32), N_EXPERTS)
    return (n_valid.reshape(1), sg, sb, so, parts, ge, gf, n_groups.reshape(1))


def _repack_w_in(w_in):
    w_in = w_in.astype(BF16)
    z = lambda n: jnp.zeros((w_in.shape[0], n), w_in.dtype)
    o = RWKV_COLS
    w_r = jnp.concatenate([w_in[:, :RW_WD], w_in[:, RW_WD:RW_WD + DECAY_LORA], z(LORA_PAD - DECAY_LORA),
                           w_in[:, RW_WD + DECAY_LORA:RW_WD + DECAY_LORA + AAA_LORA], z(LORA_PAD - AAA_LORA),
                           w_in[:, RW_WD + DECAY_LORA + AAA_LORA:o]], axis=1)
    w_m = jnp.concatenate([w_in[:, o:o + ML_IF], w_in[:, o + ML_IF:o + ML_IF + 2 * MLSTM_HEADS],
                           z(LANES - 2 * MLSTM_HEADS), w_in[:, o + ML_IF + 2 * MLSTM_HEADS:o + MLSTM_COLS]], axis=1)
    w_g = w_in[:, o + MLSTM_COLS:]
    return w_r, w_m, w_g


def kernel(x, norm1_w, w_in, b_gate, rwkv_mu, rwkv_w0, rwkv_w_up, rwkv_a0, rwkv_a_up, rwkv_g_up, rwkv_k_k,
           rwkv_k_a, rwkv_r_k, rwkv_ln_w, rwkv_ln_b, mlstm_conv_w, mlstm_i_b, mlstm_f_b, mlstm_norm_w,
           w_branch_a, w_branch_b, w_out, norm2_w, router_w, router_b, w_gu, b_gu, w_down, b_down,
           final_norm_w):
    B, S, D = x.shape
    T = B * S
    xt = x.reshape(T, D)
    assert norm1_w.shape[0] == 1, "single-layer block: the final rmsnorm is fused into the MoE combine"
    for l in range(1):
        w_r, w_m, w_g = _repack_w_in(w_in[l])
        hn, p_m, *scan_in = rwkv_in(x, norm1_w[l], w_r, w_m, rwkv_mu[l], rwkv_w0[l], rwkv_w_up[l], rwkv_a0[l],
                                    rwkv_a_up[l], rwkv_g_up[l], rwkv_k_k[l], rwkv_k_a[l])
        hn = hn.reshape(T, D)
        o_a = rwkv_scan(*scan_in, rwkv_r_k[l].reshape(1, -1), rwkv_ln_w[l].reshape(1, -1),
                        rwkv_ln_b[l].reshape(1, -1))
        o_b = mlstm_branch(p_m, mlstm_conv_w[l], mlstm_i_b[l], mlstm_f_b[l], mlstm_norm_w[l])
        x1, xp, logits = merge_project(o_a.reshape(T, RWKV_DIM), o_b.reshape(T, MLSTM_DIM), hn, w_g, b_gate[l], xt,
                                       w_branch_a[l], w_branch_b[l], w_out[l], norm2_w[l], router_w[l],
                                       router_b[l])
        dest, top_w, blocks, zrow = _routing(logits)
        xs = moe_dispatch(xp, dest, zrow, blocks[0].shape[0] * ROW_BLOCK)
        h = moe_up(xs, w_gu[l], b_gu[l], _schedule(blocks, EXPERT_FF // UP_TILE))
        ys = moe_down(h, w_down[l], b_down[l], _schedule(blocks, D // DOWN_TILE))
        xt = combine(x1, ys, dest, top_w, final_norm_w)
    return xt.reshape(B, S, D)
```

```python
import functools

import jax
import jax.numpy as jnp
import numpy as np
from jax import lax
from jax.experimental import pallas as pl
from jax.experimental.pallas import tpu as pltpu

F32 = jnp.float32
BF16 = jnp.bfloat16

D_MODEL = 2048
CHUNK = 64
NORM_EPS = 1e-6
RWKV_HEADS = 16
RWKV_HEAD_DIM = 64
RWKV_DIM = 1024
DECAY_LORA = 96
AAA_LORA = 96
GATE_LORA = 256
GN_EPS = 64e-5
RWKV_COLS = 3 * RWKV_DIM + DECAY_LORA + AAA_LORA + GATE_LORA
MLSTM_HEADS = 4
MLSTM_QK_DIM = 128
MLSTM_V_DIM = 256
MLSTM_QK = 512
MLSTM_DIM = 1024
CONV_WIDTH = 4
GATE_SOFTCAP = 15.0
MLSTM_COLS = 2 * MLSTM_QK + 2 * MLSTM_DIM + 2 * MLSTM_HEADS
N_EXPERTS = 32
TOP_K = 4
EXPERT_FF = 2048
SWIGLU_LIMIT = 7.0
SWIGLU_ALPHA = 1.702

LANES = 128
SUBLANES = 8
VMEM_LIMIT = 56 * 1024 * 1024

LORA_PAD = 128
RW_WD = 3 * RWKV_DIM
RW_AD = RW_WD + LORA_PAD
RW_GD = RW_AD + LORA_PAD
RW_COLS_P = RW_GD + GATE_LORA
ML_V = 2 * MLSTM_QK
ML_IF = ML_V + MLSTM_DIM
ML_O = ML_IF + LANES
ML_COLS_P = ML_O + MLSTM_DIM
ROUTER_PAD = 128
SCAN_SEQS = 4
MLSTM_SEQS = 1

ROW_BLOCK = 512
ROW_PART = 128


def _cparams(sem):
    return pltpu.CompilerParams(dimension_semantics=sem, vmem_limit_bytes=VMEM_LIMIT)


def _bdot(a, b):
    return jnp.dot(a.astype(BF16), b.astype(BF16), preferred_element_type=F32)


def _split3(x):
    hi = x.astype(BF16)
    r1 = x - hi.astype(F32)
    mid = r1.astype(BF16)
    lo = (r1 - mid.astype(F32)).astype(BF16)
    return hi, mid, lo


def _dot_exact_lhs(mat_bf16, x):
    hi, mid, lo = _split3(x)
    return (jnp.dot(mat_bf16, hi, preferred_element_type=F32)
            + jnp.dot(mat_bf16, mid, preferred_element_type=F32)
            + jnp.dot(mat_bf16, lo, preferred_element_type=F32))


def _dot_exact_rhs(x, mat_bf16):
    hi, mid, lo = _split3(x)
    return (jnp.dot(hi, mat_bf16, preferred_element_type=F32)
            + jnp.dot(mid, mat_bf16, preferred_element_type=F32)
            + jnp.dot(lo, mat_bf16, preferred_element_type=F32))


def _sigmoid(x):
    return 1.0 / (1.0 + jnp.exp(-x))


def _softplus(x):
    return jnp.maximum(x, 0.0) + jnp.log(1.0 + jnp.exp(-jnp.abs(x)))


def _head_sum_mat():
    r = lax.broadcasted_iota(jnp.int32, (LANES, LANES), 0) // RWKV_HEAD_DIM
    c = lax.broadcasted_iota(jnp.int32, (LANES, LANES), 1) // RWKV_HEAD_DIM
    return jnp.where(r == c, 1.0, 0.0).astype(BF16)


def _pad_rows(w, n):
    return jnp.pad(w, ((0, n - w.shape[0]), (0, 0)))


PREP_GROUP = 1024
MLSTM_PROJ_TILE = ML_COLS_P


def _rwkv_in_kernel(x_ref, n1_ref, w_ref, wm_ref, mu_ref, w0_ref, wup_ref, a0_ref, aup_ref, gup_ref, kk_ref,
                    ka_ref, hn_out, pm_out, r_out, k_out, v_out, kk_out, b_out, lw_out, g_out, carry_ref):
    i = pl.program_id(1)

    @pl.when(i == 0)
    def _():
        carry_ref[...] = jnp.zeros_like(carry_ref)

    x = x_ref[...]
    hn = (x * lax.rsqrt(jnp.mean(x * x, axis=-1, keepdims=True) + NORM_EPS) * n1_ref[...]).astype(BF16)
    hn_out[...] = hn
    tq = x.shape[0]
    row = lax.broadcasted_iota(jnp.int32, (tq, 1), 0)

    def shifted(cols):
        p = jnp.dot(hn, w_ref[:, cols], preferred_element_type=F32)
        prev = jnp.where(row == 0, carry_ref[0:1, cols], pltpu.roll(p, 1, 0))
        carry_ref[0:1, cols] = p[tq - 1:tq, :]
        return p + (prev - p) * mu_ref[:, cols]

    lora = shifted(slice(RW_WD, RW_COLS_P))
    wd_t = jnp.tanh(lora[:, 0:LORA_PAD]).astype(BF16)
    ad = lora[:, LORA_PAD:2 * LORA_PAD].astype(BF16)
    gd_s = _sigmoid(lora[:, 2 * LORA_PAD:]).astype(BF16)
    hs = _head_sum_mat()

    for c in range(0, RWKV_DIM, PREP_GROUP):
        cs = slice(c, c + PREP_GROUP)
        r = shifted(cs)
        k = shifted(slice(RWKV_DIM + c, RWKV_DIM + c + PREP_GROUP))
        v = shifted(slice(2 * RWKV_DIM + c, 2 * RWKV_DIM + c + PREP_GROUP))
        w_log = -_softplus(-(w0_ref[:, cs] + jnp.dot(wd_t, wup_ref[:, cs], preferred_element_type=F32))) - 0.5
        a = _sigmoid(a0_ref[:, cs] + jnp.dot(ad, aup_ref[:, cs], preferred_element_type=F32))
        kk = k * kk_ref[:, cs]
        nrm2 = jnp.concatenate(
            [_dot_exact_rhs(kk[:, t:t + LANES] * kk[:, t:t + LANES], hs) for t in range(0, PREP_GROUP, LANES)],
            axis=1)
        kk = kk / jnp.maximum(jnp.sqrt(nrm2), 1e-12)
        r_out[:, cs] = r.astype(r_out.dtype)
        k_out[:, cs] = (k * (1.0 + (a - 1.0) * ka_ref[:, cs])).astype(k_out.dtype)
        v_out[:, cs] = v.astype(v_out.dtype)
        kk_out[:, cs] = kk.astype(kk_out.dtype)
        b_out[:, cs] = (kk * a).astype(b_out.dtype)
        lw_out[:, cs] = -jnp.exp(w_log)
        g_out[:, cs] = jnp.dot(gd_s, gup_ref[:, cs], preferred_element_type=F32).astype(g_out.dtype)

    for c in range(0, ML_COLS_P, MLSTM_PROJ_TILE):
        cs = slice(c, c + MLSTM_PROJ_TILE)
        pm_out[:, cs] = jnp.dot(hn, wm_ref[:, cs], preferred_element_type=F32)


def rwkv_in(x, norm1_w, w_r, w_m, mu, w0, w_up, a0, a_up, g_up, k_k, k_a, tq=256):
    B, S, D = x.shape
    blk = lambda n: pl.BlockSpec((None, tq, n), lambda b, i: (b, i, 0))
    full = lambda a: pl.BlockSpec(a.shape, lambda b, i: (0,) * a.ndim, pipeline_mode=pl.Buffered(1))
    row = lambda t: t.reshape(1, -1)
    mu_p = jnp.concatenate([mu[:RW_WD], jnp.pad(mu[RW_WD:RW_WD + DECAY_LORA], (0, LORA_PAD - DECAY_LORA)),
                            jnp.pad(mu[RW_WD + DECAY_LORA:RW_WD + DECAY_LORA + AAA_LORA], (0, LORA_PAD - AAA_LORA)),
                            mu[RW_WD + DECAY_LORA + AAA_LORA:]])
    params = (row(norm1_w), w_r, w_m, row(mu_p), row(w0), _pad_rows(w_up, LORA_PAD).astype(BF16), row(a0),
              _pad_rows(a_up, LORA_PAD).astype(BF16), g_up.astype(BF16), row(k_k), row(k_a))
    out = lambda n, dt: jax.ShapeDtypeStruct((B, S, n), dt)
    return pl.pallas_call(
        _rwkv_in_kernel,
        grid=(B, S // tq),
        in_specs=[blk(D)] + [full(a) for a in params],
        out_specs=[blk(D), blk(ML_COLS_P)] + [blk(RWKV_DIM)] * 7,
        out_shape=[out(D, BF16), out(ML_COLS_P, F32)] + [out(RWKV_DIM, BF16)] * 5
        + [out(RWKV_DIM, F32), out(RWKV_DIM, BF16)],
        scratch_shapes=[pltpu.VMEM((SUBLANES, RW_COLS_P), F32)],
        compiler_params=_cparams(("parallel", "arbitrary")),
        name="rwkv_in",
    )(x, *params)


def _rwkv_scan_kernel(r_ref, k_ref, v_ref, kk_ref, b_ref, lw_ref, g_ref, rk_ref, lnw_ref, lnb_ref,
                      o_ref, h_ref):
    c = pl.program_id(1)

    @pl.when(c == 0)
    def _():
        h_ref[...] = jnp.zeros_like(h_ref)

    L = CHUNK
    L2 = 2 * L
    ri = lax.broadcasted_iota(jnp.int32, (L, L), 0)
    ci = lax.broadcasted_iota(jnp.int32, (L, L), 1)
    tril = jnp.where(ri >= ci, 1.0, 0.0).astype(BF16)

    n_seq = lw_ref.shape[0]
    cat = lambda ref: jnp.concatenate([ref[i] for i in range(n_seq)], axis=1)
    rep = lambda ref: jnp.concatenate([ref[...]] * n_seq, axis=1)
    lw = cat(lw_ref)
    cum = _dot_exact_lhs(tril, lw)
    cum_end = cum[L - 1:L, :]
    w_in = jnp.exp(cum)
    w_prev = jnp.exp(cum - lw)
    w_inv = jnp.exp(-cum)
    w_tail = jnp.exp(cum_end - cum)
    w_end = jnp.exp(cum_end)

    kk = cat(kk_ref).astype(F32)
    bb = cat(b_ref).astype(F32)
    kx = cat(k_ref).astype(F32)
    rx = cat(r_ref).astype(F32)
    vx = cat(v_ref).astype(F32)
    gx = cat(g_ref).astype(F32)
    a_hat = -kk * w_prev
    r_hat = rx * w_in
    b_hat = bb * w_inv
    k_hat = kx * w_inv
    b_til = bb * w_tail
    k_til = kx * w_tail
    rkk = rx * kx * rep(rk_ref)

    lane = lax.broadcasted_iota(jnp.int32, (1, LANES), 1)
    m_lo = jnp.where(lane < RWKV_HEAD_DIM, 1.0, 0.0)
    m_hi = 1.0 - m_lo

    def stack(x):
        return jnp.concatenate([x * m_lo, x * m_hi], axis=0)

    r2 = lax.broadcasted_iota(jnp.int32, (L2, L2), 0)
    c2 = lax.broadcasted_iota(jnp.int32, (L2, L2), 1)
    same_head = (r2 // L) == (c2 // L)
    strict = same_head & (r2 > c2)
    incl = same_head & (r2 >= c2)
    diag16 = (r2 // 16) == (c2 // 16)
    eye = jnp.where(r2 == c2, 1.0, 0.0)
    hs = _head_sum_mat()

    pairs_per_seq = RWKV_DIM // LANES
    pairs = range(n_seq * pairs_per_seq)
    sls = [slice(p * LANES, (p + 1) * LANES) for p in pairs]
    v_st = [stack(vx[:, sl]) for sl in sls]
    lhs = [jnp.concatenate([stack(a_hat[:, sl]), stack(r_hat[:, sl])], axis=0).astype(BF16) for sl in sls]
    rhs = [jnp.concatenate([stack(b_hat[:, sl]), stack(k_hat[:, sl])], axis=0).astype(BF16) for sl in sls]
    sc = [lax.dot_general(lhs[p], rhs[p], (((1,), (1,)), ((), ())), preferred_element_type=F32) for p in pairs]
    n_ab = [jnp.where(strict, sc[p][:L2, :L2], 0.0) for p in pairs]
    a_ak = [jnp.where(strict, sc[p][:L2, L2:], 0.0).astype(BF16) for p in pairs]
    a_r = [jnp.concatenate([jnp.where(incl, sc[p][L2:, :L2], 0.0), jnp.where(incl, sc[p][L2:, L2:], 0.0)],
                           axis=1).astype(BF16) for p in pairs]

    nd = [jnp.where(diag16, n_ab[p], 0.0) for p in pairs]
    noff = [(n_ab[p] - nd[p]).astype(BF16) for p in pairs]
    ndb = [nd[p].astype(BF16) for p in pairs]
    s2 = [jnp.dot(ndb[p], ndb[p], preferred_element_type=F32).astype(BF16) for p in pairs]
    s4 = [jnp.dot(s2[p], s2[p], preferred_element_type=F32).astype(BF16) for p in pairs]
    s8 = [jnp.dot(s4[p], s4[p], preferred_element_type=F32).astype(BF16) for p in pairs]
    x1 = [eye + nd[p] for p in pairs]
    x2 = [x1[p] + _bdot(x1[p], s2[p]) for p in pairs]
    x3 = [x2[p] + _bdot(x2[p], s4[p]) for p in pairs]
    t_d = [(x3[p] + _bdot(x3[p], s8[p])).astype(BF16) for p in pairs]
    m1 = [jnp.dot(t_d[p], noff[p], preferred_element_type=F32) for p in pairs]
    m1b = [m1[p].astype(BF16) for p in pairs]
    m2 = [jnp.dot(m1b[p], m1b[p], preferred_element_type=F32) for p in pairs]
    m3 = [jnp.dot(m1b[p], m2[p].astype(BF16), preferred_element_type=F32) for p in pairs]
    t_inv = [jnp.dot((eye + m1[p] + m2[p] + m3[p]).astype(BF16), t_d[p], preferred_element_type=F32).astype(BF16)
             for p in pairs]

    h0 = [h_ref[p] for p in pairs]
    ah = [jnp.dot(lhs[p], h0[p].astype(BF16), preferred_element_type=F32) for p in pairs]
    x = [ah[p][:L2] + jnp.dot(a_ak[p], v_st[p].astype(BF16), preferred_element_type=F32) for p in pairs]
    u = [jnp.dot(t_inv[p], x[p].astype(BF16), preferred_element_type=F32) for p in pairs]
    uv = [jnp.concatenate([u[p], v_st[p]], axis=0).astype(BF16) for p in pairs]
    y_st = [ah[p][L2:] + jnp.dot(a_r[p], uv[p], preferred_element_type=F32) for p in pairs]
    y = [y_st[p][:L] + y_st[p][L:] for p in pairs]

    for p in pairs:
        sl = sls[p]
        upd_l = jnp.concatenate([stack(b_til[:, sl]), stack(k_til[:, sl])], axis=0).astype(BF16)
        upd = lax.dot_general(upd_l, uv[p], (((0,), (0,)), ((), ())), preferred_element_type=F32)
        w_col = jnp.sum(eye * w_end[:, sl], axis=1, keepdims=True)
        h_ref[p] = w_col * h0[p] + upd

    def head_sums(vals):
        parts = []
        for t in vals:
            hi = t.astype(BF16)
            parts += [hi, (t - hi.astype(F32)).astype(BF16)]
        res = jnp.dot(jnp.concatenate(parts, axis=0), hs, preferred_element_type=F32)
        return [res[2 * i * L:(2 * i + 1) * L] + res[(2 * i + 1) * L:(2 * i + 2) * L] for i in range(len(vals))]

    sums1 = [head_sums([y[p], rkk[:, sls[p]]]) for p in pairs]
    d = [y[p] - sums1[p][0] * (1.0 / RWKV_HEAD_DIM) for p in pairs]
    var = [head_sums([d[p] * d[p]])[0] * (1.0 / RWKV_HEAD_DIM) for p in pairs]
    for p in pairs:
        sl = sls[p]
        psl = sls[p % pairs_per_seq]
        yn = d[p] * lax.rsqrt(var[p] + GN_EPS) * lnw_ref[:, psl] + lnb_ref[:, psl]
        o_ref[p // pairs_per_seq, :, psl] = ((yn + sums1[p][1] * vx[:, sl]) * gx[:, sl]).astype(o_ref.dtype)


def rwkv_scan(r, k, v, kk, b, lw, g, r_k, ln_w, ln_b):
    B, S, C = r.shape
    blk = pl.BlockSpec((SCAN_SEQS, CHUNK, C), lambda bb, c: (bb, c, 0))
    full = pl.BlockSpec((1, C), lambda bb, c: (0, 0))
    return pl.pallas_call(
        _rwkv_scan_kernel,
        grid=(B // SCAN_SEQS, S // CHUNK),
        in_specs=[blk] * 7 + [full] * 3,
        out_specs=blk,
        out_shape=jax.ShapeDtypeStruct((B, S, C), BF16),
        scratch_shapes=[pltpu.VMEM((SCAN_SEQS * C // LANES, LANES, LANES), F32)],
        compiler_params=_cparams(("parallel", "arbitrary")),
        name="rwkv_scan",
    )(r, k, v, kk, b, lw, g, r_k, ln_w, ln_b)


def _mlstm_kernel(p_ref, convw_ref, ifb_ref, nw_ref, o_ref, carry_ref, c_ref, n_ref, m_ref):
    ci = pl.program_id(1)

    @pl.when(ci == 0)
    def _():
        carry_ref[...] = jnp.zeros_like(carry_ref)
        c_ref[...] = jnp.zeros_like(c_ref)
        n_ref[...] = jnp.zeros_like(n_ref)
        m_ref[...] = jnp.zeros_like(m_ref)

    L = CHUNK
    n_seq = p_ref.shape[0]
    lane = lax.broadcasted_iota(jnp.int32, (1, LANES), 1)
    ri = lax.broadcasted_iota(jnp.int32, (L, L), 0)
    cj = lax.broadcasted_iota(jnp.int32, (L, L), 1)
    causal = ri >= cj
    tril = jnp.where(causal, 1.0, 0.0).astype(BF16)
    triu = jnp.where(ri <= cj, 1.0, 0.0).astype(BF16)

    qk, comb, comb_t, b_col, b_row = [], [], [], [], []
    for i in range(n_seq):
        u = p_ref[i, :, 0:ML_V]
        ext = jnp.concatenate([carry_ref[i], u], axis=0)
        carry_ref[i] = u[L - SUBLANES:L, :]
        conv = convw_ref[CONV_WIDTH - 1:CONV_WIDTH, :] * u
        for j in range(1, CONV_WIDTH):
            conv = conv + convw_ref[CONV_WIDTH - 1 - j:CONV_WIDTH - j, :] * pltpu.roll(ext, j, 0)[SUBLANES:, :]
        qk.append(conv * _sigmoid(conv))
        pre = GATE_SOFTCAP * jnp.tanh((p_ref[i, :, ML_IF:ML_O] + ifb_ref[...]) * (1.0 / GATE_SOFTCAP))
        cb = jnp.where(lane < MLSTM_HEADS, pre, -_softplus(-pre))
        comb.append(cb)
        comb_t.append(cb.T)
        b_col.append(_dot_exact_lhs(tril, cb))
        b_row.append(_dot_exact_rhs(comb_t[i], triu))

    H = range(n_seq * MLSTM_HEADS)
    sq = [v // MLSTM_HEADS for v in H]
    hd = [v % MLSTM_HEADS for v in H]
    dk, dv = MLSTM_QK_DIM, MLSTM_V_DIM
    qh = [qk[sq[v]][:, hd[v] * dk:(hd[v] + 1) * dk] * (dk ** -0.5) for v in H]
    kh = [qk[sq[v]][:, MLSTM_QK + hd[v] * dk:MLSTM_QK + (hd[v] + 1) * dk] for v in H]
    vh = [p_ref[sq[v], :, ML_V + hd[v] * dv:ML_V + (hd[v] + 1) * dv].astype(BF16) for v in H]
    qb = [qh[h].astype(BF16) for h in H]
    bcol = [b_col[sq[v]][:, MLSTM_HEADS + hd[v]:MLSTM_HEADS + hd[v] + 1] for v in H]
    brow = [b_row[sq[v]][MLSTM_HEADS + hd[v]:MLSTM_HEADS + hd[v] + 1, :] for v in H]
    m_prev = [m_ref[h][0:1, 0:1] for h in H]
    n_prev = [n_ref[h][0:1, :] for h in H]
    c_prev = [c_ref[h] for h in H]

    qk_t = [lax.dot_general(qb[h], kh[h].astype(BF16), (((1,), (1,)), ((), ())), preferred_element_type=F32)
            for h in H]
    qc = [jnp.dot(qb[h], c_prev[h].astype(BF16), preferred_element_type=F32) for h in H]
    dm = [jnp.where(causal, bcol[h] - brow[h] + comb_t[sq[h]][hd[h]:hd[h] + 1, :], -jnp.inf) for h in H]
    inter = [bcol[h] + m_prev[h] for h in H]
    m_t = [jnp.maximum(inter[h], jnp.max(dm[h], axis=-1, keepdims=True)) for h in H]
    s = [qk_t[h] * jnp.exp(dm[h] - m_t[h]) for h in H]
    w_inter = [jnp.exp(inter[h] - m_t[h]) for h in H]
    num = [jnp.dot(s[h].astype(BF16), vh[h], preferred_element_type=F32) + w_inter[h] * qc[h] for h in H]
    den = [jnp.sum(s[h], axis=-1, keepdims=True) + w_inter[h] * jnp.sum(qh[h] * n_prev[h], axis=-1, keepdims=True)
           for h in H]
    hh = [num[h] / jnp.maximum(jnp.abs(den[h]), jnp.exp(-m_t[h])) for h in H]

    g_tot = [bcol[h][L - 1:L, :] for h in H]
    a = [comb[sq[h]][:, hd[h]:hd[h] + 1] + g_tot[h] - bcol[h] for h in H]
    m_new = [jnp.maximum(g_tot[h] + m_prev[h], jnp.max(a[h], axis=0, keepdims=True)) for h in H]
    dec = [jnp.exp(g_tot[h] + m_prev[h] - m_new[h]) for h in H]
    wkk = [jnp.exp(a[h] - m_new[h]) * kh[h] for h in H]
    for h in H:
        c_ref[h] = dec[h] * c_prev[h] + lax.dot_general(wkk[h].astype(BF16), vh[h], (((0,), (0,)), ((), ())),
                                                        preferred_element_type=F32)
        n_ref[h] = jnp.broadcast_to(dec[h] * n_prev[h] + jnp.sum(wkk[h], axis=0, keepdims=True),
                                    (SUBLANES, LANES))
        m_ref[h] = jnp.broadcast_to(m_new[h], (SUBLANES, LANES))
    for h in H:
        vs = slice(hd[h] * dv, (hd[h] + 1) * dv)
        hn = hh[h] * lax.rsqrt(jnp.mean(hh[h] * hh[h], axis=-1, keepdims=True) + NORM_EPS)
        o_raw = p_ref[sq[h], :, ML_O + hd[h] * dv:ML_O + (hd[h] + 1) * dv]
        o_ref[sq[h], :, vs] = (hn * nw_ref[:, vs] * _sigmoid(o_raw)).astype(o_ref.dtype)


def mlstm_branch(p_pad, conv_w, i_b, f_b, norm_w):
    B, S, C = p_pad.shape
    ifb = jnp.pad(jnp.concatenate([i_b, f_b]), (0, LANES - 2 * MLSTM_HEADS)).reshape(1, LANES)
    full = lambda a: pl.BlockSpec(a.shape, lambda b, c: (0,) * a.ndim)
    nw = norm_w.reshape(1, MLSTM_DIM)
    return pl.pallas_call(
        _mlstm_kernel,
        grid=(B // MLSTM_SEQS, S // CHUNK),
        in_specs=[pl.BlockSpec((MLSTM_SEQS, CHUNK, C), lambda b, c: (b, c, 0)), full(conv_w), full(ifb), full(nw)],
        out_specs=pl.BlockSpec((MLSTM_SEQS, CHUNK, MLSTM_DIM), lambda b, c: (b, c, 0)),
        out_shape=jax.ShapeDtypeStruct((B, S, MLSTM_DIM), BF16),
        scratch_shapes=[pltpu.VMEM((MLSTM_SEQS, SUBLANES, ML_V), F32),
                        pltpu.VMEM((MLSTM_SEQS * MLSTM_HEADS, MLSTM_QK_DIM, MLSTM_V_DIM), F32),
                        pltpu.VMEM((MLSTM_SEQS * MLSTM_HEADS, SUBLANES, LANES), F32),
                        pltpu.VMEM((MLSTM_SEQS * MLSTM_HEADS, SUBLANES, LANES), F32)],
        compiler_params=_cparams(("parallel", "arbitrary")),
        name="mlstm_scan",
    )(p_pad, conv_w, ifb, nw)


HI_MASK = 0xFFFF0000


def _pack_bf16_pairs(hb_f32):
    c = hb_f32.shape[1] // 2
    u = pltpu.bitcast(hb_f32, jnp.uint32)
    return u[:, c:] | (u[:, :c] >> 16)


ROW_TILES = D_MODEL // 2 // LANES


def _store_tiled(ref, val, first_row=0):
    n = val.shape[0]
    for t in range(ROW_TILES):
        ref[pl.ds(first_row * ROW_TILES + t, n, stride=ROW_TILES), :] = val[:, t * LANES:(t + 1) * LANES]


def _load_tiled(ref, n, first_row=0):
    return jnp.concatenate([ref[pl.ds(first_row * ROW_TILES + t, n, stride=ROW_TILES), :] for t in range(ROW_TILES)],
                           axis=1)


def _unpack_bf16_pairs(xu):
    lo = pltpu.bitcast(xu << 16, F32).astype(BF16)
    hi = pltpu.bitcast(xu & jnp.uint32(HI_MASK), F32).astype(BF16)
    return lo, hi


def _merge_kernel(oa_ref, ob_ref, hn_ref, bg_ref, x_ref, wa_ref, wb_ref, wg_ref, wo_ref, n2_ref, rwh_ref, rwl_ref,
                  rb_ref, x1_ref, xp_ref, lg_ref):
    hn1 = hn_ref[...]

    def gated(o_ref, w_ref, cols):
        gate = _sigmoid(jnp.dot(hn1, wg_ref[:, cols], preferred_element_type=F32) + bg_ref[:, cols])
        return gate * jnp.dot(o_ref[...], w_ref[...], preferred_element_type=F32)

    merged = gated(oa_ref, wa_ref, slice(0, D_MODEL)) + gated(ob_ref, wb_ref, slice(D_MODEL, 2 * D_MODEL))
    x1 = x_ref[...] + jnp.dot(merged.astype(BF16), wo_ref[...], preferred_element_type=F32)
    x1_ref[...] = x1
    hn = x1 * lax.rsqrt(jnp.mean(x1 * x1, axis=-1, keepdims=True) + NORM_EPS) * n2_ref[...]
    hi = hn.astype(BF16)
    _store_tiled(xp_ref, _pack_bf16_pairs(hi.astype(F32)))
    lo = (hn - hi.astype(F32)).astype(BF16)
    lg_ref[...] = (jnp.dot(hi, rwh_ref[...], preferred_element_type=F32)
                   + jnp.dot(hi, rwl_ref[...], preferred_element_type=F32)
                   + jnp.dot(lo, rwh_ref[...], preferred_element_type=F32) + rb_ref[...])


def merge_project(o_a, o_b, hn, w_gate, b_gate, x, w_a, w_b, w_out, norm2_w, router_w, router_b, tm=256):
    T, D = x.shape
    rw = jnp.pad(router_w, ((0, 0), (0, ROUTER_PAD - N_EXPERTS)))
    rw_hi = rw.astype(BF16)
    rw_lo = (rw - rw_hi.astype(F32)).astype(BF16)
    rb = jnp.pad(router_b, (0, ROUTER_PAD - N_EXPERTS)).reshape(1, ROUTER_PAD)
    rows = lambda n: pl.BlockSpec((tm, n), lambda i: (i, 0))
    full = lambda a: pl.BlockSpec(a.shape, lambda i: (0, 0), pipeline_mode=pl.Buffered(1))
    params = (w_a.astype(BF16), w_b.astype(BF16), w_gate, w_out.astype(BF16), norm2_w.reshape(1, D), rw_hi, rw_lo,
              rb)
    bg = b_gate.reshape(1, 2 * D)
    return pl.pallas_call(
        _merge_kernel,
        grid=(T // tm,),
        in_specs=[rows(o_a.shape[1]), rows(o_b.shape[1]), rows(D), full(bg), rows(D)] + [full(a) for a in params],
        out_specs=[rows(D), pl.BlockSpec((tm * ROW_TILES, LANES), lambda i: (i, 0)), rows(ROUTER_PAD)],
        out_shape=[jax.ShapeDtypeStruct((T, D), F32), jax.ShapeDtypeStruct((T * ROW_TILES, LANES), jnp.uint32),
                   jax.ShapeDtypeStruct((T, ROUTER_PAD), F32)],
        compiler_params=_cparams(("parallel",)),
        name="merge_project",
    )(o_a, o_b, hn, bg, x, *params)


UP_TILE = 1024
DOWN_TILE = 2048
RANK_GROUP = 256
DISPATCH_TOKENS = 256
COMBINE_TOKENS = 128


def _row_copy(src, dst, src_at, dst_at, sem):
    return pltpu.make_async_copy(src.at[pl.ds(pl.multiple_of(src_at, ROW_TILES), ROW_TILES), :],
                                 dst.at[pl.ds(pl.multiple_of(dst_at, ROW_TILES), ROW_TILES), :], sem)


def _dispatch_kernel(zrow_ref, dest_ref, xp_ref, xs_hbm, zbuf, zsem, sem):
    i = pl.program_id(0)
    tq = dest_ref.shape[2] // TOP_K

    def zero_copy(k):
        at = pl.multiple_of(zrow_ref[k], ROW_BLOCK * ROW_TILES)
        return pltpu.make_async_copy(zbuf, xs_hbm.at[pl.ds(at, ROW_BLOCK * ROW_TILES), :], zsem)

    @pl.when(i == 0)
    def _():
        zbuf[...] = jnp.zeros_like(zbuf)
        for k in range(zrow_ref.shape[0]):
            pl.when(zrow_ref[k] >= 0)(lambda k=k: zero_copy(k).start())
        for k in range(zrow_ref.shape[0]):
            pl.when(zrow_ref[k] >= 0)(lambda k=k: zero_copy(k).wait())

    def body(q, carry):
        base = pl.multiple_of(q * SUBLANES, SUBLANES)
        for r in range(SUBLANES):
            for j in range(TOP_K):
                _row_copy(xp_ref, xs_hbm, (base + r) * ROW_TILES, dest_ref[0, 0, TOP_K * (base + r) + j],
                          sem).start(priority=j % 2)
        return carry

    lax.fori_loop(0, tq // SUBLANES, body, 0)
    for j in range(TOP_K):
        pltpu.make_async_copy(xp_ref, xs_hbm.at[pl.ds(0, tq * ROW_TILES), :], sem).wait()


def moe_dispatch(xp, dest, zrow, n_rows):
    T = xp.shape[0] // ROW_TILES
    tq = DISPATCH_TOKENS
    dest_blk = (dest * ROW_TILES).reshape(T // tq, 1, TOP_K * tq)
    zrow = jnp.where(zrow >= 0, zrow * ROW_TILES, -1)
    return pl.pallas_call(
        _dispatch_kernel,
        grid_spec=pltpu.PrefetchScalarGridSpec(
            num_scalar_prefetch=1,
            grid=(T // tq,),
            in_specs=[pl.BlockSpec((1, 1, TOP_K * tq), lambda i, zr: (i, 0, 0), memory_space=pltpu.SMEM),
                      pl.BlockSpec((tq * ROW_TILES, LANES), lambda i, zr: (i, 0))],
            out_specs=pl.BlockSpec(memory_space=pl.ANY),
            scratch_shapes=[pltpu.VMEM((ROW_BLOCK * ROW_TILES, LANES), xp.dtype), pltpu.SemaphoreType.DMA(()),
                            pltpu.SemaphoreType.DMA(())]),
        out_shape=jax.ShapeDtypeStruct((n_rows * ROW_TILES, LANES), xp.dtype),
        compiler_params=_cparams(("arbitrary",)),
        name="moe_dispatch",
    )(zrow, dest_blk, xp)


SCHED_NV, SCHED_SG, SCHED_SB, SCHED_SO, SCHED_PARTS, SCHED_GE, SCHED_GF, SCHED_NG = range(8)


def _stream_weights(s, sched, copies, on_arrival):
    sg_ref = sched[SCHED_SG]
    g = sg_ref[s]
    first = (s < sched[SCHED_NV][0]) & ((s == 0) | (g != sg_ref[jnp.maximum(s - 1, 0)]))

    @pl.when(first)
    def _():
        @pl.when(s == 0)
        def _():
            for c in copies(g):
                c.start()

        for c in copies(g):
            c.wait()
        on_arrival()

        @pl.when(g + 1 < sched[SCHED_NG][0])
        def _():
            for c in copies(g + 1):
                c.start()


def _for_used_rows(s, sched, out_ref, compute):
    parts = sched[SCHED_PARTS][s]
    rows_per_row = out_ref.shape[0] // ROW_BLOCK

    @pl.when(s >= sched[SCHED_NV][0])
    def _():
        out_ref[...] = jnp.zeros_like(out_ref)

    def variant(rows):
        compute(rows)
        if rows < ROW_BLOCK:
            rest = rows * rows_per_row
            out_ref[rest:, :] = jnp.zeros((out_ref.shape[0] - rest, out_ref.shape[1]), out_ref.dtype)

    for p in range(1, ROW_BLOCK // ROW_PART + 1):
        pl.when((s < sched[SCHED_NV][0]) & (parts == p))(functools.partial(variant, p * ROW_PART))


def _moe_up_kernel(*refs):
    sched = refs[:8]
    xs_ref, w_hbm, bg_ref, bu_ref, h_ref, wbuf, wgb, wub, sems = refs[8:]
    s = pl.program_id(0)
    tf = wgb.shape[1]
    n_ff = w_hbm.shape[2] // 2

    def copies(g):
        e = sched[SCHED_GE][g]
        col = pl.multiple_of(sched[SCHED_GF][g] * tf, tf)
        return [pltpu.make_async_copy(w_hbm.at[e, :, pl.ds(half * n_ff + col, tf)], wbuf.at[half], sems.at[half])
                for half in range(2)]

    def on_arrival():
        wgb[...] = wbuf[0].astype(BF16)
        wub[...] = wbuf[1].astype(BF16)

    _stream_weights(s, sched, copies, on_arrival)

    def compute(rows):
        lo, hi = _unpack_bf16_pairs(_load_tiled(xs_ref, rows))
        half = lo.shape[1]

        def proj(wb, b_ref):
            return (jnp.dot(lo, wb[:half, :], preferred_element_type=F32)
                    + jnp.dot(hi, wb[half:, :], preferred_element_type=F32) + b_ref[...])

        gate = jnp.minimum(proj(wgb, bg_ref), SWIGLU_LIMIT)
        up = jnp.clip(proj(wub, bu_ref), -SWIGLU_LIMIT, SWIGLU_LIMIT)
        h_ref[:rows, :] = ((up + 1.0) * gate * _sigmoid(SWIGLU_ALPHA * gate)).astype(h_ref.dtype)

    _for_used_rows(s, sched, h_ref, compute)


def _step_expert(s, r):
    return r[SCHED_GE][r[SCHED_SG][s]]


def _step_tile(s, r):
    return r[SCHED_GF][r[SCHED_SG][s]]


def moe_up(xs, w_gu, b_gu, sched):
    P = xs.shape[0] // ROW_TILES
    E, D, F2 = w_gu.shape
    F = F2 // 2
    tf = UP_TILE
    nf = F // tf
    b3 = b_gu.reshape(E, 1, F2)
    return pl.pallas_call(
        _moe_up_kernel,
        grid_spec=pltpu.PrefetchScalarGridSpec(
            num_scalar_prefetch=len(sched),
            grid=(sched[SCHED_SG].shape[0],),
            in_specs=[pl.BlockSpec((ROW_BLOCK * ROW_TILES, LANES), lambda s, *r: (r[SCHED_SB][s], 0)),
                      pl.BlockSpec(memory_space=pl.ANY),
                      pl.BlockSpec((None, 1, tf), lambda s, *r: (_step_expert(s, r), 0, _step_tile(s, r))),
                      pl.BlockSpec((None, 1, tf), lambda s, *r: (_step_expert(s, r), 0, nf + _step_tile(s, r)))],
            out_specs=pl.BlockSpec((ROW_BLOCK, tf), lambda s, *r: (r[SCHED_SB][s], r[SCHED_SO][s])),
            scratch_shapes=[pltpu.VMEM((2, D, tf), F32), pltpu.VMEM((D, tf), BF16), pltpu.VMEM((D, tf), BF16),
                            pltpu.SemaphoreType.DMA((2,))]),
        out_shape=jax.ShapeDtypeStruct((P, F), BF16),
        compiler_params=_cparams(("arbitrary",)),
        name="moe_up",
    )(*sched, xs, w_gu, b3, b3)


def _moe_down_kernel(*refs):
    sched = refs[:8]
    h_ref, w_hbm, bd_ref, y_ref, wbuf, wdb, sem = refs[8:]
    s = pl.program_id(0)
    tn = wdb.shape[1]

    def copies(g):
        col = pl.multiple_of(sched[SCHED_GF][g] * tn, tn)
        return [pltpu.make_async_copy(w_hbm.at[sched[SCHED_GE][g], :, pl.ds(col, tn)], wbuf, sem)]

    def on_arrival():
        wdb[...] = wbuf[...].astype(BF16)

    _stream_weights(s, sched, copies, on_arrival)

    def compute(rows):
        y = jnp.dot(h_ref[:rows, :], wdb[...], preferred_element_type=F32) + bd_ref[...]
        _store_tiled(y_ref, _pack_bf16_pairs(y.astype(BF16).astype(F32)))

    _for_used_rows(s, sched, y_ref, compute)


def moe_down(h, w_down, b_down, sched):
    P, F = h.shape
    E, _, D = w_down.shape
    tn = DOWN_TILE
    assert tn == D, "the packed output pairs column c with column c + D/2"
    b3 = b_down.reshape(E, 1, D)
    return pl.pallas_call(
        _moe_down_kernel,
        grid_spec=pltpu.PrefetchScalarGridSpec(
            num_scalar_prefetch=len(sched),
            grid=(sched[SCHED_SG].shape[0],),
            in_specs=[pl.BlockSpec((ROW_BLOCK, F), lambda s, *r: (r[SCHED_SB][s], 0)),
                      pl.BlockSpec(memory_space=pl.ANY),
                      pl.BlockSpec((None, 1, tn), lambda s, *r: (_step_expert(s, r), 0, _step_tile(s, r)))],
            out_specs=pl.BlockSpec((ROW_BLOCK * ROW_TILES, LANES), lambda s, *r: (r[SCHED_SB][s], 0)),
            scratch_shapes=[pltpu.VMEM((F, tn), F32), pltpu.VMEM((F, tn), BF16), pltpu.SemaphoreType.DMA(())]),
        out_shape=jax.ShapeDtypeStruct((P * ROW_TILES, LANES), jnp.uint32),
        compiler_params=_cparams(("arbitrary",)),
        name="moe_down",
    )(*sched, h, w_down, b3)


def _combine_kernel(dcur_ref, dnxt_ref, x1_ref, w_ref, fw_ref, ys_hbm, o_ref, buf, sems):
    i = pl.program_id(0)
    n_steps = pl.num_programs(0)
    tq = x1_ref.shape[0]
    n = TOP_K * tq
    slot = lax.rem(i, 2)

    def issue(idx_ref, sl):
        def body(q, carry):
            base = pl.multiple_of(q * SUBLANES, SUBLANES)
            for r in range(SUBLANES):
                _row_copy(ys_hbm, buf.at[sl], idx_ref[0, 0, base + r], (base + r) * ROW_TILES,
                          sems.at[sl]).start(priority=r % 2)
            return carry

        lax.fori_loop(0, n // SUBLANES, body, 0)

    def wait_slot(sl):
        pltpu.make_async_copy(ys_hbm.at[pl.ds(0, n * ROW_TILES), :], buf.at[sl], sems.at[sl]).wait()

    pl.when(i == 0)(lambda: issue(dcur_ref, 0))
    wait_slot(slot)
    for t in range(n):
        _row_copy(ys_hbm, buf.at[1 - slot], dnxt_ref[0, 0, t], t * ROW_TILES,
                  sems.at[1 - slot]).start(priority=t % 2)

    half = ROW_TILES * LANES
    acc_lo = x1_ref[:, :half]
    acc_hi = x1_ref[:, half:]
    for j in range(TOP_K):
        lo, hi = _unpack_bf16_pairs(_load_tiled(buf.at[slot], tq, first_row=j * tq))
        acc_lo = acc_lo + w_ref[:, j:j + 1] * lo.astype(F32)
        acc_hi = acc_hi + w_ref[:, j:j + 1] * hi.astype(F32)
    ssq = jnp.sum(acc_lo * acc_lo, axis=-1, keepdims=True) + jnp.sum(acc_hi * acc_hi, axis=-1, keepdims=True)
    scale = lax.rsqrt(ssq * (1.0 / (2 * half)) + NORM_EPS)
    o_ref[:, :half] = acc_lo * scale * fw_ref[:, :half]
    o_ref[:, half:] = acc_hi * scale * fw_ref[:, half:]
    pl.when(i == n_steps - 1)(lambda: wait_slot(1 - slot))


def combine(x1, ys, dest, top_w, final_w):
    T, D = x1.shape
    tq = COMBINE_TOKENS
    nblk = T // tq
    dest_blk = (dest * ROW_TILES).reshape(nblk, tq, TOP_K).transpose(0, 2, 1).reshape(nblk, 1, TOP_K * tq)
    idx_spec = lambda f: pl.BlockSpec((1, 1, TOP_K * tq), f, memory_space=pltpu.SMEM)
    return pl.pallas_call(
        _combine_kernel,
        grid=(nblk,),
        in_specs=[idx_spec(lambda i: (i, 0, 0)),
                  idx_spec(lambda i: (jnp.minimum(i + 1, nblk - 1), 0, 0)),
                  pl.BlockSpec((tq, D), lambda i: (i, 0)),
                  pl.BlockSpec((tq, TOP_K), lambda i: (i, 0)),
                  pl.BlockSpec((1, D), lambda i: (0, 0)),
                  pl.BlockSpec(memory_space=pl.ANY)],
        out_specs=pl.BlockSpec((tq, D), lambda i: (i, 0)),
        out_shape=jax.ShapeDtypeStruct((T, D), F32),
        scratch_shapes=[pltpu.VMEM((2, TOP_K * tq * ROW_TILES, LANES), ys.dtype), pltpu.SemaphoreType.DMA((2,))],
        compiler_params=_cparams(("arbitrary",)),
        name="moe_combine",
    )(dest_blk, dest_blk, x1, top_w, final_w.reshape(1, D), ys)


def _routing(logits):
    T = logits.shape[0]
    TK = T * TOP_K
    NB = TK // ROW_BLOCK + N_EXPERTS
    top_logits, top_idx = lax.top_k(logits[:, :N_EXPERTS], TOP_K)
    top_w = jax.nn.softmax(top_logits, axis=-1)
    experts = jnp.arange(N_EXPERTS, dtype=jnp.int32)
    onehot = (top_idx.reshape(TK, 1) == experts[None, :]).astype(F32)
    oh = onehot.reshape(TK // RANK_GROUP, RANK_GROUP, N_EXPERTS)
    local = jnp.einsum("ts,gse->gte", jnp.tril(jnp.ones((RANK_GROUP, RANK_GROUP), F32)), oh)
    tot = local[:, -1, :]
    offs = jnp.cumsum(tot, axis=0) - tot
    counts = (offs[-1] + tot[-1]).astype(jnp.int32)
    padded = (counts + ROW_BLOCK - 1) // ROW_BLOCK * ROW_BLOCK
    pad_end = jnp.cumsum(padded).astype(jnp.int32)
    pad_start = pad_end - padded
    dest = jnp.sum(oh * (local + (offs + (pad_start.astype(F32) - 1.0)[None, :])[:, None, :]), axis=-1)
    dest = dest.reshape(T, TOP_K).astype(jnp.int32)

    blk = jnp.arange(NB, dtype=jnp.int32)
    block_e = jnp.minimum(jnp.sum((pad_end[None, :] <= (blk * ROW_BLOCK)[:, None]).astype(jnp.int32), axis=1),
                          N_EXPERTS - 1)
    of_block = lambda table: jnp.sum(jnp.where(block_e[:, None] == experts[None, :], table[None, :], 0), axis=1)
    nb_used = pad_end[-1] // ROW_BLOCK
    rows_left = of_block(counts) - (blk - of_block(pad_start // ROW_BLOCK)) * ROW_BLOCK
    block_parts = jnp.where(blk < nb_used, jnp.clip((rows_left + ROW_PART - 1) // ROW_PART, 1, ROW_BLOCK // ROW_PART), 0)
    used = padded > 0
    block_pos = of_block(jnp.cumsum(used.astype(jnp.int32)) - 1)
    tail = nb_used + experts
    zrow = jnp.concatenate([jnp.where(used, pad_end - ROW_BLOCK, -1),
                            jnp.where(tail < NB, tail * ROW_BLOCK, -1)]).astype(jnp.int32)
    return dest, top_w, (block_e, block_parts, block_pos, used, nb_used), zrow


def _schedule(blocks, n_tiles):
    block_e, block_parts, block_pos, used, nb_used = blocks
    nb = block_e.shape[0]
    b = jnp.asarray(np.tile(np.arange(nb, dtype=np.int32), n_tiles))
    f = jnp.asarray(np.repeat(np.arange(n_tiles, dtype=np.int32), nb))
    rep = lambda a: jnp.tile(a, n_tiles)
    n_groups = (jnp.sum(used) * n_tiles).astype(jnp.int32)
    n_valid = (nb_used * n_tiles).astype(jnp.int32)
    group = jnp.minimum(rep(block_pos) * n_tiles + f, n_groups - 1)
    key = jnp.where(b < nb_used, (rep(block_e) * n_tiles + f) * nb + b, (N_EXPERTS * n_tiles + f) * nb + b)
    _, group, sb, so, parts = lax.sort((key, group, b, f, rep(block_parts)), num_keys=1)
    sg = jnp.where(jnp.arange(nb * n_tiles) < n_valid, group, n_groups - 1)
    experts = jnp.arange(N_EXPERTS, dtype=jnp.int32)
    used_first = jnp.argsort(jnp.where(used, experts, N_EXPERTS + experts)).astype(jnp.int32)
    ge = jnp.repeat(used_first, n_tiles)
    gf = jnp.tile(jnp.arange(n_tiles, dtype=jnp.int32), N_EXPERTS)
    return (n_valid.reshape(1), sg, sb, so, parts, ge, gf, n_groups.reshape(1))


def _repack_w_in(w_in):
    w_in = w_in.astype(BF16)
    z = lambda n: jnp.zeros((w_in.shape[0], n), w_in.dtype)
    o = RWKV_COLS
    w_r = jnp.concatenate([w_in[:, :RW_WD], w_in[:, RW_WD:RW_WD + DECAY_LORA], z(LORA_PAD - DECAY_LORA),
                           w_in[:, RW_WD + DECAY_LORA:RW_WD + DECAY_LORA + AAA_LORA], z(LORA_PAD - AAA_LORA),
                           w_in[:, RW_WD + DECAY_LORA + AAA_LORA:o]], axis=1)
    w_m = jnp.concatenate([w_in[:, o:o + ML_IF], w_in[:, o + ML_IF:o + ML_IF + 2 * MLSTM_HEADS],
                           z(LANES - 2 * MLSTM_HEADS), w_in[:, o + ML_IF + 2 * MLSTM_HEADS:o + MLSTM_COLS]], axis=1)
    w_g = w_in[:, o + MLSTM_COLS:]
    return w_r, w_m, w_g


def kernel(x, norm1_w, w_in, b_gate, rwkv_mu, rwkv_w0, rwkv_w_up, rwkv_a0, rwkv_a_up, rwkv_g_up, rwkv_k_k,
           rwkv_k_a, rwkv_r_k, rwkv_ln_w, rwkv_ln_b, mlstm_conv_w, mlstm_i_b, mlstm_f_b, mlstm_norm_w,
           w_branch_a, w_branch_b, w_out, norm2_w, router_w, router_b, w_gu, b_gu, w_down, b_down,
           final_norm_w):
    B, S, D = x.shape
    T = B * S
    xt = x.reshape(T, D)
    assert norm1_w.shape[0] == 1, "single-layer block: the final rmsnorm is fused into the MoE combine"
    for l in range(1):
        w_r, w_m, w_g = _repack_w_in(w_in[l])
        hn, p_m, *scan_in = rwkv_in(x, norm1_w[l], w_r, w_m, rwkv_mu[l], rwkv_w0[l], rwkv_w_up[l], rwkv_a0[l],
                                    rwkv_a_up[l], rwkv_g_up[l], rwkv_k_k[l], rwkv_k_a[l])
        hn = hn.reshape(T, D)
        o_a = rwkv_scan(*scan_in, rwkv_r_k[l].reshape(1, -1), rwkv_ln_w[l].reshape(1, -1),
                        rwkv_ln_b[l].reshape(1, -1))
        o_b = mlstm_branch(p_m, mlstm_conv_w[l], mlstm_i_b[l], mlstm_f_b[l], mlstm_norm_w[l])
        x1, xp, logits = merge_project(o_a.reshape(T, RWKV_DIM), o_b.reshape(T, MLSTM_DIM), hn, w_g, b_gate[l], xt,
                                       w_branch_a[l], w_branch_b[l], w_out[l], norm2_w[l], router_w[l],
                                       router_b[l])
        dest, top_w, blocks, zrow = _routing(logits)
        xs = moe_dispatch(xp, dest, zrow, blocks[0].shape[0] * ROW_BLOCK)
        h = moe_up(xs, w_gu[l], b_gu[l], _schedule(blocks, EXPERT_FF // UP_TILE))
        ys = moe_down(h, w_down[l], b_down[l], _schedule(blocks, D // DOWN_TILE))
        xt = combine(x1, ys, dest, top_w, final_norm_w)
    return xt.reshape(B, S, D)
```

```python
import functools

import jax
import jax.numpy as jnp
import numpy as np
from jax import lax
from jax.experimental import pallas as pl
from jax.experimental.pallas import tpu as pltpu

F32 = jnp.float32
BF16 = jnp.bfloat16

D_MODEL = 2048
CHUNK = 64
NORM_EPS = 1e-6
RWKV_HEADS = 16
RWKV_HEAD_DIM = 64
RWKV_DIM = 1024
DECAY_LORA = 96
AAA_LORA = 96
GATE_LORA = 256
GN_EPS = 64e-5
RWKV_COLS = 3 * RWKV_DIM + DECAY_LORA + AAA_LORA + GATE_LORA
MLSTM_HEADS = 4
MLSTM_QK_DIM = 128
MLSTM_V_DIM = 256
MLSTM_QK = 512
MLSTM_DIM = 1024
CONV_WIDTH = 4
GATE_SOFTCAP = 15.0
MLSTM_COLS = 2 * MLSTM_QK + 2 * MLSTM_DIM + 2 * MLSTM_HEADS
N_EXPERTS = 32
TOP_K = 4
EXPERT_FF = 2048
SWIGLU_LIMIT = 7.0
SWIGLU_ALPHA = 1.702

LANES = 128
SUBLANES = 8
VMEM_LIMIT = 56 * 1024 * 1024

LORA_PAD = 128
RW_WD = 3 * RWKV_DIM
RW_AD = RW_WD + LORA_PAD
RW_GD = RW_AD + LORA_PAD
RW_COLS_P = RW_GD + GATE_LORA
ML_V = 2 * MLSTM_QK
ML_IF = ML_V + MLSTM_DIM
ML_O = ML_IF + LANES
ML_COLS_P = ML_O + MLSTM_DIM
ROUTER_PAD = 128
SCAN_SEQS = 4
MLSTM_SEQS = 1

ROW_BLOCK = 512
ROW_PART = 128


def _cparams(sem):
    return pltpu.CompilerParams(dimension_semantics=sem, vmem_limit_bytes=VMEM_LIMIT)


def _bdot(a, b):
    return jnp.dot(a.astype(BF16), b.astype(BF16), preferred_element_type=F32)


def _split3(x):
    hi = x.astype(BF16)
    r1 = x - hi.astype(F32)
    mid = r1.astype(BF16)
    lo = (r1 - mid.astype(F32)).astype(BF16)
    return hi, mid, lo


def _dot_exact_lhs(mat_bf16, x):
    hi, mid, lo = _split3(x)
    return (jnp.dot(mat_bf16, hi, preferred_element_type=F32)
            + jnp.dot(mat_bf16, mid, preferred_element_type=F32)
            + jnp.dot(mat_bf16, lo, preferred_element_type=F32))


def _dot_exact_rhs(x, mat_bf16):
    hi, mid, lo = _split3(x)
    return (jnp.dot(hi, mat_bf16, preferred_element_type=F32)
            + jnp.dot(mid, mat_bf16, preferred_element_type=F32)
            + jnp.dot(lo, mat_bf16, preferred_element_type=F32))


def _sigmoid(x):
    return 1.0 / (1.0 + jnp.exp(-x))


def _softplus(x):
    return jnp.maximum(x, 0.0) + jnp.log(1.0 + jnp.exp(-jnp.abs(x)))


def _head_sum_mat():
    r = lax.broadcasted_iota(jnp.int32, (LANES, LANES), 0) // RWKV_HEAD_DIM
    c = lax.broadcasted_iota(jnp.int32, (LANES, LANES), 1) // RWKV_HEAD_DIM
    return jnp.where(r == c, 1.0, 0.0).astype(BF16)


def _pad_rows(w, n):
    return jnp.pad(w, ((0, n - w.shape[0]), (0, 0)))


PREP_GROUP = 1024
MLSTM_PROJ_TILE = ML_COLS_P


def _rwkv_in_kernel(x_ref, n1_ref, w_ref, wm_ref, mu_ref, w0_ref, wup_ref, a0_ref, aup_ref, gup_ref, kk_ref,
                    ka_ref, hn_out, pm_out, r_out, k_out, v_out, kk_out, b_out, lw_out, g_out, carry_ref):
    i = pl.program_id(1)

    @pl.when(i == 0)
    def _():
        carry_ref[...] = jnp.zeros_like(carry_ref)

    x = x_ref[...]
    hn = (x * lax.rsqrt(jnp.mean(x * x, axis=-1, keepdims=True) + NORM_EPS) * n1_ref[...]).astype(BF16)
    hn_out[...] = hn
    tq = x.shape[0]
    row = lax.broadcasted_iota(jnp.int32, (tq, 1), 0)

    def shifted(cols):
        p = jnp.dot(hn, w_ref[:, cols], preferred_element_type=F32)
        prev = jnp.where(row == 0, carry_ref[0:1, cols], pltpu.roll(p, 1, 0))
        carry_ref[0:1, cols] = p[tq - 1:tq, :]
        return p + (prev - p) * mu_ref[:, cols]

    lora = shifted(slice(RW_WD, RW_COLS_P))
    wd_t = jnp.tanh(lora[:, 0:LORA_PAD]).astype(BF16)
    ad = lora[:, LORA_PAD:2 * LORA_PAD].astype(BF16)
    gd_s = _sigmoid(lora[:, 2 * LORA_PAD:]).astype(BF16)
    hs = _head_sum_mat()

    for c in range(0, RWKV_DIM, PREP_GROUP):
        cs = slice(c, c + PREP_GROUP)
        r = shifted(cs)
        k = shifted(slice(RWKV_DIM + c, RWKV_DIM + c + PREP_GROUP))
        v = shifted(slice(2 * RWKV_DIM + c, 2 * RWKV_DIM + c + PREP_GROUP))
        w_log = -_softplus(-(w0_ref[:, cs] + jnp.dot(wd_t, wup_ref[:, cs], preferred_element_type=F32))) - 0.5
        a = _sigmoid(a0_ref[:, cs] + jnp.dot(ad, aup_ref[:, cs], preferred_element_type=F32))
        kk = k * kk_ref[:, cs]
        nrm2 = jnp.concatenate(
            [_dot_exact_rhs(kk[:, t:t + LANES] * kk[:, t:t + LANES], hs) for t in range(0, PREP_GROUP, LANES)],
            axis=1)
        kk = kk / jnp.maximum(jnp.sqrt(nrm2), 1e-12)
        r_out[:, cs] = r.astype(r_out.dtype)
        k_out[:, cs] = (k * (1.0 + (a - 1.0) * ka_ref[:, cs])).astype(k_out.dtype)
        v_out[:, cs] = v.astype(v_out.dtype)
        kk_out[:, cs] = kk.astype(kk_out.dtype)
        b_out[:, cs] = (kk * a).astype(b_out.dtype)
        lw_out[:, cs] = -jnp.exp(w_log)
        g_out[:, cs] = jnp.dot(gd_s, gup_ref[:, cs], preferred_element_type=F32).astype(g_out.dtype)

    for c in range(0, ML_COLS_P, MLSTM_PROJ_TILE):
        cs = slice(c, c + MLSTM_PROJ_TILE)
        pm_out[:, cs] = jnp.dot(hn, wm_ref[:, cs], preferred_element_type=F32)


def rwkv_in(x, norm1_w, w_r, w_m, mu, w0, w_up, a0, a_up, g_up, k_k, k_a, tq=256):
    B, S, D = x.shape
    blk = lambda n: pl.BlockSpec((None, tq, n), lambda b, i: (b, i, 0))
    full = lambda a: pl.BlockSpec(a.shape, lambda b, i: (0,) * a.ndim, pipeline_mode=pl.Buffered(1))
    row = lambda t: t.reshape(1, -1)
    mu_p = jnp.concatenate([mu[:RW_WD], jnp.pad(mu[RW_WD:RW_WD + DECAY_LORA], (0, LORA_PAD - DECAY_LORA)),
                            jnp.pad(mu[RW_WD + DECAY_LORA:RW_WD + DECAY_LORA + AAA_LORA], (0, LORA_PAD - AAA_LORA)),
                            mu[RW_WD + DECAY_LORA + AAA_LORA:]])
    params = (row(norm1_w), w_r, w_m, row(mu_p), row(w0), _pad_rows(w_up, LORA_PAD).astype(BF16), row(a0),
              _pad_rows(a_up, LORA_PAD).astype(BF16), g_up.astype(BF16), row(k_k), row(k_a))
    out = lambda n, dt: jax.ShapeDtypeStruct((B, S, n), dt)
    return pl.pallas_call(
        _rwkv_in_kernel,
        grid=(B, S // tq),
        in_specs=[blk(D)] + [full(a) for a in params],
        out_specs=[blk(D), blk(ML_COLS_P)] + [blk(RWKV_DIM)] * 7,
        out_shape=[out(D, BF16), out(ML_COLS_P, F32)] + [out(RWKV_DIM, BF16)] * 5
        + [out(RWKV_DIM, F32), out(RWKV_DIM, BF16)],
        scratch_shapes=[pltpu.VMEM((SUBLANES, RW_COLS_P), F32)],
        compiler_params=_cparams(("parallel", "arbitrary")),
        name="rwkv_in",
    )(x, *params)


def _rwkv_scan_kernel(r_ref, k_ref, v_ref, kk_ref, b_ref, lw_ref, g_ref, rk_ref, lnw_ref, lnb_ref,
                      o_ref, h_ref):
    c = pl.program_id(1)

    @pl.when(c == 0)
    def _():
        h_ref[...] = jnp.zeros_like(h_ref)

    L = CHUNK
    L2 = 2 * L
    ri = lax.broadcasted_iota(jnp.int32, (L, L), 0)
    ci = lax.broadcasted_iota(jnp.int32, (L, L), 1)
    tril = jnp.where(ri >= ci, 1.0, 0.0).astype(BF16)

    n_seq = lw_ref.shape[0]
    cat = lambda ref: jnp.concatenate([ref[i] for i in range(n_seq)], axis=1)
    rep = lambda ref: jnp.concatenate([ref[...]] * n_seq, axis=1)
    lw = cat(lw_ref)
    cum = _dot_exact_lhs(tril, lw)
    cum_end = cum[L - 1:L, :]
    w_in = jnp.exp(cum)
    w_prev = jnp.exp(cum - lw)
    w_inv = jnp.exp(-cum)
    w_tail = jnp.exp(cum_end - cum)
    w_end = jnp.exp(cum_end)

    kk = cat(kk_ref).astype(F32)
    bb = cat(b_ref).astype(F32)
    kx = cat(k_ref).astype(F32)
    rx = cat(r_ref).astype(F32)
    vx = cat(v_ref).astype(F32)
    gx = cat(g_ref).astype(F32)
    a_hat = -kk * w_prev
    r_hat = rx * w_in
    b_hat = bb * w_inv
    k_hat = kx * w_inv
    b_til = bb * w_tail
    k_til = kx * w_tail
    rkk = rx * kx * rep(rk_ref)

    lane = lax.broadcasted_iota(jnp.int32, (1, LANES), 1)
    m_lo = jnp.where(lane < RWKV_HEAD_DIM, 1.0, 0.0)
    m_hi = 1.0 - m_lo

    def stack(x):
        return jnp.concatenate([x * m_lo, x * m_hi], axis=0)

    r2 = lax.broadcasted_iota(jnp.int32, (L2, L2), 0)
    c2 = lax.broadcasted_iota(jnp.int32, (L2, L2), 1)
    same_head = (r2 // L) == (c2 // L)
    strict = same_head & (r2 > c2)
    incl = same_head & (r2 >= c2)
    diag16 = (r2 // 16) == (c2 // 16)
    eye = jnp.where(r2 == c2, 1.0, 0.0)
    hs = _head_sum_mat()

    pairs_per_seq = RWKV_DIM // LANES
    pairs = range(n_seq * pairs_per_seq)
    sls = [slice(p * LANES, (p + 1) * LANES) for p in pairs]
    v_st = [stack(vx[:, sl]) for sl in sls]
    lhs = [jnp.concatenate([stack(a_hat[:, sl]), stack(r_hat[:, sl])], axis=0).astype(BF16) for sl in sls]
    rhs = [jnp.concatenate([stack(b_hat[:, sl]), stack(k_hat[:, sl])], axis=0).astype(BF16) for sl in sls]
    sc = [lax.dot_general(lhs[p], rhs[p], (((1,), (1,)), ((), ())), preferred_element_type=F32) for p in pairs]
    n_ab = [jnp.where(strict, sc[p][:L2, :L2], 0.0) for p in pairs]
    a_ak = [jnp.where(strict, sc[p][:L2, L2:], 0.0).astype(BF16) for p in pairs]
    a_r = [jnp.concatenate([jnp.where(incl, sc[p][L2:, :L2], 0.0), jnp.where(incl, sc[p][L2:, L2:], 0.0)],
                           axis=1).astype(BF16) for p in pairs]

    nd = [jnp.where(diag16, n_ab[p], 0.0) for p in pairs]
    noff = [(n_ab[p] - nd[p]).astype(BF16) for p in pairs]
    ndb = [nd[p].astype(BF16) for p in pairs]
    s2 = [jnp.dot(ndb[p], ndb[p], preferred_element_type=F32).astype(BF16) for p in pairs]
    s4 = [jnp.dot(s2[p], s2[p], preferred_element_type=F32).astype(BF16) for p in pairs]
    s8 = [jnp.dot(s4[p], s4[p], preferred_element_type=F32).astype(BF16) for p in pairs]
    x1 = [eye + nd[p] for p in pairs]
    x2 = [x1[p] + _bdot(x1[p], s2[p]) for p in pairs]
    x3 = [x2[p] + _bdot(x2[p], s4[p]) for p in pairs]
    t_d = [(x3[p] + _bdot(x3[p], s8[p])).astype(BF16) for p in pairs]
    m1 = [jnp.dot(t_d[p], noff[p], preferred_element_type=F32) for p in pairs]
    m1b = [m1[p].astype(BF16) for p in pairs]
    m2 = [jnp.dot(m1b[p], m1b[p], preferred_element_type=F32) for p in pairs]
    m3 = [jnp.dot(m1b[p], m2[p].astype(BF16), preferred_element_type=F32) for p in pairs]
    t_inv = [jnp.dot((eye + m1[p] + m2[p] + m3[p]).astype(BF16), t_d[p], preferred_element_type=F32).astype(BF16)
             for p in pairs]

    h0 = [h_ref[p] for p in pairs]
    ah = [jnp.dot(lhs[p], h0[p].astype(BF16), preferred_element_type=F32) for p in pairs]
    x = [ah[p][:L2] + jnp.dot(a_ak[p], v_st[p].astype(BF16), preferred_element_type=F32) for p in pairs]
    u = [jnp.dot(t_inv[p], x[p].astype(BF16), preferred_element_type=F32) for p in pairs]
    uv = [jnp.concatenate([u[p], v_st[p]], axis=0).astype(BF16) for p in pairs]
    y_st = [ah[p][L2:] + jnp.dot(a_r[p], uv[p], preferred_element_type=F32) for p in pairs]
    y = [y_st[p][:L] + y_st[p][L:] for p in pairs]

    for p in pairs:
        sl = sls[p]
        upd_l = jnp.concatenate([stack(b_til[:, sl]), stack(k_til[:, sl])], axis=0).astype(BF16)
        upd = lax.dot_general(upd_l, uv[p], (((0,), (0,)), ((), ())), preferred_element_type=F32)
        w_col = jnp.sum(eye * w_end[:, sl], axis=1, keepdims=True)
        h_ref[p] = w_col * h0[p] + upd

    def head_sums(vals):
        parts = []
        for t in vals:
            hi = t.astype(BF16)
            parts += [hi, (t - hi.astype(F32)).astype(BF16)]
        res = jnp.dot(jnp.concatenate(parts, axis=0), hs, preferred_element_type=F32)
        return [res[2 * i * L:(2 * i + 1) * L] + res[(2 * i + 1) * L:(2 * i + 2) * L] for i in range(len(vals))]

    sums1 = [head_sums([y[p], rkk[:, sls[p]]]) for p in pairs]
    d = [y[p] - sums1[p][0] * (1.0 / RWKV_HEAD_DIM) for p in pairs]
    var = [head_sums([d[p] * d[p]])[0] * (1.0 / RWKV_HEAD_DIM) for p in pairs]
    for p in pairs:
        sl = sls[p]
        psl = sls[p % pairs_per_seq]
        yn = d[p] * lax.rsqrt(var[p] + GN_EPS) * lnw_ref[:, psl] + lnb_ref[:, psl]
        o_ref[p // pairs_per_seq, :, psl] = ((yn + sums1[p][1] * vx[:, sl]) * gx[:, sl]).astype(o_ref.dtype)


def rwkv_scan(r, k, v, kk, b, lw, g, r_k, ln_w, ln_b):
    B, S, C = r.shape
    blk = pl.BlockSpec((SCAN_SEQS, CHUNK, C), lambda bb, c: (bb, c, 0))
    full = pl.BlockSpec((1, C), lambda bb, c: (0, 0))
    return pl.pallas_call(
        _rwkv_scan_kernel,
        grid=(B // SCAN_SEQS, S // CHUNK),
        in_specs=[blk] * 7 + [full] * 3,
        out_specs=blk,
        out_shape=jax.ShapeDtypeStruct((B, S, C), BF16),
        scratch_shapes=[pltpu.VMEM((SCAN_SEQS * C // LANES, LANES, LANES), F32)],
        compiler_params=_cparams(("parallel", "arbitrary")),
        name="rwkv_scan",
    )(r, k, v, kk, b, lw, g, r_k, ln_w, ln_b)


def _mlstm_kernel(p_ref, convw_ref, ifb_ref, nw_ref, o_ref, carry_ref, c_ref, n_ref, m_ref):
    ci = pl.program_id(1)

    @pl.when(ci == 0)
    def _():
        carry_ref[...] = jnp.zeros_like(carry_ref)
        c_ref[...] = jnp.zeros_like(c_ref)
        n_ref[...] = jnp.zeros_like(n_ref)
        m_ref[...] = jnp.zeros_like(m_ref)

    L = CHUNK
    n_seq = p_ref.shape[0]
    lane = lax.broadcasted_iota(jnp.int32, (1, LANES), 1)
    ri = lax.broadcasted_iota(jnp.int32, (L, L), 0)
    cj = lax.broadcasted_iota(jnp.int32, (L, L), 1)
    causal = ri >= cj
    tril = jnp.where(causal, 1.0, 0.0).astype(BF16)
    triu = jnp.where(ri <= cj, 1.0, 0.0).astype(BF16)

    qk, comb, comb_t, b_col, b_row = [], [], [], [], []
    for i in range(n_seq):
        u = p_ref[i, :, 0:ML_V]
        ext = jnp.concatenate([carry_ref[i], u], axis=0)
        carry_ref[i] = u[L - SUBLANES:L, :]
        conv = convw_ref[CONV_WIDTH - 1:CONV_WIDTH, :] * u
        for j in range(1, CONV_WIDTH):
            conv = conv + convw_ref[CONV_WIDTH - 1 - j:CONV_WIDTH - j, :] * pltpu.roll(ext, j, 0)[SUBLANES:, :]
        qk.append(conv * _sigmoid(conv))
        pre = GATE_SOFTCAP * jnp.tanh((p_ref[i, :, ML_IF:ML_O] + ifb_ref[...]) * (1.0 / GATE_SOFTCAP))
        cb = jnp.where(lane < MLSTM_HEADS, pre, -_softplus(-pre))
        comb.append(cb)
        comb_t.append(cb.T)
        b_col.append(_dot_exact_lhs(tril, cb))
        b_row.append(_dot_exact_rhs(comb_t[i], triu))

    H = range(n_seq * MLSTM_HEADS)
    sq = [v // MLSTM_HEADS for v in H]
    hd = [v % MLSTM_HEADS for v in H]
    dk, dv = MLSTM_QK_DIM, MLSTM_V_DIM
    qh = [qk[sq[v]][:, hd[v] * dk:(hd[v] + 1) * dk] * (dk ** -0.5) for v in H]
    kh = [qk[sq[v]][:, MLSTM_QK + hd[v] * dk:MLSTM_QK + (hd[v] + 1) * dk] for v in H]
    vh = [p_ref[sq[v], :, ML_V + hd[v] * dv:ML_V + (hd[v] + 1) * dv].astype(BF16) for v in H]
    qb = [qh[h].astype(BF16) for h in H]
    bcol = [b_col[sq[v]][:, MLSTM_HEADS + hd[v]:MLSTM_HEADS + hd[v] + 1] for v in H]
    brow = [b_row[sq[v]][MLSTM_HEADS + hd[v]:MLSTM_HEADS + hd[v] + 1, :] for v in H]
    m_prev = [m_ref[h][0:1, 0:1] for h in H]
    n_prev = [n_ref[h][0:1, :] for h in H]
    c_prev = [c_ref[h] for h in H]

    qk_t = [lax.dot_general(qb[h], kh[h].astype(BF16), (((1,), (1,)), ((), ())), preferred_element_type=F32)
            for h in H]
    qc = [jnp.dot(qb[h], c_prev[h].astype(BF16), preferred_element_type=F32) for h in H]
    dm = [jnp.where(causal, bcol[h] - brow[h] + comb_t[sq[h]][hd[h]:hd[h] + 1, :], -jnp.inf) for h in H]
    inter = [bcol[h] + m_prev[h] for h in H]
    m_t = [jnp.maximum(inter[h], jnp.max(dm[h], axis=-1, keepdims=True)) for h in H]
    s = [qk_t[h] * jnp.exp(dm[h] - m_t[h]) for h in H]
    w_inter = [jnp.exp(inter[h] - m_t[h]) for h in H]
    num = [jnp.dot(s[h].astype(BF16), vh[h], preferred_element_type=F32) + w_inter[h] * qc[h] for h in H]
    den = [jnp.sum(s[h], axis=-1, keepdims=True) + w_inter[h] * jnp.sum(qh[h] * n_prev[h], axis=-1, keepdims=True)
           for h in H]
    hh = [num[h] / jnp.maximum(jnp.abs(den[h]), jnp.exp(-m_t[h])) for h in H]

    g_tot = [bcol[h][L - 1:L, :] for h in H]
    a = [comb[sq[h]][:, hd[h]:hd[h] + 1] + g_tot[h] - bcol[h] for h in H]
    m_new = [jnp.maximum(g_tot[h] + m_prev[h], jnp.max(a[h], axis=0, keepdims=True)) for h in H]
    dec = [jnp.exp(g_tot[h] + m_prev[h] - m_new[h]) for h in H]
    wkk = [jnp.exp(a[h] - m_new[h]) * kh[h] for h in H]
    for h in H:
        c_ref[h] = dec[h] * c_prev[h] + lax.dot_general(wkk[h].astype(BF16), vh[h], (((0,), (0,)), ((), ())),
                                                        preferred_element_type=F32)
        n_ref[h] = jnp.broadcast_to(dec[h] * n_prev[h] + jnp.sum(wkk[h], axis=0, keepdims=True),
                                    (SUBLANES, LANES))
        m_ref[h] = jnp.broadcast_to(m_new[h], (SUBLANES, LANES))
    for h in H:
        vs = slice(hd[h] * dv, (hd[h] + 1) * dv)
        hn = hh[h] * lax.rsqrt(jnp.mean(hh[h] * hh[h], axis=-1, keepdims=True) + NORM_EPS)
        o_raw = p_ref[sq[h], :, ML_O + hd[h] * dv:ML_O + (hd[h] + 1) * dv]
        o_ref[sq[h], :, vs] = (hn * nw_ref[:, vs] * _sigmoid(o_raw)).astype(o_ref.dtype)


def mlstm_branch(p_pad, conv_w, i_b, f_b, norm_w):
    B, S, C = p_pad.shape
    ifb = jnp.pad(jnp.concatenate([i_b, f_b]), (0, LANES - 2 * MLSTM_HEADS)).reshape(1, LANES)
    full = lambda a: pl.BlockSpec(a.shape, lambda b, c: (0,) * a.ndim)
    nw = norm_w.reshape(1, MLSTM_DIM)
    return pl.pallas_call(
        _mlstm_kernel,
        grid=(B // MLSTM_SEQS, S // CHUNK),
        in_specs=[pl.BlockSpec((MLSTM_SEQS, CHUNK, C), lambda b, c: (b, c, 0)), full(conv_w), full(ifb), full(nw)],
        out_specs=pl.BlockSpec((MLSTM_SEQS, CHUNK, MLSTM_DIM), lambda b, c: (b, c, 0)),
        out_shape=jax.ShapeDtypeStruct((B, S, MLSTM_DIM), BF16),
        scratch_shapes=[pltpu.VMEM((MLSTM_SEQS, SUBLANES, ML_V), F32),
                        pltpu.VMEM((MLSTM_SEQS * MLSTM_HEADS, MLSTM_QK_DIM, MLSTM_V_DIM), F32),
                        pltpu.VMEM((MLSTM_SEQS * MLSTM_HEADS, SUBLANES, LANES), F32),
                        pltpu.VMEM((MLSTM_SEQS * MLSTM_HEADS, SUBLANES, LANES), F32)],
        compiler_params=_cparams(("parallel", "arbitrary")),
        name="mlstm_scan",
    )(p_pad, conv_w, ifb, nw)


HI_MASK = 0xFFFF0000


def _pack_bf16_pairs(hb_f32):
    c = hb_f32.shape[1] // 2
    u = pltpu.bitcast(hb_f32, jnp.uint32)
    return u[:, c:] | (u[:, :c] >> 16)


ROW_TILES = D_MODEL // 2 // LANES


def _store_tiled(ref, val, first_row=0):
    n = val.shape[0]
    for t in range(ROW_TILES):
        ref[pl.ds(first_row * ROW_TILES + t, n, stride=ROW_TILES), :] = val[:, t * LANES:(t + 1) * LANES]


def _load_tiled(ref, n, first_row=0):
    return jnp.concatenate([ref[pl.ds(first_row * ROW_TILES + t, n, stride=ROW_TILES), :] for t in range(ROW_TILES)],
                           axis=1)


def _unpack_bf16_pairs(xu):
    lo = pltpu.bitcast(xu << 16, F32).astype(BF16)
    hi = pltpu.bitcast(xu & jnp.uint32(HI_MASK), F32).astype(BF16)
    return lo, hi


def _merge_kernel(oa_ref, ob_ref, hn_ref, bg_ref, x_ref, wa_ref, wb_ref, wg_ref, wo_ref, n2_ref, rwh_ref, rwl_ref,
                  rb_ref, x1_ref, xp_ref, lg_ref):
    hn1 = hn_ref[...]

    def gated(o_ref, w_ref, cols):
        gate = _sigmoid(jnp.dot(hn1, wg_ref[:, cols], preferred_element_type=F32) + bg_ref[:, cols])
        return gate * jnp.dot(o_ref[...], w_ref[...], preferred_element_type=F32)

    merged = gated(oa_ref, wa_ref, slice(0, D_MODEL)) + gated(ob_ref, wb_ref, slice(D_MODEL, 2 * D_MODEL))
    x1 = x_ref[...] + jnp.dot(merged.astype(BF16), wo_ref[...], preferred_element_type=F32)
    x1_ref[...] = x1
    hn = x1 * lax.rsqrt(jnp.mean(x1 * x1, axis=-1, keepdims=True) + NORM_EPS) * n2_ref[...]
    hi = hn.astype(BF16)
    _store_tiled(xp_ref, _pack_bf16_pairs(hi.astype(F32)))
    lo = (hn - hi.astype(F32)).astype(BF16)
    lg_ref[...] = (jnp.dot(hi, rwh_ref[...], preferred_element_type=F32)
                   + jnp.dot(hi, rwl_ref[...], preferred_element_type=F32)
                   + jnp.dot(lo, rwh_ref[...], preferred_element_type=F32) + rb_ref[...])


def merge_project(o_a, o_b, hn, w_gate, b_gate, x, w_a, w_b, w_out, norm2_w, router_w, router_b, tm=256):
    T, D = x.shape
    rw = jnp.pad(router_w, ((0, 0), (0, ROUTER_PAD - N_EXPERTS)))
    rw_hi = rw.astype(BF16)
    rw_lo = (rw - rw_hi.astype(F32)).astype(BF16)
    rb = jnp.pad(router_b, (0, ROUTER_PAD - N_EXPERTS)).reshape(1, ROUTER_PAD)
    rows = lambda n: pl.BlockSpec((tm, n), lambda i: (i, 0))
    full = lambda a: pl.BlockSpec(a.shape, lambda i: (0, 0), pipeline_mode=pl.Buffered(1))
    params = (w_a.astype(BF16), w_b.astype(BF16), w_gate, w_out.astype(BF16), norm2_w.reshape(1, D), rw_hi, rw_lo,
              rb)
    bg = b_gate.reshape(1, 2 * D)
    return pl.pallas_call(
        _merge_kernel,
        grid=(T // tm,),
        in_specs=[rows(o_a.shape[1]), rows(o_b.shape[1]), rows(D), full(bg), rows(D)] + [full(a) for a in params],
        out_specs=[rows(D), pl.BlockSpec((tm * ROW_TILES, LANES), lambda i: (i, 0)), rows(ROUTER_PAD)],
        out_shape=[jax.ShapeDtypeStruct((T, D), F32), jax.ShapeDtypeStruct((T * ROW_TILES, LANES), jnp.uint32),
                   jax.ShapeDtypeStruct((T, ROUTER_PAD), F32)],
        compiler_params=_cparams(("parallel",)),
        name="merge_project",
    )(o_a, o_b, hn, bg, x, *params)


UP_TILE = 1024
DOWN_TILE = 2048
RANK_GROUP = 256
DISPATCH_TOKENS = 512
COMBINE_TOKENS = 256


def _row_copy(src, dst, src_at, dst_at, sem):
    return pltpu.make_async_copy(src.at[pl.ds(pl.multiple_of(src_at, ROW_TILES), ROW_TILES), :],
                                 dst.at[pl.ds(pl.multiple_of(dst_at, ROW_TILES), ROW_TILES), :], sem)


def _dispatch_kernel(zrow_ref, dest_ref, xp_ref, xs_hbm, zbuf, zsem, sem):
    i = pl.program_id(0)
    tq = dest_ref.shape[2] // TOP_K

    def zero_copy(k):
        at = pl.multiple_of(zrow_ref[k], ROW_BLOCK * ROW_TILES)
        return pltpu.make_async_copy(zbuf, xs_hbm.at[pl.ds(at, ROW_BLOCK * ROW_TILES), :], zsem)

    @pl.when(i == 0)
    def _():
        zbuf[...] = jnp.zeros_like(zbuf)
        for k in range(zrow_ref.shape[0]):
            pl.when(zrow_ref[k] >= 0)(lambda k=k: zero_copy(k).start())
        for k in range(zrow_ref.shape[0]):
            pl.when(zrow_ref[k] >= 0)(lambda k=k: zero_copy(k).wait())

    def body(q, carry):
        base = pl.multiple_of(q * SUBLANES, SUBLANES)
        for r in range(SUBLANES):
            for j in range(TOP_K):
                _row_copy(xp_ref, xs_hbm, (base + r) * ROW_TILES, dest_ref[0, 0, TOP_K * (base + r) + j],
                          sem).start(priority=j % 2)
        return carry

    lax.fori_loop(0, tq // SUBLANES, body, 0)
    for j in range(TOP_K):
        pltpu.make_async_copy(xp_ref, xs_hbm.at[pl.ds(0, tq * ROW_TILES), :], sem).wait()


def moe_dispatch(xp, dest, zrow, n_rows):
    T = xp.shape[0] // ROW_TILES
    tq = DISPATCH_TOKENS
    dest_blk = (dest * ROW_TILES).reshape(T // tq, 1, TOP_K * tq)
    zrow = jnp.where(zrow >= 0, zrow * ROW_TILES, -1)
    return pl.pallas_call(
        _dispatch_kernel,
        grid_spec=pltpu.PrefetchScalarGridSpec(
            num_scalar_prefetch=1,
            grid=(T // tq,),
            in_specs=[pl.BlockSpec((1, 1, TOP_K * tq), lambda i, zr: (i, 0, 0), memory_space=pltpu.SMEM),
                      pl.BlockSpec((tq * ROW_TILES, LANES), lambda i, zr: (i, 0))],
            out_specs=pl.BlockSpec(memory_space=pl.ANY),
            scratch_shapes=[pltpu.VMEM((ROW_BLOCK * ROW_TILES, LANES), xp.dtype), pltpu.SemaphoreType.DMA(()),
                            pltpu.SemaphoreType.DMA(())]),
        out_shape=jax.ShapeDtypeStruct((n_rows * ROW_TILES, LANES), xp.dtype),
        compiler_params=_cparams(("arbitrary",)),
        name="moe_dispatch",
    )(zrow, dest_blk, xp)


SCHED_NV, SCHED_SG, SCHED_SB, SCHED_SO, SCHED_PARTS, SCHED_GE, SCHED_GF, SCHED_NG = range(8)


def _stream_weights(s, sched, copies, on_arrival):
    sg_ref = sched[SCHED_SG]
    g = sg_ref[s]
    first = (s < sched[SCHED_NV][0]) & ((s == 0) | (g != sg_ref[jnp.maximum(s - 1, 0)]))

    @pl.when(first)
    def _():
        @pl.when(s == 0)
        def _():
            for c in copies(g):
                c.start()

        for c in copies(g):
            c.wait()
        on_arrival()

        @pl.when(g + 1 < sched[SCHED_NG][0])
        def _():
            for c in copies(g + 1):
                c.start()


def _for_used_rows(s, sched, out_ref, compute):
    parts = sched[SCHED_PARTS][s]
    rows_per_row = out_ref.shape[0] // ROW_BLOCK

    @pl.when(s >= sched[SCHED_NV][0])
    def _():
        out_ref[...] = jnp.zeros_like(out_ref)

    def variant(rows):
        compute(rows)
        if rows < ROW_BLOCK:
            rest = rows * rows_per_row
            out_ref[rest:, :] = jnp.zeros((out_ref.shape[0] - rest, out_ref.shape[1]), out_ref.dtype)

    for p in range(1, ROW_BLOCK // ROW_PART + 1):
        pl.when((s < sched[SCHED_NV][0]) & (parts == p))(functools.partial(variant, p * ROW_PART))


def _moe_up_kernel(*refs):
    sched = refs[:8]
    xs_ref, w_hbm, bg_ref, bu_ref, h_ref, wbuf, wgb, wub, sems = refs[8:]
    s = pl.program_id(0)
    tf = wgb.shape[1]
    n_ff = w_hbm.shape[2] // 2

    def copies(g):
        e = sched[SCHED_GE][g]
        col = pl.multiple_of(sched[SCHED_GF][g] * tf, tf)
        return [pltpu.make_async_copy(w_hbm.at[e, :, pl.ds(half * n_ff + col, tf)], wbuf.at[half], sems.at[half])
                for half in range(2)]

    def on_arrival():
        wgb[...] = wbuf[0].astype(BF16)
        wub[...] = wbuf[1].astype(BF16)

    _stream_weights(s, sched, copies, on_arrival)

    def compute(rows):
        lo, hi = _unpack_bf16_pairs(_load_tiled(xs_ref, rows))
        half = lo.shape[1]

        def proj(wb, b_ref):
            return (jnp.dot(lo, wb[:half, :], preferred_element_type=F32)
                    + jnp.dot(hi, wb[half:, :], preferred_element_type=F32) + b_ref[...])

        gate = jnp.minimum(proj(wgb, bg_ref), SWIGLU_LIMIT)
        up = jnp.clip(proj(wub, bu_ref), -SWIGLU_LIMIT, SWIGLU_LIMIT)
        h_ref[:rows, :] = ((up + 1.0) * gate * _sigmoid(SWIGLU_ALPHA * gate)).astype(h_ref.dtype)

    _for_used_rows(s, sched, h_ref, compute)


def _step_expert(s, r):
    return r[SCHED_GE][r[SCHED_SG][s]]


def _step_tile(s, r):
    return r[SCHED_GF][r[SCHED_SG][s]]


def moe_up(xs, w_gu, b_gu, sched):
    P = xs.shape[0] // ROW_TILES
    E, D, F2 = w_gu.shape
    F = F2 // 2
    tf = UP_TILE
    nf = F // tf
    b3 = b_gu.reshape(E, 1, F2)
    return pl.pallas_call(
        _moe_up_kernel,
        grid_spec=pltpu.PrefetchScalarGridSpec(
            num_scalar_prefetch=len(sched),
            grid=(sched[SCHED_SG].shape[0],),
            in_specs=[pl.BlockSpec((ROW_BLOCK * ROW_TILES, LANES), lambda s, *r: (r[SCHED_SB][s], 0)),
                      pl.BlockSpec(memory_space=pl.ANY),
                      pl.BlockSpec((None, 1, tf), lambda s, *r: (_step_expert(s, r), 0, _step_tile(s, r))),
                      pl.BlockSpec((None, 1, tf), lambda s, *r: (_step_expert(s, r), 0, nf + _step_tile(s, r)))],
            out_specs=pl.BlockSpec((ROW_BLOCK, tf), lambda s, *r: (r[SCHED_SB][s], r[SCHED_SO][s])),
            scratch_shapes=[pltpu.VMEM((2, D, tf), F32), pltpu.VMEM((D, tf), BF16), pltpu.VMEM((D, tf), BF16),
                            pltpu.SemaphoreType.DMA((2,))]),
        out_shape=jax.ShapeDtypeStruct((P, F), BF16),
        compiler_params=_cparams(("arbitrary",)),
        name="moe_up",
    )(*sched, xs, w_gu, b3, b3)


def _moe_down_kernel(*refs):
    sched = refs[:8]
    h_ref, w_hbm, bd_ref, y_ref, wbuf, wdb, sem = refs[8:]
    s = pl.program_id(0)
    tn = wdb.shape[1]

    def copies(g):
        col = pl.multiple_of(sched[SCHED_GF][g] * tn, tn)
        return [pltpu.make_async_copy(w_hbm.at[sched[SCHED_GE][g], :, pl.ds(col, tn)], wbuf, sem)]

    def on_arrival():
        wdb[...] = wbuf[...].astype(BF16)

    _stream_weights(s, sched, copies, on_arrival)

    def compute(rows):
        y = jnp.dot(h_ref[:rows, :], wdb[...], preferred_element_type=F32) + bd_ref[...]
        _store_tiled(y_ref, _pack_bf16_pairs(y.astype(BF16).astype(F32)))

    _for_used_rows(s, sched, y_ref, compute)


def moe_down(h, w_down, b_down, sched):
    P, F = h.shape
    E, _, D = w_down.shape
    tn = DOWN_TILE
    assert tn == D, "the packed output pairs column c with column c + D/2"
    b3 = b_down.reshape(E, 1, D)
    return pl.pallas_call(
        _moe_down_kernel,
        grid_spec=pltpu.PrefetchScalarGridSpec(
            num_scalar_prefetch=len(sched),
            grid=(sched[SCHED_SG].shape[0],),
            in_specs=[pl.BlockSpec((ROW_BLOCK, F), lambda s, *r: (r[SCHED_SB][s], 0)),
                      pl.BlockSpec(memory_space=pl.ANY),
                      pl.BlockSpec((None, 1, tn), lambda s, *r: (_step_expert(s, r), 0, _step_tile(s, r)))],
            out_specs=pl.BlockSpec((ROW_BLOCK * ROW_TILES, LANES), lambda s, *r: (r[SCHED_SB][s], 0)),
            scratch_shapes=[pltpu.VMEM((F, tn), F32), pltpu.VMEM((F, tn), BF16), pltpu.SemaphoreType.DMA(())]),
        out_shape=jax.ShapeDtypeStruct((P * ROW_TILES, LANES), jnp.uint32),
        compiler_params=_cparams(("arbitrary",)),
        name="moe_down",
    )(*sched, h, w_down, b3)


def _combine_kernel(dcur_ref, dnxt_ref, x1_ref, w_ref, fw_ref, ys_hbm, o_ref, buf, sems):
    i = pl.program_id(0)
    n_steps = pl.num_programs(0)
    tq = x1_ref.shape[0]
    n = TOP_K * tq
    slot = lax.rem(i, 2)

    def issue(idx_ref, sl):
        def body(q, carry):
            base = pl.multiple_of(q * SUBLANES, SUBLANES)
            for r in range(SUBLANES):
                _row_copy(ys_hbm, buf.at[sl], idx_ref[0, 0, base + r], (base + r) * ROW_TILES,
                          sems.at[sl]).start(priority=r % 2)
            return carry

        lax.fori_loop(0, n // SUBLANES, body, 0)

    def wait_slot(sl):
        pltpu.make_async_copy(ys_hbm.at[pl.ds(0, n * ROW_TILES), :], buf.at[sl], sems.at[sl]).wait()

    pl.when(i == 0)(lambda: issue(dcur_ref, 0))
    wait_slot(slot)
    for t in range(n):
        _row_copy(ys_hbm, buf.at[1 - slot], dnxt_ref[0, 0, t], t * ROW_TILES,
                  sems.at[1 - slot]).start(priority=t % 2)

    half = ROW_TILES * LANES
    acc_lo = x1_ref[:, :half]
    acc_hi = x1_ref[:, half:]
    for j in range(TOP_K):
        lo, hi = _unpack_bf16_pairs(_load_tiled(buf.at[slot], tq, first_row=j * tq))
        acc_lo = acc_lo + w_ref[:, j:j + 1] * lo.astype(F32)
        acc_hi = acc_hi + w_ref[:, j:j + 1] * hi.astype(F32)
    ssq = jnp.sum(acc_lo * acc_lo, axis=-1, keepdims=True) + jnp.sum(acc_hi * acc_hi, axis=-1, keepdims=True)
    scale = lax.rsqrt(ssq * (1.0 / (2 * half)) + NORM_EPS)
    o_ref[:, :half] = acc_lo * scale * fw_ref[:, :half]
    o_ref[:, half:] = acc_hi * scale * fw_ref[:, half:]
    pl.when(i == n_steps - 1)(lambda: wait_slot(1 - slot))


def combine(x1, ys, dest, top_w, final_w):
    T, D = x1.shape
    tq = COMBINE_TOKENS
    nblk = T // tq
    dest_blk = (dest * ROW_TILES).reshape(nblk, tq, TOP_K).transpose(0, 2, 1).reshape(nblk, 1, TOP_K * tq)
    idx_spec = lambda f: pl.BlockSpec((1, 1, TOP_K * tq), f, memory_space=pltpu.SMEM)
    return pl.pallas_call(
        _combine_kernel,
        grid=(nblk,),
        in_specs=[idx_spec(lambda i: (i, 0, 0)),
                  idx_spec(lambda i: (jnp.minimum(i + 1, nblk - 1), 0, 0)),
                  pl.BlockSpec((tq, D), lambda i: (i, 0)),
                  pl.BlockSpec((tq, TOP_K), lambda i: (i, 0)),
                  pl.BlockSpec((1, D), lambda i: (0, 0)),
                  pl.BlockSpec(memory_space=pl.ANY)],
        out_specs=pl.BlockSpec((tq, D), lambda i: (i, 0)),
        out_shape=jax.ShapeDtypeStruct((T, D), F32),
        scratch_shapes=[pltpu.VMEM((2, TOP_K * tq * ROW_TILES, LANES), ys.dtype), pltpu.SemaphoreType.DMA((2,))],
        compiler_params=_cparams(("arbitrary",)),
        name="moe_combine",
    )(dest_blk, dest_blk, x1, top_w, final_w.reshape(1, D), ys)


def _routing(logits):
    T = logits.shape[0]
    TK = T * TOP_K
    NB = TK // ROW_BLOCK + N_EXPERTS
    top_logits, top_idx = lax.top_k(logits[:, :N_EXPERTS], TOP_K)
    top_w = jax.nn.softmax(top_logits, axis=-1)
    experts = jnp.arange(N_EXPERTS, dtype=jnp.int32)
    onehot = (top_idx.reshape(TK, 1) == experts[None, :]).astype(F32)
    oh = onehot.reshape(TK // RANK_GROUP, RANK_GROUP, N_EXPERTS)
    local = jnp.einsum("ts,gse->gte", jnp.tril(jnp.ones((RANK_GROUP, RANK_GROUP), F32)), oh)
    tot = local[:, -1, :]
    offs = jnp.cumsum(tot, axis=0) - tot
    counts = (offs[-1] + tot[-1]).astype(jnp.int32)
    padded = (counts + ROW_BLOCK - 1) // ROW_BLOCK * ROW_BLOCK
    pad_end = jnp.cumsum(padded).astype(jnp.int32)
    pad_start = pad_end - padded
    dest = jnp.sum(oh * (local + (offs + (pad_start.astype(F32) - 1.0)[None, :])[:, None, :]), axis=-1)
    dest = dest.reshape(T, TOP_K).astype(jnp.int32)

    blk = jnp.arange(NB, dtype=jnp.int32)
    block_e = jnp.minimum(jnp.sum((pad_end[None, :] <= (blk * ROW_BLOCK)[:, None]).astype(jnp.int32), axis=1),
                          N_EXPERTS - 1)
    of_block = lambda table: jnp.sum(jnp.where(block_e[:, None] == experts[None, :], table[None, :], 0), axis=1)
    nb_used = pad_end[-1] // ROW_BLOCK
    rows_left = of_block(counts) - (blk - of_block(pad_start // ROW_BLOCK)) * ROW_BLOCK
    block_parts = jnp.where(blk < nb_used, jnp.clip((rows_left + ROW_PART - 1) // ROW_PART, 1, ROW_BLOCK // ROW_PART), 0)
    used = padded > 0
    block_pos = of_block(jnp.cumsum(used.astype(jnp.int32)) - 1)
    tail = nb_used + experts
    zrow = jnp.concatenate([jnp.where(used, pad_end - ROW_BLOCK, -1),
                            jnp.where(tail < NB, tail * ROW_BLOCK, -1)]).astype(jnp.int32)
    return dest, top_w, (block_e, block_parts, block_pos, used, nb_used), zrow


def _schedule(blocks, n_tiles):
    block_e, block_parts, block_pos, used, nb_used = blocks
    nb = block_e.shape[0]
    b = jnp.asarray(np.tile(np.arange(nb, dtype=np.int32), n_tiles))
    f = jnp.asarray(np.repeat(np.arange(n_tiles, dtype=np.int32), nb))
    rep = lambda a: jnp.tile(a, n_tiles)
    n_groups = (jnp.sum(used) * n_tiles).astype(jnp.int32)
    n_valid = (nb_used * n_tiles).astype(jnp.int32)
    group = jnp.minimum(rep(block_pos) * n_tiles + f, n_groups - 1)
    key = jnp.where(b < nb_used, (rep(block_e) * n_tiles + f) * nb + b, (N_EXPERTS * n_tiles + f) * nb + b)
    _, group, sb, so, parts = lax.sort((key, group, b, f, rep(block_parts)), num_keys=1)
    sg = jnp.where(jnp.arange(nb * n_tiles) < n_valid, group, n_groups - 1)
    experts = jnp.arange(N_EXPERTS, dtype=jnp.int32)
    used_first = jnp.argsort(jnp.where(used, experts, N_EXPERTS + experts)).astype(jnp.int32)
    ge = jnp.repeat(used_first, n_tiles)
    gf = jnp.tile(jnp.arange(n_tiles, dtype=jnp.int32), N_EXPERTS)
    return (n_valid.reshape(1), sg, sb, so, parts, ge, gf, n_groups.reshape(1))


def _repack_w_in(w_in):
    w_in = w_in.astype(BF16)
    z = lambda n: jnp.zeros((w_in.shape[0], n), w_in.dtype)
    o = RWKV_COLS
    w_r = jnp.concatenate([w_in[:, :RW_WD], w_in[:, RW_WD:RW_WD + DECAY_LORA], z(LORA_PAD - DECAY_LORA),
                           w_in[:, RW_WD + DECAY_LORA:RW_WD + DECAY_LORA + AAA_LORA], z(LORA_PAD - AAA_LORA),
                           w_in[:, RW_WD + DECAY_LORA + AAA_LORA:o]], axis=1)
    w_m = jnp.concatenate([w_in[:, o:o + ML_IF], w_in[:, o + ML_IF:o + ML_IF + 2 * MLSTM_HEADS],
                           z(LANES - 2 * MLSTM_HEADS), w_in[:, o + ML_IF + 2 * MLSTM_HEADS:o + MLSTM_COLS]], axis=1)
    w_g = w_in[:, o + MLSTM_COLS:]
    return w_r, w_m, w_g


def kernel(x, norm1_w, w_in, b_gate, rwkv_mu, rwkv_w0, rwkv_w_up, rwkv_a0, rwkv_a_up, rwkv_g_up, rwkv_k_k,
           rwkv_k_a, rwkv_r_k, rwkv_ln_w, rwkv_ln_b, mlstm_conv_w, mlstm_i_b, mlstm_f_b, mlstm_norm_w,
           w_branch_a, w_branch_b, w_out, norm2_w, router_w, router_b, w_gu, b_gu, w_down, b_down,
           final_norm_w):
    B, S, D = x.shape
    T = B * S
    xt = x.reshape(T, D)
    assert norm1_w.shape[0] == 1, "single-layer block: the final rmsnorm is fused into the MoE combine"
    for l in range(1):
        w_r, w_m, w_g = _repack_w_in(w_in[l])
        hn, p_m, *scan_in = rwkv_in(x, norm1_w[l], w_r, w_m, rwkv_mu[l], rwkv_w0[l], rwkv_w_up[l], rwkv_a0[l],
                                    rwkv_a_up[l], rwkv_g_up[l], rwkv_k_k[l], rwkv_k_a[l])
        hn = hn.reshape(T, D)
        o_a = rwkv_scan(*scan_in, rwkv_r_k[l].reshape(1, -1), rwkv_ln_w[l].reshape(1, -1),
                        rwkv_ln_b[l].reshape(1, -1))
        o_b = mlstm_branch(p_m, mlstm_conv_w[l], mlstm_i_b[l], mlstm_f_b[l], mlstm_norm_w[l])
        x1, xp, logits = merge_project(o_a.reshape(T, RWKV_DIM), o_b.reshape(T, MLSTM_DIM), hn, w_g, b_gate[l], xt,
                                       w_branch_a[l], w_branch_b[l], w_out[l], norm2_w[l], router_w[l],
                                       router_b[l])
        dest, top_w, blocks, zrow = _routing(logits)
        xs = moe_dispatch(xp, dest, zrow, blocks[0].shape[0] * ROW_BLOCK)
        h = moe_up(xs, w_gu[l], b_gu[l], _schedule(blocks, EXPERT_FF // UP_TILE))
        ys = moe_down(h, w_down[l], b_down[l], _schedule(blocks, D // DOWN_TILE))
        xt = combine(x1, ys, dest, top_w, final_norm_w)
    return xt.reshape(B, S, D)
```

```python
import functools

import jax
import jax.numpy as jnp
import numpy as np
from jax import lax
from jax.experimental import pallas as pl
from jax.experimental.pallas import tpu as pltpu

F32 = jnp.float32
BF16 = jnp.bfloat16

D_MODEL = 2048
CHUNK = 64
NORM_EPS = 1e-6
RWKV_HEADS = 16
RWKV_HEAD_DIM = 64
RWKV_DIM = 1024
DECAY_LORA = 96
AAA_LORA = 96
GATE_LORA = 256
GN_EPS = 64e-5
RWKV_COLS = 3 * RWKV_DIM + DECAY_LORA + AAA_LORA + GATE_LORA
MLSTM_HEADS = 4
MLSTM_QK_DIM = 128
MLSTM_V_DIM = 256
MLSTM_QK = 512
MLSTM_DIM = 1024
CONV_WIDTH = 4
GATE_SOFTCAP = 15.0
MLSTM_COLS = 2 * MLSTM_QK + 2 * MLSTM_DIM + 2 * MLSTM_HEADS
N_EXPERTS = 32
TOP_K = 4
EXPERT_FF = 2048
SWIGLU_LIMIT = 7.0
SWIGLU_ALPHA = 1.702

LANES = 128
SUBLANES = 8
VMEM_LIMIT = 56 * 1024 * 1024

LORA_PAD = 128
RW_WD = 3 * RWKV_DIM
RW_AD = RW_WD + LORA_PAD
RW_GD = RW_AD + LORA_PAD
RW_COLS_P = RW_GD + GATE_LORA
ML_V = 2 * MLSTM_QK
ML_IF = ML_V + MLSTM_DIM
ML_O = ML_IF + LANES
ML_COLS_P = ML_O + MLSTM_DIM
ROUTER_PAD = 128
SCAN_SEQS = 4
MLSTM_SEQS = 1

ROW_BLOCK = 512
ROW_PART = 128


def _cparams(sem):
    return pltpu.CompilerParams(dimension_semantics=sem, vmem_limit_bytes=VMEM_LIMIT)


def _bdot(a, b):
    return jnp.dot(a.astype(BF16), b.astype(BF16), preferred_element_type=F32)


def _split3(x):
    hi = x.astype(BF16)
    r1 = x - hi.astype(F32)
    mid = r1.astype(BF16)
    lo = (r1 - mid.astype(F32)).astype(BF16)
    return hi, mid, lo


def _dot_exact_lhs(mat_bf16, x):
    hi, mid, lo = _split3(x)
    return (jnp.dot(mat_bf16, hi, preferred_element_type=F32)
            + jnp.dot(mat_bf16, mid, preferred_element_type=F32)
            + jnp.dot(mat_bf16, lo, preferred_element_type=F32))


def _dot_exact_rhs(x, mat_bf16):
    hi, mid, lo = _split3(x)
    return (jnp.dot(hi, mat_bf16, preferred_element_type=F32)
            + jnp.dot(mid, mat_bf16, preferred_element_type=F32)
            + jnp.dot(lo, mat_bf16, preferred_element_type=F32))


def _sigmoid(x):
    return 1.0 / (1.0 + jnp.exp(-x))


def _softplus(x):
    return jnp.maximum(x, 0.0) + jnp.log(1.0 + jnp.exp(-jnp.abs(x)))


def _head_sum_mat():
    r = lax.broadcasted_iota(jnp.int32, (LANES, LANES), 0) // RWKV_HEAD_DIM
    c = lax.broadcasted_iota(jnp.int32, (LANES, LANES), 1) // RWKV_HEAD_DIM
    return jnp.where(r == c, 1.0, 0.0).astype(BF16)


def _pad_rows(w, n):
    return jnp.pad(w, ((0, n - w.shape[0]), (0, 0)))


PREP_GROUP = 1024
MLSTM_PROJ_TILE = ML_COLS_P


def _rwkv_in_kernel(x_ref, n1_ref, w_ref, wm_ref, mu_ref, w0_ref, wup_ref, a0_ref, aup_ref, gup_ref, kk_ref,
                    ka_ref, hn_out, pm_out, r_out, k_out, v_out, kk_out, b_out, lw_out, g_out, carry_ref):
    i = pl.program_id(1)

    @pl.when(i == 0)
    def _():
        carry_ref[...] = jnp.zeros_like(carry_ref)

    x = x_ref[...]
    hn = (x * lax.rsqrt(jnp.mean(x * x, axis=-1, keepdims=True) + NORM_EPS) * n1_ref[...]).astype(BF16)
    hn_out[...] = hn
    tq = x.shape[0]
    row = lax.broadcasted_iota(jnp.int32, (tq, 1), 0)

    def shifted(cols):
        p = jnp.dot(hn, w_ref[:, cols], preferred_element_type=F32)
        prev = jnp.where(row == 0, carry_ref[0:1, cols], pltpu.roll(p, 1, 0))
        carry_ref[0:1, cols] = p[tq - 1:tq, :]
        return p + (prev - p) * mu_ref[:, cols]

    lora = shifted(slice(RW_WD, RW_COLS_P))
    wd_t = jnp.tanh(lora[:, 0:LORA_PAD]).astype(BF16)
    ad = lora[:, LORA_PAD:2 * LORA_PAD].astype(BF16)
    gd_s = _sigmoid(lora[:, 2 * LORA_PAD:]).astype(BF16)
    hs = _head_sum_mat()

    for c in range(0, RWKV_DIM, PREP_GROUP):
        cs = slice(c, c + PREP_GROUP)
        r = shifted(cs)
        k = shifted(slice(RWKV_DIM + c, RWKV_DIM + c + PREP_GROUP))
        v = shifted(slice(2 * RWKV_DIM + c, 2 * RWKV_DIM + c + PREP_GROUP))
        w_log = -_softplus(-(w0_ref[:, cs] + jnp.dot(wd_t, wup_ref[:, cs], preferred_element_type=F32))) - 0.5
        a = _sigmoid(a0_ref[:, cs] + jnp.dot(ad, aup_ref[:, cs], preferred_element_type=F32))
        kk = k * kk_ref[:, cs]
        nrm2 = jnp.concatenate(
            [_dot_exact_rhs(kk[:, t:t + LANES] * kk[:, t:t + LANES], hs) for t in range(0, PREP_GROUP, LANES)],
            axis=1)
        kk = kk / jnp.maximum(jnp.sqrt(nrm2), 1e-12)
        r_out[:, cs] = r.astype(r_out.dtype)
        k_out[:, cs] = (k * (1.0 + (a - 1.0) * ka_ref[:, cs])).astype(k_out.dtype)
        v_out[:, cs] = v.astype(v_out.dtype)
        kk_out[:, cs] = kk.astype(kk_out.dtype)
        b_out[:, cs] = (kk * a).astype(b_out.dtype)
        lw_out[:, cs] = -jnp.exp(w_log)
        g_out[:, cs] = jnp.dot(gd_s, gup_ref[:, cs], preferred_element_type=F32).astype(g_out.dtype)

    for c in range(0, ML_COLS_P, MLSTM_PROJ_TILE):
        cs = slice(c, c + MLSTM_PROJ_TILE)
        pm_out[:, cs] = jnp.dot(hn, wm_ref[:, cs], preferred_element_type=F32)


def rwkv_in(x, norm1_w, w_r, w_m, mu, w0, w_up, a0, a_up, g_up, k_k, k_a, tq=256):
    B, S, D = x.shape
    blk = lambda n: pl.BlockSpec((None, tq, n), lambda b, i: (b, i, 0))
    full = lambda a: pl.BlockSpec(a.shape, lambda b, i: (0,) * a.ndim, pipeline_mode=pl.Buffered(1))
    row = lambda t: t.reshape(1, -1)
    mu_p = jnp.concatenate([mu[:RW_WD], jnp.pad(mu[RW_WD:RW_WD + DECAY_LORA], (0, LORA_PAD - DECAY_LORA)),
                            jnp.pad(mu[RW_WD + DECAY_LORA:RW_WD + DECAY_LORA + AAA_LORA], (0, LORA_PAD - AAA_LORA)),
                            mu[RW_WD + DECAY_LORA + AAA_LORA:]])
    params = (row(norm1_w), w_r, w_m, row(mu_p), row(w0), _pad_rows(w_up, LORA_PAD).astype(BF16), row(a0),
              _pad_rows(a_up, LORA_PAD).astype(BF16), g_up.astype(BF16), row(k_k), row(k_a))
    out = lambda n, dt: jax.ShapeDtypeStruct((B, S, n), dt)
    return pl.pallas_call(
        _rwkv_in_kernel,
        grid=(B, S // tq),
        in_specs=[blk(D)] + [full(a) for a in params],
        out_specs=[blk(D), blk(ML_COLS_P)] + [blk(RWKV_DIM)] * 7,
        out_shape=[out(D, BF16), out(ML_COLS_P, F32)] + [out(RWKV_DIM, BF16)] * 5
        + [out(RWKV_DIM, F32), out(RWKV_DIM, BF16)],
        scratch_shapes=[pltpu.VMEM((SUBLANES, RW_COLS_P), F32)],
        compiler_params=_cparams(("parallel", "arbitrary")),
        name="rwkv_in",
    )(x, *params)


def _rwkv_scan_kernel(r_ref, k_ref, v_ref, kk_ref, b_ref, lw_ref, g_ref, rk_ref, lnw_ref, lnb_ref,
                      o_ref, h_ref):
    c = pl.program_id(1)

    @pl.when(c == 0)
    def _():
        h_ref[...] = jnp.zeros_like(h_ref)

    L = CHUNK
    L2 = 2 * L
    ri = lax.broadcasted_iota(jnp.int32, (L, L), 0)
    ci = lax.broadcasted_iota(jnp.int32, (L, L), 1)
    tril = jnp.where(ri >= ci, 1.0, 0.0).astype(BF16)

    n_seq = lw_ref.shape[0]
    cat = lambda ref: jnp.concatenate([ref[i] for i in range(n_seq)], axis=1)
    rep = lambda ref: jnp.concatenate([ref[...]] * n_seq, axis=1)
    lw = cat(lw_ref)
    cum = _dot_exact_lhs(tril, lw)
    cum_end = cum[L - 1:L, :]
    w_in = jnp.exp(cum)
    w_prev = jnp.exp(cum - lw)
    w_inv = jnp.exp(-cum)
    w_tail = jnp.exp(cum_end - cum)
    w_end = jnp.exp(cum_end)

    kk = cat(kk_ref).astype(F32)
    bb = cat(b_ref).astype(F32)
    kx = cat(k_ref).astype(F32)
    rx = cat(r_ref).astype(F32)
    vx = cat(v_ref).astype(F32)
    gx = cat(g_ref).astype(F32)
    a_hat = -kk * w_prev
    r_hat = rx * w_in
    b_hat = bb * w_inv
    k_hat = kx * w_inv
    b_til = bb * w_tail
    k_til = kx * w_tail
    rkk = rx * kx * rep(rk_ref)

    lane = lax.broadcasted_iota(jnp.int32, (1, LANES), 1)
    m_lo = jnp.where(lane < RWKV_HEAD_DIM, 1.0, 0.0)
    m_hi = 1.0 - m_lo

    def stack(x):
        return jnp.concatenate([x * m_lo, x * m_hi], axis=0)

    r2 = lax.broadcasted_iota(jnp.int32, (L2, L2), 0)
    c2 = lax.broadcasted_iota(jnp.int32, (L2, L2), 1)
    same_head = (r2 // L) == (c2 // L)
    strict = same_head & (r2 > c2)
    incl = same_head & (r2 >= c2)
    diag16 = (r2 // 16) == (c2 // 16)
    eye = jnp.where(r2 == c2, 1.0, 0.0)
    hs = _head_sum_mat()

    pairs_per_seq = RWKV_DIM // LANES
    pairs = range(n_seq * pairs_per_seq)
    sls = [slice(p * LANES, (p + 1) * LANES) for p in pairs]
    v_st = [stack(vx[:, sl]) for sl in sls]
    lhs = [jnp.concatenate([stack(a_hat[:, sl]), stack(r_hat[:, sl])], axis=0).astype(BF16) for sl in sls]
    rhs = [jnp.concatenate([stack(b_hat[:, sl]), stack(k_hat[:, sl])], axis=0).astype(BF16) for sl in sls]
    sc = [lax.dot_general(lhs[p], rhs[p], (((1,), (1,)), ((), ())), preferred_element_type=F32) for p in pairs]
    n_ab = [jnp.where(strict, sc[p][:L2, :L2], 0.0) for p in pairs]
    a_ak = [jnp.where(strict, sc[p][:L2, L2:], 0.0).astype(BF16) for p in pairs]
    a_r = [jnp.concatenate([jnp.where(incl, sc[p][L2:, :L2], 0.0), jnp.where(incl, sc[p][L2:, L2:], 0.0)],
                           axis=1).astype(BF16) for p in pairs]

    nd = [jnp.where(diag16, n_ab[p], 0.0) for p in pairs]
    noff = [(n_ab[p] - nd[p]).astype(BF16) for p in pairs]
    ndb = [nd[p].astype(BF16) for p in pairs]
    s2 = [jnp.dot(ndb[p], ndb[p], preferred_element_type=F32).astype(BF16) for p in pairs]
    s4 = [jnp.dot(s2[p], s2[p], preferred_element_type=F32).astype(BF16) for p in pairs]
    s8 = [jnp.dot(s4[p], s4[p], preferred_element_type=F32).astype(BF16) for p in pairs]
    x1 = [eye + nd[p] for p in pairs]
    x2 = [x1[p] + _bdot(x1[p], s2[p]) for p in pairs]
    x3 = [x2[p] + _bdot(x2[p], s4[p]) for p in pairs]
    t_d = [(x3[p] + _bdot(x3[p], s8[p])).astype(BF16) for p in pairs]
    m1 = [jnp.dot(t_d[p], noff[p], preferred_element_type=F32) for p in pairs]
    m1b = [m1[p].astype(BF16) for p in pairs]
    m2 = [jnp.dot(m1b[p], m1b[p], preferred_element_type=F32) for p in pairs]
    m3 = [jnp.dot(m1b[p], m2[p].astype(BF16), preferred_element_type=F32) for p in pairs]
    t_inv = [jnp.dot((eye + m1[p] + m2[p] + m3[p]).astype(BF16), t_d[p], preferred_element_type=F32).astype(BF16)
             for p in pairs]

    h0 = [h_ref[p] for p in pairs]
    ah = [jnp.dot(lhs[p], h0[p].astype(BF16), preferred_element_type=F32) for p in pairs]
    x = [ah[p][:L2] + jnp.dot(a_ak[p], v_st[p].astype(BF16), preferred_element_type=F32) for p in pairs]
    u = [jnp.dot(t_inv[p], x[p].astype(BF16), preferred_element_type=F32) for p in pairs]
    uv = [jnp.concatenate([u[p], v_st[p]], axis=0).astype(BF16) for p in pairs]
    y_st = [ah[p][L2:] + jnp.dot(a_r[p], uv[p], preferred_element_type=F32) for p in pairs]
    y = [y_st[p][:L] + y_st[p][L:] for p in pairs]

    for p in pairs:
        sl = sls[p]
        upd_l = jnp.concatenate([stack(b_til[:, sl]), stack(k_til[:, sl])], axis=0).astype(BF16)
        upd = lax.dot_general(upd_l, uv[p], (((0,), (0,)), ((), ())), preferred_element_type=F32)
        w_col = jnp.sum(eye * w_end[:, sl], axis=1, keepdims=True)
        h_ref[p] = w_col * h0[p] + upd

    def head_sums(vals):
        parts = []
        for t in vals:
            hi = t.astype(BF16)
            parts += [hi, (t - hi.astype(F32)).astype(BF16)]
        res = jnp.dot(jnp.concatenate(parts, axis=0), hs, preferred_element_type=F32)
        return [res[2 * i * L:(2 * i + 1) * L] + res[(2 * i + 1) * L:(2 * i + 2) * L] for i in range(len(vals))]

    sums1 = [head_sums([y[p], rkk[:, sls[p]]]) for p in pairs]
    d = [y[p] - sums1[p][0] * (1.0 / RWKV_HEAD_DIM) for p in pairs]
    var = [head_sums([d[p] * d[p]])[0] * (1.0 / RWKV_HEAD_DIM) for p in pairs]
    for p in pairs:
        sl = sls[p]
        psl = sls[p % pairs_per_seq]
        yn = d[p] * lax.rsqrt(var[p] + GN_EPS) * lnw_ref[:, psl] + lnb_ref[:, psl]
        o_ref[p // pairs_per_seq, :, psl] = ((yn + sums1[p][1] * vx[:, sl]) * gx[:, sl]).astype(o_ref.dtype)


def rwkv_scan(r, k, v, kk, b, lw, g, r_k, ln_w, ln_b):
    B, S, C = r.shape
    blk = pl.BlockSpec((SCAN_SEQS, CHUNK, C), lambda bb, c: (bb, c, 0))
    full = pl.BlockSpec((1, C), lambda bb, c: (0, 0))
    return pl.pallas_call(
        _rwkv_scan_kernel,
        grid=(B // SCAN_SEQS, S // CHUNK),
        in_specs=[blk] * 7 + [full] * 3,
        out_specs=blk,
        out_shape=jax.ShapeDtypeStruct((B, S, C), BF16),
        scratch_shapes=[pltpu.VMEM((SCAN_SEQS * C // LANES, LANES, LANES), F32)],
        compiler_params=_cparams(("parallel", "arbitrary")),
        name="rwkv_scan",
    )(r, k, v, kk, b, lw, g, r_k, ln_w, ln_b)


def _mlstm_kernel(p_ref, convw_ref, ifb_ref, nw_ref, o_ref, carry_ref, c_ref, n_ref, m_ref):
    ci = pl.program_id(1)

    @pl.when(ci == 0)
    def _():
        carry_ref[...] = jnp.zeros_like(carry_ref)
        c_ref[...] = jnp.zeros_like(c_ref)
        n_ref[...] = jnp.zeros_like(n_ref)
        m_ref[...] = jnp.zeros_like(m_ref)

    L = CHUNK
    n_seq = p_ref.shape[0]
    lane = lax.broadcasted_iota(jnp.int32, (1, LANES), 1)
    ri = lax.broadcasted_iota(jnp.int32, (L, L), 0)
    cj = lax.broadcasted_iota(jnp.int32, (L, L), 1)
    causal = ri >= cj
    tril = jnp.where(causal, 1.0, 0.0).astype(BF16)
    triu = jnp.where(ri <= cj, 1.0, 0.0).astype(BF16)

    qk, comb, comb_t, b_col, b_row = [], [], [], [], []
    for i in range(n_seq):
        u = p_ref[i, :, 0:ML_V]
        ext = jnp.concatenate([carry_ref[i], u], axis=0)
        carry_ref[i] = u[L - SUBLANES:L, :]
        conv = convw_ref[CONV_WIDTH - 1:CONV_WIDTH, :] * u
        for j in range(1, CONV_WIDTH):
            conv = conv + convw_ref[CONV_WIDTH - 1 - j:CONV_WIDTH - j, :] * pltpu.roll(ext, j, 0)[SUBLANES:, :]
        qk.append(conv * _sigmoid(conv))
        pre = GATE_SOFTCAP * jnp.tanh((p_ref[i, :, ML_IF:ML_O] + ifb_ref[...]) * (1.0 / GATE_SOFTCAP))
        cb = jnp.where(lane < MLSTM_HEADS, pre, -_softplus(-pre))
        comb.append(cb)
        comb_t.append(cb.T)
        b_col.append(_dot_exact_lhs(tril, cb))
        b_row.append(_dot_exact_rhs(comb_t[i], triu))

    H = range(n_seq * MLSTM_HEADS)
    sq = [v // MLSTM_HEADS for v in H]
    hd = [v % MLSTM_HEADS for v in H]
    dk, dv = MLSTM_QK_DIM, MLSTM_V_DIM
    qh = [qk[sq[v]][:, hd[v] * dk:(hd[v] + 1) * dk] * (dk ** -0.5) for v in H]
    kh = [qk[sq[v]][:, MLSTM_QK + hd[v] * dk:MLSTM_QK + (hd[v] + 1) * dk] for v in H]
    vh = [p_ref[sq[v], :, ML_V + hd[v] * dv:ML_V + (hd[v] + 1) * dv].astype(BF16) for v in H]
    qb = [qh[h].astype(BF16) for h in H]
    bcol = [b_col[sq[v]][:, MLSTM_HEADS + hd[v]:MLSTM_HEADS + hd[v] + 1] for v in H]
    brow = [b_row[sq[v]][MLSTM_HEADS + hd[v]:MLSTM_HEADS + hd[v] + 1, :] for v in H]
    m_prev = [m_ref[h][0:1, 0:1] for h in H]
    n_prev = [n_ref[h][0:1, :] for h in H]
    c_prev = [c_ref[h] for h in H]

    qk_t = [lax.dot_general(qb[h], kh[h].astype(BF16), (((1,), (1,)), ((), ())), preferred_element_type=F32)
            for h in H]
    qc = [jnp.dot(qb[h], c_prev[h].astype(BF16), preferred_element_type=F32) for h in H]
    dm = [jnp.where(causal, bcol[h] - brow[h] + comb_t[sq[h]][hd[h]:hd[h] + 1, :], -jnp.inf) for h in H]
    inter = [bcol[h] + m_prev[h] for h in H]
    m_t = [jnp.maximum(inter[h], jnp.max(dm[h], axis=-1, keepdims=True)) for h in H]
    s = [qk_t[h] * jnp.exp(dm[h] - m_t[h]) for h in H]
    w_inter = [jnp.exp(inter[h] - m_t[h]) for h in H]
    num = [jnp.dot(s[h].astype(BF16), vh[h], preferred_element_type=F32) + w_inter[h] * qc[h] for h in H]
    den = [jnp.sum(s[h], axis=-1, keepdims=True) + w_inter[h] * jnp.sum(qh[h] * n_prev[h], axis=-1, keepdims=True)
           for h in H]
    hh = [num[h] / jnp.maximum(jnp.abs(den[h]), jnp.exp(-m_t[h])) for h in H]

    g_tot = [bcol[h][L - 1:L, :] for h in H]
    a = [comb[sq[h]][:, hd[h]:hd[h] + 1] + g_tot[h] - bcol[h] for h in H]
    m_new = [jnp.maximum(g_tot[h] + m_prev[h], jnp.max(a[h], axis=0, keepdims=True)) for h in H]
    dec = [jnp.exp(g_tot[h] + m_prev[h] - m_new[h]) for h in H]
    wkk = [jnp.exp(a[h] - m_new[h]) * kh[h] for h in H]
    for h in H:
        c_ref[h] = dec[h] * c_prev[h] + lax.dot_general(wkk[h].astype(BF16), vh[h], (((0,), (0,)), ((), ())),
                                                        preferred_element_type=F32)
        n_ref[h] = jnp.broadcast_to(dec[h] * n_prev[h] + jnp.sum(wkk[h], axis=0, keepdims=True),
                                    (SUBLANES, LANES))
        m_ref[h] = jnp.broadcast_to(m_new[h], (SUBLANES, LANES))
    for h in H:
        vs = slice(hd[h] * dv, (hd[h] + 1) * dv)
        hn = hh[h] * lax.rsqrt(jnp.mean(hh[h] * hh[h], axis=-1, keepdims=True) + NORM_EPS)
        o_raw = p_ref[sq[h], :, ML_O + hd[h] * dv:ML_O + (hd[h] + 1) * dv]
        o_ref[sq[h], :, vs] = (hn * nw_ref[:, vs] * _sigmoid(o_raw)).astype(o_ref.dtype)


def mlstm_branch(p_pad, conv_w, i_b, f_b, norm_w):
    B, S, C = p_pad.shape
    ifb = jnp.pad(jnp.concatenate([i_b, f_b]), (0, LANES - 2 * MLSTM_HEADS)).reshape(1, LANES)
    full = lambda a: pl.BlockSpec(a.shape, lambda b, c: (0,) * a.ndim)
    nw = norm_w.reshape(1, MLSTM_DIM)
    return pl.pallas_call(
        _mlstm_kernel,
        grid=(B // MLSTM_SEQS, S // CHUNK),
        in_specs=[pl.BlockSpec((MLSTM_SEQS, CHUNK, C), lambda b, c: (b, c, 0)), full(conv_w), full(ifb), full(nw)],
        out_specs=pl.BlockSpec((MLSTM_SEQS, CHUNK, MLSTM_DIM), lambda b, c: (b, c, 0)),
        out_shape=jax.ShapeDtypeStruct((B, S, MLSTM_DIM), BF16),
        scratch_shapes=[pltpu.VMEM((MLSTM_SEQS, SUBLANES, ML_V), F32),
                        pltpu.VMEM((MLSTM_SEQS * MLSTM_HEADS, MLSTM_QK_DIM, MLSTM_V_DIM), F32),
                        pltpu.VMEM((MLSTM_SEQS * MLSTM_HEADS, SUBLANES, LANES), F32),
                        pltpu.VMEM((MLSTM_SEQS * MLSTM_HEADS, SUBLANES, LANES), F32)],
        compiler_params=_cparams(("parallel", "arbitrary")),
        name="mlstm_scan",
    )(p_pad, conv_w, ifb, nw)


HI_MASK = 0xFFFF0000


def _pack_bf16_pairs(hb_f32):
    c = hb_f32.shape[1] // 2
    u = pltpu.bitcast(hb_f32, jnp.uint32)
    return u[:, c:] | (u[:, :c] >> 16)


ROW_TILES = D_MODEL // 2 // LANES


def _store_tiled(ref, val, first_row=0):
    n = val.shape[0]
    for t in range(ROW_TILES):
        ref[pl.ds(first_row * ROW_TILES + t, n, stride=ROW_TILES), :] = val[:, t * LANES:(t + 1) * LANES]


def _load_tiled(ref, n, first_row=0):
    return jnp.concatenate([ref[pl.ds(first_row * ROW_TILES + t, n, stride=ROW_TILES), :] for t in range(ROW_TILES)],
                           axis=1)


def _unpack_bf16_pairs(xu):
    lo = pltpu.bitcast(xu << 16, F32).astype(BF16)
    hi = pltpu.bitcast(xu & jnp.uint32(HI_MASK), F32).astype(BF16)
    return lo, hi


def _merge_kernel(oa_ref, ob_ref, hn_ref, bg_ref, x_ref, wa_ref, wb_ref, wg_ref, wo_ref, n2_ref, rwh_ref, rwl_ref,
                  rb_ref, x1_ref, xp_ref, lg_ref):
    hn1 = hn_ref[...]

    def gated(o_ref, w_ref, cols):
        gate = _sigmoid(jnp.dot(hn1, wg_ref[:, cols], preferred_element_type=F32) + bg_ref[:, cols])
        return gate * jnp.dot(o_ref[...], w_ref[...], preferred_element_type=F32)

    merged = gated(oa_ref, wa_ref, slice(0, D_MODEL)) + gated(ob_ref, wb_ref, slice(D_MODEL, 2 * D_MODEL))
    x1 = x_ref[...] + jnp.dot(merged.astype(BF16), wo_ref[...], preferred_element_type=F32)
    x1_ref[...] = x1
    hn = x1 * lax.rsqrt(jnp.mean(x1 * x1, axis=-1, keepdims=True) + NORM_EPS) * n2_ref[...]
    hi = hn.astype(BF16)
    _store_tiled(xp_ref, _pack_bf16_pairs(hi.astype(F32)))
    lo = (hn - hi.astype(F32)).astype(BF16)
    lg_ref[...] = (jnp.dot(hi, rwh_ref[...], preferred_element_type=F32)
                   + jnp.dot(hi, rwl_ref[...], preferred_element_type=F32)
                   + jnp.dot(lo, rwh_ref[...], preferred_element_type=F32) + rb_ref[...])


def merge_project(o_a, o_b, hn, w_gate, b_gate, x, w_a, w_b, w_out, norm2_w, router_w, router_b, tm=256):
    T, D = x.shape
    rw = jnp.pad(router_w, ((0, 0), (0, ROUTER_PAD - N_EXPERTS)))
    rw_hi = rw.astype(BF16)
    rw_lo = (rw - rw_hi.astype(F32)).astype(BF16)
    rb = jnp.pad(router_b, (0, ROUTER_PAD - N_EXPERTS)).reshape(1, ROUTER_PAD)
    rows = lambda n: pl.BlockSpec((tm, n), lambda i: (i, 0))
    full = lambda a: pl.BlockSpec(a.shape, lambda i: (0, 0), pipeline_mode=pl.Buffered(1))
    params = (w_a.astype(BF16), w_b.astype(BF16), w_gate, w_out.astype(BF16), norm2_w.reshape(1, D), rw_hi, rw_lo,
              rb)
    bg = b_gate.reshape(1, 2 * D)
    return pl.pallas_call(
        _merge_kernel,
        grid=(T // tm,),
        in_specs=[rows(o_a.shape[1]), rows(o_b.shape[1]), rows(D), full(bg), rows(D)] + [full(a) for a in params],
        out_specs=[rows(D), pl.BlockSpec((tm * ROW_TILES, LANES), lambda i: (i, 0)), rows(ROUTER_PAD)],
        out_shape=[jax.ShapeDtypeStruct((T, D), F32), jax.ShapeDtypeStruct((T * ROW_TILES, LANES), jnp.uint32),
                   jax.ShapeDtypeStruct((T, ROUTER_PAD), F32)],
        compiler_params=_cparams(("parallel",)),
        name="merge_project",
    )(o_a, o_b, hn, bg, x, *params)


UP_TILE = 1024
DOWN_TILE = 2048
RANK_GROUP = 256
DISPATCH_TOKENS = 512
COMBINE_TOKENS = 256


def _row_copy(src, dst, src_at, dst_at, sem):
    return pltpu.make_async_copy(src.at[pl.ds(pl.multiple_of(src_at, ROW_TILES), ROW_TILES), :],
                                 dst.at[pl.ds(pl.multiple_of(dst_at, ROW_TILES), ROW_TILES), :], sem)


def _dispatch_kernel(zrow_ref, dest_ref, xp_ref, xs_hbm, zbuf, zsem, sem):
    i = pl.program_id(0)
    tq = dest_ref.shape[2] // TOP_K

    def zero_copy(k):
        at = pl.multiple_of(zrow_ref[k], ROW_BLOCK * ROW_TILES)
        return pltpu.make_async_copy(zbuf, xs_hbm.at[pl.ds(at, ROW_BLOCK * ROW_TILES), :], zsem)

    @pl.when(i == 0)
    def _():
        zbuf[...] = jnp.zeros_like(zbuf)
        for k in range(zrow_ref.shape[0]):
            pl.when(zrow_ref[k] >= 0)(lambda k=k: zero_copy(k).start())
        for k in range(zrow_ref.shape[0]):
            pl.when(zrow_ref[k] >= 0)(lambda k=k: zero_copy(k).wait())

    def body(q, carry):
        base = pl.multiple_of(q * SUBLANES, SUBLANES)
        for r in range(SUBLANES):
            for j in range(TOP_K):
                _row_copy(xp_ref, xs_hbm, (base + r) * ROW_TILES, dest_ref[0, 0, TOP_K * (base + r) + j],
                          sem).start(priority=j % 2)
        return carry

    lax.fori_loop(0, tq // SUBLANES, body, 0)
    for j in range(TOP_K):
        pltpu.make_async_copy(xp_ref, xs_hbm.at[pl.ds(0, tq * ROW_TILES), :], sem).wait()


def moe_dispatch(xp, dest, zrow, n_rows):
    T = xp.shape[0] // ROW_TILES
    tq = DISPATCH_TOKENS
    dest_blk = (dest * ROW_TILES).reshape(T // tq, 1, TOP_K * tq)
    zrow = jnp.where(zrow >= 0, zrow * ROW_TILES, -1)
    return pl.pallas_call(
        _dispatch_kernel,
        grid_spec=pltpu.PrefetchScalarGridSpec(
            num_scalar_prefetch=1,
            grid=(T // tq,),
            in_specs=[pl.BlockSpec((1, 1, TOP_K * tq), lambda i, zr: (i, 0, 0), memory_space=pltpu.SMEM),
                      pl.BlockSpec((tq * ROW_TILES, LANES), lambda i, zr: (i, 0))],
            out_specs=pl.BlockSpec(memory_space=pl.ANY),
            scratch_shapes=[pltpu.VMEM((ROW_BLOCK * ROW_TILES, LANES), xp.dtype), pltpu.SemaphoreType.DMA(()),
                            pltpu.SemaphoreType.DMA(())]),
        out_shape=jax.ShapeDtypeStruct((n_rows * ROW_TILES, LANES), xp.dtype),
        compiler_params=_cparams(("arbitrary",)),
        name="moe_dispatch",
    )(zrow, dest_blk, xp)


SCHED_NV, SCHED_SG, SCHED_SB, SCHED_SO, SCHED_PARTS, SCHED_GE, SCHED_GF, SCHED_NG = range(8)


def _stream_weights(s, sched, copies, on_arrival):
    sg_ref = sched[SCHED_SG]
    g = sg_ref[s]
    first = (s < sched[SCHED_NV][0]) & ((s == 0) | (g != sg_ref[jnp.maximum(s - 1, 0)]))

    @pl.when(first)
    def _():
        @pl.when(s == 0)
        def _():
            for c in copies(g):
                c.start()

        for c in copies(g):
            c.wait()
        on_arrival()

        @pl.when(g + 1 < sched[SCHED_NG][0])
        def _():
            for c in copies(g + 1):
                c.start()


def _for_used_rows(s, sched, out_ref, compute):
    parts = sched[SCHED_PARTS][s]
    rows_per_row = out_ref.shape[0] // ROW_BLOCK

    @pl.when(s >= sched[SCHED_NV][0])
    def _():
        out_ref[...] = jnp.zeros_like(out_ref)

    def variant(rows):
        compute(rows)
        if rows < ROW_BLOCK:
            rest = rows * rows_per_row
            out_ref[rest:, :] = jnp.zeros((out_ref.shape[0] - rest, out_ref.shape[1]), out_ref.dtype)

    for p in range(1, ROW_BLOCK // ROW_PART + 1):
        pl.when((s < sched[SCHED_NV][0]) & (parts == p))(functools.partial(variant, p * ROW_PART))


def _moe_up_kernel(*refs):
    sched = refs[:8]
    xs_ref, w_hbm, bg_ref, bu_ref, h_ref, wbuf, wgb, wub, sems = refs[8:]
    s = pl.program_id(0)
    tf = wgb.shape[1]
    n_ff = w_hbm.shape[2] // 2

    def copies(g):
        e = sched[SCHED_GE][g]
        col = pl.multiple_of(sched[SCHED_GF][g] * tf, tf)
        return [pltpu.make_async_copy(w_hbm.at[e, :, pl.ds(half * n_ff + col, tf)], wbuf.at[half], sems.at[half])
                for half in range(2)]

    def on_arrival():
        wgb[...] = wbuf[0].astype(BF16)
        wub[...] = wbuf[1].astype(BF16)

    _stream_weights(s, sched, copies, on_arrival)

    def compute(rows):
        lo, hi = _unpack_bf16_pairs(_load_tiled(xs_ref, rows))
        half = lo.shape[1]

        def proj(wb, b_ref):
            return (jnp.dot(lo, wb[:half, :], preferred_element_type=F32)
                    + jnp.dot(hi, wb[half:, :], preferred_element_type=F32) + b_ref[...])

        gate = jnp.minimum(proj(wgb, bg_ref), SWIGLU_LIMIT)
        up = jnp.clip(proj(wub, bu_ref), -SWIGLU_LIMIT, SWIGLU_LIMIT)
        h_ref[:rows, :] = ((up + 1.0) * gate * _sigmoid(SWIGLU_ALPHA * gate)).astype(h_ref.dtype)

    _for_used_rows(s, sched, h_ref, compute)


def _step_expert(s, r):
    return r[SCHED_GE][r[SCHED_SG][s]]


def _step_tile(s, r):
    return r[SCHED_GF][r[SCHED_SG][s]]


def moe_up(xs, w_gu, b_gu, sched):
    P = xs.shape[0] // ROW_TILES
    E, D, F2 = w_gu.shape
    F = F2 // 2
    tf = UP_TILE
    nf = F // tf
    b3 = b_gu.reshape(E, 1, F2)
    return pl.pallas_call(
        _moe_up_kernel,
        grid_spec=pltpu.PrefetchScalarGridSpec(
            num_scalar_prefetch=len(sched),
            grid=(sched[SCHED_SG].shape[0],),
            in_specs=[pl.BlockSpec((ROW_BLOCK * ROW_TILES, LANES), lambda s, *r: (r[SCHED_SB][s], 0)),
                      pl.BlockSpec(memory_space=pl.ANY),
                      pl.BlockSpec((None, 1, tf), lambda s, *r: (_step_expert(s, r), 0, _step_tile(s, r))),
                      pl.BlockSpec((None, 1, tf), lambda s, *r: (_step_expert(s, r), 0, nf + _step_tile(s, r)))],
            out_specs=pl.BlockSpec((ROW_BLOCK, tf), lambda s, *r: (r[SCHED_SB][s], r[SCHED_SO][s])),
            scratch_shapes=[pltpu.VMEM((2, D, tf), F32), pltpu.VMEM((D, tf), BF16), pltpu.VMEM((D, tf), BF16),
                            pltpu.SemaphoreType.DMA((2,))]),
        out_shape=jax.ShapeDtypeStruct((P, F), BF16),
        compiler_params=_cparams(("arbitrary",)),
        name="moe_up",
    )(*sched, xs, w_gu, b3, b3)


def _moe_down_kernel(*refs):
    sched = refs[:8]
    h_ref, w_hbm, bd_ref, y_ref, wbuf, wdb, sem = refs[8:]
    s = pl.program_id(0)
    tn = wdb.shape[1]

    def copies(g):
        col = pl.multiple_of(sched[SCHED_GF][g] * tn, tn)
        return [pltpu.make_async_copy(w_hbm.at[sched[SCHED_GE][g], :, pl.ds(col, tn)], wbuf, sem)]

    def on_arrival():
        wdb[...] = wbuf[...].astype(BF16)

    _stream_weights(s, sched, copies, on_arrival)

    def compute(rows):
        y = jnp.dot(h_ref[:rows, :], wdb[...], preferred_element_type=F32) + bd_ref[...]
        _store_tiled(y_ref, _pack_bf16_pairs(y.astype(BF16).astype(F32)))

    _for_used_rows(s, sched, y_ref, compute)


def moe_down(h, w_down, b_down, sched):
    P, F = h.shape
    E, _, D = w_down.shape
    tn = DOWN_TILE
    assert tn == D, "the packed output pairs column c with column c + D/2"
    b3 = b_down.reshape(E, 1, D)
    return pl.pallas_call(
        _moe_down_kernel,
        grid_spec=pltpu.PrefetchScalarGridSpec(
            num_scalar_prefetch=len(sched),
            grid=(sched[SCHED_SG].shape[0],),
            in_specs=[pl.BlockSpec((ROW_BLOCK, F), lambda s, *r: (r[SCHED_SB][s], 0)),
                      pl.BlockSpec(memory_space=pl.ANY),
                      pl.BlockSpec((None, 1, tn), lambda s, *r: (_step_expert(s, r), 0, _step_tile(s, r)))],
            out_specs=pl.BlockSpec((ROW_BLOCK * ROW_TILES, LANES), lambda s, *r: (r[SCHED_SB][s], 0)),
            scratch_shapes=[pltpu.VMEM((F, tn), F32), pltpu.VMEM((F, tn), BF16), pltpu.SemaphoreType.DMA(())]),
        out_shape=jax.ShapeDtypeStruct((P * ROW_TILES, LANES), jnp.uint32),
        compiler_params=_cparams(("arbitrary",)),
        name="moe_down",
    )(*sched, h, w_down, b3)


def _combine_kernel(dcur_ref, dnxt_ref, x1_ref, w_ref, fw_ref, ys_hbm, o_ref, buf, sems):
    i = pl.program_id(0)
    n_steps = pl.num_programs(0)
    tq = x1_ref.shape[0]
    n = TOP_K * tq
    slot = lax.rem(i, 2)

    def issue(idx_ref, sl):
        def body(q, carry):
            base = pl.multiple_of(q * SUBLANES, SUBLANES)
            for r in range(SUBLANES):
                _row_copy(ys_hbm, buf.at[sl], idx_ref[0, 0, base + r], (base + r) * ROW_TILES,
                          sems.at[sl]).start(priority=r % 2)
            return carry

        lax.fori_loop(0, n // SUBLANES, body, 0)

    def wait_slot(sl):
        pltpu.make_async_copy(ys_hbm.at[pl.ds(0, n * ROW_TILES), :], buf.at[sl], sems.at[sl]).wait()

    pl.when(i == 0)(lambda: issue(dcur_ref, 0))
    wait_slot(slot)
    for t in range(n):
        _row_copy(ys_hbm, buf.at[1 - slot], dnxt_ref[0, 0, t], t * ROW_TILES,
                  sems.at[1 - slot]).start(priority=t % 2)

    half = ROW_TILES * LANES
    acc_lo = x1_ref[:, :half]
    acc_hi = x1_ref[:, half:]
    for j in range(TOP_K):
        lo, hi = _unpack_bf16_pairs(_load_tiled(buf.at[slot], tq, first_row=j * tq))
        acc_lo = acc_lo + w_ref[:, j:j + 1] * lo.astype(F32)
        acc_hi = acc_hi + w_ref[:, j:j + 1] * hi.astype(F32)
    ssq = jnp.sum(acc_lo * acc_lo, axis=-1, keepdims=True) + jnp.sum(acc_hi * acc_hi, axis=-1, keepdims=True)
    scale = lax.rsqrt(ssq * (1.0 / (2 * half)) + NORM_EPS)
    o_ref[:, :half] = acc_lo * scale * fw_ref[:, :half]
    o_ref[:, half:] = acc_hi * scale * fw_ref[:, half:]
    pl.when(i == n_steps - 1)(lambda: wait_slot(1 - slot))


def combine(x1, ys, dest, top_w, final_w):
    T, D = x1.shape
    tq = COMBINE_TOKENS
    nblk = T // tq
    dest_blk = (dest * ROW_TILES).reshape(nblk, tq, TOP_K).transpose(0, 2, 1).reshape(nblk, 1, TOP_K * tq)
    idx_spec = lambda f: pl.BlockSpec((1, 1, TOP_K * tq), f, memory_space=pltpu.SMEM)
    return pl.pallas_call(
        _combine_kernel,
        grid=(nblk,),
        in_specs=[idx_spec(lambda i: (i, 0, 0)),
                  idx_spec(lambda i: (jnp.minimum(i + 1, nblk - 1), 0, 0)),
                  pl.BlockSpec((tq, D), lambda i: (i, 0)),
                  pl.BlockSpec((tq, TOP_K), lambda i: (i, 0)),
                  pl.BlockSpec((1, D), lambda i: (0, 0)),
                  pl.BlockSpec(memory_space=pl.ANY)],
        out_specs=pl.BlockSpec((tq, D), lambda i: (i, 0)),
        out_shape=jax.ShapeDtypeStruct((T, D), F32),
        scratch_shapes=[pltpu.VMEM((2, TOP_K * tq * ROW_TILES, LANES), ys.dtype), pltpu.SemaphoreType.DMA((2,))],
        compiler_params=_cparams(("arbitrary",)),
        name="moe_combine",
    )(dest_blk, dest_blk, x1, top_w, final_w.reshape(1, D), ys)


def _routing(logits):
    T = logits.shape[0]
    TK = T * TOP_K
    NB = TK // ROW_BLOCK + N_EXPERTS
    top_logits, top_idx = lax.top_k(logits[:, :N_EXPERTS], TOP_K)
    top_w = jax.nn.softmax(top_logits, axis=-1)
    experts = jnp.arange(N_EXPERTS, dtype=jnp.int32)
    onehot = (top_idx.reshape(TK, 1) == experts[None, :]).astype(F32)
    oh = onehot.reshape(TK // RANK_GROUP, RANK_GROUP, N_EXPERTS)
    local = jnp.einsum("ts,gse->gte", jnp.tril(jnp.ones((RANK_GROUP, RANK_GROUP), F32)), oh)
    tot = local[:, -1, :]
    offs = jnp.cumsum(tot, axis=0) - tot
    counts = (offs[-1] + tot[-1]).astype(jnp.int32)
    padded = (counts + ROW_BLOCK - 1) // ROW_BLOCK * ROW_BLOCK
    pad_end = jnp.cumsum(padded).astype(jnp.int32)
    pad_start = pad_end - padded
    dest = jnp.sum(oh * (local + (offs + (pad_start.astype(F32) - 1.0)[None, :])[:, None, :]), axis=-1)
    dest = dest.reshape(T, TOP_K).astype(jnp.int32)

    blk = jnp.arange(NB, dtype=jnp.int32)
    block_e = jnp.minimum(jnp.sum((pad_end[None, :] <= (blk * ROW_BLOCK)[:, None]).astype(jnp.int32), axis=1),
                          N_EXPERTS - 1)
    of_block = lambda table: jnp.sum(jnp.where(block_e[:, None] == experts[None, :], table[None, :], 0), axis=1)
    nb_used = pad_end[-1] // ROW_BLOCK
    rows_left = of_block(counts) - (blk - of_block(pad_start // ROW_BLOCK)) * ROW_BLOCK
    block_parts = jnp.where(blk < nb_used, jnp.clip((rows_left + ROW_PART - 1) // ROW_PART, 1, ROW_BLOCK // ROW_PART), 0)
    used = padded > 0
    block_pos = of_block(jnp.cumsum(used.astype(jnp.int32)) - 1)
    tail = nb_used + experts
    zrow = jnp.concatenate([jnp.where(used, pad_end - ROW_BLOCK, -1),
                            jnp.where(tail < NB, tail * ROW_BLOCK, -1)]).astype(jnp.int32)
    return dest, top_w, (block_e, block_parts, block_pos, used, nb_used), zrow


def _schedule(blocks, n_tiles):
    block_e, block_parts, block_pos, used, nb_used = blocks
    nb = block_e.shape[0]
    b = jnp.asarray(np.tile(np.arange(nb, dtype=np.int32), n_tiles))
    f = jnp.asarray(np.repeat(np.arange(n_tiles, dtype=np.int32), nb))
    rep = lambda a: jnp.tile(a, n_tiles)
    n_groups = (jnp.sum(used) * n_tiles).astype(jnp.int32)
    n_valid = (nb_used * n_tiles).astype(jnp.int32)
    group = jnp.minimum(rep(block_pos) * n_tiles + f, n_groups - 1)
    key = jnp.where(b < nb_used, (rep(block_e) * n_tiles + f) * nb + b, (N_EXPERTS * n_tiles + f) * nb + b)
    _, group, sb, so, parts = lax.sort((key, group, b, f, rep(block_parts)), num_keys=1)
    sg = jnp.where(jnp.arange(nb * n_tiles) < n_valid, group, n_groups - 1)
    experts = jnp.arange(N_EXPERTS, dtype=jnp.int32)
    used_first = jnp.argsort(jnp.where(used, experts, N_EXPERTS + experts)).astype(jnp.int32)
    ge = jnp.repeat(used_first, n_tiles)
    gf = jnp.tile(jnp.arange(n_tiles, dtype=jnp.int32), N_EXPERTS)
    return (n_valid.reshape(1), sg, sb, so, parts, ge, gf, n_groups.reshape(1))


REPACK_ROWS = 256


def _repack_kernel(w_ref, wr_ref, wm_ref, wg_ref):
    w = w_ref[...]
    z = lambda n: jnp.zeros((w.shape[0], n), w.dtype)
    o = RWKV_COLS
    a, b = RW_WD + DECAY_LORA, RW_WD + DECAY_LORA + AAA_LORA
    wr_ref[...] = jnp.concatenate([w[:, :a], z(LORA_PAD - DECAY_LORA), w[:, a:b], z(LORA_PAD - AAA_LORA),
                                   w[:, b:o]], axis=1).astype(wr_ref.dtype)
    g = o + ML_IF + 2 * MLSTM_HEADS
    wm_ref[...] = jnp.concatenate([w[:, o:g], z(LANES - 2 * MLSTM_HEADS), w[:, g:o + MLSTM_COLS]],
                                  axis=1).astype(wm_ref.dtype)
    wg_ref[...] = w[:, o + MLSTM_COLS:].astype(wg_ref.dtype)


def _repack_w_in(w_in):
    D, C = w_in.shape
    blk = lambda n: pl.BlockSpec((REPACK_ROWS, n), lambda i: (i, 0))
    n_gate = C - RWKV_COLS - MLSTM_COLS
    return pl.pallas_call(
        _repack_kernel,
        grid=(D // REPACK_ROWS,),
        in_specs=[blk(C)],
        out_specs=[blk(RW_COLS_P), blk(ML_COLS_P), blk(n_gate)],
        out_shape=[jax.ShapeDtypeStruct((D, n), BF16) for n in (RW_COLS_P, ML_COLS_P, n_gate)],
        compiler_params=_cparams(("parallel",)),
        name="repack_w_in",
    )(w_in)


def kernel(x, norm1_w, w_in, b_gate, rwkv_mu, rwkv_w0, rwkv_w_up, rwkv_a0, rwkv_a_up, rwkv_g_up, rwkv_k_k,
           rwkv_k_a, rwkv_r_k, rwkv_ln_w, rwkv_ln_b, mlstm_conv_w, mlstm_i_b, mlstm_f_b, mlstm_norm_w,
           w_branch_a, w_branch_b, w_out, norm2_w, router_w, router_b, w_gu, b_gu, w_down, b_down,
           final_norm_w):
    B, S, D = x.shape
    T = B * S
    xt = x.reshape(T, D)
    assert norm1_w.shape[0] == 1, "single-layer block: the final rmsnorm is fused into the MoE combine"
    for l in range(1):
        w_r, w_m, w_g = _repack_w_in(w_in[l])
        hn, p_m, *scan_in = rwkv_in(x, norm1_w[l], w_r, w_m, rwkv_mu[l], rwkv_w0[l], rwkv_w_up[l], rwkv_a0[l],
                                    rwkv_a_up[l], rwkv_g_up[l], rwkv_k_k[l], rwkv_k_a[l])
        hn = hn.reshape(T, D)
        o_a = rwkv_scan(*scan_in, rwkv_r_k[l].reshape(1, -1), rwkv_ln_w[l].reshape(1, -1),
                        rwkv_ln_b[l].reshape(1, -1))
        o_b = mlstm_branch(p_m, mlstm_conv_w[l], mlstm_i_b[l], mlstm_f_b[l], mlstm_norm_w[l])
        x1, xp, logits = merge_project(o_a.reshape(T, RWKV_DIM), o_b.reshape(T, MLSTM_DIM), hn, w_g, b_gate[l], xt,
                                       w_branch_a[l], w_branch_b[l], w_out[l], norm2_w[l], router_w[l],
                                       router_b[l])
        dest, top_w, blocks, zrow = _routing(logits)
        xs = moe_dispatch(xp, dest, zrow, blocks[0].shape[0] * ROW_BLOCK)
        h = moe_up(xs, w_gu[l], b_gu[l], _schedule(blocks, EXPERT_FF // UP_TILE))
        ys = moe_down(h, w_down[l], b_down[l], _schedule(blocks, D // DOWN_TILE))
        xt = combine(x1, ys, dest, top_w, final_norm_w)
    return xt.reshape(B, S, D)
```

```python
import functools

import jax
import jax.numpy as jnp
import numpy as np
from jax import lax
from jax.experimental import pallas as pl
from jax.experimental.pallas import tpu as pltpu

F32 = jnp.float32
BF16 = jnp.bfloat16

D_MODEL = 2048
CHUNK = 64
NORM_EPS = 1e-6
RWKV_HEADS = 16
RWKV_HEAD_DIM = 64
RWKV_DIM = 1024
DECAY_LORA = 96
AAA_LORA = 96
GATE_LORA = 256
GN_EPS = 64e-5
RWKV_COLS = 3 * RWKV_DIM + DECAY_LORA + AAA_LORA + GATE_LORA
MLSTM_HEADS = 4
MLSTM_QK_DIM = 128
MLSTM_V_DIM = 256
MLSTM_QK = 512
MLSTM_DIM = 1024
CONV_WIDTH = 4
GATE_SOFTCAP = 15.0
MLSTM_COLS = 2 * MLSTM_QK + 2 * MLSTM_DIM + 2 * MLSTM_HEADS
N_EXPERTS = 32
TOP_K = 4
EXPERT_FF = 2048
SWIGLU_LIMIT = 7.0
SWIGLU_ALPHA = 1.702

LANES = 128
SUBLANES = 8
VMEM_LIMIT = 56 * 1024 * 1024

LORA_PAD = 128
RW_WD = 3 * RWKV_DIM
RW_AD = RW_WD + LORA_PAD
RW_GD = RW_AD + LORA_PAD
RW_COLS_P = RW_GD + GATE_LORA
ML_V = 2 * MLSTM_QK
ML_IF = ML_V + MLSTM_DIM
ML_O = ML_IF + LANES
ML_COLS_P = ML_O + MLSTM_DIM
ROUTER_PAD = 128
SCAN_SEQS = 4
MLSTM_SEQS = 1

ROW_BLOCK = 512
ROW_PART = 128


def _cparams(sem):
    return pltpu.CompilerParams(dimension_semantics=sem, vmem_limit_bytes=VMEM_LIMIT)


def _bdot(a, b):
    return jnp.dot(a.astype(BF16), b.astype(BF16), preferred_element_type=F32)


def _split3(x):
    hi = x.astype(BF16)
    r1 = x - hi.astype(F32)
    mid = r1.astype(BF16)
    lo = (r1 - mid.astype(F32)).astype(BF16)
    return hi, mid, lo


def _dot_exact_lhs(mat_bf16, x):
    hi, mid, lo = _split3(x)
    return (jnp.dot(mat_bf16, hi, preferred_element_type=F32)
            + jnp.dot(mat_bf16, mid, preferred_element_type=F32)
            + jnp.dot(mat_bf16, lo, preferred_element_type=F32))


def _dot_exact_rhs(x, mat_bf16):
    hi, mid, lo = _split3(x)
    return (jnp.dot(hi, mat_bf16, preferred_element_type=F32)
            + jnp.dot(mid, mat_bf16, preferred_element_type=F32)
            + jnp.dot(lo, mat_bf16, preferred_element_type=F32))


def _sigmoid(x):
    return 1.0 / (1.0 + jnp.exp(-x))


def _softplus(x):
    return jnp.maximum(x, 0.0) + jnp.log(1.0 + jnp.exp(-jnp.abs(x)))


def _head_sum_mat():
    r = lax.broadcasted_iota(jnp.int32, (LANES, LANES), 0) // RWKV_HEAD_DIM
    c = lax.broadcasted_iota(jnp.int32, (LANES, LANES), 1) // RWKV_HEAD_DIM
    return jnp.where(r == c, 1.0, 0.0).astype(BF16)


def _pad_rows(w, n):
    return jnp.pad(w, ((0, n - w.shape[0]), (0, 0)))


PREP_GROUP = 1024
MLSTM_PROJ_TILE = ML_COLS_P


def _rwkv_in_kernel(x_ref, n1_ref, w_ref, wm_ref, mu_ref, w0_ref, wup_ref, a0_ref, aup_ref, gup_ref, kk_ref,
                    ka_ref, hn_out, pm_out, r_out, k_out, v_out, kk_out, b_out, lw_out, g_out, carry_ref):
    i = pl.program_id(1)

    @pl.when(i == 0)
    def _():
        carry_ref[...] = jnp.zeros_like(carry_ref)

    x = x_ref[...]
    hn = (x * lax.rsqrt(jnp.mean(x * x, axis=-1, keepdims=True) + NORM_EPS) * n1_ref[...]).astype(BF16)
    hn_out[...] = hn
    tq = x.shape[0]
    row = lax.broadcasted_iota(jnp.int32, (tq, 1), 0)

    def shifted(cols):
        p = jnp.dot(hn, w_ref[:, cols], preferred_element_type=F32)
        prev = jnp.where(row == 0, carry_ref[0:1, cols], pltpu.roll(p, 1, 0))
        carry_ref[0:1, cols] = p[tq - 1:tq, :]
        return p + (prev - p) * mu_ref[:, cols]

    lora = shifted(slice(RW_WD, RW_COLS_P))
    wd_t = jnp.tanh(lora[:, 0:LORA_PAD]).astype(BF16)
    ad = lora[:, LORA_PAD:2 * LORA_PAD].astype(BF16)
    gd_s = _sigmoid(lora[:, 2 * LORA_PAD:]).astype(BF16)
    hs = _head_sum_mat()

    for c in range(0, RWKV_DIM, PREP_GROUP):
        cs = slice(c, c + PREP_GROUP)
        r = shifted(cs)
        k = shifted(slice(RWKV_DIM + c, RWKV_DIM + c + PREP_GROUP))
        v = shifted(slice(2 * RWKV_DIM + c, 2 * RWKV_DIM + c + PREP_GROUP))
        w_log = -_softplus(-(w0_ref[:, cs] + jnp.dot(wd_t, wup_ref[:, cs], preferred_element_type=F32))) - 0.5
        a = _sigmoid(a0_ref[:, cs] + jnp.dot(ad, aup_ref[:, cs], preferred_element_type=F32))
        kk = k * kk_ref[:, cs]
        nrm2 = jnp.concatenate(
            [_dot_exact_rhs(kk[:, t:t + LANES] * kk[:, t:t + LANES], hs) for t in range(0, PREP_GROUP, LANES)],
            axis=1)
        kk = kk / jnp.maximum(jnp.sqrt(nrm2), 1e-12)
        r_out[:, cs] = r.astype(r_out.dtype)
        k_out[:, cs] = (k * (1.0 + (a - 1.0) * ka_ref[:, cs])).astype(k_out.dtype)
        v_out[:, cs] = v.astype(v_out.dtype)
        kk_out[:, cs] = kk.astype(kk_out.dtype)
        b_out[:, cs] = (kk * a).astype(b_out.dtype)
        lw_out[:, cs] = -jnp.exp(w_log)
        g_out[:, cs] = jnp.dot(gd_s, gup_ref[:, cs], preferred_element_type=F32).astype(g_out.dtype)

    for c in range(0, ML_COLS_P, MLSTM_PROJ_TILE):
        cs = slice(c, c + MLSTM_PROJ_TILE)
        pm_out[:, cs] = jnp.dot(hn, wm_ref[:, cs], preferred_element_type=F32)


def rwkv_in(x, norm1_w, w_r, w_m, mu, w0, w_up, a0, a_up, g_up, k_k, k_a, tq=256):
    B, S, D = x.shape
    blk = lambda n: pl.BlockSpec((None, tq, n), lambda b, i: (b, i, 0))
    full = lambda a: pl.BlockSpec(a.shape, lambda b, i: (0,) * a.ndim, pipeline_mode=pl.Buffered(1))
    row = lambda t: t.reshape(1, -1)
    mu_p = jnp.concatenate([mu[:RW_WD], jnp.pad(mu[RW_WD:RW_WD + DECAY_LORA], (0, LORA_PAD - DECAY_LORA)),
                            jnp.pad(mu[RW_WD + DECAY_LORA:RW_WD + DECAY_LORA + AAA_LORA], (0, LORA_PAD - AAA_LORA)),
                            mu[RW_WD + DECAY_LORA + AAA_LORA:]])
    params = (row(norm1_w), w_r, w_m, row(mu_p), row(w0), _pad_rows(w_up, LORA_PAD).astype(BF16), row(a0),
              _pad_rows(a_up, LORA_PAD).astype(BF16), g_up.astype(BF16), row(k_k), row(k_a))
    out = lambda n, dt: jax.ShapeDtypeStruct((B, S, n), dt)
    return pl.pallas_call(
        _rwkv_in_kernel,
        grid=(B, S // tq),
        in_specs=[blk(D)] + [full(a) for a in params],
        out_specs=[blk(D), blk(ML_COLS_P)] + [blk(RWKV_DIM)] * 7,
        out_shape=[out(D, BF16), out(ML_COLS_P, F32)] + [out(RWKV_DIM, BF16)] * 5
        + [out(RWKV_DIM, F32), out(RWKV_DIM, BF16)],
        scratch_shapes=[pltpu.VMEM((SUBLANES, RW_COLS_P), F32)],
        compiler_params=_cparams(("parallel", "arbitrary")),
        name="rwkv_in",
    )(x, *params)


def _rwkv_scan_kernel(r_ref, k_ref, v_ref, kk_ref, b_ref, lw_ref, g_ref, rk_ref, lnw_ref, lnb_ref,
                      o_ref, h_ref):
    c = pl.program_id(1)

    @pl.when(c == 0)
    def _():
        h_ref[...] = jnp.zeros_like(h_ref)

    L = CHUNK
    L2 = 2 * L
    ri = lax.broadcasted_iota(jnp.int32, (L, L), 0)
    ci = lax.broadcasted_iota(jnp.int32, (L, L), 1)
    tril = jnp.where(ri >= ci, 1.0, 0.0).astype(BF16)

    n_seq = lw_ref.shape[0]
    cat = lambda ref: jnp.concatenate([ref[i] for i in range(n_seq)], axis=1)
    rep = lambda ref: jnp.concatenate([ref[...]] * n_seq, axis=1)
    lw = cat(lw_ref)
    cum = _dot_exact_lhs(tril, lw)
    cum_end = cum[L - 1:L, :]
    w_in = jnp.exp(cum)
    w_prev = jnp.exp(cum - lw)
    w_inv = jnp.exp(-cum)
    w_tail = jnp.exp(cum_end - cum)
    w_end = jnp.exp(cum_end)

    kk = cat(kk_ref).astype(F32)
    bb = cat(b_ref).astype(F32)
    kx = cat(k_ref).astype(F32)
    rx = cat(r_ref).astype(F32)
    vx = cat(v_ref).astype(F32)
    gx = cat(g_ref).astype(F32)
    a_hat = -kk * w_prev
    r_hat = rx * w_in
    b_hat = bb * w_inv
    k_hat = kx * w_inv
    b_til = bb * w_tail
    k_til = kx * w_tail
    rkk = rx * kx * rep(rk_ref)

    lane = lax.broadcasted_iota(jnp.int32, (1, LANES), 1)
    m_lo = jnp.where(lane < RWKV_HEAD_DIM, 1.0, 0.0)
    m_hi = 1.0 - m_lo

    def stack(x):
        return jnp.concatenate([x * m_lo, x * m_hi], axis=0)

    r2 = lax.broadcasted_iota(jnp.int32, (L2, L2), 0)
    c2 = lax.broadcasted_iota(jnp.int32, (L2, L2), 1)
    same_head = (r2 // L) == (c2 // L)
    strict = same_head & (r2 > c2)
    incl = same_head & (r2 >= c2)
    diag16 = (r2 // 16) == (c2 // 16)
    eye = jnp.where(r2 == c2, 1.0, 0.0)
    hs = _head_sum_mat()

    pairs_per_seq = RWKV_DIM // LANES
    pairs = range(n_seq * pairs_per_seq)
    sls = [slice(p * LANES, (p + 1) * LANES) for p in pairs]
    v_st = [stack(vx[:, sl]) for sl in sls]
    lhs = [jnp.concatenate([stack(a_hat[:, sl]), stack(r_hat[:, sl])], axis=0).astype(BF16) for sl in sls]
    rhs = [jnp.concatenate([stack(b_hat[:, sl]), stack(k_hat[:, sl])], axis=0).astype(BF16) for sl in sls]
    sc = [lax.dot_general(lhs[p], rhs[p], (((1,), (1,)), ((), ())), preferred_element_type=F32) for p in pairs]
    n_ab = [jnp.where(strict, sc[p][:L2, :L2], 0.0) for p in pairs]
    a_ak = [jnp.where(strict, sc[p][:L2, L2:], 0.0).astype(BF16) for p in pairs]
    a_r = [jnp.concatenate([jnp.where(incl, sc[p][L2:, :L2], 0.0), jnp.where(incl, sc[p][L2:, L2:], 0.0)],
                           axis=1).astype(BF16) for p in pairs]

    nd = [jnp.where(diag16, n_ab[p], 0.0) for p in pairs]
    noff = [(n_ab[p] - nd[p]).astype(BF16) for p in pairs]
    ndb = [nd[p].astype(BF16) for p in pairs]
    s2 = [jnp.dot(ndb[p], ndb[p], preferred_element_type=F32).astype(BF16) for p in pairs]
    s4 = [jnp.dot(s2[p], s2[p], preferred_element_type=F32).astype(BF16) for p in pairs]
    s8 = [jnp.dot(s4[p], s4[p], preferred_element_type=F32).astype(BF16) for p in pairs]
    x1 = [eye + nd[p] for p in pairs]
    x2 = [x1[p] + _bdot(x1[p], s2[p]) for p in pairs]
    x3 = [x2[p] + _bdot(x2[p], s4[p]) for p in pairs]
    t_d = [(x3[p] + _bdot(x3[p], s8[p])).astype(BF16) for p in pairs]
    m1 = [jnp.dot(t_d[p], noff[p], preferred_element_type=F32) for p in pairs]
    m1b = [m1[p].astype(BF16) for p in pairs]
    m2 = [jnp.dot(m1b[p], m1b[p], preferred_element_type=F32) for p in pairs]
    m3 = [jnp.dot(m1b[p], m2[p].astype(BF16), preferred_element_type=F32) for p in pairs]
    t_inv = [jnp.dot((eye + m1[p] + m2[p] + m3[p]).astype(BF16), t_d[p], preferred_element_type=F32).astype(BF16)
             for p in pairs]

    h0 = [h_ref[p] for p in pairs]
    ah = [jnp.dot(lhs[p], h0[p].astype(BF16), preferred_element_type=F32) for p in pairs]
    x = [ah[p][:L2] + jnp.dot(a_ak[p], v_st[p].astype(BF16), preferred_element_type=F32) for p in pairs]
    u = [jnp.dot(t_inv[p], x[p].astype(BF16), preferred_element_type=F32) for p in pairs]
    uv = [jnp.concatenate([u[p], v_st[p]], axis=0).astype(BF16) for p in pairs]
    y_st = [ah[p][L2:] + jnp.dot(a_r[p], uv[p], preferred_element_type=F32) for p in pairs]
    y = [y_st[p][:L] + y_st[p][L:] for p in pairs]

    for p in pairs:
        sl = sls[p]
        upd_l = jnp.concatenate([stack(b_til[:, sl]), stack(k_til[:, sl])], axis=0).astype(BF16)
        upd = lax.dot_general(upd_l, uv[p], (((0,), (0,)), ((), ())), preferred_element_type=F32)
        w_col = jnp.sum(eye * w_end[:, sl], axis=1, keepdims=True)
        h_ref[p] = w_col * h0[p] + upd

    def head_sums(vals):
        parts = []
        for t in vals:
            hi = t.astype(BF16)
            parts += [hi, (t - hi.astype(F32)).astype(BF16)]
        res = jnp.dot(jnp.concatenate(parts, axis=0), hs, preferred_element_type=F32)
        return [res[2 * i * L:(2 * i + 1) * L] + res[(2 * i + 1) * L:(2 * i + 2) * L] for i in range(len(vals))]

    sums1 = [head_sums([y[p], rkk[:, sls[p]]]) for p in pairs]
    d = [y[p] - sums1[p][0] * (1.0 / RWKV_HEAD_DIM) for p in pairs]
    var = [head_sums([d[p] * d[p]])[0] * (1.0 / RWKV_HEAD_DIM) for p in pairs]
    for p in pairs:
        sl = sls[p]
        psl = sls[p % pairs_per_seq]
        yn = d[p] * lax.rsqrt(var[p] + GN_EPS) * lnw_ref[:, psl] + lnb_ref[:, psl]
        o_ref[p // pairs_per_seq, :, psl] = ((yn + sums1[p][1] * vx[:, sl]) * gx[:, sl]).astype(o_ref.dtype)


def rwkv_scan(r, k, v, kk, b, lw, g, r_k, ln_w, ln_b):
    B, S, C = r.shape
    blk = pl.BlockSpec((SCAN_SEQS, CHUNK, C), lambda bb, c: (bb, c, 0))
    full = pl.BlockSpec((1, C), lambda bb, c: (0, 0))
    return pl.pallas_call(
        _rwkv_scan_kernel,
        grid=(B // SCAN_SEQS, S // CHUNK),
        in_specs=[blk] * 7 + [full] * 3,
        out_specs=blk,
        out_shape=jax.ShapeDtypeStruct((B, S, C), BF16),
        scratch_shapes=[pltpu.VMEM((SCAN_SEQS * C // LANES, LANES, LANES), F32)],
        compiler_params=_cparams(("parallel", "arbitrary")),
        name="rwkv_scan",
    )(r, k, v, kk, b, lw, g, r_k, ln_w, ln_b)


def _mlstm_kernel(p_ref, convw_ref, ifb_ref, nw_ref, o_ref, carry_ref, c_ref, n_ref, m_ref):
    ci = pl.program_id(1)

    @pl.when(ci == 0)
    def _():
        carry_ref[...] = jnp.zeros_like(carry_ref)
        c_ref[...] = jnp.zeros_like(c_ref)
        n_ref[...] = jnp.zeros_like(n_ref)
        m_ref[...] = jnp.zeros_like(m_ref)

    L = CHUNK
    n_seq = p_ref.shape[0]
    lane = lax.broadcasted_iota(jnp.int32, (1, LANES), 1)
    ri = lax.broadcasted_iota(jnp.int32, (L, L), 0)
    cj = lax.broadcasted_iota(jnp.int32, (L, L), 1)
    causal = ri >= cj
    tril = jnp.where(causal, 1.0, 0.0).astype(BF16)
    triu = jnp.where(ri <= cj, 1.0, 0.0).astype(BF16)

    qk, comb, comb_t, b_col, b_row = [], [], [], [], []
    for i in range(n_seq):
        u = p_ref[i, :, 0:ML_V]
        ext = jnp.concatenate([carry_ref[i], u], axis=0)
        carry_ref[i] = u[L - SUBLANES:L, :]
        conv = convw_ref[CONV_WIDTH - 1:CONV_WIDTH, :] * u
        for j in range(1, CONV_WIDTH):
            conv = conv + convw_ref[CONV_WIDTH - 1 - j:CONV_WIDTH - j, :] * pltpu.roll(ext, j, 0)[SUBLANES:, :]
        qk.append(conv * _sigmoid(conv))
        pre = GATE_SOFTCAP * jnp.tanh((p_ref[i, :, ML_IF:ML_O] + ifb_ref[...]) * (1.0 / GATE_SOFTCAP))
        cb = jnp.where(lane < MLSTM_HEADS, pre, -_softplus(-pre))
        comb.append(cb)
        comb_t.append(cb.T)
        b_col.append(_dot_exact_lhs(tril, cb))
        b_row.append(_dot_exact_rhs(comb_t[i], triu))

    H = range(n_seq * MLSTM_HEADS)
    sq = [v // MLSTM_HEADS for v in H]
    hd = [v % MLSTM_HEADS for v in H]
    dk, dv = MLSTM_QK_DIM, MLSTM_V_DIM
    qh = [qk[sq[v]][:, hd[v] * dk:(hd[v] + 1) * dk] * (dk ** -0.5) for v in H]
    kh = [qk[sq[v]][:, MLSTM_QK + hd[v] * dk:MLSTM_QK + (hd[v] + 1) * dk] for v in H]
    vh = [p_ref[sq[v], :, ML_V + hd[v] * dv:ML_V + (hd[v] + 1) * dv].astype(BF16) for v in H]
    qb = [qh[h].astype(BF16) for h in H]
    bcol = [b_col[sq[v]][:, MLSTM_HEADS + hd[v]:MLSTM_HEADS + hd[v] + 1] for v in H]
    brow = [b_row[sq[v]][MLSTM_HEADS + hd[v]:MLSTM_HEADS + hd[v] + 1, :] for v in H]
    m_prev = [m_ref[h][0:1, 0:1] for h in H]
    n_prev = [n_ref[h][0:1, :] for h in H]
    c_prev = [c_ref[h] for h in H]

    qk_t = [lax.dot_general(qb[h], kh[h].astype(BF16), (((1,), (1,)), ((), ())), preferred_element_type=F32)
            for h in H]
    qc = [jnp.dot(qb[h], c_prev[h].astype(BF16), preferred_element_type=F32) for h in H]
    dm = [jnp.where(causal, bcol[h] - brow[h] + comb_t[sq[h]][hd[h]:hd[h] + 1, :], -jnp.inf) for h in H]
    inter = [bcol[h] + m_prev[h] for h in H]
    m_t = [jnp.maximum(inter[h], jnp.max(dm[h], axis=-1, keepdims=True)) for h in H]
    s = [qk_t[h] * jnp.exp(dm[h] - m_t[h]) for h in H]
    w_inter = [jnp.exp(inter[h] - m_t[h]) for h in H]
    num = [jnp.dot(s[h].astype(BF16), vh[h], preferred_element_type=F32) + w_inter[h] * qc[h] for h in H]
    den = [jnp.sum(s[h], axis=-1, keepdims=True) + w_inter[h] * jnp.sum(qh[h] * n_prev[h], axis=-1, keepdims=True)
           for h in H]
    hh = [num[h] / jnp.maximum(jnp.abs(den[h]), jnp.exp(-m_t[h])) for h in H]

    g_tot = [bcol[h][L - 1:L, :] for h in H]
    a = [comb[sq[h]][:, hd[h]:hd[h] + 1] + g_tot[h] - bcol[h] for h in H]
    m_new = [jnp.maximum(g_tot[h] + m_prev[h], jnp.max(a[h], axis=0, keepdims=True)) for h in H]
    dec = [jnp.exp(g_tot[h] + m_prev[h] - m_new[h]) for h in H]
    wkk = [jnp.exp(a[h] - m_new[h]) * kh[h] for h in H]
    for h in H:
        c_ref[h] = dec[h] * c_prev[h] + lax.dot_general(wkk[h].astype(BF16), vh[h], (((0,), (0,)), ((), ())),
                                                        preferred_element_type=F32)
        n_ref[h] = jnp.broadcast_to(dec[h] * n_prev[h] + jnp.sum(wkk[h], axis=0, keepdims=True),
                                    (SUBLANES, LANES))
        m_ref[h] = jnp.broadcast_to(m_new[h], (SUBLANES, LANES))
    for h in H:
        vs = slice(hd[h] * dv, (hd[h] + 1) * dv)
        hn = hh[h] * lax.rsqrt(jnp.mean(hh[h] * hh[h], axis=-1, keepdims=True) + NORM_EPS)
        o_raw = p_ref[sq[h], :, ML_O + hd[h] * dv:ML_O + (hd[h] + 1) * dv]
        o_ref[sq[h], :, vs] = (hn * nw_ref[:, vs] * _sigmoid(o_raw)).astype(o_ref.dtype)


def mlstm_branch(p_pad, conv_w, i_b, f_b, norm_w):
    B, S, C = p_pad.shape
    ifb = jnp.pad(jnp.concatenate([i_b, f_b]), (0, LANES - 2 * MLSTM_HEADS)).reshape(1, LANES)
    full = lambda a: pl.BlockSpec(a.shape, lambda b, c: (0,) * a.ndim)
    nw = norm_w.reshape(1, MLSTM_DIM)
    return pl.pallas_call(
        _mlstm_kernel,
        grid=(B // MLSTM_SEQS, S // CHUNK),
        in_specs=[pl.BlockSpec((MLSTM_SEQS, CHUNK, C), lambda b, c: (b, c, 0)), full(conv_w), full(ifb), full(nw)],
        out_specs=pl.BlockSpec((MLSTM_SEQS, CHUNK, MLSTM_DIM), lambda b, c: (b, c, 0)),
        out_shape=jax.ShapeDtypeStruct((B, S, MLSTM_DIM), BF16),
        scratch_shapes=[pltpu.VMEM((MLSTM_SEQS, SUBLANES, ML_V), F32),
                        pltpu.VMEM((MLSTM_SEQS * MLSTM_HEADS, MLSTM_QK_DIM, MLSTM_V_DIM), F32),
                        pltpu.VMEM((MLSTM_SEQS * MLSTM_HEADS, SUBLANES, LANES), F32),
                        pltpu.VMEM((MLSTM_SEQS * MLSTM_HEADS, SUBLANES, LANES), F32)],
        compiler_params=_cparams(("parallel", "arbitrary")),
        name="mlstm_scan",
    )(p_pad, conv_w, ifb, nw)


HI_MASK = 0xFFFF0000


def _pack_bf16_pairs(hb_f32):
    c = hb_f32.shape[1] // 2
    u = pltpu.bitcast(hb_f32, jnp.uint32)
    return u[:, c:] | (u[:, :c] >> 16)


ROW_TILES = D_MODEL // 2 // LANES


def _store_tiled(ref, val, first_row=0):
    n = val.shape[0]
    for t in range(ROW_TILES):
        ref[pl.ds(first_row * ROW_TILES + t, n, stride=ROW_TILES), :] = val[:, t * LANES:(t + 1) * LANES]


def _load_tiled(ref, n, first_row=0):
    return jnp.concatenate([ref[pl.ds(first_row * ROW_TILES + t, n, stride=ROW_TILES), :] for t in range(ROW_TILES)],
                           axis=1)


def _unpack_bf16_pairs(xu):
    lo = pltpu.bitcast(xu << 16, F32).astype(BF16)
    hi = pltpu.bitcast(xu & jnp.uint32(HI_MASK), F32).astype(BF16)
    return lo, hi


def _merge_kernel(oa_ref, ob_ref, hn_ref, bg_ref, x_ref, wa_ref, wb_ref, wg_ref, wo_ref, n2_ref, rwh_ref, rwl_ref,
                  rb_ref, x1_ref, xp_ref, lg_ref):
    hn1 = hn_ref[...]

    def gated(o_ref, w_ref, cols):
        gate = _sigmoid(jnp.dot(hn1, wg_ref[:, cols], preferred_element_type=F32) + bg_ref[:, cols])
        return gate * jnp.dot(o_ref[...], w_ref[...], preferred_element_type=F32)

    merged = gated(oa_ref, wa_ref, slice(0, D_MODEL)) + gated(ob_ref, wb_ref, slice(D_MODEL, 2 * D_MODEL))
    x1 = x_ref[...] + jnp.dot(merged.astype(BF16), wo_ref[...], preferred_element_type=F32)
    x1_ref[...] = x1
    hn = x1 * lax.rsqrt(jnp.mean(x1 * x1, axis=-1, keepdims=True) + NORM_EPS) * n2_ref[...]
    hi = hn.astype(BF16)
    _store_tiled(xp_ref, _pack_bf16_pairs(hi.astype(F32)))
    lo = (hn - hi.astype(F32)).astype(BF16)
    lg_ref[...] = (jnp.dot(hi, rwh_ref[...], preferred_element_type=F32)
                   + jnp.dot(hi, rwl_ref[...], preferred_element_type=F32)
                   + jnp.dot(lo, rwh_ref[...], preferred_element_type=F32) + rb_ref[...])


def merge_project(o_a, o_b, hn, w_gate, b_gate, x, w_a, w_b, w_out, norm2_w, router_w, router_b, tm=256):
    T, D = x.shape
    rw = jnp.pad(router_w, ((0, 0), (0, ROUTER_PAD - N_EXPERTS)))
    rw_hi = rw.astype(BF16)
    rw_lo = (rw - rw_hi.astype(F32)).astype(BF16)
    rb = jnp.pad(router_b, (0, ROUTER_PAD - N_EXPERTS)).reshape(1, ROUTER_PAD)
    rows = lambda n: pl.BlockSpec((tm, n), lambda i: (i, 0))
    full = lambda a: pl.BlockSpec(a.shape, lambda i: (0, 0), pipeline_mode=pl.Buffered(1))
    params = (w_a.astype(BF16), w_b.astype(BF16), w_gate, w_out.astype(BF16), norm2_w.reshape(1, D), rw_hi, rw_lo,
              rb)
    bg = b_gate.reshape(1, 2 * D)
    return pl.pallas_call(
        _merge_kernel,
        grid=(T // tm,),
        in_specs=[rows(o_a.shape[1]), rows(o_b.shape[1]), rows(D), full(bg), rows(D)] + [full(a) for a in params],
        out_specs=[rows(D), pl.BlockSpec((tm * ROW_TILES, LANES), lambda i: (i, 0)), rows(ROUTER_PAD)],
        out_shape=[jax.ShapeDtypeStruct((T, D), F32), jax.ShapeDtypeStruct((T * ROW_TILES, LANES), jnp.uint32),
                   jax.ShapeDtypeStruct((T, ROUTER_PAD), F32)],
        compiler_params=_cparams(("parallel",)),
        name="merge_project",
    )(o_a, o_b, hn, bg, x, *params)


UP_TILE = 1024
DOWN_TILE = 2048
RANK_GROUP = 256
DISPATCH_TOKENS = 1024
COMBINE_TOKENS = 512


def _row_copy(src, dst, src_at, dst_at, sem):
    return pltpu.make_async_copy(src.at[pl.ds(pl.multiple_of(src_at, ROW_TILES), ROW_TILES), :],
                                 dst.at[pl.ds(pl.multiple_of(dst_at, ROW_TILES), ROW_TILES), :], sem)


def _dispatch_kernel(zrow_ref, dest_ref, xp_ref, xs_hbm, zbuf, zsem, sem):
    i = pl.program_id(0)
    tq = dest_ref.shape[2] // TOP_K

    def zero_copy(k):
        at = pl.multiple_of(zrow_ref[k], ROW_BLOCK * ROW_TILES)
        return pltpu.make_async_copy(zbuf, xs_hbm.at[pl.ds(at, ROW_BLOCK * ROW_TILES), :], zsem)

    @pl.when(i == 0)
    def _():
        zbuf[...] = jnp.zeros_like(zbuf)
        for k in range(zrow_ref.shape[0]):
            pl.when(zrow_ref[k] >= 0)(lambda k=k: zero_copy(k).start())
        for k in range(zrow_ref.shape[0]):
            pl.when(zrow_ref[k] >= 0)(lambda k=k: zero_copy(k).wait())

    def body(q, carry):
        base = pl.multiple_of(q * SUBLANES, SUBLANES)
        for r in range(SUBLANES):
            for j in range(TOP_K):
                _row_copy(xp_ref, xs_hbm, (base + r) * ROW_TILES, dest_ref[0, 0, TOP_K * (base + r) + j],
                          sem).start(priority=j % 2)
        return carry

    lax.fori_loop(0, tq // SUBLANES, body, 0)
    for j in range(TOP_K):
        pltpu.make_async_copy(xp_ref, xs_hbm.at[pl.ds(0, tq * ROW_TILES), :], sem).wait()


def moe_dispatch(xp, dest, zrow, n_rows):
    T = xp.shape[0] // ROW_TILES
    tq = DISPATCH_TOKENS
    dest_blk = (dest * ROW_TILES).reshape(T // tq, 1, TOP_K * tq)
    zrow = jnp.where(zrow >= 0, zrow * ROW_TILES, -1)
    return pl.pallas_call(
        _dispatch_kernel,
        grid_spec=pltpu.PrefetchScalarGridSpec(
            num_scalar_prefetch=1,
            grid=(T // tq,),
            in_specs=[pl.BlockSpec((1, 1, TOP_K * tq), lambda i, zr: (i, 0, 0), memory_space=pltpu.SMEM),
                      pl.BlockSpec((tq * ROW_TILES, LANES), lambda i, zr: (i, 0))],
            out_specs=pl.BlockSpec(memory_space=pl.ANY),
            scratch_shapes=[pltpu.VMEM((ROW_BLOCK * ROW_TILES, LANES), xp.dtype), pltpu.SemaphoreType.DMA(()),
                            pltpu.SemaphoreType.DMA(())]),
        out_shape=jax.ShapeDtypeStruct((n_rows * ROW_TILES, LANES), xp.dtype),
        compiler_params=_cparams(("arbitrary",)),
        name="moe_dispatch",
    )(zrow, dest_blk, xp)


SCHED_NV, SCHED_SG, SCHED_SB, SCHED_SO, SCHED_PARTS, SCHED_GE, SCHED_GF, SCHED_NG = range(8)


def _stream_weights(s, sched, copies, on_arrival):
    sg_ref = sched[SCHED_SG]
    g = sg_ref[s]
    first = (s < sched[SCHED_NV][0]) & ((s == 0) | (g != sg_ref[jnp.maximum(s - 1, 0)]))

    @pl.when(first)
    def _():
        @pl.when(s == 0)
        def _():
            for c in copies(g):
                c.start()

        for c in copies(g):
            c.wait()
        on_arrival()

        @pl.when(g + 1 < sched[SCHED_NG][0])
        def _():
            for c in copies(g + 1):
                c.start()


def _for_used_rows(s, sched, out_ref, compute):
    parts = sched[SCHED_PARTS][s]
    rows_per_row = out_ref.shape[0] // ROW_BLOCK

    @pl.when(s >= sched[SCHED_NV][0])
    def _():
        out_ref[...] = jnp.zeros_like(out_ref)

    def variant(rows):
        compute(rows)
        if rows < ROW_BLOCK:
            rest = rows * rows_per_row
            out_ref[rest:, :] = jnp.zeros((out_ref.shape[0] - rest, out_ref.shape[1]), out_ref.dtype)

    for p in range(1, ROW_BLOCK // ROW_PART + 1):
        pl.when((s < sched[SCHED_NV][0]) & (parts == p))(functools.partial(variant, p * ROW_PART))


def _moe_up_kernel(*refs):
    sched = refs[:8]
    xs_ref, w_hbm, bg_ref, bu_ref, h_ref, wbuf, wgb, wub, sems = refs[8:]
    s = pl.program_id(0)
    tf = wgb.shape[1]
    n_ff = w_hbm.shape[2] // 2

    def copies(g):
        e = sched[SCHED_GE][g]
        col = pl.multiple_of(sched[SCHED_GF][g] * tf, tf)
        return [pltpu.make_async_copy(w_hbm.at[e, :, pl.ds(half * n_ff + col, tf)], wbuf.at[half], sems.at[half])
                for half in range(2)]

    def on_arrival():
        wgb[...] = wbuf[0].astype(BF16)
        wub[...] = wbuf[1].astype(BF16)

    _stream_weights(s, sched, copies, on_arrival)

    def compute(rows):
        lo, hi = _unpack_bf16_pairs(_load_tiled(xs_ref, rows))
        half = lo.shape[1]

        def proj(wb, b_ref):
            return (jnp.dot(lo, wb[:half, :], preferred_element_type=F32)
                    + jnp.dot(hi, wb[half:, :], preferred_element_type=F32) + b_ref[...])

        gate = jnp.minimum(proj(wgb, bg_ref), SWIGLU_LIMIT)
        up = jnp.clip(proj(wub, bu_ref), -SWIGLU_LIMIT, SWIGLU_LIMIT)
        h_ref[:rows, :] = ((up + 1.0) * gate * _sigmoid(SWIGLU_ALPHA * gate)).astype(h_ref.dtype)

    _for_used_rows(s, sched, h_ref, compute)


def _step_expert(s, r):
    return r[SCHED_GE][r[SCHED_SG][s]]


def _step_tile(s, r):
    return r[SCHED_GF][r[SCHED_SG][s]]


def moe_up(xs, w_gu, b_gu, sched):
    P = xs.shape[0] // ROW_TILES
    E, D, F2 = w_gu.shape
    F = F2 // 2
    tf = UP_TILE
    nf = F // tf
    b3 = b_gu.reshape(E, 1, F2)
    return pl.pallas_call(
        _moe_up_kernel,
        grid_spec=pltpu.PrefetchScalarGridSpec(
            num_scalar_prefetch=len(sched),
            grid=(sched[SCHED_SG].shape[0],),
            in_specs=[pl.BlockSpec((ROW_BLOCK * ROW_TILES, LANES), lambda s, *r: (r[SCHED_SB][s], 0)),
                      pl.BlockSpec(memory_space=pl.ANY),
                      pl.BlockSpec((None, 1, tf), lambda s, *r: (_step_expert(s, r), 0, _step_tile(s, r))),
                      pl.BlockSpec((None, 1, tf), lambda s, *r: (_step_expert(s, r), 0, nf + _step_tile(s, r)))],
            out_specs=pl.BlockSpec((ROW_BLOCK, tf), lambda s, *r: (r[SCHED_SB][s], r[SCHED_SO][s])),
            scratch_shapes=[pltpu.VMEM((2, D, tf), F32), pltpu.VMEM((D, tf), BF16), pltpu.VMEM((D, tf), BF16),
                            pltpu.SemaphoreType.DMA((2,))]),
        out_shape=jax.ShapeDtypeStruct((P, F), BF16),
        compiler_params=_cparams(("arbitrary",)),
        name="moe_up",
    )(*sched, xs, w_gu, b3, b3)


def _moe_down_kernel(*refs):
    sched = refs[:8]
    h_ref, w_hbm, bd_ref, y_ref, wbuf, wdb, sem = refs[8:]
    s = pl.program_id(0)
    tn = wdb.shape[1]

    def copies(g):
        col = pl.multiple_of(sched[SCHED_GF][g] * tn, tn)
        return [pltpu.make_async_copy(w_hbm.at[sched[SCHED_GE][g], :, pl.ds(col, tn)], wbuf, sem)]

    def on_arrival():
        wdb[...] = wbuf[...].astype(BF16)

    _stream_weights(s, sched, copies, on_arrival)

    def compute(rows):
        y = jnp.dot(h_ref[:rows, :], wdb[...], preferred_element_type=F32) + bd_ref[...]
        _store_tiled(y_ref, _pack_bf16_pairs(y.astype(BF16).astype(F32)))

    _for_used_rows(s, sched, y_ref, compute)


def moe_down(h, w_down, b_down, sched):
    P, F = h.shape
    E, _, D = w_down.shape
    tn = DOWN_TILE
    assert tn == D, "the packed output pairs column c with column c + D/2"
    b3 = b_down.reshape(E, 1, D)
    return pl.pallas_call(
        _moe_down_kernel,
        grid_spec=pltpu.PrefetchScalarGridSpec(
            num_scalar_prefetch=len(sched),
            grid=(sched[SCHED_SG].shape[0],),
            in_specs=[pl.BlockSpec((ROW_BLOCK, F), lambda s, *r: (r[SCHED_SB][s], 0)),
                      pl.BlockSpec(memory_space=pl.ANY),
                      pl.BlockSpec((None, 1, tn), lambda s, *r: (_step_expert(s, r), 0, _step_tile(s, r)))],
            out_specs=pl.BlockSpec((ROW_BLOCK * ROW_TILES, LANES), lambda s, *r: (r[SCHED_SB][s], 0)),
            scratch_shapes=[pltpu.VMEM((F, tn), F32), pltpu.VMEM((F, tn), BF16), pltpu.SemaphoreType.DMA(())]),
        out_shape=jax.ShapeDtypeStruct((P * ROW_TILES, LANES), jnp.uint32),
        compiler_params=_cparams(("arbitrary",)),
        name="moe_down",
    )(*sched, h, w_down, b3)


def _combine_kernel(dcur_ref, dnxt_ref, x1_ref, w_ref, fw_ref, ys_hbm, o_ref, buf, sems):
    i = pl.program_id(0)
    n_steps = pl.num_programs(0)
    tq = x1_ref.shape[0]
    n = TOP_K * tq
    slot = lax.rem(i, 2)

    def issue(idx_ref, sl):
        def body(q, carry):
            base = pl.multiple_of(q * SUBLANES, SUBLANES)
            for r in range(SUBLANES):
                _row_copy(ys_hbm, buf.at[sl], idx_ref[0, 0, base + r], (base + r) * ROW_TILES,
                          sems.at[sl]).start(priority=r % 2)
            return carry

        lax.fori_loop(0, n // SUBLANES, body, 0)

    def wait_slot(sl):
        pltpu.make_async_copy(ys_hbm.at[pl.ds(0, n * ROW_TILES), :], buf.at[sl], sems.at[sl]).wait()

    pl.when(i == 0)(lambda: issue(dcur_ref, 0))
    wait_slot(slot)
    for t in range(n):
        _row_copy(ys_hbm, buf.at[1 - slot], dnxt_ref[0, 0, t], t * ROW_TILES,
                  sems.at[1 - slot]).start(priority=t % 2)

    half = ROW_TILES * LANES
    acc_lo = x1_ref[:, :half]
    acc_hi = x1_ref[:, half:]
    for j in range(TOP_K):
        lo, hi = _unpack_bf16_pairs(_load_tiled(buf.at[slot], tq, first_row=j * tq))
        acc_lo = acc_lo + w_ref[:, j:j + 1] * lo.astype(F32)
        acc_hi = acc_hi + w_ref[:, j:j + 1] * hi.astype(F32)
    ssq = jnp.sum(acc_lo * acc_lo, axis=-1, keepdims=True) + jnp.sum(acc_hi * acc_hi, axis=-1, keepdims=True)
    scale = lax.rsqrt(ssq * (1.0 / (2 * half)) + NORM_EPS)
    o_ref[:, :half] = acc_lo * scale * fw_ref[:, :half]
    o_ref[:, half:] = acc_hi * scale * fw_ref[:, half:]
    pl.when(i == n_steps - 1)(lambda: wait_slot(1 - slot))


def combine(x1, ys, dest, top_w, final_w):
    T, D = x1.shape
    tq = COMBINE_TOKENS
    nblk = T // tq
    dest_blk = (dest * ROW_TILES).reshape(nblk, tq, TOP_K).transpose(0, 2, 1).reshape(nblk, 1, TOP_K * tq)
    idx_spec = lambda f: pl.BlockSpec((1, 1, TOP_K * tq), f, memory_space=pltpu.SMEM)
    return pl.pallas_call(
        _combine_kernel,
        grid=(nblk,),
        in_specs=[idx_spec(lambda i: (i, 0, 0)),
                  idx_spec(lambda i: (jnp.minimum(i + 1, nblk - 1), 0, 0)),
                  pl.BlockSpec((tq, D), lambda i: (i, 0)),
                  pl.BlockSpec((tq, TOP_K), lambda i: (i, 0)),
                  pl.BlockSpec((1, D), lambda i: (0, 0)),
                  pl.BlockSpec(memory_space=pl.ANY)],
        out_specs=pl.BlockSpec((tq, D), lambda i: (i, 0)),
        out_shape=jax.ShapeDtypeStruct((T, D), F32),
        scratch_shapes=[pltpu.VMEM((2, TOP_K * tq * ROW_TILES, LANES), ys.dtype), pltpu.SemaphoreType.DMA((2,))],
        compiler_params=_cparams(("arbitrary",)),
        name="moe_combine",
    )(dest_blk, dest_blk, x1, top_w, final_w.reshape(1, D), ys)


def _routing(logits):
    T = logits.shape[0]
    TK = T * TOP_K
    NB = TK // ROW_BLOCK + N_EXPERTS
    top_logits, top_idx = lax.top_k(logits[:, :N_EXPERTS], TOP_K)
    top_w = jax.nn.softmax(top_logits, axis=-1)
    experts = jnp.arange(N_EXPERTS, dtype=jnp.int32)
    onehot = (top_idx.reshape(TK, 1) == experts[None, :]).astype(F32)
    oh = onehot.reshape(TK // RANK_GROUP, RANK_GROUP, N_EXPERTS)
    local = jnp.einsum("ts,gse->gte", jnp.tril(jnp.ones((RANK_GROUP, RANK_GROUP), F32)), oh)
    tot = local[:, -1, :]
    offs = jnp.cumsum(tot, axis=0) - tot
    counts = (offs[-1] + tot[-1]).astype(jnp.int32)
    padded = (counts + ROW_BLOCK - 1) // ROW_BLOCK * ROW_BLOCK
    pad_end = jnp.cumsum(padded).astype(jnp.int32)
    pad_start = pad_end - padded
    dest = jnp.sum(oh * (local + (offs + (pad_start.astype(F32) - 1.0)[None, :])[:, None, :]), axis=-1)
    dest = dest.reshape(T, TOP_K).astype(jnp.int32)

    blk = jnp.arange(NB, dtype=jnp.int32)
    block_e = jnp.minimum(jnp.sum((pad_end[None, :] <= (blk * ROW_BLOCK)[:, None]).astype(jnp.int32), axis=1),
                          N_EXPERTS - 1)
    of_block = lambda table: jnp.sum(jnp.where(block_e[:, None] == experts[None, :], table[None, :], 0), axis=1)
    nb_used = pad_end[-1] // ROW_BLOCK
    rows_left = of_block(counts) - (blk - of_block(pad_start // ROW_BLOCK)) * ROW_BLOCK
    block_parts = jnp.where(blk < nb_used, jnp.clip((rows_left + ROW_PART - 1) // ROW_PART, 1, ROW_BLOCK // ROW_PART), 0)
    used = padded > 0
    block_pos = of_block(jnp.cumsum(used.astype(jnp.int32)) - 1)
    tail = nb_used + experts
    zrow = jnp.concatenate([jnp.where(used, pad_end - ROW_BLOCK, -1),
                            jnp.where(tail < NB, tail * ROW_BLOCK, -1)]).astype(jnp.int32)
    return dest, top_w, (block_e, block_parts, block_pos, used, nb_used), zrow


def _schedule(blocks, n_tiles):
    block_e, block_parts, block_pos, used, nb_used = blocks
    nb = block_e.shape[0]
    b = jnp.asarray(np.tile(np.arange(nb, dtype=np.int32), n_tiles))
    f = jnp.asarray(np.repeat(np.arange(n_tiles, dtype=np.int32), nb))
    rep = lambda a: jnp.tile(a, n_tiles)
    n_groups = (jnp.sum(used) * n_tiles).astype(jnp.int32)
    n_valid = (nb_used * n_tiles).astype(jnp.int32)
    group = jnp.minimum(rep(block_pos) * n_tiles + f, n_groups - 1)
    key = jnp.where(b < nb_used, (rep(block_e) * n_tiles + f) * nb + b, (N_EXPERTS * n_tiles + f) * nb + b)
    _, group, sb, so, parts = lax.sort((key, group, b, f, rep(block_parts)), num_keys=1)
    sg = jnp.where(jnp.arange(nb * n_tiles) < n_valid, group, n_groups - 1)
    experts = jnp.arange(N_EXPERTS, dtype=jnp.int32)
    used_first = jnp.argsort(jnp.where(used, experts, N_EXPERTS + experts)).astype(jnp.int32)
    ge = jnp.repeat(used_first, n_tiles)
    gf = jnp.tile(jnp.arange(n_tiles, dtype=jnp.int32), N_EXPERTS)
    return (n_valid.reshape(1), sg, sb, so, parts, ge, gf, n_groups.reshape(1))


REPACK_ROWS = 256


def _repack_kernel(w_ref, wr_ref, wm_ref, wg_ref):
    w = w_ref[...]
    z = lambda n: jnp.zeros((w.shape[0], n), w.dtype)
    o = RWKV_COLS
    a, b = RW_WD + DECAY_LORA, RW_WD + DECAY_LORA + AAA_LORA
    wr_ref[...] = jnp.concatenate([w[:, :a], z(LORA_PAD - DECAY_LORA), w[:, a:b], z(LORA_PAD - AAA_LORA),
                                   w[:, b:o]], axis=1).astype(wr_ref.dtype)
    g = o + ML_IF + 2 * MLSTM_HEADS
    wm_ref[...] = jnp.concatenate([w[:, o:g], z(LANES - 2 * MLSTM_HEADS), w[:, g:o + MLSTM_COLS]],
                                  axis=1).astype(wm_ref.dtype)
    wg_ref[...] = w[:, o + MLSTM_COLS:].astype(wg_ref.dtype)


def _repack_w_in(w_in, layer):
    _, D, C = w_in.shape
    blk = lambda n: pl.BlockSpec((REPACK_ROWS, n), lambda i: (i, 0))
    n_gate = C - RWKV_COLS - MLSTM_COLS
    return pl.pallas_call(
        _repack_kernel,
        grid=(D // REPACK_ROWS,),
        in_specs=[pl.BlockSpec((None, REPACK_ROWS, C), lambda i: (layer, i, 0))],
        out_specs=[blk(RW_COLS_P), blk(ML_COLS_P), blk(n_gate)],
        out_shape=[jax.ShapeDtypeStruct((D, n), BF16) for n in (RW_COLS_P, ML_COLS_P, n_gate)],
        compiler_params=_cparams(("parallel",)),
        name="repack_w_in",
    )(w_in)


def kernel(x, norm1_w, w_in, b_gate, rwkv_mu, rwkv_w0, rwkv_w_up, rwkv_a0, rwkv_a_up, rwkv_g_up, rwkv_k_k,
           rwkv_k_a, rwkv_r_k, rwkv_ln_w, rwkv_ln_b, mlstm_conv_w, mlstm_i_b, mlstm_f_b, mlstm_norm_w,
           w_branch_a, w_branch_b, w_out, norm2_w, router_w, router_b, w_gu, b_gu, w_down, b_down,
           final_norm_w):
    B, S, D = x.shape
    T = B * S
    xt = x.reshape(T, D)
    assert norm1_w.shape[0] == 1, "single-layer block: the final rmsnorm is fused into the MoE combine"
    for l in range(1):
        w_r, w_m, w_g = _repack_w_in(w_in, l)
        hn, p_m, *scan_in = rwkv_in(x, norm1_w[l], w_r, w_m, rwkv_mu[l], rwkv_w0[l], rwkv_w_up[l], rwkv_a0[l],
                                    rwkv_a_up[l], rwkv_g_up[l], rwkv_k_k[l], rwkv_k_a[l])
        hn = hn.reshape(T, D)
        o_a = rwkv_scan(*scan_in, rwkv_r_k[l].reshape(1, -1), rwkv_ln_w[l].reshape(1, -1),
                        rwkv_ln_b[l].reshape(1, -1))
        o_b = mlstm_branch(p_m, mlstm_conv_w[l], mlstm_i_b[l], mlstm_f_b[l], mlstm_norm_w[l])
        x1, xp, logits = merge_project(o_a.reshape(T, RWKV_DIM), o_b.reshape(T, MLSTM_DIM), hn, w_g, b_gate[l], xt,
                                       w_branch_a[l], w_branch_b[l], w_out[l], norm2_w[l], router_w[l],
                                       router_b[l])
        dest, top_w, blocks, zrow = _routing(logits)
        xs = moe_dispatch(xp, dest, zrow, blocks[0].shape[0] * ROW_BLOCK)
        h = moe_up(xs, w_gu[l], b_gu[l], _schedule(blocks, EXPERT_FF // UP_TILE))
        ys = moe_down(h, w_down[l], b_down[l], _schedule(blocks, D // DOWN_TILE))
        xt = combine(x1, ys, dest, top_w, final_norm_w)
    return xt.reshape(B, S, D)
```

```python
import functools

import jax
import jax.numpy as jnp
import numpy as np
from jax import lax
from jax.experimental import pallas as pl
from jax.experimental.pallas import tpu as pltpu

F32 = jnp.float32
BF16 = jnp.bfloat16

D_MODEL = 2048
CHUNK = 64
NORM_EPS = 1e-6
RWKV_HEADS = 16
RWKV_HEAD_DIM = 64
RWKV_DIM = 1024
DECAY_LORA = 96
AAA_LORA = 96
GATE_LORA = 256
GN_EPS = 64e-5
RWKV_COLS = 3 * RWKV_DIM + DECAY_LORA + AAA_LORA + GATE_LORA
MLSTM_HEADS = 4
MLSTM_QK_DIM = 128
MLSTM_V_DIM = 256
MLSTM_QK = 512
MLSTM_DIM = 1024
CONV_WIDTH = 4
GATE_SOFTCAP = 15.0
MLSTM_COLS = 2 * MLSTM_QK + 2 * MLSTM_DIM + 2 * MLSTM_HEADS
N_EXPERTS = 32
TOP_K = 4
EXPERT_FF = 2048
SWIGLU_LIMIT = 7.0
SWIGLU_ALPHA = 1.702

LANES = 128
SUBLANES = 8
VMEM_LIMIT = 56 * 1024 * 1024

LORA_PAD = 128
RW_WD = 3 * RWKV_DIM
RW_AD = RW_WD + LORA_PAD
RW_GD = RW_AD + LORA_PAD
RW_COLS_P = RW_GD + GATE_LORA
ML_V = 2 * MLSTM_QK
ML_IF = ML_V + MLSTM_DIM
ML_O = ML_IF + LANES
ML_COLS_P = ML_O + MLSTM_DIM
ROUTER_PAD = 128
SCAN_SEQS = 4
MLSTM_SEQS = 1

ROW_BLOCK = 512
ROW_PART = 128


def _cparams(sem):
    return pltpu.CompilerParams(dimension_semantics=sem, vmem_limit_bytes=VMEM_LIMIT)


def _bdot(a, b):
    return jnp.dot(a.astype(BF16), b.astype(BF16), preferred_element_type=F32)


def _split3(x):
    hi = x.astype(BF16)
    r1 = x - hi.astype(F32)
    mid = r1.astype(BF16)
    lo = (r1 - mid.astype(F32)).astype(BF16)
    return hi, mid, lo


def _dot_exact_lhs(mat_bf16, x):
    hi, mid, lo = _split3(x)
    return (jnp.dot(mat_bf16, hi, preferred_element_type=F32)
            + jnp.dot(mat_bf16, mid, preferred_element_type=F32)
            + jnp.dot(mat_bf16, lo, preferred_element_type=F32))


def _dot_exact_rhs(x, mat_bf16):
    hi, mid, lo = _split3(x)
    return (jnp.dot(hi, mat_bf16, preferred_element_type=F32)
            + jnp.dot(mid, mat_bf16, preferred_element_type=F32)
            + jnp.dot(lo, mat_bf16, preferred_element_type=F32))


def _sigmoid(x):
    return 1.0 / (1.0 + jnp.exp(-x))


def _softplus(x):
    return jnp.maximum(x, 0.0) + jnp.log(1.0 + jnp.exp(-jnp.abs(x)))


def _head_sum_mat():
    r = lax.broadcasted_iota(jnp.int32, (LANES, LANES), 0) // RWKV_HEAD_DIM
    c = lax.broadcasted_iota(jnp.int32, (LANES, LANES), 1) // RWKV_HEAD_DIM
    return jnp.where(r == c, 1.0, 0.0).astype(BF16)


def _pad_rows(w, n):
    return jnp.pad(w, ((0, n - w.shape[0]), (0, 0)))


PREP_GROUP = 1024
MLSTM_PROJ_TILE = ML_COLS_P


def _rwkv_in_kernel(x_ref, n1_ref, w_ref, wm_ref, mu_ref, w0_ref, wup_ref, a0_ref, aup_ref, gup_ref, kk_ref,
                    ka_ref, hn_out, pm_out, r_out, k_out, v_out, kk_out, b_out, lw_out, g_out, carry_ref):
    i = pl.program_id(1)

    @pl.when(i == 0)
    def _():
        carry_ref[...] = jnp.zeros_like(carry_ref)

    x = x_ref[...]
    hn = (x * lax.rsqrt(jnp.mean(x * x, axis=-1, keepdims=True) + NORM_EPS) * n1_ref[...]).astype(BF16)
    hn_out[...] = hn
    tq = x.shape[0]
    row = lax.broadcasted_iota(jnp.int32, (tq, 1), 0)

    def shifted(cols):
        p = jnp.dot(hn, w_ref[:, cols], preferred_element_type=F32)
        prev = jnp.where(row == 0, carry_ref[0:1, cols], pltpu.roll(p, 1, 0))
        carry_ref[0:1, cols] = p[tq - 1:tq, :]
        return p + (prev - p) * mu_ref[:, cols]

    lora = shifted(slice(RW_WD, RW_COLS_P))
    wd_t = jnp.tanh(lora[:, 0:LORA_PAD]).astype(BF16)
    ad = lora[:, LORA_PAD:2 * LORA_PAD].astype(BF16)
    gd_s = _sigmoid(lora[:, 2 * LORA_PAD:]).astype(BF16)
    hs = _head_sum_mat()

    for c in range(0, RWKV_DIM, PREP_GROUP):
        cs = slice(c, c + PREP_GROUP)
        r = shifted(cs)
        k = shifted(slice(RWKV_DIM + c, RWKV_DIM + c + PREP_GROUP))
        v = shifted(slice(2 * RWKV_DIM + c, 2 * RWKV_DIM + c + PREP_GROUP))
        w_log = -_softplus(-(w0_ref[:, cs] + jnp.dot(wd_t, wup_ref[:, cs], preferred_element_type=F32))) - 0.5
        a = _sigmoid(a0_ref[:, cs] + jnp.dot(ad, aup_ref[:, cs], preferred_element_type=F32))
        kk = k * kk_ref[:, cs]
        nrm2 = jnp.concatenate(
            [_dot_exact_rhs(kk[:, t:t + LANES] * kk[:, t:t + LANES], hs) for t in range(0, PREP_GROUP, LANES)],
            axis=1)
        kk = kk / jnp.maximum(jnp.sqrt(nrm2), 1e-12)
        r_out[:, cs] = r.astype(r_out.dtype)
        k_out[:, cs] = (k * (1.0 + (a - 1.0) * ka_ref[:, cs])).astype(k_out.dtype)
        v_out[:, cs] = v.astype(v_out.dtype)
        kk_out[:, cs] = kk.astype(kk_out.dtype)
        b_out[:, cs] = (kk * a).astype(b_out.dtype)
        lw_out[:, cs] = -jnp.exp(w_log)
        g_out[:, cs] = jnp.dot(gd_s, gup_ref[:, cs], preferred_element_type=F32).astype(g_out.dtype)

    for c in range(0, ML_COLS_P, MLSTM_PROJ_TILE):
        cs = slice(c, c + MLSTM_PROJ_TILE)
        pm_out[:, cs] = jnp.dot(hn, wm_ref[:, cs], preferred_element_type=F32)


def rwkv_in(x, norm1_w, w_r, w_m, mu, w0, w_up, a0, a_up, g_up, k_k, k_a, tq=256):
    B, S, D = x.shape
    blk = lambda n: pl.BlockSpec((None, tq, n), lambda b, i: (b, i, 0))
    full = lambda a: pl.BlockSpec(a.shape, lambda b, i: (0,) * a.ndim, pipeline_mode=pl.Buffered(1))
    row = lambda t: t.reshape(1, -1)
    mu_p = jnp.concatenate([mu[:RW_WD], jnp.pad(mu[RW_WD:RW_WD + DECAY_LORA], (0, LORA_PAD - DECAY_LORA)),
                            jnp.pad(mu[RW_WD + DECAY_LORA:RW_WD + DECAY_LORA + AAA_LORA], (0, LORA_PAD - AAA_LORA)),
                            mu[RW_WD + DECAY_LORA + AAA_LORA:]])
    params = (row(norm1_w), w_r, w_m, row(mu_p), row(w0), _pad_rows(w_up, LORA_PAD).astype(BF16), row(a0),
              _pad_rows(a_up, LORA_PAD).astype(BF16), g_up.astype(BF16), row(k_k), row(k_a))
    out = lambda n, dt: jax.ShapeDtypeStruct((B, S, n), dt)
    return pl.pallas_call(
        _rwkv_in_kernel,
        grid=(B, S // tq),
        in_specs=[blk(D)] + [full(a) for a in params],
        out_specs=[blk(D), blk(ML_COLS_P)] + [blk(RWKV_DIM)] * 7,
        out_shape=[out(D, BF16), out(ML_COLS_P, F32)] + [out(RWKV_DIM, BF16)] * 5
        + [out(RWKV_DIM, F32), out(RWKV_DIM, BF16)],
        scratch_shapes=[pltpu.VMEM((SUBLANES, RW_COLS_P), F32)],
        compiler_params=_cparams(("parallel", "arbitrary")),
        name="rwkv_in",
    )(x, *params)


def _rwkv_scan_kernel(r_ref, k_ref, v_ref, kk_ref, b_ref, lw_ref, g_ref, rk_ref, lnw_ref, lnb_ref,
                      o_ref, h_ref):
    c = pl.program_id(1)

    @pl.when(c == 0)
    def _():
        h_ref[...] = jnp.zeros_like(h_ref)

    L = CHUNK
    L2 = 2 * L
    ri = lax.broadcasted_iota(jnp.int32, (L, L), 0)
    ci = lax.broadcasted_iota(jnp.int32, (L, L), 1)
    tril = jnp.where(ri >= ci, 1.0, 0.0).astype(BF16)

    n_seq = lw_ref.shape[0]
    cat = lambda ref: jnp.concatenate([ref[i] for i in range(n_seq)], axis=1)
    rep = lambda ref: jnp.concatenate([ref[...]] * n_seq, axis=1)
    lw = cat(lw_ref)
    cum = _dot_exact_lhs(tril, lw)
    cum_end = cum[L - 1:L, :]
    w_in = jnp.exp(cum)
    w_prev = jnp.exp(cum - lw)
    w_inv = jnp.exp(-cum)
    w_tail = jnp.exp(cum_end - cum)
    w_end = jnp.exp(cum_end)

    kk = cat(kk_ref).astype(F32)
    bb = cat(b_ref).astype(F32)
    kx = cat(k_ref).astype(F32)
    rx = cat(r_ref).astype(F32)
    vx = cat(v_ref).astype(F32)
    gx = cat(g_ref).astype(F32)
    a_hat = -kk * w_prev
    r_hat = rx * w_in
    b_hat = bb * w_inv
    k_hat = kx * w_inv
    b_til = bb * w_tail
    k_til = kx * w_tail
    rkk = rx * kx * rep(rk_ref)

    lane = lax.broadcasted_iota(jnp.int32, (1, LANES), 1)
    m_lo = jnp.where(lane < RWKV_HEAD_DIM, 1.0, 0.0)
    m_hi = 1.0 - m_lo

    def stack(x):
        return jnp.concatenate([x * m_lo, x * m_hi], axis=0)

    r2 = lax.broadcasted_iota(jnp.int32, (L2, L2), 0)
    c2 = lax.broadcasted_iota(jnp.int32, (L2, L2), 1)
    same_head = (r2 // L) == (c2 // L)
    strict = same_head & (r2 > c2)
    incl = same_head & (r2 >= c2)
    diag16 = (r2 // 16) == (c2 // 16)
    eye = jnp.where(r2 == c2, 1.0, 0.0)
    hs = _head_sum_mat()

    pairs_per_seq = RWKV_DIM // LANES
    pairs = range(n_seq * pairs_per_seq)
    sls = [slice(p * LANES, (p + 1) * LANES) for p in pairs]
    v_st = [stack(vx[:, sl]) for sl in sls]
    lhs = [jnp.concatenate([stack(a_hat[:, sl]), stack(r_hat[:, sl])], axis=0).astype(BF16) for sl in sls]
    rhs = [jnp.concatenate([stack(b_hat[:, sl]), stack(k_hat[:, sl])], axis=0).astype(BF16) for sl in sls]
    sc = [lax.dot_general(lhs[p], rhs[p], (((1,), (1,)), ((), ())), preferred_element_type=F32) for p in pairs]
    n_ab = [jnp.where(strict, sc[p][:L2, :L2], 0.0) for p in pairs]
    a_ak = [jnp.where(strict, sc[p][:L2, L2:], 0.0).astype(BF16) for p in pairs]
    a_r = [jnp.concatenate([jnp.where(incl, sc[p][L2:, :L2], 0.0), jnp.where(incl, sc[p][L2:, L2:], 0.0)],
                           axis=1).astype(BF16) for p in pairs]

    nd = [jnp.where(diag16, n_ab[p], 0.0) for p in pairs]
    noff = [(n_ab[p] - nd[p]).astype(BF16) for p in pairs]
    ndb = [nd[p].astype(BF16) for p in pairs]
    s2 = [jnp.dot(ndb[p], ndb[p], preferred_element_type=F32).astype(BF16) for p in pairs]
    s4 = [jnp.dot(s2[p], s2[p], preferred_element_type=F32).astype(BF16) for p in pairs]
    s8 = [jnp.dot(s4[p], s4[p], preferred_element_type=F32).astype(BF16) for p in pairs]
    x1 = [eye + nd[p] for p in pairs]
    x2 = [x1[p] + _bdot(x1[p], s2[p]) for p in pairs]
    x3 = [x2[p] + _bdot(x2[p], s4[p]) for p in pairs]
    t_d = [(x3[p] + _bdot(x3[p], s8[p])).astype(BF16) for p in pairs]
    m1 = [jnp.dot(t_d[p], noff[p], preferred_element_type=F32) for p in pairs]
    m1b = [m1[p].astype(BF16) for p in pairs]
    m2 = [jnp.dot(m1b[p], m1b[p], preferred_element_type=F32) for p in pairs]
    m3 = [jnp.dot(m1b[p], m2[p].astype(BF16), preferred_element_type=F32) for p in pairs]
    t_inv = [jnp.dot((eye + m1[p] + m2[p] + m3[p]).astype(BF16), t_d[p], preferred_element_type=F32).astype(BF16)
             for p in pairs]

    h0 = [h_ref[p] for p in pairs]
    ah = [jnp.dot(lhs[p], h0[p].astype(BF16), preferred_element_type=F32) for p in pairs]
    x = [ah[p][:L2] + jnp.dot(a_ak[p], v_st[p].astype(BF16), preferred_element_type=F32) for p in pairs]
    u = [jnp.dot(t_inv[p], x[p].astype(BF16), preferred_element_type=F32) for p in pairs]
    uv = [jnp.concatenate([u[p], v_st[p]], axis=0).astype(BF16) for p in pairs]
    y_st = [ah[p][L2:] + jnp.dot(a_r[p], uv[p], preferred_element_type=F32) for p in pairs]
    y = [y_st[p][:L] + y_st[p][L:] for p in pairs]

    for p in pairs:
        sl = sls[p]
        upd_l = jnp.concatenate([stack(b_til[:, sl]), stack(k_til[:, sl])], axis=0).astype(BF16)
        upd = lax.dot_general(upd_l, uv[p], (((0,), (0,)), ((), ())), preferred_element_type=F32)
        w_col = jnp.sum(eye * w_end[:, sl], axis=1, keepdims=True)
        h_ref[p] = w_col * h0[p] + upd

    def head_sums(vals):
        parts = []
        for t in vals:
            hi = t.astype(BF16)
            parts += [hi, (t - hi.astype(F32)).astype(BF16)]
        res = jnp.dot(jnp.concatenate(parts, axis=0), hs, preferred_element_type=F32)
        return [res[2 * i * L:(2 * i + 1) * L] + res[(2 * i + 1) * L:(2 * i + 2) * L] for i in range(len(vals))]

    sums1 = [head_sums([y[p], rkk[:, sls[p]]]) for p in pairs]
    d = [y[p] - sums1[p][0] * (1.0 / RWKV_HEAD_DIM) for p in pairs]
    var = [head_sums([d[p] * d[p]])[0] * (1.0 / RWKV_HEAD_DIM) for p in pairs]
    for p in pairs:
        sl = sls[p]
        psl = sls[p % pairs_per_seq]
        yn = d[p] * lax.rsqrt(var[p] + GN_EPS) * lnw_ref[:, psl] + lnb_ref[:, psl]
        o_ref[p // pairs_per_seq, :, psl] = ((yn + sums1[p][1] * vx[:, sl]) * gx[:, sl]).astype(o_ref.dtype)


def rwkv_scan(r, k, v, kk, b, lw, g, r_k, ln_w, ln_b):
    B, S, C = r.shape
    blk = pl.BlockSpec((SCAN_SEQS, CHUNK, C), lambda bb, c: (bb, c, 0))
    full = pl.BlockSpec((1, C), lambda bb, c: (0, 0))
    return pl.pallas_call(
        _rwkv_scan_kernel,
        grid=(B // SCAN_SEQS, S // CHUNK),
        in_specs=[blk] * 7 + [full] * 3,
        out_specs=blk,
        out_shape=jax.ShapeDtypeStruct((B, S, C), BF16),
        scratch_shapes=[pltpu.VMEM((SCAN_SEQS * C // LANES, LANES, LANES), F32)],
        compiler_params=_cparams(("parallel", "arbitrary")),
        name="rwkv_scan",
    )(r, k, v, kk, b, lw, g, r_k, ln_w, ln_b)


def _mlstm_kernel(p_ref, convw_ref, ifb_ref, nw_ref, o_ref, carry_ref, c_ref, n_ref, m_ref):
    ci = pl.program_id(1)

    @pl.when(ci == 0)
    def _():
        carry_ref[...] = jnp.zeros_like(carry_ref)
        c_ref[...] = jnp.zeros_like(c_ref)
        n_ref[...] = jnp.zeros_like(n_ref)
        m_ref[...] = jnp.zeros_like(m_ref)

    L = CHUNK
    n_seq = p_ref.shape[0]
    lane = lax.broadcasted_iota(jnp.int32, (1, LANES), 1)
    ri = lax.broadcasted_iota(jnp.int32, (L, L), 0)
    cj = lax.broadcasted_iota(jnp.int32, (L, L), 1)
    causal = ri >= cj
    tril = jnp.where(causal, 1.0, 0.0).astype(BF16)
    triu = jnp.where(ri <= cj, 1.0, 0.0).astype(BF16)

    qk, comb, comb_t, b_col, b_row = [], [], [], [], []
    for i in range(n_seq):
        u = p_ref[i, :, 0:ML_V]
        ext = jnp.concatenate([carry_ref[i], u], axis=0)
        carry_ref[i] = u[L - SUBLANES:L, :]
        conv = convw_ref[CONV_WIDTH - 1:CONV_WIDTH, :] * u
        for j in range(1, CONV_WIDTH):
            conv = conv + convw_ref[CONV_WIDTH - 1 - j:CONV_WIDTH - j, :] * pltpu.roll(ext, j, 0)[SUBLANES:, :]
        qk.append(conv * _sigmoid(conv))
        pre = GATE_SOFTCAP * jnp.tanh((p_ref[i, :, ML_IF:ML_O] + ifb_ref[...]) * (1.0 / GATE_SOFTCAP))
        cb = jnp.where(lane < MLSTM_HEADS, pre, -_softplus(-pre))
        comb.append(cb)
        comb_t.append(cb.T)
        b_col.append(_dot_exact_lhs(tril, cb))
        b_row.append(_dot_exact_rhs(comb_t[i], triu))

    H = range(n_seq * MLSTM_HEADS)
    sq = [v // MLSTM_HEADS for v in H]
    hd = [v % MLSTM_HEADS for v in H]
    dk, dv = MLSTM_QK_DIM, MLSTM_V_DIM
    qh = [qk[sq[v]][:, hd[v] * dk:(hd[v] + 1) * dk] * (dk ** -0.5) for v in H]
    kh = [qk[sq[v]][:, MLSTM_QK + hd[v] * dk:MLSTM_QK + (hd[v] + 1) * dk] for v in H]
    vh = [p_ref[sq[v], :, ML_V + hd[v] * dv:ML_V + (hd[v] + 1) * dv].astype(BF16) for v in H]
    qb = [qh[h].astype(BF16) for h in H]
    bcol = [b_col[sq[v]][:, MLSTM_HEADS + hd[v]:MLSTM_HEADS + hd[v] + 1] for v in H]
    brow = [b_row[sq[v]][MLSTM_HEADS + hd[v]:MLSTM_HEADS + hd[v] + 1, :] for v in H]
    m_prev = [m_ref[h][0:1, 0:1] for h in H]
    n_prev = [n_ref[h][0:1, :] for h in H]
    c_prev = [c_ref[h] for h in H]

    qk_t = [lax.dot_general(qb[h], kh[h].astype(BF16), (((1,), (1,)), ((), ())), preferred_element_type=F32)
            for h in H]
    qc = [jnp.dot(qb[h], c_prev[h].astype(BF16), preferred_element_type=F32) for h in H]
    dm = [jnp.where(causal, bcol[h] - brow[h] + comb_t[sq[h]][hd[h]:hd[h] + 1, :], -jnp.inf) for h in H]
    inter = [bcol[h] + m_prev[h] for h in H]
    m_t = [jnp.maximum(inter[h], jnp.max(dm[h], axis=-1, keepdims=True)) for h in H]
    s = [qk_t[h] * jnp.exp(dm[h] - m_t[h]) for h in H]
    w_inter = [jnp.exp(inter[h] - m_t[h]) for h in H]
    num = [jnp.dot(s[h].astype(BF16), vh[h], preferred_element_type=F32) + w_inter[h] * qc[h] for h in H]
    den = [jnp.sum(s[h], axis=-1, keepdims=True) + w_inter[h] * jnp.sum(qh[h] * n_prev[h], axis=-1, keepdims=True)
           for h in H]
    hh = [num[h] / jnp.maximum(jnp.abs(den[h]), jnp.exp(-m_t[h])) for h in H]

    g_tot = [bcol[h][L - 1:L, :] for h in H]
    a = [comb[sq[h]][:, hd[h]:hd[h] + 1] + g_tot[h] - bcol[h] for h in H]
    m_new = [jnp.maximum(g_tot[h] + m_prev[h], jnp.max(a[h], axis=0, keepdims=True)) for h in H]
    dec = [jnp.exp(g_tot[h] + m_prev[h] - m_new[h]) for h in H]
    wkk = [jnp.exp(a[h] - m_new[h]) * kh[h] for h in H]
    for h in H:
        c_ref[h] = dec[h] * c_prev[h] + lax.dot_general(wkk[h].astype(BF16), vh[h], (((0,), (0,)), ((), ())),
                                                        preferred_element_type=F32)
        n_ref[h] = jnp.broadcast_to(dec[h] * n_prev[h] + jnp.sum(wkk[h], axis=0, keepdims=True),
                                    (SUBLANES, LANES))
        m_ref[h] = jnp.broadcast_to(m_new[h], (SUBLANES, LANES))
    for h in H:
        vs = slice(hd[h] * dv, (hd[h] + 1) * dv)
        hn = hh[h] * lax.rsqrt(jnp.mean(hh[h] * hh[h], axis=-1, keepdims=True) + NORM_EPS)
        o_raw = p_ref[sq[h], :, ML_O + hd[h] * dv:ML_O + (hd[h] + 1) * dv]
        o_ref[sq[h], :, vs] = (hn * nw_ref[:, vs] * _sigmoid(o_raw)).astype(o_ref.dtype)


def mlstm_branch(p_pad, conv_w, i_b, f_b, norm_w):
    B, S, C = p_pad.shape
    ifb = jnp.pad(jnp.concatenate([i_b, f_b]), (0, LANES - 2 * MLSTM_HEADS)).reshape(1, LANES)
    full = lambda a: pl.BlockSpec(a.shape, lambda b, c: (0,) * a.ndim)
    nw = norm_w.reshape(1, MLSTM_DIM)
    return pl.pallas_call(
        _mlstm_kernel,
        grid=(B // MLSTM_SEQS, S // CHUNK),
        in_specs=[pl.BlockSpec((MLSTM_SEQS, CHUNK, C), lambda b, c: (b, c, 0)), full(conv_w), full(ifb), full(nw)],
        out_specs=pl.BlockSpec((MLSTM_SEQS, CHUNK, MLSTM_DIM), lambda b, c: (b, c, 0)),
        out_shape=jax.ShapeDtypeStruct((B, S, MLSTM_DIM), BF16),
        scratch_shapes=[pltpu.VMEM((MLSTM_SEQS, SUBLANES, ML_V), F32),
                        pltpu.VMEM((MLSTM_SEQS * MLSTM_HEADS, MLSTM_QK_DIM, MLSTM_V_DIM), F32),
                        pltpu.VMEM((MLSTM_SEQS * MLSTM_HEADS, SUBLANES, LANES), F32),
                        pltpu.VMEM((MLSTM_SEQS * MLSTM_HEADS, SUBLANES, LANES), F32)],
        compiler_params=_cparams(("parallel", "arbitrary")),
        name="mlstm_scan",
    )(p_pad, conv_w, ifb, nw)


HI_MASK = 0xFFFF0000


def _pack_bf16_pairs(hb_f32):
    c = hb_f32.shape[1] // 2
    u = pltpu.bitcast(hb_f32, jnp.uint32)
    return u[:, c:] | (u[:, :c] >> 16)


ROW_TILES = D_MODEL // 2 // LANES


def _store_tiled(ref, val, first_row=0):
    n = val.shape[0]
    for t in range(ROW_TILES):
        ref[pl.ds(first_row * ROW_TILES + t, n, stride=ROW_TILES), :] = val[:, t * LANES:(t + 1) * LANES]


def _load_tiled(ref, n, first_row=0):
    return jnp.concatenate([ref[pl.ds(first_row * ROW_TILES + t, n, stride=ROW_TILES), :] for t in range(ROW_TILES)],
                           axis=1)


def _unpack_bf16_pairs(xu):
    lo = pltpu.bitcast(xu << 16, F32).astype(BF16)
    hi = pltpu.bitcast(xu & jnp.uint32(HI_MASK), F32).astype(BF16)
    return lo, hi


def _merge_kernel(oa_ref, ob_ref, hn_ref, bg_ref, x_ref, wa_ref, wb_ref, wg_ref, wo_ref, n2_ref, rwh_ref, rwl_ref,
                  rb_ref, x1_ref, xp_ref, lg_ref):
    hn1 = hn_ref[...]

    def gated(o_ref, w_ref, cols):
        gate = _sigmoid(jnp.dot(hn1, wg_ref[:, cols], preferred_element_type=F32) + bg_ref[:, cols])
        return gate * jnp.dot(o_ref[...], w_ref[...], preferred_element_type=F32)

    merged = gated(oa_ref, wa_ref, slice(0, D_MODEL)) + gated(ob_ref, wb_ref, slice(D_MODEL, 2 * D_MODEL))
    x1 = x_ref[...] + jnp.dot(merged.astype(BF16), wo_ref[...], preferred_element_type=F32)
    x1_ref[...] = x1
    hn = x1 * lax.rsqrt(jnp.mean(x1 * x1, axis=-1, keepdims=True) + NORM_EPS) * n2_ref[...]
    hi = hn.astype(BF16)
    _store_tiled(xp_ref, _pack_bf16_pairs(hi.astype(F32)))
    lo = (hn - hi.astype(F32)).astype(BF16)
    lg_ref[...] = (jnp.dot(hi, rwh_ref[...], preferred_element_type=F32)
                   + jnp.dot(hi, rwl_ref[...], preferred_element_type=F32)
                   + jnp.dot(lo, rwh_ref[...], preferred_element_type=F32) + rb_ref[...])


def merge_project(o_a, o_b, hn, w_gate, b_gate, x, w_a, w_b, w_out, norm2_w, router_w, router_b, tm=256):
    T, D = x.shape
    rw = jnp.pad(router_w, ((0, 0), (0, ROUTER_PAD - N_EXPERTS)))
    rw_hi = rw.astype(BF16)
    rw_lo = (rw - rw_hi.astype(F32)).astype(BF16)
    rb = jnp.pad(router_b, (0, ROUTER_PAD - N_EXPERTS)).reshape(1, ROUTER_PAD)
    rows = lambda n: pl.BlockSpec((tm, n), lambda i: (i, 0))
    full = lambda a: pl.BlockSpec(a.shape, lambda i: (0, 0), pipeline_mode=pl.Buffered(1))
    params = (w_a.astype(BF16), w_b.astype(BF16), w_gate, w_out.astype(BF16), norm2_w.reshape(1, D), rw_hi, rw_lo,
              rb)
    bg = b_gate.reshape(1, 2 * D)
    return pl.pallas_call(
        _merge_kernel,
        grid=(T // tm,),
        in_specs=[rows(o_a.shape[1]), rows(o_b.shape[1]), rows(D), full(bg), rows(D)] + [full(a) for a in params],
        out_specs=[rows(D), pl.BlockSpec((tm * ROW_TILES, LANES), lambda i: (i, 0)), rows(ROUTER_PAD)],
        out_shape=[jax.ShapeDtypeStruct((T, D), F32), jax.ShapeDtypeStruct((T * ROW_TILES, LANES), jnp.uint32),
                   jax.ShapeDtypeStruct((T, ROUTER_PAD), F32)],
        compiler_params=_cparams(("parallel",)),
        name="merge_project",
    )(o_a, o_b, hn, bg, x, *params)


UP_TILE = 1024
DOWN_TILE = 2048
RANK_GROUP = 256
DISPATCH_TOKENS = 1024
COMBINE_TOKENS = 256


def _row_copy(src, dst, src_at, dst_at, sem):
    return pltpu.make_async_copy(src.at[pl.ds(pl.multiple_of(src_at, ROW_TILES), ROW_TILES), :],
                                 dst.at[pl.ds(pl.multiple_of(dst_at, ROW_TILES), ROW_TILES), :], sem)


def _dispatch_kernel(zrow_ref, dest_ref, xp_ref, xs_hbm, zbuf, zsem, sem):
    i = pl.program_id(0)
    tq = dest_ref.shape[2] // TOP_K

    def zero_copy(k):
        at = pl.multiple_of(zrow_ref[k], ROW_BLOCK * ROW_TILES)
        return pltpu.make_async_copy(zbuf, xs_hbm.at[pl.ds(at, ROW_BLOCK * ROW_TILES), :], zsem)

    @pl.when(i == 0)
    def _():
        zbuf[...] = jnp.zeros_like(zbuf)
        for k in range(zrow_ref.shape[0]):
            pl.when(zrow_ref[k] >= 0)(lambda k=k: zero_copy(k).start())
        for k in range(zrow_ref.shape[0]):
            pl.when(zrow_ref[k] >= 0)(lambda k=k: zero_copy(k).wait())

    def body(q, carry):
        base = pl.multiple_of(q * SUBLANES, SUBLANES)
        for r in range(SUBLANES):
            for j in range(TOP_K):
                _row_copy(xp_ref, xs_hbm, (base + r) * ROW_TILES, dest_ref[0, 0, TOP_K * (base + r) + j],
                          sem).start(priority=j % 2)
        return carry

    lax.fori_loop(0, tq // SUBLANES, body, 0)
    for j in range(TOP_K):
        pltpu.make_async_copy(xp_ref, xs_hbm.at[pl.ds(0, tq * ROW_TILES), :], sem).wait()


def moe_dispatch(xp, dest, zrow, n_rows):
    T = xp.shape[0] // ROW_TILES
    tq = DISPATCH_TOKENS
    dest_blk = (dest * ROW_TILES).reshape(T // tq, 1, TOP_K * tq)
    zrow = jnp.where(zrow >= 0, zrow * ROW_TILES, -1)
    return pl.pallas_call(
        _dispatch_kernel,
        grid_spec=pltpu.PrefetchScalarGridSpec(
            num_scalar_prefetch=1,
            grid=(T // tq,),
            in_specs=[pl.BlockSpec((1, 1, TOP_K * tq), lambda i, zr: (i, 0, 0), memory_space=pltpu.SMEM),
                      pl.BlockSpec((tq * ROW_TILES, LANES), lambda i, zr: (i, 0))],
            out_specs=pl.BlockSpec(memory_space=pl.ANY),
            scratch_shapes=[pltpu.VMEM((ROW_BLOCK * ROW_TILES, LANES), xp.dtype), pltpu.SemaphoreType.DMA(()),
                            pltpu.SemaphoreType.DMA(())]),
        out_shape=jax.ShapeDtypeStruct((n_rows * ROW_TILES, LANES), xp.dtype),
        compiler_params=_cparams(("arbitrary",)),
        name="moe_dispatch",
    )(zrow, dest_blk, xp)


SCHED_NV, SCHED_SG, SCHED_SB, SCHED_SO, SCHED_PARTS, SCHED_GE, SCHED_GF, SCHED_NG = range(8)


def _stream_weights(s, sched, copies, on_arrival):
    sg_ref = sched[SCHED_SG]
    g = sg_ref[s]
    first = (s < sched[SCHED_NV][0]) & ((s == 0) | (g != sg_ref[jnp.maximum(s - 1, 0)]))

    @pl.when(first)
    def _():
        @pl.when(s == 0)
        def _():
            for c in copies(g):
                c.start()

        for c in copies(g):
            c.wait()
        on_arrival()

        @pl.when(g + 1 < sched[SCHED_NG][0])
        def _():
            for c in copies(g + 1):
                c.start()


def _for_used_rows(s, sched, out_ref, compute):
    parts = sched[SCHED_PARTS][s]
    rows_per_row = out_ref.shape[0] // ROW_BLOCK

    @pl.when(s >= sched[SCHED_NV][0])
    def _():
        out_ref[...] = jnp.zeros_like(out_ref)

    def variant(rows):
        compute(rows)
        if rows < ROW_BLOCK:
            rest = rows * rows_per_row
            out_ref[rest:, :] = jnp.zeros((out_ref.shape[0] - rest, out_ref.shape[1]), out_ref.dtype)

    for p in range(1, ROW_BLOCK // ROW_PART + 1):
        pl.when((s < sched[SCHED_NV][0]) & (parts == p))(functools.partial(variant, p * ROW_PART))


def _moe_up_kernel(*refs):
    sched = refs[:8]
    xs_ref, w_hbm, bg_ref, bu_ref, h_ref, wbuf, wgb, wub, sems = refs[8:]
    s = pl.program_id(0)
    tf = wgb.shape[1]
    n_ff = w_hbm.shape[2] // 2

    def copies(g):
        e = sched[SCHED_GE][g]
        col = pl.multiple_of(sched[SCHED_GF][g] * tf, tf)
        return [pltpu.make_async_copy(w_hbm.at[e, :, pl.ds(half * n_ff + col, tf)], wbuf.at[half], sems.at[half])
                for half in range(2)]

    def on_arrival():
        wgb[...] = wbuf[0].astype(BF16)
        wub[...] = wbuf[1].astype(BF16)

    _stream_weights(s, sched, copies, on_arrival)

    def compute(rows):
        lo, hi = _unpack_bf16_pairs(_load_tiled(xs_ref, rows))
        half = lo.shape[1]

        def proj(wb, b_ref):
            return (jnp.dot(lo, wb[:half, :], preferred_element_type=F32)
                    + jnp.dot(hi, wb[half:, :], preferred_element_type=F32) + b_ref[...])

        gate = jnp.minimum(proj(wgb, bg_ref), SWIGLU_LIMIT)
        up = jnp.clip(proj(wub, bu_ref), -SWIGLU_LIMIT, SWIGLU_LIMIT)
        h_ref[:rows, :] = ((up + 1.0) * gate * _sigmoid(SWIGLU_ALPHA * gate)).astype(h_ref.dtype)

    _for_used_rows(s, sched, h_ref, compute)


def _step_expert(s, r):
    return r[SCHED_GE][r[SCHED_SG][s]]


def _step_tile(s, r):
    return r[SCHED_GF][r[SCHED_SG][s]]


def moe_up(xs, w_gu, b_gu, sched):
    P = xs.shape[0] // ROW_TILES
    E, D, F2 = w_gu.shape
    F = F2 // 2
    tf = UP_TILE
    nf = F // tf
    b3 = b_gu.reshape(E, 1, F2)
    return pl.pallas_call(
        _moe_up_kernel,
        grid_spec=pltpu.PrefetchScalarGridSpec(
            num_scalar_prefetch=len(sched),
            grid=(sched[SCHED_SG].shape[0],),
            in_specs=[pl.BlockSpec((ROW_BLOCK * ROW_TILES, LANES), lambda s, *r: (r[SCHED_SB][s], 0)),
                      pl.BlockSpec(memory_space=pl.ANY),
                      pl.BlockSpec((None, 1, tf), lambda s, *r: (_step_expert(s, r), 0, _step_tile(s, r))),
                      pl.BlockSpec((None, 1, tf), lambda s, *r: (_step_expert(s, r), 0, nf + _step_tile(s, r)))],
            out_specs=pl.BlockSpec((ROW_BLOCK, tf), lambda s, *r: (r[SCHED_SB][s], r[SCHED_SO][s])),
            scratch_shapes=[pltpu.VMEM((2, D, tf), F32), pltpu.VMEM((D, tf), BF16), pltpu.VMEM((D, tf), BF16),
                            pltpu.SemaphoreType.DMA((2,))]),
        out_shape=jax.ShapeDtypeStruct((P, F), BF16),
        compiler_params=_cparams(("arbitrary",)),
        name="moe_up",
    )(*sched, xs, w_gu, b3, b3)


def _moe_down_kernel(*refs):
    sched = refs[:8]
    h_ref, w_hbm, bd_ref, y_ref, wbuf, wdb, sem = refs[8:]
    s = pl.program_id(0)
    tn = wdb.shape[1]

    def copies(g):
        col = pl.multiple_of(sched[SCHED_GF][g] * tn, tn)
        return [pltpu.make_async_copy(w_hbm.at[sched[SCHED_GE][g], :, pl.ds(col, tn)], wbuf, sem)]

    def on_arrival():
        wdb[...] = wbuf[...].astype(BF16)

    _stream_weights(s, sched, copies, on_arrival)

    def compute(rows):
        y = jnp.dot(h_ref[:rows, :], wdb[...], preferred_element_type=F32) + bd_ref[...]
        _store_tiled(y_ref, _pack_bf16_pairs(y.astype(BF16).astype(F32)))

    _for_used_rows(s, sched, y_ref, compute)


def moe_down(h, w_down, b_down, sched):
    P, F = h.shape
    E, _, D = w_down.shape
    tn = DOWN_TILE
    assert tn == D, "the packed output pairs column c with column c + D/2"
    b3 = b_down.reshape(E, 1, D)
    return pl.pallas_call(
        _moe_down_kernel,
        grid_spec=pltpu.PrefetchScalarGridSpec(
            num_scalar_prefetch=len(sched),
            grid=(sched[SCHED_SG].shape[0],),
            in_specs=[pl.BlockSpec((ROW_BLOCK, F), lambda s, *r: (r[SCHED_SB][s], 0)),
                      pl.BlockSpec(memory_space=pl.ANY),
                      pl.BlockSpec((None, 1, tn), lambda s, *r: (_step_expert(s, r), 0, _step_tile(s, r)))],
            out_specs=pl.BlockSpec((ROW_BLOCK * ROW_TILES, LANES), lambda s, *r: (r[SCHED_SB][s], 0)),
            scratch_shapes=[pltpu.VMEM((F, tn), F32), pltpu.VMEM((F, tn), BF16), pltpu.SemaphoreType.DMA(())]),
        out_shape=jax.ShapeDtypeStruct((P * ROW_TILES, LANES), jnp.uint32),
        compiler_params=_cparams(("arbitrary",)),
        name="moe_down",
    )(*sched, h, w_down, b3)


def _combine_kernel(dcur_ref, dnxt_ref, x1_ref, w_ref, fw_ref, ys_hbm, o_ref, buf, sems):
    i = pl.program_id(0)
    n_steps = pl.num_programs(0)
    tq = x1_ref.shape[0]
    n = TOP_K * tq
    slot = lax.rem(i, 2)

    def issue(idx_ref, sl):
        def body(q, carry):
            base = pl.multiple_of(q * SUBLANES, SUBLANES)
            for r in range(SUBLANES):
                _row_copy(ys_hbm, buf.at[sl], idx_ref[0, 0, base + r], (base + r) * ROW_TILES,
                          sems.at[sl]).start(priority=r % 2)
            return carry

        lax.fori_loop(0, n // SUBLANES, body, 0)

    def wait_slot(sl):
        pltpu.make_async_copy(ys_hbm.at[pl.ds(0, n * ROW_TILES), :], buf.at[sl], sems.at[sl]).wait()

    pl.when(i == 0)(lambda: issue(dcur_ref, 0))
    wait_slot(slot)
    for t in range(n):
        _row_copy(ys_hbm, buf.at[1 - slot], dnxt_ref[0, 0, t], t * ROW_TILES,
                  sems.at[1 - slot]).start(priority=t % 2)

    half = ROW_TILES * LANES
    acc_lo = x1_ref[:, :half]
    acc_hi = x1_ref[:, half:]
    for j in range(TOP_K):
        lo, hi = _unpack_bf16_pairs(_load_tiled(buf.at[slot], tq, first_row=j * tq))
        acc_lo = acc_lo + w_ref[:, j:j + 1] * lo.astype(F32)
        acc_hi = acc_hi + w_ref[:, j:j + 1] * hi.astype(F32)
    ssq = jnp.sum(acc_lo * acc_lo, axis=-1, keepdims=True) + jnp.sum(acc_hi * acc_hi, axis=-1, keepdims=True)
    scale = lax.rsqrt(ssq * (1.0 / (2 * half)) + NORM_EPS)
    o_ref[:, :half] = acc_lo * scale * fw_ref[:, :half]
    o_ref[:, half:] = acc_hi * scale * fw_ref[:, half:]
    pl.when(i == n_steps - 1)(lambda: wait_slot(1 - slot))


def combine(x1, ys, dest, top_w, final_w):
    T, D = x1.shape
    tq = COMBINE_TOKENS
    nblk = T // tq
    dest_blk = (dest * ROW_TILES).reshape(nblk, tq, TOP_K).transpose(0, 2, 1).reshape(nblk, 1, TOP_K * tq)
    idx_spec = lambda f: pl.BlockSpec((1, 1, TOP_K * tq), f, memory_space=pltpu.SMEM)
    return pl.pallas_call(
        _combine_kernel,
        grid=(nblk,),
        in_specs=[idx_spec(lambda i: (i, 0, 0)),
                  idx_spec(lambda i: (jnp.minimum(i + 1, nblk - 1), 0, 0)),
                  pl.BlockSpec((tq, D), lambda i: (i, 0)),
                  pl.BlockSpec((tq, TOP_K), lambda i: (i, 0)),
                  pl.BlockSpec((1, D), lambda i: (0, 0)),
                  pl.BlockSpec(memory_space=pl.ANY)],
        out_specs=pl.BlockSpec((tq, D), lambda i: (i, 0)),
        out_shape=jax.ShapeDtypeStruct((T, D), F32),
        scratch_shapes=[pltpu.VMEM((2, TOP_K * tq * ROW_TILES, LANES), ys.dtype), pltpu.SemaphoreType.DMA((2,))],
        compiler_params=_cparams(("arbitrary",)),
        name="moe_combine",
    )(dest_blk, dest_blk, x1, top_w, final_w.reshape(1, D), ys)


def _routing(logits):
    T = logits.shape[0]
    TK = T * TOP_K
    NB = TK // ROW_BLOCK + N_EXPERTS
    top_logits, top_idx = lax.top_k(logits[:, :N_EXPERTS], TOP_K)
    top_w = jax.nn.softmax(top_logits, axis=-1)
    experts = jnp.arange(N_EXPERTS, dtype=jnp.int32)
    onehot = (top_idx.reshape(TK, 1) == experts[None, :]).astype(F32)
    oh = onehot.reshape(TK // RANK_GROUP, RANK_GROUP, N_EXPERTS)
    local = jnp.einsum("ts,gse->gte", jnp.tril(jnp.ones((RANK_GROUP, RANK_GROUP), F32)), oh)
    tot = local[:, -1, :]
    offs = jnp.cumsum(tot, axis=0) - tot
    counts = (offs[-1] + tot[-1]).astype(jnp.int32)
    padded = (counts + ROW_BLOCK - 1) // ROW_BLOCK * ROW_BLOCK
    pad_end = jnp.cumsum(padded).astype(jnp.int32)
    pad_start = pad_end - padded
    dest = jnp.sum(oh * (local + (offs + (pad_start.astype(F32) - 1.0)[None, :])[:, None, :]), axis=-1)
    dest = dest.reshape(T, TOP_K).astype(jnp.int32)

    blk = jnp.arange(NB, dtype=jnp.int32)
    block_e = jnp.minimum(jnp.sum((pad_end[None, :] <= (blk * ROW_BLOCK)[:, None]).astype(jnp.int32), axis=1),
                          N_EXPERTS - 1)
    of_block = lambda table: jnp.sum(jnp.where(block_e[:, None] == experts[None, :], table[None, :], 0), axis=1)
    nb_used = pad_end[-1] // ROW_BLOCK
    rows_left = of_block(counts) - (blk - of_block(pad_start // ROW_BLOCK)) * ROW_BLOCK
    block_parts = jnp.where(blk < nb_used, jnp.clip((rows_left + ROW_PART - 1) // ROW_PART, 1, ROW_BLOCK // ROW_PART), 0)
    used = padded > 0
    block_pos = of_block(jnp.cumsum(used.astype(jnp.int32)) - 1)
    tail = nb_used + experts
    zrow = jnp.concatenate([jnp.where(used, pad_end - ROW_BLOCK, -1),
                            jnp.where(tail < NB, tail * ROW_BLOCK, -1)]).astype(jnp.int32)
    return dest, top_w, (block_e, block_parts, block_pos, used, nb_used), zrow


def _schedule(blocks, n_tiles):
    block_e, block_parts, block_pos, used, nb_used = blocks
    nb = block_e.shape[0]
    b = jnp.asarray(np.tile(np.arange(nb, dtype=np.int32), n_tiles))
    f = jnp.asarray(np.repeat(np.arange(n_tiles, dtype=np.int32), nb))
    rep = lambda a: jnp.tile(a, n_tiles)
    n_groups = (jnp.sum(used) * n_tiles).astype(jnp.int32)
    n_valid = (nb_used * n_tiles).astype(jnp.int32)
    group = jnp.minimum(rep(block_pos) * n_tiles + f, n_groups - 1)
    key = jnp.where(b < nb_used, (rep(block_e) * n_tiles + f) * nb + b, (N_EXPERTS * n_tiles + f) * nb + b)
    _, group, sb, so, parts = lax.sort((key, group, b, f, rep(block_parts)), num_keys=1)
    sg = jnp.where(jnp.arange(nb * n_tiles) < n_valid, group, n_groups - 1)
    experts = jnp.arange(N_EXPERTS, dtype=jnp.int32)
    used_first = jnp.argsort(jnp.where(used, experts, N_EXPERTS + experts)).astype(jnp.int32)
    ge = jnp.repeat(used_first, n_tiles)
    gf = jnp.tile(jnp.arange(n_tiles, dtype=jnp.int32), N_EXPERTS)
    return (n_valid.reshape(1), sg, sb, so, parts, ge, gf, n_groups.reshape(1))


REPACK_ROWS = 256


def _repack_kernel(w_ref, wr_ref, wm_ref, wg_ref):
    w = w_ref[...]
    z = lambda n: jnp.zeros((w.shape[0], n), w.dtype)
    o = RWKV_COLS
    a, b = RW_WD + DECAY_LORA, RW_WD + DECAY_LORA + AAA_LORA
    wr_ref[...] = jnp.concatenate([w[:, :a], z(LORA_PAD - DECAY_LORA), w[:, a:b], z(LORA_PAD - AAA_LORA),
                                   w[:, b:o]], axis=1).astype(wr_ref.dtype)
    g = o + ML_IF + 2 * MLSTM_HEADS
    wm_ref[...] = jnp.concatenate([w[:, o:g], z(LANES - 2 * MLSTM_HEADS), w[:, g:o + MLSTM_COLS]],
                                  axis=1).astype(wm_ref.dtype)
    wg_ref[...] = w[:, o + MLSTM_COLS:].astype(wg_ref.dtype)


def _repack_w_in(w_in, layer):
    _, D, C = w_in.shape
    blk = lambda n: pl.BlockSpec((REPACK_ROWS, n), lambda i: (i, 0))
    n_gate = C - RWKV_COLS - MLSTM_COLS
    return pl.pallas_call(
        _repack_kernel,
        grid=(D // REPACK_ROWS,),
        in_specs=[pl.BlockSpec((None, REPACK_ROWS, C), lambda i: (layer, i, 0))],
        out_specs=[blk(RW_COLS_P), blk(ML_COLS_P), blk(n_gate)],
        out_shape=[jax.ShapeDtypeStruct((D, n), BF16) for n in (RW_COLS_P, ML_COLS_P, n_gate)],
        compiler_params=_cparams(("parallel",)),
        name="repack_w_in",
    )(w_in)


def kernel(x, norm1_w, w_in, b_gate, rwkv_mu, rwkv_w0, rwkv_w_up, rwkv_a0, rwkv_a_up, rwkv_g_up, rwkv_k_k,
           rwkv_k_a, rwkv_r_k, rwkv_ln_w, rwkv_ln_b, mlstm_conv_w, mlstm_i_b, mlstm_f_b, mlstm_norm_w,
           w_branch_a, w_branch_b, w_out, norm2_w, router_w, router_b, w_gu, b_gu, w_down, b_down,
           final_norm_w):
    B, S, D = x.shape
    T = B * S
    xt = x.reshape(T, D)
    assert norm1_w.shape[0] == 1, "single-layer block: the final rmsnorm is fused into the MoE combine"
    for l in range(1):
        w_r, w_m, w_g = _repack_w_in(w_in, l)
        hn, p_m, *scan_in = rwkv_in(x, norm1_w[l], w_r, w_m, rwkv_mu[l], rwkv_w0[l], rwkv_w_up[l], rwkv_a0[l],
                                    rwkv_a_up[l], rwkv_g_up[l], rwkv_k_k[l], rwkv_k_a[l])
        hn = hn.reshape(T, D)
        o_a = rwkv_scan(*scan_in, rwkv_r_k[l].reshape(1, -1), rwkv_ln_w[l].reshape(1, -1),
                        rwkv_ln_b[l].reshape(1, -1))
        o_b = mlstm_branch(p_m, mlstm_conv_w[l], mlstm_i_b[l], mlstm_f_b[l], mlstm_norm_w[l])
        x1, xp, logits = merge_project(o_a.reshape(T, RWKV_DIM), o_b.reshape(T, MLSTM_DIM), hn, w_g, b_gate[l], xt,
                                       w_branch_a[l], w_branch_b[l], w_out[l], norm2_w[l], router_w[l],
                                       router_b[l])
        dest, top_w, blocks, zrow = _routing(logits)
        xs = moe_dispatch(xp, dest, zrow, blocks[0].shape[0] * ROW_BLOCK)
        h = moe_up(xs, w_gu[l], b_gu[l], _schedule(blocks, EXPERT_FF // UP_TILE))
        ys = moe_down(h, w_down[l], b_down[l], _schedule(blocks, D // DOWN_TILE))
        xt = combine(x1, ys, dest, top_w, final_norm_w)
    return xt.reshape(B, S, D)
```
